```python
import math
import jax
import jax.numpy as jnp
from jax import lax
import numpy as np

D_MODEL = 1024
BATCH = 4
SEQ = 8192
DEPTH = 1

DILATED_GROUPS = ((128, 1), (512, 4), (2048, 16))
N_GROUPS_A = len(DILATED_GROUPS)
HEADS_PER_GROUP_A = 4
HEAD_DIM_A = 128
N_HEADS_A = N_GROUPS_A * HEADS_PER_GROUP_A
WIDTH_A = HEADS_PER_GROUP_A * HEAD_DIM_A
NUM_BUCKETS = 32
MAX_DISTANCE = 2048
HGRN_HEAD_DIM = 128
N_HEADS_B = D_MODEL // HGRN_HEAD_DIM
WIDTH_B = N_HEADS_B * HGRN_HEAD_DIM
CHUNK_B = 64
MEM_LEN = 256
N_HEADS_C = 4
HEAD_DIM_C = 128
WIDTH_C = N_HEADS_C * HEAD_DIM_C
N_BRANCHES = 3
COLS_A = N_GROUPS_A * 3 * WIDTH_A
COLS_B = 4 * WIDTH_B
COLS_C = WIDTH_C
COLS_GATE = N_BRANCHES * D_MODEL
N_IN = COLS_A + COLS_B + COLS_C + COLS_GATE
N_EXPERT_GROUPS = 4
EXPERTS_PER_GROUP = 8
N_EXPERTS = N_EXPERT_GROUPS * EXPERTS_PER_GROUP
TOP_K_INNER = 2
D_EXPERT = D_MODEL // 4
MOE_BLOCK = 128
DN_ALPHA = (2 * DEPTH) ** 0.25
DN_BETA = (8 * DEPTH) ** -0.25
LN_EPS = 1e-5
RMS_EPS = 1e-6

kernel_name = 'hybrid_dilated_hgrn2_mem_hmoe_block'


def _layer_norm(x, g, b):
    xf = x.astype(jnp.float32)
    mu = jnp.mean(xf, axis=-1, keepdims=True)
    var = jnp.mean(jnp.square(xf - mu), axis=-1, keepdims=True)
    return ((xf - mu) * lax.rsqrt(var + LN_EPS) * g.astype(jnp.float32) + b.astype(jnp.float32)).astype(x.dtype)


def _t5_bucket_np(dist):
    dist = np.asarray(dist, np.int32)
    max_exact = NUM_BUCKETS // 2
    d = np.maximum(dist, 1).astype(np.float32)
    large = max_exact + (np.log(d / max_exact) / math.log(MAX_DISTANCE / max_exact) * (NUM_BUCKETS - max_exact)).astype(np.int32)
    large = np.minimum(large, NUM_BUCKETS - 1)
    return np.where(dist < max_exact, dist, large).astype(np.int32)


def _dilated_band_attention(q, k, v, bias_tab, window, dilation):
    B, S, H, Dh = q.shape
    band = window // dilation
    L = S // dilation
    nb = -(-L // band)
    Lp = nb * band

    def to_blocks(t):
        t = t.reshape(B, L, dilation, H, Dh).transpose(0, 2, 1, 3, 4)
        t = jnp.pad(t, ((0, 0), (0, 0), (0, Lp - L), (0, 0), (0, 0)))
        return t.reshape(B, dilation, nb, band, H, Dh)

    def with_prev(t):
        prev = jnp.pad(t, ((0, 0), (0, 0), (1, 0), (0, 0), (0, 0), (0, 0)))[:, :, :-1]
        return jnp.concatenate([prev, t], axis=3)

    qb = to_blocks(q)
    kc = with_prev(to_blocks(k))
    vc = with_prev(to_blocks(v))

    i = np.arange(band)[:, None]
    j = np.arange(2 * band)[None, :]
    u = i + band - j
    in_band = (u >= 0) & (u <= band)
    first_block = (np.arange(nb)[:, None, None] == 0) & (j[None] < band)
    valid = in_band[None] & ~first_block
    bucket = _t5_bucket_np(np.clip(u, 0, band) * dilation)
    bias = jnp.take(bias_tab, jnp.asarray(bucket), axis=0).astype(jnp.float32).transpose(2, 0, 1)

    s = jnp.einsum('bcnihd,bcnjhd->bcnhij', qb, kc).astype(jnp.float32) * (Dh ** -0.5)
    s = jnp.where(valid[None, None, :, None], s + bias[None, None, None], -jnp.inf)
    m = jnp.max(s, axis=-1, keepdims=True)
    p = jnp.exp(s - m)
    l = jnp.sum(p, axis=-1, keepdims=True)
    o = jnp.einsum('bcnhij,bcnjhd->bcnihd', p, vc.astype(jnp.float32))
    o = o / l.transpose(0, 1, 2, 4, 3, 5)
    lse = (m + jnp.log(l))[..., 0].transpose(0, 1, 2, 4, 3)

    def from_blocks(t):
        t = t.reshape((B, dilation, Lp) + t.shape[4:])[:, :, :L]
        t = jnp.swapaxes(t, 1, 2)
        return t.reshape((B, S) + t.shape[3:])

    return from_blocks(o), from_blocks(lse)


def _hgrn2(q, f_raw, i_in, g, lb, norm_g):
    B, S, H, Dk = q.shape
    n_chunks = S // CHUNK_B
    qf = jax.nn.silu(q.astype(jnp.float32))
    f = lb + (1.0 - lb) * jax.nn.sigmoid(f_raw.astype(jnp.float32))
    logf = jnp.log(f)
    kf = 1.0 - f
    vf = i_in.astype(jnp.float32)

    def chunks(t):
        return t.reshape(B, n_chunks, CHUNK_B, H, t.shape[-1]).transpose(1, 0, 3, 2, 4)

    qc, kc, vc, lc = chunks(qf), chunks(kf), chunks(vf), chunks(logf)
    bc = jnp.cumsum(lc, axis=3)
    causal = np.tril(np.ones((CHUNK_B, CHUNK_B), dtype=bool))

    def step(state, inp):
        qt, kt, vt, bt = inp
        o_inter = jnp.einsum('bhtk,bhkv->bhtv', qt * jnp.exp(bt), state)
        diff = jnp.where(causal[:, :, None], bt[:, :, :, None, :] - bt[:, :, None, :, :], -jnp.inf)
        att = jnp.einsum('bhtk,bhsk,bhtsk->bhts', qt, kt, jnp.exp(diff))
        o_intra = jnp.einsum('bhts,bhsv->bhtv', att, vt)
        b_last = bt[:, :, -1:, :]
        new_state = jnp.exp(b_last[:, :, 0, :, None]) * state + jnp.einsum('bhsk,bhsv->bhkv', kt * jnp.exp(b_last - bt), vt)
        return new_state, o_inter + o_intra

    s0 = jnp.zeros((B, H, Dk, vf.shape[-1]), jnp.float32)
    _, o = lax.scan(step, s0, (qc, kc, vc, bc))
    o = o.transpose(1, 0, 3, 2, 4).reshape(B, S, H, -1)
    o = o * lax.rsqrt(jnp.mean(o * o, axis=-1, keepdims=True) + RMS_EPS)
    o = o.reshape(B, S, -1) * norm_g.astype(jnp.float32) * jax.nn.silu(g.reshape(B, S, -1).astype(jnp.float32))
    return o.astype(q.dtype)


def _memory_attention(q, mem, w_kv):
    B, S = q.shape[0], q.shape[1]
    M = mem.shape[1]
    kv = (mem @ w_kv).reshape(B, M, 2, N_HEADS_C, HEAD_DIM_C)
    k, v = kv[:, :, 0], kv[:, :, 1]
    s = jnp.einsum('bshd,bmhd->bhsm', q, k).astype(jnp.float32) * (HEAD_DIM_C ** -0.5)
    p = jax.nn.softmax(s, axis=-1)
    o = jnp.einsum('bhsm,bmhd->bshd', p, v.astype(jnp.float32))
    return o.reshape(B, S, WIDTH_C).astype(q.dtype)


def _hier_moe(h, w_rg, b_rg, w_re, b_re, w_gate, w_up, w_down):
    B, S, D = h.shape
    T = B * S
    ht = h.reshape(T, D)
    glog = (ht @ w_rg).astype(jnp.float32) + b_rg.astype(jnp.float32)
    gprob = jax.nn.softmax(glog, axis=-1)
    gsel = jnp.argmax(glog, axis=-1)
    elog = ((ht @ w_re).astype(jnp.float32) + b_re.astype(jnp.float32)).reshape(T, N_EXPERT_GROUPS, EXPERTS_PER_GROUP)
    elog_sel = elog[jnp.arange(T), gsel]
    top_v, top_i = lax.top_k(elog_sel, TOP_K_INNER)
    weights = jax.nn.softmax(top_v, axis=-1) * jnp.take_along_axis(gprob, gsel[:, None], axis=1)
    eid = gsel[:, None] * EXPERTS_PER_GROUP + top_i

    M = T * TOP_K_INNER
    e_flat = eid.reshape(M)
    w_flat = weights.reshape(M)
    tok = jnp.arange(M) // TOP_K_INNER
    order = jnp.argsort(e_flat)
    e_s, tok_s, w_s = e_flat[order], tok[order], w_flat[order]
    counts = jnp.bincount(e_flat, length=N_EXPERTS)
    starts = jnp.cumsum(counts) - counts
    pcounts = (counts + MOE_BLOCK - 1) // MOE_BLOCK * MOE_BLOCK
    pends = jnp.cumsum(pcounts)
    pstarts = pends - pcounts
    dest = pstarts[e_s] + (jnp.arange(M) - starts[e_s])
    n_blocks = (M + N_EXPERTS * (MOE_BLOCK - 1) + MOE_BLOCK - 1) // MOE_BLOCK
    P = n_blocks * MOE_BLOCK
    xbuf = jnp.zeros((P, D), h.dtype).at[dest].set(ht[tok_s])
    block_e = jnp.minimum(jnp.searchsorted(pends, jnp.arange(n_blocks) * MOE_BLOCK, side='right'), N_EXPERTS - 1)

    def expert_block(args):
        xb, e = args
        return (jax.nn.silu(xb @ w_gate[e]) * (xb @ w_up[e])) @ w_down[e]

    ybuf = lax.map(expert_block, (xbuf.reshape(n_blocks, MOE_BLOCK, D), block_e)).reshape(P, D)
    y = jnp.zeros((T, D), h.dtype).at[tok_s].add(w_s[:, None].astype(h.dtype) * ybuf[dest])
    return y.reshape(B, S, D)


def setup_inputs(seed: int = 0) -> dict:
    key = jax.random.key(seed)
    ks = jax.random.split(key, 24)

    def nrm(k, shape, scale):
        return jax.random.normal(k, shape, jnp.float32) * scale

    scale_a = np.ones((N_GROUPS_A, 3, WIDTH_A), np.float32)
    scale_a[:, 2] = DN_BETA
    scale_b = np.ones((4, WIDTH_B), np.float32)
    scale_b[2] = DN_BETA
    col_scale = jnp.asarray(np.concatenate([scale_a.ravel(), scale_b.ravel(), np.ones(COLS_C + COLS_GATE, np.float32)]))
    kv_scale = jnp.asarray(np.concatenate([np.ones(WIDTH_C, np.float32), np.full(WIDTH_C, DN_BETA, np.float32)]))
    return {
        'x': nrm(ks[0], (BATCH, SEQ, D_MODEL), 1.0),
        'mem': nrm(ks[1], (BATCH, MEM_LEN, D_MODEL), 1.0),
        'rel_bias': nrm(ks[2], (NUM_BUCKETS, N_HEADS_A), 0.2),
        'hgrn_lb_logits': nrm(ks[3], (DEPTH + 1, WIDTH_B), 1.0),
        'w_in': nrm(ks[4], (DEPTH, D_MODEL, N_IN), D_MODEL ** -0.5) * col_scale,
        'w_mem_kv': nrm(ks[5], (DEPTH, D_MODEL, 2 * WIDTH_C), D_MODEL ** -0.5) * kv_scale,
        'hgrn_norm_g': 1.0 + nrm(ks[6], (DEPTH, WIDTH_B), 0.02),
        'w_branch_a': nrm(ks[7], (DEPTH, WIDTH_A, D_MODEL), WIDTH_A ** -0.5 * DN_BETA),
        'w_branch_b': nrm(ks[8], (DEPTH, WIDTH_B, D_MODEL), WIDTH_B ** -0.5 * DN_BETA),
        'w_branch_c': nrm(ks[9], (DEPTH, WIDTH_C, D_MODEL), WIDTH_C ** -0.5 * DN_BETA),
        'w_out': nrm(ks[10], (DEPTH, D_MODEL, D_MODEL), D_MODEL ** -0.5 * DN_BETA),
        'ln1_g': 1.0 + nrm(ks[11], (DEPTH, D_MODEL), 0.02),
        'ln1_b': nrm(ks[12], (DEPTH, D_MODEL), 0.02),
        'w_router_group': nrm(ks[13], (DEPTH, D_MODEL, N_EXPERT_GROUPS), D_MODEL ** -0.5),
        'b_router_group': nrm(ks[14], (DEPTH, N_EXPERT_GROUPS), 0.01),
        'w_router_expert': nrm(ks[15], (DEPTH, D_MODEL, N_EXPERTS), D_MODEL ** -0.5),
        'b_router_expert': nrm(ks[16], (DEPTH, N_EXPERTS), 0.01),
        'w_exp_gate': nrm(ks[17], (DEPTH, N_EXPERTS, D_MODEL, D_EXPERT), D_MODEL ** -0.5 * DN_BETA),
        'w_exp_up': nrm(ks[18], (DEPTH, N_EXPERTS, D_MODEL, D_EXPERT), D_MODEL ** -0.5 * DN_BETA),
        'w_exp_down': nrm(ks[19], (DEPTH, N_EXPERTS, D_EXPERT, D_MODEL), D_EXPERT ** -0.5 * DN_BETA),
        'ln2_g': 1.0 + nrm(ks[20], (DEPTH, D_MODEL), 0.02),
        'ln2_b': nrm(ks[21], (DEPTH, D_MODEL), 0.02),
    }


def reference(x, mem, rel_bias, hgrn_lb_logits, w_in, w_mem_kv, hgrn_norm_g, w_branch_a, w_branch_b, w_branch_c, w_out, ln1_g, ln1_b, w_router_group, b_router_group, w_router_expert, b_router_expert, w_exp_gate, w_exp_up, w_exp_down, ln2_g, ln2_b):
    B, S, D = x.shape
    lower_bounds = jnp.cumsum(jax.nn.softmax(hgrn_lb_logits.astype(jnp.float32), axis=0), axis=0)
    for l in range(DEPTH):
        proj = x @ w_in[l]
        pa = proj[..., :COLS_A].reshape(B, S, N_GROUPS_A, 3, HEADS_PER_GROUP_A, HEAD_DIM_A)
        pb = proj[..., COLS_A:COLS_A + COLS_B].reshape(B, S, 4, N_HEADS_B, HGRN_HEAD_DIM)
        pc = proj[..., COLS_A + COLS_B:COLS_A + COLS_B + COLS_C].reshape(B, S, N_HEADS_C, HEAD_DIM_C)
        pg = proj[..., COLS_A + COLS_B + COLS_C:].reshape(B, S, N_BRANCHES, D)

        outs, lses = [], []
        for g, (window, dilation) in enumerate(DILATED_GROUPS):
            o_g, lse_g = _dilated_band_attention(pa[:, :, g, 0], pa[:, :, g, 1], pa[:, :, g, 2], rel_bias[:, g * HEADS_PER_GROUP_A:(g + 1) * HEADS_PER_GROUP_A], window, dilation)
            outs.append(o_g)
            lses.append(lse_g)
        mix_w = jax.nn.softmax(jnp.stack(lses), axis=0)[..., None]
        o_a = jnp.sum(mix_w * jnp.stack(outs), axis=0).reshape(B, S, WIDTH_A).astype(x.dtype)

        o_b = _hgrn2(pb[:, :, 0], pb[:, :, 1], pb[:, :, 2], pb[:, :, 3], lower_bounds[l].reshape(N_HEADS_B, HGRN_HEAD_DIM), hgrn_norm_g[l])

        o_c = _memory_attention(pc, mem, w_mem_kv[l])

        gates = jax.nn.sigmoid(pg.astype(jnp.float32)).astype(x.dtype)
        merged = gates[:, :, 0] * (o_a @ w_branch_a[l]) + gates[:, :, 1] * (o_b @ w_branch_b[l]) + gates[:, :, 2] * (o_c @ w_branch_c[l])
        x = _layer_norm(DN_ALPHA * x + merged @ w_out[l], ln1_g[l], ln1_b[l])

        y = _hier_moe(x, w_router_group[l], b_router_group[l], w_router_expert[l], b_router_expert[l], w_exp_gate[l], w_exp_up[l], w_exp_down[l])
        x = _layer_norm(DN_ALPHA * x + y, ln2_g[l], ln2_b[l])
    return x
```

```python
import functools
import math

import numpy as np
import jax
import jax.numpy as jnp
from jax import lax
from jax.experimental import pallas as pl
from jax.experimental.pallas import tpu as pltpu

F32 = jnp.float32
BF16 = jnp.bfloat16

D_MODEL = 1024
DEPTH = 1
DILATED_GROUPS = ((128, 1), (512, 4), (2048, 16))
N_GROUPS_A = len(DILATED_GROUPS)
HEADS_A = 4
HEAD_DIM = 128
WIDTH_A = HEADS_A * HEAD_DIM
BAND = 128
NUM_BUCKETS = 32
MAX_DISTANCE = 2048
N_HEADS_B = 8
WIDTH_B = N_HEADS_B * HEAD_DIM
CHUNK_B = 64
N_HEADS_C = 4
WIDTH_C = N_HEADS_C * HEAD_DIM
COLS_A = N_GROUPS_A * 3 * WIDTH_A
COLS_B = 4 * WIDTH_B
COLS_C = WIDTH_C
COLS_GATE = 3 * D_MODEL
N_EXPERT_GROUPS = 4
EXPERTS_PER_GROUP = 8
N_EXPERTS = N_EXPERT_GROUPS * EXPERTS_PER_GROUP
TOP_K = 2
D_EXPERT = D_MODEL // 4
DN_ALPHA = (2 * DEPTH) ** 0.25
LN_EPS = 1e-5
RMS_EPS = 1e-6

NEG_BIG = -1e30
SAFE_LOG_DECAY = 80.0
VMEM_LIMIT = 56 * 1024 * 1024
EXPERT_BLOCK = 256

_NT = (((1,), (1,)), ((), ()))
_TN = (((0,), (0,)), ((), ()))


def _cparams(sem):
    return pltpu.CompilerParams(dimension_semantics=sem, vmem_limit_bytes=VMEM_LIMIT)


def _matmul_kernel(x_ref, w_ref, o_ref):
    o_ref[...] = jnp.dot(x_ref[...].astype(BF16), w_ref[...], preferred_element_type=F32).astype(o_ref.dtype)


def _matmul(x, w, out_dtype, tm, tn, name):
    t, k = x.shape
    n = w.shape[1]
    return pl.pallas_call(
        _matmul_kernel,
        grid=(n // tn, t // tm),
        in_specs=[pl.BlockSpec((tm, k), lambda j, i: (i, 0)), pl.BlockSpec((k, tn), lambda j, i: (0, j))],
        out_specs=pl.BlockSpec((tm, tn), lambda j, i: (i, j)),
        out_shape=jax.ShapeDtypeStruct((t, n), out_dtype),
        compiler_params=_cparams(("arbitrary", "arbitrary")),
        name=name,
    )(x, w)


def _t5_bucket_np(dist):
    dist = np.asarray(dist, np.int32)
    max_exact = NUM_BUCKETS // 2
    d = np.maximum(dist, 1).astype(np.float32)
    large = max_exact + (np.log(d / max_exact) / math.log(MAX_DISTANCE / max_exact) * (NUM_BUCKETS - max_exact)).astype(np.int32)
    large = np.minimum(large, NUM_BUCKETS - 1)
    return np.where(dist < max_exact, dist, large).astype(np.int32)


def _band_bias(bias_tab, dilation):
    i = np.arange(BAND)[:, None]
    j = np.arange(2 * BAND)[None, :]
    u = i + BAND - j
    in_band = (u >= 0) & (u <= BAND)
    bucket = _t5_bucket_np(np.clip(u, 0, BAND) * dilation)
    bias = jnp.take(bias_tab.astype(F32), jnp.asarray(bucket), axis=0).transpose(2, 0, 1)
    general = jnp.where(in_band[None], bias, NEG_BIG)
    first = jnp.where((in_band & (j >= BAND))[None], bias, NEG_BIG)
    return jnp.stack([first, general])


def _band_attn_kernel(q_ref, kp_ref, kc_ref, vp_ref, vc_ref, bias_ref, o_ref, lse_ref):
    for h in range(HEADS_A):
        sl = slice(h * HEAD_DIM, (h + 1) * HEAD_DIM)
        q = q_ref[:, sl]
        sp = lax.dot_general(q, kp_ref[:, sl], _NT, preferred_element_type=F32) + bias_ref[h, :, :BAND]
        sc = lax.dot_general(q, kc_ref[:, sl], _NT, preferred_element_type=F32) + bias_ref[h, :, BAND:]
        m = jnp.maximum(jnp.max(sp, axis=-1, keepdims=True), jnp.max(sc, axis=-1, keepdims=True))
        pp = jnp.exp(sp - m)
        pc = jnp.exp(sc - m)
        l = jnp.sum(pp, axis=-1, keepdims=True) + jnp.sum(pc, axis=-1, keepdims=True)
        o = jnp.dot(pp.astype(BF16), vp_ref[:, sl], preferred_element_type=F32)
        o = o + jnp.dot(pc.astype(BF16), vc_ref[:, sl], preferred_element_type=F32)
        o_ref[:, sl] = o / l
        lse_ref[:, sl] = jnp.broadcast_to(m + jnp.log(l), (BAND, HEAD_DIM))


def _band_attention(pa, bias, group, dilation, name):
    b, s, _ = pa.shape
    r = dilation
    l = s // r
    nb = l // BAND
    cb = COLS_A // WIDTH_A
    pav = pa.reshape(b, l, r * COLS_A)

    def col(which):
        return lambda bi, c, n: (bi, n, c * cb + group * 3 + which)

    def col_prev(which):
        return lambda bi, c, n: (bi, jnp.maximum(n - 1, 0), c * cb + group * 3 + which)

    blk = (None, BAND, WIDTH_A)
    out_spec = pl.BlockSpec(blk, lambda bi, c, n: (bi, n, c))
    o, lse = pl.pallas_call(
        _band_attn_kernel,
        grid=(b, r, nb),
        in_specs=[
            pl.BlockSpec(blk, col(0)),
            pl.BlockSpec(blk, col_prev(1)),
            pl.BlockSpec(blk, col(1)),
            pl.BlockSpec(blk, col_prev(2)),
            pl.BlockSpec(blk, col(2)),
            pl.BlockSpec((None, HEADS_A, BAND, 2 * BAND), lambda bi, c, n: (jnp.minimum(n, 1), 0, 0, 0)),
        ],
        out_specs=[out_spec, out_spec],
        out_shape=[jax.ShapeDtypeStruct((b, l, r * WIDTH_A), F32)] * 2,
        compiler_params=_cparams(("arbitrary", "arbitrary", "arbitrary")),
        name=name,
    )(pav, pav, pav, pav, pav, bias)
    return o.reshape(b, s, WIDTH_A), lse.reshape(b, s, WIDTH_A)


HGRN_TILE = 256
_N_CHUNK = HGRN_TILE // CHUNK_B


def _dynamic_row(ref, r, ls):
    blk = ref[pl.ds(pl.multiple_of((r // 8) * 8, 8), 8), ls]
    sub = lax.broadcasted_iota(jnp.int32, blk.shape, 0)
    return jnp.sum(jnp.where(sub == r % 8, blk, 0.0), axis=0, keepdims=True)


def _hgrn_kernel(q_ref, f_ref, i_ref, g_ref, lbl_ref, ng_ref, o_ref,
                 st_ref, qs_ref, ks_ref, kh_ref, vs_ref, bs_ref, ebl_ref, a_ref, os_ref):
    @pl.when(pl.program_id(1) == 0)
    def _():
        st_ref[...] = jnp.zeros_like(st_ref)

    lg = lbl_ref[...]
    e = jnp.exp(lg - jnp.max(lg, axis=0, keepdims=True))
    lb = e[0:1, :] / jnp.sum(e, axis=0, keepdims=True)

    row = lax.broadcasted_iota(jnp.int32, (CHUNK_B, CHUNK_B), 0)
    colm = lax.broadcasted_iota(jnp.int32, (CHUNK_B, CHUNK_B), 1)
    causal = row >= colm
    tri = causal.astype(F32)

    f = lb + (1.0 - lb) * jax.nn.sigmoid(f_ref[...])
    logf = jnp.log(f)
    kf = 1.0 - f
    qsil = jax.nn.silu(q_ref[...])
    min_b = None
    for c in range(_N_CHUNK):
        rs = slice(c * CHUNK_B, (c + 1) * CHUNK_B)
        b = jnp.dot(tri, logf[rs], preferred_element_type=F32, precision=lax.Precision.HIGHEST)
        b_last = b[CHUNK_B - 1:CHUNK_B, :]
        bs_ref[rs, :] = b
        ebl_ref[c:c + 1, :] = jnp.exp(b_last)
        qs_ref[rs, :] = (qsil[rs] * jnp.exp(b)).astype(BF16)
        ks_ref[rs, :] = (kf[rs] * jnp.exp(-b)).astype(BF16)
        kh_ref[rs, :] = (kf[rs] * jnp.exp(b_last - b)).astype(BF16)
        cm = jnp.min(b_last)
        min_b = cm if min_b is None else jnp.minimum(min_b, cm)
    vs_ref[...] = i_ref[...].astype(BF16)
    safe = min_b > -SAFE_LOG_DECAY

    @pl.when(safe)
    def _():
        for c in range(_N_CHUNK):
            rs = slice(c * CHUNK_B, (c + 1) * CHUNK_B)
            for h in range(N_HEADS_B):
                ls = slice(h * HEAD_DIM, (h + 1) * HEAD_DIM)
                a = lax.dot_general(qs_ref[rs, ls], ks_ref[rs, ls], _NT, preferred_element_type=F32)
                a_ref[c * N_HEADS_B + h] = jnp.where(causal, a, 0.0)

    @pl.when(jnp.logical_not(safe))
    def _():
        lane = lax.broadcasted_iota(jnp.int32, (1, CHUNK_B), 1)
        trow = lax.broadcasted_iota(jnp.int32, (CHUNK_B, 1), 0)
        for c in range(_N_CHUNK):
            rs = slice(c * CHUNK_B, (c + 1) * CHUNK_B)
            for h in range(N_HEADS_B):
                ls = slice(h * HEAD_DIM, (h + 1) * HEAD_DIM)
                bq = bs_ref[rs, ls]
                qh = jax.nn.silu(q_ref[rs, ls])

                def body(s, acc, c=c, ls=ls, bq=bq, qh=qh):
                    b_s = _dynamic_row(bs_ref, c * CHUNK_B + s, ls)
                    f_s = lb[:, ls] + (1.0 - lb[:, ls]) * jax.nn.sigmoid(_dynamic_row(f_ref, c * CHUNK_B + s, ls))
                    w = jnp.exp(jnp.minimum(bq - b_s, 0.0))
                    colv = jnp.sum(qh * (1.0 - f_s) * w, axis=-1, keepdims=True)
                    colv = jnp.where(trow >= s, colv, 0.0)
                    return acc + colv * (lane == s).astype(F32)

                a_ref[c * N_HEADS_B + h] = lax.fori_loop(0, CHUNK_B, body, jnp.zeros((CHUNK_B, CHUNK_B), F32))

    for c in range(_N_CHUNK):
        rs = slice(c * CHUNK_B, (c + 1) * CHUNK_B)
        for h in range(N_HEADS_B):
            ls = slice(h * HEAD_DIM, (h + 1) * HEAD_DIM)
            st = st_ref[h]
            vh = vs_ref[rs, ls]
            o = lax.dot_general(qs_ref[rs, ls], st.astype(BF16), _NT, preferred_element_type=F32)
            o = o + jnp.dot(a_ref[c * N_HEADS_B + h].astype(BF16), vh, preferred_element_type=F32)
            os_ref[rs, ls] = o
            st_ref[h] = st * ebl_ref[c:c + 1, ls] + lax.dot_general(vh, kh_ref[rs, ls], _TN, preferred_element_type=F32)

    ng = ng_ref[...]
    for h in range(N_HEADS_B):
        ls = slice(h * HEAD_DIM, (h + 1) * HEAD_DIM)
        o = os_ref[:, ls]
        o = o * lax.rsqrt(jnp.mean(o * o, axis=-1, keepdims=True) + RMS_EPS)
        o_ref[:, ls] = (o * ng[:, ls] * jax.nn.silu(g_ref[:, ls])).astype(o_ref.dtype)


def _hgrn2(pb, lb_logits, norm_g):
    b, s, _ = pb.shape
    tb = HGRN_TILE
    blk = (None, tb, WIDTH_B)

    def part(p):
        return pl.BlockSpec(blk, lambda bi, t: (bi, t, p))

    n_slots = lb_logits.shape[0]
    return pl.pallas_call(
        _hgrn_kernel,
        grid=(b, s // tb),
        in_specs=[part(0), part(1), part(2), part(3),
                  pl.BlockSpec((n_slots, WIDTH_B), lambda bi, t: (0, 0)),
                  pl.BlockSpec((1, WIDTH_B), lambda bi, t: (0, 0))],
        out_specs=pl.BlockSpec(blk, lambda bi, t: (bi, t, 0)),
        out_shape=jax.ShapeDtypeStruct((b, s, WIDTH_B), BF16),
        scratch_shapes=[
            pltpu.VMEM((N_HEADS_B, HEAD_DIM, HEAD_DIM), F32),
            pltpu.VMEM((tb, WIDTH_B), BF16),
            pltpu.VMEM((tb, WIDTH_B), BF16),
            pltpu.VMEM((tb, WIDTH_B), BF16),
            pltpu.VMEM((tb, WIDTH_B), BF16),
            pltpu.VMEM((tb, WIDTH_B), F32),
            pltpu.VMEM((8, WIDTH_B), F32),
            pltpu.VMEM((_N_CHUNK * N_HEADS_B, CHUNK_B, CHUNK_B), F32),
            pltpu.VMEM((tb, WIDTH_B), F32),
        ],
        compiler_params=_cparams(("arbitrary", "arbitrary")),
        name="hgrn2",
    )(pb, pb, pb, pb, lb_logits, norm_g)


MERGE_TILE = 256


def _layer_norm(y, g, b):
    mu = jnp.mean(y, axis=-1, keepdims=True)
    d = y - mu
    var = jnp.mean(d * d, axis=-1, keepdims=True)
    return d * lax.rsqrt(var + LN_EPS) * g + b


def _merge_kernel(x_ref, o1_ref, l1_ref, o2_ref, l2_ref, o3_ref, l3_ref, ob_ref, mk_ref, mv_ref,
                  wc_ref, wg_ref, wa_ref, wb_ref, wcc_ref, wo_ref, g_ref, b_ref, h_ref, oc_ref):
    x = x_ref[...]
    xb = x.astype(BF16)

    qc = jnp.dot(xb, wc_ref[...], preferred_element_type=F32).astype(BF16)
    for h in range(N_HEADS_C):
        ls = slice(h * HEAD_DIM, (h + 1) * HEAD_DIM)
        s = lax.dot_general(qc[:, ls], mk_ref[:, ls], _NT, preferred_element_type=F32)
        p = jnp.exp(s - jnp.max(s, axis=-1, keepdims=True))
        l = jnp.sum(p, axis=-1, keepdims=True)
        oc_ref[:, ls] = (jnp.dot(p.astype(BF16), mv_ref[:, ls], preferred_element_type=F32) / l).astype(BF16)

    l1, l2, l3 = l1_ref[...], l2_ref[...], l3_ref[...]
    m = jnp.maximum(jnp.maximum(l1, l2), l3)
    e1, e2, e3 = jnp.exp(l1 - m), jnp.exp(l2 - m), jnp.exp(l3 - m)
    oa = (e1 * o1_ref[...] + e2 * o2_ref[...] + e3 * o3_ref[...]) / (e1 + e2 + e3)

    gates = jax.nn.sigmoid(jnp.dot(xb, wg_ref[...], preferred_element_type=F32))
    merged = gates[:, :D_MODEL] * jnp.dot(oa.astype(BF16), wa_ref[...], preferred_element_type=F32)
    merged = merged + gates[:, D_MODEL:2 * D_MODEL] * jnp.dot(ob_ref[...], wb_ref[...], preferred_element_type=F32)
    merged = merged + gates[:, 2 * D_MODEL:] * jnp.dot(oc_ref[...], wcc_ref[...], preferred_element_type=F32)
    y = DN_ALPHA * x + jnp.dot(merged.astype(BF16), wo_ref[...], preferred_element_type=F32)
    h_ref[...] = _layer_norm(y, g_ref[...], b_ref[...])


def _merge(x, oa_parts, ob, mk, mv, wc, wg, wa, wb, wcc, wo, ln_g, ln_b):
    b, s, d = x.shape
    tm = MERGE_TILE
    tok = lambda w: pl.BlockSpec((None, tm, w), lambda bi, t: (bi, t, 0))
    full = lambda a: pl.BlockSpec(a.shape, lambda bi, t: (0,) * a.ndim)
    mem = pl.BlockSpec((None,) + mk.shape[1:], lambda bi, t: (bi, 0, 0))
    flat = [a for pair in oa_parts for a in pair]
    return pl.pallas_call(
        _merge_kernel,
        grid=(b, s // tm),
        in_specs=[tok(d)] + [tok(WIDTH_A)] * 6 + [tok(WIDTH_B), mem, mem,
                  full(wc), full(wg), full(wa), full(wb), full(wcc), full(wo), full(ln_g), full(ln_b)],
        out_specs=tok(d),
        out_shape=jax.ShapeDtypeStruct((b, s, d), F32),
        scratch_shapes=[pltpu.VMEM((tm, WIDTH_C), BF16)],
        compiler_params=_cparams(("arbitrary", "arbitrary")),
        name="merge",
    )(x, *flat, ob, mk, mv, wc, wg, wa, wb, wcc, wo, ln_g, ln_b)


ROUTER_TILE = 512
_ROUTER_ROWS = 8 + N_EXPERTS


def _router_kernel(h_ref, wr_ref, br_ref, eid_ref, wt_ref, rank_ref, cnt_ref, cnt_sc):
    tm = h_ref.shape[0]

    @pl.when(pl.program_id(0) == 0)
    def _():
        cnt_sc[...] = jnp.zeros_like(cnt_sc)

    logits = lax.dot_general(wr_ref[...], h_ref[...], _NT, preferred_element_type=F32,
                             precision=lax.Precision.HIGHEST) + br_ref[:, 0:1]
    g = [logits[i:i + 1, :] for i in range(N_EXPERT_GROUPS)]
    gmax = functools.reduce(jnp.maximum, g)
    gsel = jnp.full_like(gmax, N_EXPERT_GROUPS - 1).astype(jnp.int32)
    for i in range(N_EXPERT_GROUPS - 2, -1, -1):
        gsel = jnp.where(g[i] == gmax, i, gsel)
    gprob = 1.0 / functools.reduce(lambda a, c: a + c, [jnp.exp(gi - gmax) for gi in g])

    esel = logits[8 + (N_EXPERT_GROUPS - 1) * EXPERTS_PER_GROUP:8 + N_EXPERTS, :]
    for i in range(N_EXPERT_GROUPS - 2, -1, -1):
        esel = jnp.where(gsel == i, logits[8 + i * EXPERTS_PER_GROUP:8 + (i + 1) * EXPERTS_PER_GROUP, :], esel)
    ridx = lax.broadcasted_iota(jnp.int32, (EXPERTS_PER_GROUP, tm), 0)
    v1 = jnp.max(esel, axis=0, keepdims=True)
    i1 = jnp.min(jnp.where(esel == v1, ridx, EXPERTS_PER_GROUP), axis=0, keepdims=True)
    rest = jnp.where(ridx == i1, -jnp.inf, esel)
    v2 = jnp.max(rest, axis=0, keepdims=True)
    i2 = jnp.min(jnp.where(rest == v2, ridx, EXPERTS_PER_GROUP), axis=0, keepdims=True)
    t = jnp.exp(v2 - v1)
    w1 = gprob / (1.0 + t)
    w2 = gprob * t / (1.0 + t)
    e1 = gsel * EXPERTS_PER_GROUP + i1
    e2 = gsel * EXPERTS_PER_GROUP + i2
    eid_ref[0:1, :] = e1
    eid_ref[1:2, :] = e2
    wt_ref[0:1, :] = w1
    wt_ref[1:2, :] = w2

    erow = lax.broadcasted_iota(jnp.int32, (N_EXPERTS, tm), 0)
    oh1 = (erow == e1).astype(F32)
    oh2 = (erow == e2).astype(F32)
    both = oh1 + oh2
    before = (lax.broadcasted_iota(jnp.int32, (tm, tm), 0) < lax.broadcasted_iota(jnp.int32, (tm, tm), 1))
    excl = jnp.dot(both.astype(BF16), before.astype(BF16), preferred_element_type=F32)
    pos = cnt_sc[:, 0:1] + excl
    rank_ref[0:1, :] = jnp.sum(oh1 * pos, axis=0, keepdims=True).astype(jnp.int32)
    rank_ref[1:2, :] = jnp.sum(oh2 * pos, axis=0, keepdims=True).astype(jnp.int32)
    cnt_sc[...] = cnt_sc[...] + jnp.sum(both, axis=1, keepdims=True)
    cnt_ref[...] = cnt_sc[...].astype(jnp.int32)


def _router(h2d, wr, br):
    t = h2d.shape[0]
    tm = ROUTER_TILE
    lane = lambda dt: jax.ShapeDtypeStruct((TOP_K, t), dt)
    lspec = pl.BlockSpec((TOP_K, tm), lambda i: (0, i))
    return pl.pallas_call(
        _router_kernel,
        grid=(t // tm,),
        in_specs=[pl.BlockSpec((tm, D_MODEL), lambda i: (i, 0)),
                  pl.BlockSpec(wr.shape, lambda i: (0, 0)),
                  pl.BlockSpec(br.shape, lambda i: (0, 0))],
        out_specs=[lspec, lspec, lspec, pl.BlockSpec((N_EXPERTS, 128), lambda i: (0, 0))],
        out_shape=[lane(jnp.int32), lane(F32), lane(jnp.int32), jax.ShapeDtypeStruct((N_EXPERTS, 128), jnp.int32)],
        scratch_shapes=[pltpu.VMEM((N_EXPERTS, 128), F32)],
        compiler_params=_cparams(("arbitrary",)),
        name="router",
    )(h2d, wr, br)


def _expert_kernel(be_ref, x_ref, wg_ref, wu_ref, wd_ref, y_ref):
    x = x_ref[...]
    a = jnp.dot(x, wg_ref[...], preferred_element_type=F32)
    u = jnp.dot(x, wu_ref[...], preferred_element_type=F32)
    hmid = (jax.nn.silu(a) * u).astype(BF16)
    y_ref[...] = jnp.dot(hmid, wd_ref[...], preferred_element_type=F32)


def _experts(xbuf, block_e, wg, wu, wd):
    p, d = xbuf.shape
    eb = EXPERT_BLOCK
    grid_spec = pltpu.PrefetchScalarGridSpec(
        num_scalar_prefetch=1,
        grid=(p // eb,),
        in_specs=[pl.BlockSpec((eb, d), lambda i, be: (i, 0)),
                  pl.BlockSpec((None, d, D_EXPERT), lambda i, be: (be[i], 0, 0)),
                  pl.BlockSpec((None, d, D_EXPERT), lambda i, be: (be[i], 0, 0)),
                  pl.BlockSpec((None, D_EXPERT, d), lambda i, be: (be[i], 0, 0))],
        out_specs=pl.BlockSpec((eb, d), lambda i, be: (i, 0)),
    )
    return pl.pallas_call(
        _expert_kernel,
        grid_spec=grid_spec,
        out_shape=jax.ShapeDtypeStruct((p, d), F32),
        compiler_params=_cparams(("arbitrary",)),
        name="experts",
    )(block_e, xbuf, wg, wu, wd)


FINAL_TILE = 512


def _final_kernel(h_ref, y0_ref, y1_ref, w_ref, g_ref, b_ref, o_ref):
    w = w_ref[...]
    y = DN_ALPHA * h_ref[...] + w[:, 0:1] * y0_ref[...] + w[:, 1:2] * y1_ref[...]
    o_ref[...] = _layer_norm(y, g_ref[...], b_ref[...])


def _final(h2d, y0, y1, wt, ln_g, ln_b):
    t, d = h2d.shape
    tm = FINAL_TILE
    row = pl.BlockSpec((tm, d), lambda i: (i, 0))
    vec = pl.BlockSpec((1, d), lambda i: (0, 0))
    return pl.pallas_call(
        _final_kernel,
        grid=(t // tm,),
        in_specs=[row, row, row, pl.BlockSpec((tm, TOP_K), lambda i: (i, 0)), vec, vec],
        out_specs=row,
        out_shape=jax.ShapeDtypeStruct((t, d), F32),
        compiler_params=_cparams(("arbitrary",)),
        name="final",
    )(h2d, y0, y1, wt, ln_g, ln_b)


def kernel(x, mem, rel_bias, hgrn_lb_logits, w_in, w_mem_kv, hgrn_norm_g, w_branch_a, w_branch_b, w_branch_c, w_out, ln1_g, ln1_b, w_router_group, b_router_group, w_router_expert, b_router_expert, w_exp_gate, w_exp_up, w_exp_down, ln2_g, ln2_b):
    b, s, d = x.shape
    t = b * s
    assert d == D_MODEL and w_in.shape[0] == DEPTH == 1
    assert s % (BAND * DILATED_GROUPS[-1][1]) == 0 and s % HGRN_TILE == 0 and t % ROUTER_TILE == 0
    li = 0
    scale = HEAD_DIM ** -0.5
    x2d = x.reshape(t, d)

    w = w_in[li]
    col_scale = np.ones((COLS_A,), np.float32).reshape(N_GROUPS_A, 3, WIDTH_A)
    col_scale[:, 0] = scale
    w_a = (w[:, :COLS_A] * jnp.asarray(col_scale.reshape(-1))).astype(BF16)
    w_b = w[:, COLS_A:COLS_A + COLS_B].astype(BF16)
    w_c = (w[:, COLS_A + COLS_B:COLS_A + COLS_B + COLS_C] * scale).astype(BF16)
    w_g = w[:, COLS_A + COLS_B + COLS_C:].astype(BF16)

    pa = _matmul(x2d, w_a, BF16, 1024, COLS_A // 3, "proj_a").reshape(b, s, COLS_A)
    pb = _matmul(x2d, w_b, F32, 1024, COLS_B // 4, "proj_b").reshape(b, s, COLS_B)
    mkv = _matmul(mem.reshape(-1, d), w_mem_kv[li].astype(BF16), BF16, 256, 2 * WIDTH_C, "mem_kv")
    mkv = mkv.reshape(b, -1, 2 * WIDTH_C)
    mk, mv = mkv[..., :WIDTH_C], mkv[..., WIDTH_C:]

    oa_parts = []
    for gi, (_, dilation) in enumerate(DILATED_GROUPS):
        bias = _band_bias(rel_bias[:, gi * HEADS_A:(gi + 1) * HEADS_A], dilation)
        oa_parts.append(_band_attention(pa, bias, gi, dilation, f"band_attn_{gi}"))

    ob = _hgrn2(pb, hgrn_lb_logits.astype(F32), hgrn_norm_g[li].reshape(1, WIDTH_B))

    h = _merge(x, oa_parts, ob, mk, mv, w_c, w_g, w_branch_a[li].astype(BF16), w_branch_b[li].astype(BF16),
               w_branch_c[li].astype(BF16), w_out[li].astype(BF16), ln1_g[li].reshape(1, d), ln1_b[li].reshape(1, d))
    h2d = h.reshape(t, d)

    wr = jnp.zeros((_ROUTER_ROWS, d), F32)
    wr = wr.at[:N_EXPERT_GROUPS].set(w_router_group[li].T).at[8:].set(w_router_expert[li].T)
    br = jnp.zeros((_ROUTER_ROWS,), F32).at[:N_EXPERT_GROUPS].set(b_router_group[li]).at[8:].set(b_router_expert[li])
    br = jnp.broadcast_to(br[:, None], (_ROUTER_ROWS, 128))
    eid, wt, rank, counts = _router(h2d, wr, br)

    eb = EXPERT_BLOCK
    counts = counts[:, 0]
    pcounts = (counts + eb - 1) // eb * eb
    pends = jnp.cumsum(pcounts)
    pstarts = pends - pcounts
    n_blocks = (t * TOP_K + N_EXPERTS * (eb - 1) + eb - 1) // eb
    block_e = jnp.minimum(jnp.searchsorted(pends, jnp.arange(n_blocks) * eb, side='right'), N_EXPERTS - 1).astype(jnp.int32)
    dest = pstarts[eid] + rank

    hb = h2d.astype(BF16)
    xbuf = jnp.zeros((n_blocks * eb, d), BF16).at[dest[0]].set(hb).at[dest[1]].set(hb)
    ybuf = _experts(xbuf, block_e, w_exp_gate[li].astype(BF16), w_exp_up[li].astype(BF16), w_exp_down[li].astype(BF16))
    out = _final(h2d, ybuf[dest[0]], ybuf[dest[1]], wt.T, ln2_g[li].reshape(1, d), ln2_b[li].reshape(1, d))
    return out.reshape(b, s, d)
```

```python
import functools
import math

import numpy as np
import jax
import jax.numpy as jnp
from jax import lax
from jax.experimental import pallas as pl
from jax.experimental.pallas import tpu as pltpu

F32 = jnp.float32
BF16 = jnp.bfloat16

D_MODEL = 1024
DEPTH = 1
DILATED_GROUPS = ((128, 1), (512, 4), (2048, 16))
N_GROUPS_A = len(DILATED_GROUPS)
HEADS_A = 4
HEAD_DIM = 128
WIDTH_A = HEADS_A * HEAD_DIM
BAND = 128
NUM_BUCKETS = 32
MAX_DISTANCE = 2048
N_HEADS_B = 8
WIDTH_B = N_HEADS_B * HEAD_DIM
CHUNK_B = 64
N_HEADS_C = 4
WIDTH_C = N_HEADS_C * HEAD_DIM
COLS_A = N_GROUPS_A * 3 * WIDTH_A
COLS_B = 4 * WIDTH_B
COLS_C = WIDTH_C
COLS_GATE = 3 * D_MODEL
N_EXPERT_GROUPS = 4
EXPERTS_PER_GROUP = 8
N_EXPERTS = N_EXPERT_GROUPS * EXPERTS_PER_GROUP
TOP_K = 2
D_EXPERT = D_MODEL // 4
DN_ALPHA = (2 * DEPTH) ** 0.25
LN_EPS = 1e-5
RMS_EPS = 1e-6

NEG_BIG = -1e30
SAFE_LOG_DECAY = 80.0
VMEM_LIMIT = 56 * 1024 * 1024
EXPERT_BLOCK = 256

_NT = (((1,), (1,)), ((), ()))
_TN = (((0,), (0,)), ((), ()))


def _cparams(sem):
    return pltpu.CompilerParams(dimension_semantics=sem, vmem_limit_bytes=VMEM_LIMIT)


def _matmul_kernel(x_ref, w_ref, o_ref):
    o_ref[...] = jnp.dot(x_ref[...].astype(BF16), w_ref[...], preferred_element_type=F32).astype(o_ref.dtype)


def _matmul(x, w, out_dtype, tm, tn, name):
    t, k = x.shape
    n = w.shape[1]
    return pl.pallas_call(
        _matmul_kernel,
        grid=(n // tn, t // tm),
        in_specs=[pl.BlockSpec((tm, k), lambda j, i: (i, 0)), pl.BlockSpec((k, tn), lambda j, i: (0, j))],
        out_specs=pl.BlockSpec((tm, tn), lambda j, i: (i, j)),
        out_shape=jax.ShapeDtypeStruct((t, n), out_dtype),
        compiler_params=_cparams(("arbitrary", "arbitrary")),
        name=name,
    )(x, w)


def _t5_bucket_np(dist):
    dist = np.asarray(dist, np.int32)
    max_exact = NUM_BUCKETS // 2
    d = np.maximum(dist, 1).astype(np.float32)
    large = max_exact + (np.log(d / max_exact) / math.log(MAX_DISTANCE / max_exact) * (NUM_BUCKETS - max_exact)).astype(np.int32)
    large = np.minimum(large, NUM_BUCKETS - 1)
    return np.where(dist < max_exact, dist, large).astype(np.int32)


def _band_bias(bias_tab, dilation):
    i = np.arange(BAND)[:, None]
    j = np.arange(2 * BAND)[None, :]
    u = i + BAND - j
    in_band = (u >= 0) & (u <= BAND)
    bucket = _t5_bucket_np(np.clip(u, 0, BAND) * dilation)
    onehot = np.asarray(bucket[:, :, None] == np.arange(NUM_BUCKETS)[None, None, :], np.float32)
    bias = jnp.einsum('pqb,bh->hpq', jnp.asarray(onehot), bias_tab.astype(F32), precision=lax.Precision.HIGHEST)
    general = jnp.where(in_band[None], bias, NEG_BIG)
    first = jnp.where((in_band & (j >= BAND))[None], bias, NEG_BIG)
    return jnp.stack([first, general])


LANES = 128


def _proj_perm_kernel(*refs, r):
    n_slab = D_MODEL // LANES
    x_refs, (w_ref, o_ref, xp_ref) = refs[:n_slab], refs[n_slab:]
    tm = xp_ref.shape[0]
    n = tm // r
    for j, x_ref in enumerate(x_refs):
        for c in range(r):
            rows = x_ref[pl.ds(c, n, stride=r), :] if r > 1 else x_ref[...]
            xp_ref[c * n:(c + 1) * n, j * LANES:(j + 1) * LANES] = rows.astype(BF16)
    res = jnp.dot(xp_ref[...], w_ref[...], preferred_element_type=F32).astype(o_ref.dtype)
    for c in range(r):
        o_ref[c] = res[c * n:(c + 1) * n]


PROJ_A_TILE = 1024


def _proj_classes(x, w, r, name):
    b, s, d = x.shape
    n = w.shape[1]
    tm = PROJ_A_TILE
    n_slab = d // LANES
    slabs = [pl.BlockSpec((None, tm, LANES), lambda bi, t, j=j: (bi, t, j)) for j in range(n_slab)]
    return pl.pallas_call(
        functools.partial(_proj_perm_kernel, r=r),
        grid=(b, s // tm),
        in_specs=slabs + [pl.BlockSpec((d, n), lambda bi, t: (0, 0))],
        out_specs=pl.BlockSpec((None, r, tm // r, n), lambda bi, t: (bi, 0, t, 0)),
        out_shape=jax.ShapeDtypeStruct((b, r, s // r, n), BF16),
        scratch_shapes=[pltpu.VMEM((tm, d), BF16)],
        compiler_params=_cparams(("arbitrary", "arbitrary")),
        name=name,
    )(*([x] * n_slab), w)


def _band_attn_kernel(q_ref, kp_ref, kc_ref, vp_ref, vc_ref, bias_ref, o_ref, lse_ref, *, r):
    rows = pl.ds(pl.program_id(2), BAND, stride=r) if r > 1 else slice(None)
    for h in range(HEADS_A):
        sl = slice(h * HEAD_DIM, (h + 1) * HEAD_DIM)
        q = q_ref[:, sl]
        sp = lax.dot_general(q, kp_ref[:, sl], _NT, preferred_element_type=F32) + bias_ref[h, :, :BAND]
        sc = lax.dot_general(q, kc_ref[:, sl], _NT, preferred_element_type=F32) + bias_ref[h, :, BAND:]
        m = jnp.maximum(jnp.max(sp, axis=-1, keepdims=True), jnp.max(sc, axis=-1, keepdims=True))
        pp = jnp.exp(sp - m)
        pc = jnp.exp(sc - m)
        l = jnp.sum(pp, axis=-1, keepdims=True) + jnp.sum(pc, axis=-1, keepdims=True)
        o = jnp.dot(pp.astype(BF16), vp_ref[:, sl], preferred_element_type=F32)
        o = o + jnp.dot(pc.astype(BF16), vc_ref[:, sl], preferred_element_type=F32)
        o_ref[h, rows, :] = o / l
        lse_ref[h, rows, :] = jnp.broadcast_to(m + jnp.log(l), (BAND, HEAD_DIM))


def _band_attention(pg, bias, name):
    b, r, l, _ = pg.shape
    blk = (None, None, BAND, WIDTH_A)

    def cur(which):
        return pl.BlockSpec(blk, lambda bi, n, c: (bi, c, n, which))

    def prev(which):
        return pl.BlockSpec(blk, lambda bi, n, c: (bi, c, jnp.maximum(n - 1, 0), which))

    out_spec = pl.BlockSpec((None, HEADS_A, BAND * r, HEAD_DIM), lambda bi, n, c: (bi, 0, n, 0))
    return pl.pallas_call(
        functools.partial(_band_attn_kernel, r=r),
        grid=(b, l // BAND, r),
        in_specs=[cur(0), prev(1), cur(1), prev(2), cur(2),
                  pl.BlockSpec((None, HEADS_A, BAND, 2 * BAND), lambda bi, n, c: (jnp.minimum(n, 1), 0, 0, 0))],
        out_specs=[out_spec, out_spec],
        out_shape=[jax.ShapeDtypeStruct((b, HEADS_A, l * r, HEAD_DIM), F32)] * 2,
        compiler_params=_cparams(("arbitrary", "arbitrary", "arbitrary")),
        name=name,
    )(pg, pg, pg, pg, pg, bias)


HGRN_TILE = 256
_N_CHUNK = HGRN_TILE // CHUNK_B


def _dynamic_row(ref, r, ls):
    blk = ref[pl.ds(pl.multiple_of((r // 8) * 8, 8), 8), ls]
    sub = lax.broadcasted_iota(jnp.int32, blk.shape, 0)
    return jnp.sum(jnp.where(sub == r % 8, blk, 0.0), axis=0, keepdims=True)


def _hgrn_kernel(q_ref, f_ref, i_ref, g_ref, lbl_ref, ng_ref, o_ref,
                 st_ref, qs_ref, ks_ref, kh_ref, vs_ref, bs_ref, ebl_ref, a_ref, os_ref):
    @pl.when(pl.program_id(1) == 0)
    def _():
        st_ref[...] = jnp.zeros_like(st_ref)

    lg = lbl_ref[...]
    e = jnp.exp(lg - jnp.max(lg, axis=0, keepdims=True))
    lb = e[0:1, :] / jnp.sum(e, axis=0, keepdims=True)

    row = lax.broadcasted_iota(jnp.int32, (CHUNK_B, CHUNK_B), 0)
    colm = lax.broadcasted_iota(jnp.int32, (CHUNK_B, CHUNK_B), 1)
    causal = row >= colm
    tri = causal.astype(F32)

    f = lb + (1.0 - lb) * jax.nn.sigmoid(f_ref[...])
    logf = jnp.log(f)
    kf = 1.0 - f
    qsil = jax.nn.silu(q_ref[...])
    min_b = None
    for c in range(_N_CHUNK):
        rs = slice(c * CHUNK_B, (c + 1) * CHUNK_B)
        b = jnp.dot(tri, logf[rs], preferred_element_type=F32, precision=lax.Precision.HIGHEST)
        b_last = b[CHUNK_B - 1:CHUNK_B, :]
        bs_ref[rs, :] = b
        ebl_ref[c:c + 1, :] = jnp.exp(b_last)
        qs_ref[rs, :] = (qsil[rs] * jnp.exp(b)).astype(BF16)
        ks_ref[rs, :] = (kf[rs] * jnp.exp(-b)).astype(BF16)
        kh_ref[rs, :] = (kf[rs] * jnp.exp(b_last - b)).astype(BF16)
        cm = jnp.min(b_last)
        min_b = cm if min_b is None else jnp.minimum(min_b, cm)
    vs_ref[...] = i_ref[...].astype(BF16)
    safe = min_b > -SAFE_LOG_DECAY

    @pl.when(safe)
    def _():
        for c in range(_N_CHUNK):
            rs = slice(c * CHUNK_B, (c + 1) * CHUNK_B)
            for h in range(N_HEADS_B):
                ls = slice(h * HEAD_DIM, (h + 1) * HEAD_DIM)
                a = lax.dot_general(qs_ref[rs, ls], ks_ref[rs, ls], _NT, preferred_element_type=F32)
                a_ref[c * N_HEADS_B + h] = jnp.where(causal, a, 0.0)

    @pl.when(jnp.logical_not(safe))
    def _():
        lane = lax.broadcasted_iota(jnp.int32, (1, CHUNK_B), 1)
        trow = lax.broadcasted_iota(jnp.int32, (CHUNK_B, 1), 0)
        for c in range(_N_CHUNK):
            rs = slice(c * CHUNK_B, (c + 1) * CHUNK_B)
            for h in range(N_HEADS_B):
                ls = slice(h * HEAD_DIM, (h + 1) * HEAD_DIM)
                bq = bs_ref[rs, ls]
                qh = jax.nn.silu(q_ref[rs, ls])

                def body(s, acc, c=c, ls=ls, bq=bq, qh=qh):
                    b_s = _dynamic_row(bs_ref, c * CHUNK_B + s, ls)
                    f_s = lb[:, ls] + (1.0 - lb[:, ls]) * jax.nn.sigmoid(_dynamic_row(f_ref, c * CHUNK_B + s, ls))
                    w = jnp.exp(jnp.minimum(bq - b_s, 0.0))
                    colv = jnp.sum(qh * (1.0 - f_s) * w, axis=-1, keepdims=True)
                    colv = jnp.where(trow >= s, colv, 0.0)
                    return acc + colv * (lane == s).astype(F32)

                a_ref[c * N_HEADS_B + h] = lax.fori_loop(0, CHUNK_B, body, jnp.zeros((CHUNK_B, CHUNK_B), F32))

    for c in range(_N_CHUNK):
        rs = slice(c * CHUNK_B, (c + 1) * CHUNK_B)
        for h in range(N_HEADS_B):
            ls = slice(h * HEAD_DIM, (h + 1) * HEAD_DIM)
            st = st_ref[h]
            vh = vs_ref[rs, ls]
            o = lax.dot_general(qs_ref[rs, ls], st.astype(BF16), _NT, preferred_element_type=F32)
            o = o + jnp.dot(a_ref[c * N_HEADS_B + h].astype(BF16), vh, preferred_element_type=F32)
            os_ref[rs, ls] = o
            st_ref[h] = st * ebl_ref[c:c + 1, ls] + lax.dot_general(vh, kh_ref[rs, ls], _TN, preferred_element_type=F32)

    ng = ng_ref[...]
    for h in range(N_HEADS_B):
        ls = slice(h * HEAD_DIM, (h + 1) * HEAD_DIM)
        o = os_ref[:, ls]
        o = o * lax.rsqrt(jnp.mean(o * o, axis=-1, keepdims=True) + RMS_EPS)
        o_ref[:, ls] = (o * ng[:, ls] * jax.nn.silu(g_ref[:, ls])).astype(o_ref.dtype)


def _hgrn2(pb, lb_logits, norm_g):
    b, s, _ = pb.shape
    tb = HGRN_TILE
    blk = (None, tb, WIDTH_B)

    def part(p):
        return pl.BlockSpec(blk, lambda bi, t: (bi, t, p))

    n_slots = lb_logits.shape[0]
    return pl.pallas_call(
        _hgrn_kernel,
        grid=(b, s // tb),
        in_specs=[part(0), part(1), part(2), part(3),
                  pl.BlockSpec((n_slots, WIDTH_B), lambda bi, t: (0, 0)),
                  pl.BlockSpec((1, WIDTH_B), lambda bi, t: (0, 0))],
        out_specs=pl.BlockSpec(blk, lambda bi, t: (bi, t, 0)),
        out_shape=jax.ShapeDtypeStruct((b, s, WIDTH_B), BF16),
        scratch_shapes=[
            pltpu.VMEM((N_HEADS_B, HEAD_DIM, HEAD_DIM), F32),
            pltpu.VMEM((tb, WIDTH_B), BF16),
            pltpu.VMEM((tb, WIDTH_B), BF16),
            pltpu.VMEM((tb, WIDTH_B), BF16),
            pltpu.VMEM((tb, WIDTH_B), BF16),
            pltpu.VMEM((tb, WIDTH_B), F32),
            pltpu.VMEM((8, WIDTH_B), F32),
            pltpu.VMEM((_N_CHUNK * N_HEADS_B, CHUNK_B, CHUNK_B), F32),
            pltpu.VMEM((tb, WIDTH_B), F32),
        ],
        compiler_params=_cparams(("arbitrary", "arbitrary")),
        name="hgrn2",
    )(pb, pb, pb, pb, lb_logits, norm_g)


MERGE_TILE = 256


def _layer_norm(y, g, b):
    mu = jnp.mean(y, axis=-1, keepdims=True)
    d = y - mu
    var = jnp.mean(d * d, axis=-1, keepdims=True)
    return d * lax.rsqrt(var + LN_EPS) * g + b


def _merge_kernel(x_ref, o1_ref, l1_ref, o2_ref, l2_ref, o3_ref, l3_ref, ob_ref, mk_ref, mv_ref,
                  wc_ref, wg_ref, wa_ref, wb_ref, wcc_ref, wo_ref, g_ref, b_ref, h_ref, oc_ref):
    x = x_ref[...]
    xb = x.astype(BF16)

    qc = jnp.dot(xb, wc_ref[...], preferred_element_type=F32).astype(BF16)
    for h in range(N_HEADS_C):
        ls = slice(h * HEAD_DIM, (h + 1) * HEAD_DIM)
        s = lax.dot_general(qc[:, ls], mk_ref[:, ls], _NT, preferred_element_type=F32)
        p = jnp.exp(s - jnp.max(s, axis=-1, keepdims=True))
        l = jnp.sum(p, axis=-1, keepdims=True)
        oc_ref[:, ls] = (jnp.dot(p.astype(BF16), mv_ref[:, ls], preferred_element_type=F32) / l).astype(BF16)

    oa_heads = []
    for h in range(HEADS_A):
        l1, l2, l3 = l1_ref[h], l2_ref[h], l3_ref[h]
        m = jnp.maximum(jnp.maximum(l1, l2), l3)
        e1, e2, e3 = jnp.exp(l1 - m), jnp.exp(l2 - m), jnp.exp(l3 - m)
        oa_heads.append(((e1 * o1_ref[h] + e2 * o2_ref[h] + e3 * o3_ref[h]) / (e1 + e2 + e3)).astype(BF16))
    oa = jnp.concatenate(oa_heads, axis=-1)

    gates = jax.nn.sigmoid(jnp.dot(xb, wg_ref[...], preferred_element_type=F32))
    merged = gates[:, :D_MODEL] * jnp.dot(oa, wa_ref[...], preferred_element_type=F32)
    merged = merged + gates[:, D_MODEL:2 * D_MODEL] * jnp.dot(ob_ref[...], wb_ref[...], preferred_element_type=F32)
    merged = merged + gates[:, 2 * D_MODEL:] * jnp.dot(oc_ref[...], wcc_ref[...], preferred_element_type=F32)
    y = DN_ALPHA * x + jnp.dot(merged.astype(BF16), wo_ref[...], preferred_element_type=F32)
    h_ref[...] = _layer_norm(y, g_ref[...], b_ref[...])


def _merge(x, oa_parts, ob, mk, mv, wc, wg, wa, wb, wcc, wo, ln_g, ln_b):
    b, s, d = x.shape
    tm = MERGE_TILE
    tok = lambda w: pl.BlockSpec((None, tm, w), lambda bi, t: (bi, t, 0))
    full = lambda a: pl.BlockSpec(a.shape, lambda bi, t: (0,) * a.ndim)
    mem = pl.BlockSpec((None,) + mk.shape[1:], lambda bi, t: (bi, 0, 0))
    flat = [a for pair in oa_parts for a in pair]
    return pl.pallas_call(
        _merge_kernel,
        grid=(b, s // tm),
        in_specs=[tok(d)] + [pl.BlockSpec((None, HEADS_A, tm, HEAD_DIM), lambda bi, t: (bi, 0, t, 0))] * 6 + [tok(WIDTH_B), mem, mem,
                  full(wc), full(wg), full(wa), full(wb), full(wcc), full(wo), full(ln_g), full(ln_b)],
        out_specs=tok(d),
        out_shape=jax.ShapeDtypeStruct((b, s, d), F32),
        scratch_shapes=[pltpu.VMEM((tm, WIDTH_C), BF16)],
        compiler_params=_cparams(("arbitrary", "arbitrary")),
        name="merge",
    )(x, *flat, ob, mk, mv, wc, wg, wa, wb, wcc, wo, ln_g, ln_b)


ROUTER_TILE = 512
_ROUTER_ROWS = 8 + N_EXPERTS


def _router_kernel(h_ref, wr_ref, br_ref, eid_ref, wt_ref, rank_ref, cnt_ref, cnt_sc):
    tm = h_ref.shape[0]

    @pl.when(pl.program_id(0) == 0)
    def _():
        cnt_sc[...] = jnp.zeros_like(cnt_sc)

    logits = lax.dot_general(wr_ref[...], h_ref[...], _NT, preferred_element_type=F32,
                             precision=lax.Precision.HIGHEST) + br_ref[:, 0:1]
    g = [logits[i:i + 1, :] for i in range(N_EXPERT_GROUPS)]
    gmax = functools.reduce(jnp.maximum, g)
    gsel = jnp.full_like(gmax, N_EXPERT_GROUPS - 1).astype(jnp.int32)
    for i in range(N_EXPERT_GROUPS - 2, -1, -1):
        gsel = jnp.where(g[i] == gmax, i, gsel)
    gprob = 1.0 / functools.reduce(lambda a, c: a + c, [jnp.exp(gi - gmax) for gi in g])

    esel = logits[8 + (N_EXPERT_GROUPS - 1) * EXPERTS_PER_GROUP:8 + N_EXPERTS, :]
    for i in range(N_EXPERT_GROUPS - 2, -1, -1):
        esel = jnp.where(gsel == i, logits[8 + i * EXPERTS_PER_GROUP:8 + (i + 1) * EXPERTS_PER_GROUP, :], esel)
    ridx = lax.broadcasted_iota(jnp.int32, (EXPERTS_PER_GROUP, tm), 0)
    v1 = jnp.max(esel, axis=0, keepdims=True)
    i1 = jnp.min(jnp.where(esel == v1, ridx, EXPERTS_PER_GROUP), axis=0, keepdims=True)
    rest = jnp.where(ridx == i1, -jnp.inf, esel)
    v2 = jnp.max(rest, axis=0, keepdims=True)
    i2 = jnp.min(jnp.where(rest == v2, ridx, EXPERTS_PER_GROUP), axis=0, keepdims=True)
    t = jnp.exp(v2 - v1)
    w1 = gprob / (1.0 + t)
    w2 = gprob * t / (1.0 + t)
    e1 = gsel * EXPERTS_PER_GROUP + i1
    e2 = gsel * EXPERTS_PER_GROUP + i2
    eid_ref[0:1, :] = e1
    eid_ref[1:2, :] = e2
    wt_ref[0:1, :] = w1
    wt_ref[1:2, :] = w2

    erow = lax.broadcasted_iota(jnp.int32, (N_EXPERTS, tm), 0)
    oh1 = (erow == e1).astype(F32)
    oh2 = (erow == e2).astype(F32)
    both = oh1 + oh2
    before = (lax.broadcasted_iota(jnp.int32, (tm, tm), 0) < lax.broadcasted_iota(jnp.int32, (tm, tm), 1))
    excl = jnp.dot(both.astype(BF16), before.astype(BF16), preferred_element_type=F32)
    pos = cnt_sc[:, 0:1] + excl
    rank_ref[0:1, :] = jnp.sum(oh1 * pos, axis=0, keepdims=True).astype(jnp.int32)
    rank_ref[1:2, :] = jnp.sum(oh2 * pos, axis=0, keepdims=True).astype(jnp.int32)
    cnt_sc[...] = cnt_sc[...] + jnp.sum(both, axis=1, keepdims=True)
    cnt_ref[...] = cnt_sc[...].astype(jnp.int32)


def _router(h2d, wr, br):
    t = h2d.shape[0]
    tm = ROUTER_TILE
    lane = lambda dt: jax.ShapeDtypeStruct((TOP_K, t), dt)
    lspec = pl.BlockSpec((TOP_K, tm), lambda i: (0, i))
    return pl.pallas_call(
        _router_kernel,
        grid=(t // tm,),
        in_specs=[pl.BlockSpec((tm, D_MODEL), lambda i: (i, 0)),
                  pl.BlockSpec(wr.shape, lambda i: (0, 0)),
                  pl.BlockSpec(br.shape, lambda i: (0, 0))],
        out_specs=[lspec, lspec, lspec, pl.BlockSpec((N_EXPERTS, 128), lambda i: (0, 0))],
        out_shape=[lane(jnp.int32), lane(F32), lane(jnp.int32), jax.ShapeDtypeStruct((N_EXPERTS, 128), jnp.int32)],
        scratch_shapes=[pltpu.VMEM((N_EXPERTS, 128), F32)],
        compiler_params=_cparams(("arbitrary",)),
        name="router",
    )(h2d, wr, br)


DISPATCH_TILE = 512


def _row_copy(src_ref, src_row, dst_ref, dst_row, sem):
    return pltpu.make_async_copy(src_ref.at[pl.ds(src_row, 1)], dst_ref.at[pl.ds(dst_row, 1)], sem)


def _dispatch_kernel(dest_ref, h_ref, xz_ref, xbuf_ref, sem):
    del xz_ref
    tm = h_ref.shape[0]

    def issue(i, carry):
        for k in range(TOP_K):
            _row_copy(h_ref, i, xbuf_ref, dest_ref[k, i], sem).start()
        return carry

    def drain(i, carry):
        for k in range(TOP_K):
            _row_copy(h_ref, i, xbuf_ref, dest_ref[k, i], sem).wait()
        return carry

    lax.fori_loop(0, tm, issue, 0)
    lax.fori_loop(0, tm, drain, 0)


def _dispatch(h2d, dest, n_rows):
    t, d = h2d.shape
    tm = DISPATCH_TILE
    return pl.pallas_call(
        _dispatch_kernel,
        grid=(t // tm,),
        in_specs=[pl.BlockSpec((TOP_K, tm), lambda i: (0, i), memory_space=pltpu.SMEM),
                  pl.BlockSpec((tm, d), lambda i: (i, 0)),
                  pl.BlockSpec(memory_space=pl.ANY)],
        out_specs=pl.BlockSpec(memory_space=pl.ANY),
        out_shape=jax.ShapeDtypeStruct((n_rows, d), F32),
        scratch_shapes=[pltpu.SemaphoreType.DMA(())],
        input_output_aliases={2: 0},
        compiler_params=_cparams(("arbitrary",)),
        name="dispatch",
    )(dest, h2d, jnp.zeros((n_rows, d), F32))


def _expert_kernel(be_ref, x_ref, wg_ref, wu_ref, wd_ref, y_ref):
    x = x_ref[...].astype(BF16)
    a = jnp.dot(x, wg_ref[...], preferred_element_type=F32)
    u = jnp.dot(x, wu_ref[...], preferred_element_type=F32)
    hmid = (jax.nn.silu(a) * u).astype(BF16)
    y_ref[...] = jnp.dot(hmid, wd_ref[...], preferred_element_type=F32)


def _experts(xbuf, block_e, wg, wu, wd):
    p, d = xbuf.shape
    eb = EXPERT_BLOCK
    grid_spec = pltpu.PrefetchScalarGridSpec(
        num_scalar_prefetch=1,
        grid=(p // eb,),
        in_specs=[pl.BlockSpec((eb, d), lambda i, be: (i, 0)),
                  pl.BlockSpec((None, d, D_EXPERT), lambda i, be: (be[i], 0, 0)),
                  pl.BlockSpec((None, d, D_EXPERT), lambda i, be: (be[i], 0, 0)),
                  pl.BlockSpec((None, D_EXPERT, d), lambda i, be: (be[i], 0, 0))],
        out_specs=pl.BlockSpec((eb, d), lambda i, be: (i, 0)),
    )
    return pl.pallas_call(
        _expert_kernel,
        grid_spec=grid_spec,
        out_shape=jax.ShapeDtypeStruct((p, d), F32),
        compiler_params=_cparams(("arbitrary",)),
        name="experts",
    )(block_e, xbuf, wg, wu, wd)


FINAL_TILE = 512


def _final_kernel(dest_ref, h_ref, w_ref, g_ref, b_ref, ybuf_ref, o_ref, ys_ref, sem):
    tm = h_ref.shape[0]

    def issue(i, carry):
        for k in range(TOP_K):
            _row_copy(ybuf_ref, dest_ref[k, i], ys_ref.at[k], i, sem).start()
        return carry

    def drain(i, carry):
        for k in range(TOP_K):
            _row_copy(ybuf_ref, dest_ref[k, i], ys_ref.at[k], i, sem).wait()
        return carry

    lax.fori_loop(0, tm, issue, 0)
    lax.fori_loop(0, tm, drain, 0)
    w = w_ref[...]
    y = DN_ALPHA * h_ref[...] + w[:, 0:1] * ys_ref[0] + w[:, 1:2] * ys_ref[1]
    o_ref[...] = _layer_norm(y, g_ref[...], b_ref[...])


def _final(h2d, ybuf, dest, wt, ln_g, ln_b):
    t, d = h2d.shape
    tm = FINAL_TILE
    row = pl.BlockSpec((tm, d), lambda i: (i, 0))
    vec = pl.BlockSpec((1, d), lambda i: (0, 0))
    return pl.pallas_call(
        _final_kernel,
        grid=(t // tm,),
        in_specs=[pl.BlockSpec((TOP_K, tm), lambda i: (0, i), memory_space=pltpu.SMEM),
                  row, pl.BlockSpec((tm, TOP_K), lambda i: (i, 0)), vec, vec,
                  pl.BlockSpec(memory_space=pl.ANY)],
        out_specs=row,
        out_shape=jax.ShapeDtypeStruct((t, d), F32),
        scratch_shapes=[pltpu.VMEM((TOP_K, tm, d), F32), pltpu.SemaphoreType.DMA(())],
        compiler_params=_cparams(("arbitrary",)),
        name="final",
    )(dest, h2d, wt, ln_g, ln_b, ybuf)


def kernel(x, mem, rel_bias, hgrn_lb_logits, w_in, w_mem_kv, hgrn_norm_g, w_branch_a, w_branch_b, w_branch_c, w_out, ln1_g, ln1_b, w_router_group, b_router_group, w_router_expert, b_router_expert, w_exp_gate, w_exp_up, w_exp_down, ln2_g, ln2_b):
    b, s, d = x.shape
    t = b * s
    assert d == D_MODEL and w_in.shape[0] == DEPTH == 1
    assert s % (BAND * DILATED_GROUPS[-1][1]) == 0 and s % HGRN_TILE == 0 and t % ROUTER_TILE == 0
    li = 0
    scale = HEAD_DIM ** -0.5
    x2d = x.reshape(t, d)

    w = w_in[li]
    col_scale = np.ones((COLS_A,), np.float32).reshape(N_GROUPS_A, 3, WIDTH_A)
    col_scale[:, 0] = scale
    w_a = (w[:, :COLS_A] * jnp.asarray(col_scale.reshape(-1))).astype(BF16)
    w_b = w[:, COLS_A:COLS_A + COLS_B].astype(BF16)
    w_c = (w[:, COLS_A + COLS_B:COLS_A + COLS_B + COLS_C] * scale).astype(BF16)
    w_g = w[:, COLS_A + COLS_B + COLS_C:].astype(BF16)

    pb = _matmul(x2d, w_b, F32, 1024, COLS_B // 4, "proj_b").reshape(b, s, COLS_B)
    mkv = _matmul(mem.reshape(-1, d), w_mem_kv[li].astype(BF16), BF16, 256, 2 * WIDTH_C, "mem_kv")
    mkv = mkv.reshape(b, -1, 2 * WIDTH_C)
    mk, mv = mkv[..., :WIDTH_C], mkv[..., WIDTH_C:]

    oa_parts = []
    for gi, (_, dilation) in enumerate(DILATED_GROUPS):
        bias = _band_bias(rel_bias[:, gi * HEADS_A:(gi + 1) * HEADS_A], dilation)
        pg = _proj_classes(x, w_a[:, gi * 3 * WIDTH_A:(gi + 1) * 3 * WIDTH_A], dilation, f"proj_a_{gi}")
        oa_parts.append(_band_attention(pg, bias, f"band_attn_{gi}"))

    ob = _hgrn2(pb, hgrn_lb_logits.astype(F32), hgrn_norm_g[li].reshape(1, WIDTH_B))

    h = _merge(x, oa_parts, ob, mk, mv, w_c, w_g, w_branch_a[li].astype(BF16), w_branch_b[li].astype(BF16),
               w_branch_c[li].astype(BF16), w_out[li].astype(BF16), ln1_g[li].reshape(1, d), ln1_b[li].reshape(1, d))
    h2d = h.reshape(t, d)

    wr = jnp.zeros((_ROUTER_ROWS, d), F32)
    wr = wr.at[:N_EXPERT_GROUPS].set(w_router_group[li].T).at[8:].set(w_router_expert[li].T)
    br = jnp.zeros((_ROUTER_ROWS,), F32).at[:N_EXPERT_GROUPS].set(b_router_group[li]).at[8:].set(b_router_expert[li])
    br = jnp.broadcast_to(br[:, None], (_ROUTER_ROWS, 128))
    eid, wt, rank, counts = _router(h2d, wr, br)

    eb = EXPERT_BLOCK
    counts = counts[:, 0]
    pcounts = (counts + eb - 1) // eb * eb
    pends = jnp.cumsum(pcounts)
    pstarts = pends - pcounts
    n_blocks = (t * TOP_K + N_EXPERTS * (eb - 1) + eb - 1) // eb
    block_start = jnp.arange(n_blocks, dtype=jnp.int32) * eb
    block_e = jnp.minimum(jnp.sum(pends[None, :] <= block_start[:, None], axis=1), N_EXPERTS - 1).astype(jnp.int32)
    is_e = eid[None] == jnp.arange(N_EXPERTS, dtype=jnp.int32)[:, None, None]
    dest = rank + jnp.sum(jnp.where(is_e, pstarts[:, None, None], 0), axis=0)

    xbuf = _dispatch(h2d, dest, n_blocks * eb)
    ybuf = _experts(xbuf, block_e, w_exp_gate[li].astype(BF16), w_exp_up[li].astype(BF16), w_exp_down[li].astype(BF16))
    out = _final(h2d, ybuf, dest, wt.T, ln2_g[li].reshape(1, d), ln2_b[li].reshape(1, d))
    return out.reshape(b, s, d)
```

```python
import functools
import math

import numpy as np
import jax
import jax.numpy as jnp
from jax import lax
from jax.experimental import pallas as pl
from jax.experimental.pallas import tpu as pltpu

F32 = jnp.float32
BF16 = jnp.bfloat16

D_MODEL = 1024
DEPTH = 1
DILATED_GROUPS = ((128, 1), (512, 4), (2048, 16))
N_GROUPS_A = len(DILATED_GROUPS)
HEADS_A = 4
HEAD_DIM = 128
WIDTH_A = HEADS_A * HEAD_DIM
BAND = 128
NUM_BUCKETS = 32
MAX_DISTANCE = 2048
N_HEADS_B = 8
WIDTH_B = N_HEADS_B * HEAD_DIM
CHUNK_B = 64
N_HEADS_C = 4
WIDTH_C = N_HEADS_C * HEAD_DIM
COLS_A = N_GROUPS_A * 3 * WIDTH_A
COLS_B = 4 * WIDTH_B
COLS_C = WIDTH_C
COLS_GATE = 3 * D_MODEL
N_EXPERT_GROUPS = 4
EXPERTS_PER_GROUP = 8
N_EXPERTS = N_EXPERT_GROUPS * EXPERTS_PER_GROUP
TOP_K = 2
D_EXPERT = D_MODEL // 4
DN_ALPHA = (2 * DEPTH) ** 0.25
LN_EPS = 1e-5
RMS_EPS = 1e-6

NEG_BIG = -1e30
SAFE_LOG_DECAY = 80.0
VMEM_LIMIT = 56 * 1024 * 1024
EXPERT_BLOCK = 512

_NT = (((1,), (1,)), ((), ()))
_TN = (((0,), (0,)), ((), ()))


def _cparams(sem):
    return pltpu.CompilerParams(dimension_semantics=sem, vmem_limit_bytes=VMEM_LIMIT)


def _matmul_kernel(x_ref, w_ref, o_ref):
    o_ref[...] = jnp.dot(x_ref[...].astype(BF16), w_ref[...], preferred_element_type=F32).astype(o_ref.dtype)


def _matmul(x, w, out_dtype, tm, tn, name):
    t, k = x.shape
    n = w.shape[1]
    return pl.pallas_call(
        _matmul_kernel,
        grid=(n // tn, t // tm),
        in_specs=[pl.BlockSpec((tm, k), lambda j, i: (i, 0)), pl.BlockSpec((k, tn), lambda j, i: (0, j))],
        out_specs=pl.BlockSpec((tm, tn), lambda j, i: (i, j)),
        out_shape=jax.ShapeDtypeStruct((t, n), out_dtype),
        compiler_params=_cparams(("arbitrary", "arbitrary")),
        name=name,
    )(x, w)


def _t5_bucket_np(dist):
    dist = np.asarray(dist, np.int32)
    max_exact = NUM_BUCKETS // 2
    d = np.maximum(dist, 1).astype(np.float32)
    large = max_exact + (np.log(d / max_exact) / math.log(MAX_DISTANCE / max_exact) * (NUM_BUCKETS - max_exact)).astype(np.int32)
    large = np.minimum(large, NUM_BUCKETS - 1)
    return np.where(dist < max_exact, dist, large).astype(np.int32)


def _band_bias(bias_tab, dilation):
    i = np.arange(BAND)[:, None]
    j = np.arange(2 * BAND)[None, :]
    u = i + BAND - j
    in_band = (u >= 0) & (u <= BAND)
    bucket = _t5_bucket_np(np.clip(u, 0, BAND) * dilation)
    onehot = np.asarray(bucket[:, :, None] == np.arange(NUM_BUCKETS)[None, None, :], np.float32)
    bias = jnp.einsum('pqb,bh->hpq', jnp.asarray(onehot), bias_tab.astype(F32), precision=lax.Precision.HIGHEST)
    general = jnp.where(in_band[None], bias, NEG_BIG)
    first = jnp.where((in_band & (j >= BAND))[None], bias, NEG_BIG)
    return jnp.stack([first, general])


LANES = 128


def _proj_perm_kernel(*refs, r):
    n_slab = D_MODEL // LANES
    x_refs, (w_ref, o_ref, xp_ref) = refs[:n_slab], refs[n_slab:]
    tm = xp_ref.shape[0]
    n = tm // r
    for j, x_ref in enumerate(x_refs):
        for c in range(r):
            rows = x_ref[pl.ds(c, n, stride=r), :] if r > 1 else x_ref[...]
            xp_ref[c * n:(c + 1) * n, j * LANES:(j + 1) * LANES] = rows.astype(BF16)
    res = jnp.dot(xp_ref[...], w_ref[...], preferred_element_type=F32).astype(o_ref.dtype)
    for c in range(r):
        o_ref[c] = res[c * n:(c + 1) * n]


PROJ_A_TILE = 1024


def _proj_classes(x, w, r, name):
    b, s, d = x.shape
    n = w.shape[1]
    tm = PROJ_A_TILE
    n_slab = d // LANES
    slabs = [pl.BlockSpec((None, tm, LANES), lambda bi, t, j=j: (bi, t, j)) for j in range(n_slab)]
    return pl.pallas_call(
        functools.partial(_proj_perm_kernel, r=r),
        grid=(b, s // tm),
        in_specs=slabs + [pl.BlockSpec((d, n), lambda bi, t: (0, 0))],
        out_specs=pl.BlockSpec((None, r, tm // r, n), lambda bi, t: (bi, 0, t, 0)),
        out_shape=jax.ShapeDtypeStruct((b, r, s // r, n), BF16),
        scratch_shapes=[pltpu.VMEM((tm, d), BF16)],
        compiler_params=_cparams(("arbitrary", "arbitrary")),
        name=name,
    )(*([x] * n_slab), w)


BAND_SPAN = 2048


def _band_attn_kernel(q_ref, k_ref, v_ref, kp_ref, vp_ref, bias_ref, o_ref, lse_ref, *, r):
    nq = q_ref.shape[1] // BAND
    variant = jnp.minimum(pl.program_id(1), 1)
    ones = jnp.ones((BAND, HEAD_DIM), BF16)
    heads = [slice(h * HEAD_DIM, (h + 1) * HEAD_DIM) for h in range(HEADS_A)]
    units = [(c, j) for c in range(r) for j in range(nq)]

    def prev_of(k_or_v_ref, prev_ref, c, j, sl):
        return prev_ref[c, :, sl] if j == 0 else k_or_v_ref[c, (j - 1) * BAND:j * BAND, sl]

    def scores(c, j):
        cur = slice(j * BAND, (j + 1) * BAND)
        out = []
        for h, sl in enumerate(heads):
            q = q_ref[c, cur, sl]
            bias = bias_ref[variant, h] if j == 0 else bias_ref[1, h]
            sp = lax.dot_general(q, prev_of(k_ref, kp_ref, c, j, sl), _NT, preferred_element_type=F32)
            sc = lax.dot_general(q, k_ref[c, cur, sl], _NT, preferred_element_type=F32)
            out.append((sp + bias[:, :BAND], sc + bias[:, BAND:]))
        return out

    nxt = scores(*units[0])
    for u, (c, j) in enumerate(units):
        s_all = nxt
        if u + 1 < len(units):
            nxt = scores(*units[u + 1])
        cur = slice(j * BAND, (j + 1) * BAND)
        rows = pl.ds(j * BAND * r + c, BAND, stride=r) if r > 1 else cur
        probs = []
        for sp, sc in s_all:
            m = jnp.max(jnp.maximum(sp, sc), axis=-1, keepdims=True)
            probs.append((m, jnp.exp(sp - m).astype(BF16), jnp.exp(sc - m).astype(BF16)))
        for h, sl in enumerate(heads):
            m, pp, pc = probs[h]
            l = jnp.dot(pp, ones, preferred_element_type=F32) + jnp.dot(pc, ones, preferred_element_type=F32)
            o = jnp.dot(pp, prev_of(v_ref, vp_ref, c, j, sl), preferred_element_type=F32)
            o = o + jnp.dot(pc, v_ref[c, cur, sl], preferred_element_type=F32)
            o_ref[h, rows, :] = o / l
            lse_ref[h, rows, :] = m + jnp.log(l)


def _band_attention(pg, bias, name):
    b, r, l, _ = pg.shape
    nq = BAND_SPAN // (BAND * r)
    cur = lambda which: pl.BlockSpec((None, r, nq * BAND, WIDTH_A), lambda bi, n: (bi, 0, n, which))
    prev = lambda which: pl.BlockSpec((None, r, BAND, WIDTH_A), lambda bi, n: (bi, 0, jnp.maximum(n * nq - 1, 0), which))
    out_spec = pl.BlockSpec((None, HEADS_A, BAND_SPAN, HEAD_DIM), lambda bi, n: (bi, 0, n, 0))
    return pl.pallas_call(
        functools.partial(_band_attn_kernel, r=r),
        grid=(b, l * r // BAND_SPAN),
        in_specs=[cur(0), cur(1), cur(2), prev(1), prev(2), pl.BlockSpec(bias.shape, lambda bi, n: (0, 0, 0, 0))],
        out_specs=[out_spec, out_spec],
        out_shape=[jax.ShapeDtypeStruct((b, HEADS_A, l * r, HEAD_DIM), F32)] * 2,
        compiler_params=_cparams(("arbitrary", "arbitrary")),
        name=name,
    )(pg, pg, pg, pg, pg, bias)


HGRN_TILE = 256
_N_CHUNK = HGRN_TILE // CHUNK_B


def _dynamic_row(ref, r, ls):
    blk = ref[pl.ds(pl.multiple_of((r // 8) * 8, 8), 8), ls]
    sub = lax.broadcasted_iota(jnp.int32, blk.shape, 0)
    return jnp.sum(jnp.where(sub == r % 8, blk, 0.0), axis=0, keepdims=True)


def _hgrn_kernel(q_ref, f_ref, i_ref, g_ref, lbl_ref, ng_ref, o_ref,
                 st_ref, qs_ref, ks_ref, kh_ref, vs_ref, bs_ref, ebl_ref, a_ref, os_ref):
    @pl.when(pl.program_id(1) == 0)
    def _():
        st_ref[...] = jnp.zeros_like(st_ref)

    lg = lbl_ref[...]
    e = jnp.exp(lg - jnp.max(lg, axis=0, keepdims=True))
    lb = e[0:1, :] / jnp.sum(e, axis=0, keepdims=True)

    row = lax.broadcasted_iota(jnp.int32, (CHUNK_B, CHUNK_B), 0)
    colm = lax.broadcasted_iota(jnp.int32, (CHUNK_B, CHUNK_B), 1)
    causal = row >= colm
    tri = causal.astype(F32)

    f = lb + (1.0 - lb) * jax.nn.sigmoid(f_ref[...])
    logf = jnp.log(f)
    kf = 1.0 - f
    qsil = jax.nn.silu(q_ref[...])
    min_b = None
    for c in range(_N_CHUNK):
        rs = slice(c * CHUNK_B, (c + 1) * CHUNK_B)
        b = jnp.dot(tri, logf[rs], preferred_element_type=F32, precision=lax.Precision.HIGHEST)
        b_last = b[CHUNK_B - 1:CHUNK_B, :]
        bs_ref[rs, :] = b
        ebl_ref[c:c + 1, :] = jnp.exp(b_last)
        qs_ref[rs, :] = (qsil[rs] * jnp.exp(b)).astype(BF16)
        ks_ref[rs, :] = (kf[rs] * jnp.exp(-b)).astype(BF16)
        kh_ref[rs, :] = (kf[rs] * jnp.exp(b_last - b)).astype(BF16)
        cm = jnp.min(b_last)
        min_b = cm if min_b is None else jnp.minimum(min_b, cm)
    vs_ref[...] = i_ref[...].astype(BF16)
    safe = min_b > -SAFE_LOG_DECAY

    @pl.when(safe)
    def _():
        for c in range(_N_CHUNK):
            rs = slice(c * CHUNK_B, (c + 1) * CHUNK_B)
            for h in range(N_HEADS_B):
                ls = slice(h * HEAD_DIM, (h + 1) * HEAD_DIM)
                a = lax.dot_general(qs_ref[rs, ls], ks_ref[rs, ls], _NT, preferred_element_type=F32)
                a_ref[c * N_HEADS_B + h] = jnp.where(causal, a, 0.0)

    @pl.when(jnp.logical_not(safe))
    def _():
        lane = lax.broadcasted_iota(jnp.int32, (1, CHUNK_B), 1)
        trow = lax.broadcasted_iota(jnp.int32, (CHUNK_B, 1), 0)
        for c in range(_N_CHUNK):
            rs = slice(c * CHUNK_B, (c + 1) * CHUNK_B)
            for h in range(N_HEADS_B):
                ls = slice(h * HEAD_DIM, (h + 1) * HEAD_DIM)
                bq = bs_ref[rs, ls]
                qh = jax.nn.silu(q_ref[rs, ls])

                def body(s, acc, c=c, ls=ls, bq=bq, qh=qh):
                    b_s = _dynamic_row(bs_ref, c * CHUNK_B + s, ls)
                    f_s = lb[:, ls] + (1.0 - lb[:, ls]) * jax.nn.sigmoid(_dynamic_row(f_ref, c * CHUNK_B + s, ls))
                    w = jnp.exp(jnp.minimum(bq - b_s, 0.0))
                    colv = jnp.sum(qh * (1.0 - f_s) * w, axis=-1, keepdims=True)
                    colv = jnp.where(trow >= s, colv, 0.0)
                    return acc + colv * (lane == s).astype(F32)

                a_ref[c * N_HEADS_B + h] = lax.fori_loop(0, CHUNK_B, body, jnp.zeros((CHUNK_B, CHUNK_B), F32))

    for c in range(_N_CHUNK):
        rs = slice(c * CHUNK_B, (c + 1) * CHUNK_B)
        for h in range(N_HEADS_B):
            ls = slice(h * HEAD_DIM, (h + 1) * HEAD_DIM)
            st = st_ref[h]
            vh = vs_ref[rs, ls]
            o = lax.dot_general(qs_ref[rs, ls], st.astype(BF16), _NT, preferred_element_type=F32)
            o = o + jnp.dot(a_ref[c * N_HEADS_B + h].astype(BF16), vh, preferred_element_type=F32)
            os_ref[rs, ls] = o
            st_ref[h] = st * ebl_ref[c:c + 1, ls] + lax.dot_general(vh, kh_ref[rs, ls], _TN, preferred_element_type=F32)

    ng = ng_ref[...]
    for h in range(N_HEADS_B):
        ls = slice(h * HEAD_DIM, (h + 1) * HEAD_DIM)
        o = os_ref[:, ls]
        o = o * lax.rsqrt(jnp.mean(o * o, axis=-1, keepdims=True) + RMS_EPS)
        o_ref[:, ls] = (o * ng[:, ls] * jax.nn.silu(g_ref[:, ls])).astype(o_ref.dtype)


def _hgrn2(pb, lb_logits, norm_g):
    b, s, _ = pb.shape
    tb = HGRN_TILE
    blk = (None, tb, WIDTH_B)

    def part(p):
        return pl.BlockSpec(blk, lambda bi, t: (bi, t, p))

    n_slots = lb_logits.shape[0]
    return pl.pallas_call(
        _hgrn_kernel,
        grid=(b, s // tb),
        in_specs=[part(0), part(1), part(2), part(3),
                  pl.BlockSpec((n_slots, WIDTH_B), lambda bi, t: (0, 0)),
                  pl.BlockSpec((1, WIDTH_B), lambda bi, t: (0, 0))],
        out_specs=pl.BlockSpec(blk, lambda bi, t: (bi, t, 0)),
        out_shape=jax.ShapeDtypeStruct((b, s, WIDTH_B), BF16),
        scratch_shapes=[
            pltpu.VMEM((N_HEADS_B, HEAD_DIM, HEAD_DIM), F32),
            pltpu.VMEM((tb, WIDTH_B), BF16),
            pltpu.VMEM((tb, WIDTH_B), BF16),
            pltpu.VMEM((tb, WIDTH_B), BF16),
            pltpu.VMEM((tb, WIDTH_B), BF16),
            pltpu.VMEM((tb, WIDTH_B), F32),
            pltpu.VMEM((8, WIDTH_B), F32),
            pltpu.VMEM((_N_CHUNK * N_HEADS_B, CHUNK_B, CHUNK_B), F32),
            pltpu.VMEM((tb, WIDTH_B), F32),
        ],
        compiler_params=_cparams(("arbitrary", "arbitrary")),
        name="hgrn2",
    )(pb, pb, pb, pb, lb_logits, norm_g)


MERGE_TILE = 256


def _layer_norm(y, g, b):
    mu = jnp.mean(y, axis=-1, keepdims=True)
    d = y - mu
    var = jnp.mean(d * d, axis=-1, keepdims=True)
    return d * lax.rsqrt(var + LN_EPS) * g + b


def _merge_kernel(x_ref, o1_ref, l1_ref, o2_ref, l2_ref, o3_ref, l3_ref, ob_ref, mk_ref, mv_ref,
                  wc_ref, wg_ref, wa_ref, wb_ref, wcc_ref, wo_ref, g_ref, b_ref, h_ref, oc_ref):
    x = x_ref[...]
    xb = x.astype(BF16)

    qc = jnp.dot(xb, wc_ref[...], preferred_element_type=F32).astype(BF16)
    heads_c = [slice(h * HEAD_DIM, (h + 1) * HEAD_DIM) for h in range(N_HEADS_C)]
    s_all = [lax.dot_general(qc[:, ls], mk_ref[:, ls], _NT, preferred_element_type=F32) for ls in heads_c]
    p_all = [jnp.exp(s - jnp.max(s, axis=-1, keepdims=True)).astype(BF16) for s in s_all]
    ones = jnp.ones((mk_ref.shape[0], HEAD_DIM), BF16)
    for ls, p in zip(heads_c, p_all):
        l = jnp.dot(p, ones, preferred_element_type=F32)
        oc_ref[:, ls] = (jnp.dot(p, mv_ref[:, ls], preferred_element_type=F32) / l).astype(BF16)

    oa_heads = []
    for h in range(HEADS_A):
        l1, l2, l3 = l1_ref[h], l2_ref[h], l3_ref[h]
        m = jnp.maximum(jnp.maximum(l1, l2), l3)
        e1, e2, e3 = jnp.exp(l1 - m), jnp.exp(l2 - m), jnp.exp(l3 - m)
        oa_heads.append(((e1 * o1_ref[h] + e2 * o2_ref[h] + e3 * o3_ref[h]) / (e1 + e2 + e3)).astype(BF16))
    oa = jnp.concatenate(oa_heads, axis=-1)

    gates = jax.nn.sigmoid(jnp.dot(xb, wg_ref[...], preferred_element_type=F32))
    merged = gates[:, :D_MODEL] * jnp.dot(oa, wa_ref[...], preferred_element_type=F32)
    merged = merged + gates[:, D_MODEL:2 * D_MODEL] * jnp.dot(ob_ref[...], wb_ref[...], preferred_element_type=F32)
    merged = merged + gates[:, 2 * D_MODEL:] * jnp.dot(oc_ref[...], wcc_ref[...], preferred_element_type=F32)
    y = DN_ALPHA * x + jnp.dot(merged.astype(BF16), wo_ref[...], preferred_element_type=F32)
    h_ref[...] = _layer_norm(y, g_ref[...], b_ref[...])


def _merge(x, oa_parts, ob, mk, mv, wc, wg, wa, wb, wcc, wo, ln_g, ln_b):
    b, s, d = x.shape
    tm = MERGE_TILE
    tok = lambda w: pl.BlockSpec((None, tm, w), lambda bi, t: (bi, t, 0))
    full = lambda a: pl.BlockSpec(a.shape, lambda bi, t: (0,) * a.ndim)
    mem = pl.BlockSpec((None,) + mk.shape[1:], lambda bi, t: (bi, 0, 0))
    flat = [a for pair in oa_parts for a in pair]
    return pl.pallas_call(
        _merge_kernel,
        grid=(b, s // tm),
        in_specs=[tok(d)] + [pl.BlockSpec((None, HEADS_A, tm, HEAD_DIM), lambda bi, t: (bi, 0, t, 0))] * 6 + [tok(WIDTH_B), mem, mem,
                  full(wc), full(wg), full(wa), full(wb), full(wcc), full(wo), full(ln_g), full(ln_b)],
        out_specs=tok(d),
        out_shape=jax.ShapeDtypeStruct((b, s, d), F32),
        scratch_shapes=[pltpu.VMEM((tm, WIDTH_C), BF16)],
        compiler_params=_cparams(("arbitrary", "arbitrary")),
        name="merge",
    )(x, *flat, ob, mk, mv, wc, wg, wa, wb, wcc, wo, ln_g, ln_b)


ROUTER_TILE = 512
_ROUTER_ROWS = 8 + N_EXPERTS


def _router_kernel(h_ref, wr_ref, br_ref, gsel_ref, wt_ref, rank_ref, cnt_ref, cnt_sc):
    tm = h_ref.shape[0]

    @pl.when(pl.program_id(0) == 0)
    def _():
        cnt_sc[...] = jnp.zeros_like(cnt_sc)

    logits = lax.dot_general(wr_ref[...], h_ref[...], _NT, preferred_element_type=F32,
                             precision=lax.Precision.HIGHEST) + br_ref[:, 0:1]
    g = [logits[i:i + 1, :] for i in range(N_EXPERT_GROUPS)]
    gmax = functools.reduce(jnp.maximum, g)
    gsel = jnp.full_like(gmax, N_EXPERT_GROUPS - 1).astype(jnp.int32)
    for i in range(N_EXPERT_GROUPS - 2, -1, -1):
        gsel = jnp.where(g[i] == gmax, i, gsel)
    gprob = 1.0 / functools.reduce(lambda a, c: a + c, [jnp.exp(gi - gmax) for gi in g])

    esel = logits[8 + (N_EXPERT_GROUPS - 1) * EXPERTS_PER_GROUP:8 + N_EXPERTS, :]
    for i in range(N_EXPERT_GROUPS - 2, -1, -1):
        esel = jnp.where(gsel == i, logits[8 + i * EXPERTS_PER_GROUP:8 + (i + 1) * EXPERTS_PER_GROUP, :], esel)
    ridx = lax.broadcasted_iota(jnp.int32, (EXPERTS_PER_GROUP, tm), 0)
    v1 = jnp.max(esel, axis=0, keepdims=True)
    i1 = jnp.min(jnp.where(esel == v1, ridx, EXPERTS_PER_GROUP), axis=0, keepdims=True)
    rest = jnp.where(ridx == i1, -jnp.inf, esel)
    v2 = jnp.max(rest, axis=0, keepdims=True)
    i2 = jnp.min(jnp.where(rest == v2, ridx, EXPERTS_PER_GROUP), axis=0, keepdims=True)
    t = jnp.exp(v2 - v1)
    w1 = gprob / (1.0 + t)
    w2 = gprob * t / (1.0 + t)
    wt_ref[...] = jnp.where(ridx == i1, w1, jnp.where(ridx == i2, w2, 0.0))
    gsel_ref[...] = gsel

    grow = lax.broadcasted_iota(jnp.int32, (8, tm), 0)
    onehot = (grow == gsel).astype(F32)
    before = (lax.broadcasted_iota(jnp.int32, (tm, tm), 0) < lax.broadcasted_iota(jnp.int32, (tm, tm), 1))
    excl = jnp.dot(onehot.astype(BF16), before.astype(BF16), preferred_element_type=F32)
    pos = cnt_sc[:, 0:1] + excl
    rank_ref[...] = jnp.sum(onehot * pos, axis=0, keepdims=True).astype(jnp.int32)
    cnt_sc[...] = cnt_sc[...] + jnp.sum(onehot, axis=1, keepdims=True)
    cnt_ref[...] = cnt_sc[...].astype(jnp.int32)


def _router(h2d, wr, br):
    t = h2d.shape[0]
    tm = ROUTER_TILE
    lane = lambda rows, dt: jax.ShapeDtypeStruct((rows, t), dt)
    lspec = lambda rows: pl.BlockSpec((rows, tm), lambda i: (0, i))
    return pl.pallas_call(
        _router_kernel,
        grid=(t // tm,),
        in_specs=[pl.BlockSpec((tm, D_MODEL), lambda i: (i, 0)),
                  pl.BlockSpec(wr.shape, lambda i: (0, 0)),
                  pl.BlockSpec(br.shape, lambda i: (0, 0))],
        out_specs=[lspec(1), lspec(EXPERTS_PER_GROUP), lspec(1), pl.BlockSpec((8, LANES), lambda i: (0, 0))],
        out_shape=[lane(1, jnp.int32), lane(EXPERTS_PER_GROUP, F32), lane(1, jnp.int32),
                   jax.ShapeDtypeStruct((8, LANES), jnp.int32)],
        scratch_shapes=[pltpu.VMEM((8, LANES), F32)],
        compiler_params=_cparams(("arbitrary",)),
        name="router",
    )(h2d, wr, br)


DISPATCH_TILE = 512


SUBLANES = 8
_HALF = D_MODEL // 2
_HI_MASK = 0xFFFF0000


def _tile_copy(src_ref, src_tok, dst_ref, dst_tok, sem):
    src = src_ref.at[pl.ds(pl.multiple_of(src_tok * SUBLANES, SUBLANES), SUBLANES)]
    dst = dst_ref.at[pl.ds(pl.multiple_of(dst_tok * SUBLANES, SUBLANES), SUBLANES)]
    return pltpu.make_async_copy(src, dst, sem)


def _tile_rows(ref, j, n):
    return ref.at[pl.ds(j, n, stride=SUBLANES), :]


def _dispatch_kernel(dest_ref, h_ref, w_ref, xz_ref, xbuf_ref, pay_ref, sem):
    del xz_ref
    tm = h_ref.shape[0]
    bits = pltpu.bitcast(h_ref[...].astype(BF16).astype(F32), jnp.uint32)
    for j in range(_HALF // LANES):
        lo = bits[:, j * LANES:(j + 1) * LANES] >> 16
        hi = bits[:, _HALF + j * LANES:_HALF + (j + 1) * LANES] & jnp.uint32(_HI_MASK)
        _tile_rows(pay_ref, j, tm)[...] = lo | hi
    _tile_rows(pay_ref, _HALF // LANES, tm)[...] = pltpu.bitcast(w_ref[...], jnp.uint32)
    for j in range(_HALF // LANES + 1, SUBLANES):
        _tile_rows(pay_ref, j, tm)[...] = jnp.zeros((tm, LANES), jnp.uint32)

    def issue(i, carry):
        _tile_copy(pay_ref, i, xbuf_ref, dest_ref[0, i], sem).start()
        return carry

    def drain(i, carry):
        _tile_copy(pay_ref, i, xbuf_ref, dest_ref[0, i], sem).wait()
        return carry

    lax.fori_loop(0, tm, issue, 0)
    lax.fori_loop(0, tm, drain, 0)


def _dispatch(h2d, wrow, dest, n_rows):
    t, d = h2d.shape
    tm = DISPATCH_TILE
    return pl.pallas_call(
        _dispatch_kernel,
        grid=(t // tm,),
        in_specs=[pl.BlockSpec((1, tm), lambda i: (0, i), memory_space=pltpu.SMEM),
                  pl.BlockSpec((tm, d), lambda i: (i, 0)),
                  pl.BlockSpec((tm, LANES), lambda i: (i, 0)),
                  pl.BlockSpec(memory_space=pl.ANY)],
        out_specs=pl.BlockSpec(memory_space=pl.ANY),
        out_shape=jax.ShapeDtypeStruct((n_rows * SUBLANES, LANES), jnp.uint32),
        scratch_shapes=[pltpu.VMEM((tm * SUBLANES, LANES), jnp.uint32), pltpu.SemaphoreType.DMA(())],
        input_output_aliases={3: 0},
        compiler_params=_cparams(("arbitrary",)),
        name="dispatch",
    )(dest, h2d, wrow, jnp.zeros((n_rows * SUBLANES, LANES), jnp.uint32))


def _expert_kernel(bg_ref, x_ref, wg_ref, wu_ref, wd_ref, y_ref):
    eb = x_ref.shape[0] // SUBLANES
    lo, hi = [], []
    for j in range(_HALF // LANES):
        word = _tile_rows(x_ref, j, eb)[...]
        lo.append(pltpu.bitcast(word << 16, F32))
        hi.append(pltpu.bitcast(word & jnp.uint32(_HI_MASK), F32))
    x = jnp.concatenate(lo + hi, axis=-1).astype(BF16)
    w8 = pltpu.bitcast(_tile_rows(x_ref, _HALF // LANES, eb)[...], F32)
    y = jnp.zeros((eb, D_MODEL), F32)

    def gate_up(e):
        return jnp.dot(x, wg_ref[e], preferred_element_type=F32), jnp.dot(x, wu_ref[e], preferred_element_type=F32)

    nxt = gate_up(0)
    for e in range(EXPERTS_PER_GROUP):
        a, u = nxt
        if e + 1 < EXPERTS_PER_GROUP:
            nxt = gate_up(e + 1)
        we = w8[:, e:e + 1]
        hid = jnp.where(we != 0.0, jax.nn.silu(a) * u * we, 0.0).astype(BF16)
        y = y + jnp.dot(hid, wd_ref[e], preferred_element_type=F32)
    for j in range(SUBLANES):
        _tile_rows(y_ref, j, eb)[...] = y[:, j * LANES:(j + 1) * LANES]


def _experts(xbuf, block_g, wg, wu, wd):
    eb = EXPERT_BLOCK
    blk = pl.BlockSpec((eb * SUBLANES, LANES), lambda i, bg: (i, 0))
    gw = lambda a: pl.BlockSpec((EXPERTS_PER_GROUP,) + a.shape[1:], lambda i, bg: (bg[i], 0, 0))
    grid_spec = pltpu.PrefetchScalarGridSpec(
        num_scalar_prefetch=1,
        grid=(xbuf.shape[0] // (eb * SUBLANES),),
        in_specs=[blk, gw(wg), gw(wu), gw(wd)],
        out_specs=blk,
    )
    return pl.pallas_call(
        _expert_kernel,
        grid_spec=grid_spec,
        out_shape=jax.ShapeDtypeStruct(xbuf.shape, F32),
        compiler_params=_cparams(("arbitrary",)),
        name="experts",
    )(block_g, xbuf, wg, wu, wd)


FINAL_TILE = 512


def _final_kernel(dest_ref, h_ref, g_ref, b_ref, ybuf_ref, o_ref, ys_ref, sem):
    tm = h_ref.shape[0]

    def issue(i, carry):
        _tile_copy(ybuf_ref, dest_ref[0, i], ys_ref, i, sem).start()
        return carry

    def drain(i, carry):
        _tile_copy(ybuf_ref, dest_ref[0, i], ys_ref, i, sem).wait()
        return carry

    lax.fori_loop(0, tm, issue, 0)
    lax.fori_loop(0, tm, drain, 0)
    y = jnp.concatenate([_tile_rows(ys_ref, j, tm)[...] for j in range(SUBLANES)], axis=-1)
    o_ref[...] = _layer_norm(DN_ALPHA * h_ref[...] + y, g_ref[...], b_ref[...])


def _final(h2d, ybuf, dest, ln_g, ln_b):
    t, d = h2d.shape
    tm = FINAL_TILE
    row = pl.BlockSpec((tm, d), lambda i: (i, 0))
    vec = pl.BlockSpec((1, d), lambda i: (0, 0))
    return pl.pallas_call(
        _final_kernel,
        grid=(t // tm,),
        in_specs=[pl.BlockSpec((1, tm), lambda i: (0, i), memory_space=pltpu.SMEM),
                  row, vec, vec, pl.BlockSpec(memory_space=pl.ANY)],
        out_specs=row,
        out_shape=jax.ShapeDtypeStruct((t, d), F32),
        scratch_shapes=[pltpu.VMEM((tm * SUBLANES, LANES), F32), pltpu.SemaphoreType.DMA(())],
        compiler_params=_cparams(("arbitrary",)),
        name="final",
    )(dest, h2d, ln_g, ln_b, ybuf)


def kernel(x, mem, rel_bias, hgrn_lb_logits, w_in, w_mem_kv, hgrn_norm_g, w_branch_a, w_branch_b, w_branch_c, w_out, ln1_g, ln1_b, w_router_group, b_router_group, w_router_expert, b_router_expert, w_exp_gate, w_exp_up, w_exp_down, ln2_g, ln2_b):
    b, s, d = x.shape
    t = b * s
    assert d == D_MODEL and w_in.shape[0] == DEPTH == 1
    assert s % (BAND * DILATED_GROUPS[-1][1]) == 0 and s % HGRN_TILE == 0 and t % ROUTER_TILE == 0
    li = 0
    scale = HEAD_DIM ** -0.5
    x2d = x.reshape(t, d)

    w = w_in[li]
    col_scale = np.ones((COLS_A,), np.float32).reshape(N_GROUPS_A, 3, WIDTH_A)
    col_scale[:, 0] = scale
    w_a = (w[:, :COLS_A] * jnp.asarray(col_scale.reshape(-1))).astype(BF16)
    w_b = w[:, COLS_A:COLS_A + COLS_B].astype(BF16)
    w_c = (w[:, COLS_A + COLS_B:COLS_A + COLS_B + COLS_C] * scale).astype(BF16)
    w_g = w[:, COLS_A + COLS_B + COLS_C:].astype(BF16)

    pb = _matmul(x2d, w_b, F32, 1024, COLS_B // 4, "proj_b").reshape(b, s, COLS_B)
    mkv = _matmul(mem.reshape(-1, d), w_mem_kv[li].astype(BF16), BF16, 256, 2 * WIDTH_C, "mem_kv")
    mkv = mkv.reshape(b, -1, 2 * WIDTH_C)
    mk, mv = mkv[..., :WIDTH_C], mkv[..., WIDTH_C:]

    oa_parts = []
    for gi, (_, dilation) in enumerate(DILATED_GROUPS):
        bias = _band_bias(rel_bias[:, gi * HEADS_A:(gi + 1) * HEADS_A], dilation)
        pg = _proj_classes(x, w_a[:, gi * 3 * WIDTH_A:(gi + 1) * 3 * WIDTH_A], dilation, f"proj_a_{gi}")
        oa_parts.append(_band_attention(pg, bias, f"band_attn_{gi}"))

    ob = _hgrn2(pb, hgrn_lb_logits.astype(F32), hgrn_norm_g[li].reshape(1, WIDTH_B))

    h = _merge(x, oa_parts, ob, mk, mv, w_c, w_g, w_branch_a[li].astype(BF16), w_branch_b[li].astype(BF16),
               w_branch_c[li].astype(BF16), w_out[li].astype(BF16), ln1_g[li].reshape(1, d), ln1_b[li].reshape(1, d))
    h2d = h.reshape(t, d)

    wr = jnp.zeros((_ROUTER_ROWS, d), F32)
    wr = wr.at[:N_EXPERT_GROUPS].set(w_router_group[li].T).at[8:].set(w_router_expert[li].T)
    br = jnp.zeros((_ROUTER_ROWS,), F32).at[:N_EXPERT_GROUPS].set(b_router_group[li]).at[8:].set(b_router_expert[li])
    br = jnp.broadcast_to(br[:, None], (_ROUTER_ROWS, 128))
    gsel, wt, rank, counts = _router(h2d, wr, br)

    eb = EXPERT_BLOCK
    counts = counts[:N_EXPERT_GROUPS, 0]
    pcounts = (counts + eb - 1) // eb * eb
    pends = jnp.cumsum(pcounts)
    pstarts = pends - pcounts
    n_blocks = (t + N_EXPERT_GROUPS * (eb - 1) + eb - 1) // eb
    block_start = jnp.arange(n_blocks, dtype=jnp.int32) * eb
    block_g = jnp.minimum(jnp.sum(pends[None, :] <= block_start[:, None], axis=1), N_EXPERT_GROUPS - 1).astype(jnp.int32)
    is_g = gsel == jnp.arange(N_EXPERT_GROUPS, dtype=jnp.int32)[:, None]
    dest = rank + jnp.sum(jnp.where(is_g, pstarts[:, None], 0), axis=0, keepdims=True)
    wrow = jnp.zeros((t, LANES), F32).at[:, :EXPERTS_PER_GROUP].set(wt.T)

    xbuf = _dispatch(h2d, wrow, dest, n_blocks * eb)
    ybuf = _experts(xbuf, block_g, w_exp_gate[li].astype(BF16), w_exp_up[li].astype(BF16), w_exp_down[li].astype(BF16))
    out = _final(h2d, ybuf, dest, ln2_g[li].reshape(1, d), ln2_b[li].reshape(1, d))
    return out.reshape(b, s, d)
```

```python
import functools
import math

import numpy as np
import jax
import jax.numpy as jnp
from jax import lax
from jax.experimental import pallas as pl
from jax.experimental.pallas import tpu as pltpu

F32 = jnp.float32
BF16 = jnp.bfloat16

D_MODEL = 1024
DEPTH = 1
DILATED_GROUPS = ((128, 1), (512, 4), (2048, 16))
N_GROUPS_A = len(DILATED_GROUPS)
HEADS_A = 4
HEAD_DIM = 128
WIDTH_A = HEADS_A * HEAD_DIM
BAND = 128
NUM_BUCKETS = 32
MAX_DISTANCE = 2048
N_HEADS_B = 8
WIDTH_B = N_HEADS_B * HEAD_DIM
CHUNK_B = 64
N_HEADS_C = 4
WIDTH_C = N_HEADS_C * HEAD_DIM
COLS_A = N_GROUPS_A * 3 * WIDTH_A
COLS_B = 4 * WIDTH_B
COLS_C = WIDTH_C
COLS_GATE = 3 * D_MODEL
N_EXPERT_GROUPS = 4
EXPERTS_PER_GROUP = 8
N_EXPERTS = N_EXPERT_GROUPS * EXPERTS_PER_GROUP
TOP_K = 2
D_EXPERT = D_MODEL // 4
DN_ALPHA = (2 * DEPTH) ** 0.25
LN_EPS = 1e-5
RMS_EPS = 1e-6

NEG_BIG = -1e30
SAFE_LOG_DECAY = 80.0
VMEM_LIMIT = 56 * 1024 * 1024
EXPERT_BLOCK = 512

_NT = (((1,), (1,)), ((), ()))
_TN = (((0,), (0,)), ((), ()))


def _cparams(sem):
    return pltpu.CompilerParams(dimension_semantics=sem, vmem_limit_bytes=VMEM_LIMIT)


def _matmul_kernel(x_ref, w_ref, o_ref):
    o_ref[...] = jnp.dot(x_ref[...].astype(BF16), w_ref[...], preferred_element_type=F32).astype(o_ref.dtype)


def _matmul(x, w, out_dtype, tm, tn, name):
    t, k = x.shape
    n = w.shape[1]
    return pl.pallas_call(
        _matmul_kernel,
        grid=(n // tn, t // tm),
        in_specs=[pl.BlockSpec((tm, k), lambda j, i: (i, 0)), pl.BlockSpec((k, tn), lambda j, i: (0, j))],
        out_specs=pl.BlockSpec((tm, tn), lambda j, i: (i, j)),
        out_shape=jax.ShapeDtypeStruct((t, n), out_dtype),
        compiler_params=_cparams(("arbitrary", "arbitrary")),
        name=name,
    )(x, w)


def _t5_bucket_np(dist):
    dist = np.asarray(dist, np.int32)
    max_exact = NUM_BUCKETS // 2
    d = np.maximum(dist, 1).astype(np.float32)
    large = max_exact + (np.log(d / max_exact) / math.log(MAX_DISTANCE / max_exact) * (NUM_BUCKETS - max_exact)).astype(np.int32)
    large = np.minimum(large, NUM_BUCKETS - 1)
    return np.where(dist < max_exact, dist, large).astype(np.int32)


def _band_bias(bias_tab, dilation):
    i = np.arange(BAND)[:, None]
    j = np.arange(2 * BAND)[None, :]
    u = i + BAND - j
    in_band = (u >= 0) & (u <= BAND)
    bucket = _t5_bucket_np(np.clip(u, 0, BAND) * dilation)
    onehot = np.asarray(bucket[:, :, None] == np.arange(NUM_BUCKETS)[None, None, :], np.float32)
    bias = jnp.einsum('pqb,bh->hpq', jnp.asarray(onehot), bias_tab.astype(F32), precision=lax.Precision.HIGHEST)
    general = jnp.where(in_band[None], bias, NEG_BIG)
    first = jnp.where((in_band & (j >= BAND))[None], bias, NEG_BIG)
    return jnp.stack([first, general])


LANES = 128


def _proj_perm_kernel(*refs, r):
    n_slab = D_MODEL // LANES
    x_refs, (w_ref, o_ref, xp_ref) = refs[:n_slab], refs[n_slab:]
    tm = xp_ref.shape[0]
    n = tm // r
    for j, x_ref in enumerate(x_refs):
        for c in range(r):
            rows = x_ref[pl.ds(c, n, stride=r), :] if r > 1 else x_ref[...]
            xp_ref[c * n:(c + 1) * n, j * LANES:(j + 1) * LANES] = rows.astype(BF16)
    res = jnp.dot(xp_ref[...], w_ref[...], preferred_element_type=F32).astype(o_ref.dtype)
    for c in range(r):
        o_ref[c] = res[c * n:(c + 1) * n]


PROJ_A_TILE = 1024


def _proj_classes(x, w, r, name):
    b, s, d = x.shape
    n = w.shape[1]
    tm = PROJ_A_TILE
    n_slab = d // LANES
    slabs = [pl.BlockSpec((None, tm, LANES), lambda bi, t, j=j: (bi, t, j)) for j in range(n_slab)]
    return pl.pallas_call(
        functools.partial(_proj_perm_kernel, r=r),
        grid=(b, s // tm),
        in_specs=slabs + [pl.BlockSpec((d, n), lambda bi, t: (0, 0))],
        out_specs=pl.BlockSpec((None, r, tm // r, n), lambda bi, t: (bi, 0, t, 0)),
        out_shape=jax.ShapeDtypeStruct((b, r, s // r, n), BF16),
        scratch_shapes=[pltpu.VMEM((tm, d), BF16)],
        compiler_params=_cparams(("arbitrary", "arbitrary")),
        name=name,
    )(*([x] * n_slab), w)


BAND_SPAN = 2048


def _band_attn_kernel(q_ref, k_ref, v_ref, kp_ref, vp_ref, bias_ref, o_ref, lse_ref, *, r):
    nq = q_ref.shape[1] // BAND
    variant = jnp.minimum(pl.program_id(1), 1)
    ones = jnp.ones((BAND, HEAD_DIM), BF16)
    heads = [slice(h * HEAD_DIM, (h + 1) * HEAD_DIM) for h in range(HEADS_A)]
    units = [(c, j) for c in range(r) for j in range(nq)]

    def prev_of(k_or_v_ref, prev_ref, c, j, sl):
        return prev_ref[c, :, sl] if j == 0 else k_or_v_ref[c, (j - 1) * BAND:j * BAND, sl]

    def scores(c, j):
        cur = slice(j * BAND, (j + 1) * BAND)
        out = []
        for h, sl in enumerate(heads):
            q = q_ref[c, cur, sl]
            bias = bias_ref[variant, h] if j == 0 else bias_ref[1, h]
            sp = lax.dot_general(q, prev_of(k_ref, kp_ref, c, j, sl), _NT, preferred_element_type=F32)
            sc = lax.dot_general(q, k_ref[c, cur, sl], _NT, preferred_element_type=F32)
            out.append((sp + bias[:, :BAND], sc + bias[:, BAND:]))
        return out

    nxt = scores(*units[0])
    for u, (c, j) in enumerate(units):
        s_all = nxt
        if u + 1 < len(units):
            nxt = scores(*units[u + 1])
        cur = slice(j * BAND, (j + 1) * BAND)
        rows = pl.ds(j * BAND * r + c, BAND, stride=r) if r > 1 else cur
        probs = []
        for sp, sc in s_all:
            m = jnp.max(jnp.maximum(sp, sc), axis=-1, keepdims=True)
            probs.append((m, jnp.exp(sp - m).astype(BF16), jnp.exp(sc - m).astype(BF16)))
        for h, sl in enumerate(heads):
            m, pp, pc = probs[h]
            l = jnp.dot(pp, ones, preferred_element_type=F32) + jnp.dot(pc, ones, preferred_element_type=F32)
            o = jnp.dot(pp, prev_of(v_ref, vp_ref, c, j, sl), preferred_element_type=F32)
            o = o + jnp.dot(pc, v_ref[c, cur, sl], preferred_element_type=F32)
            o_ref[h, rows, :] = o / l
            lse_ref[h, rows, :] = m + jnp.log(l)


def _band_attention(pg, bias, name):
    b, r, l, _ = pg.shape
    nq = BAND_SPAN // (BAND * r)
    cur = lambda which: pl.BlockSpec((None, r, nq * BAND, WIDTH_A), lambda bi, n: (bi, 0, n, which))
    prev = lambda which: pl.BlockSpec((None, r, BAND, WIDTH_A), lambda bi, n: (bi, 0, jnp.maximum(n * nq - 1, 0), which))
    out_spec = pl.BlockSpec((None, HEADS_A, BAND_SPAN, HEAD_DIM), lambda bi, n: (bi, 0, n, 0))
    return pl.pallas_call(
        functools.partial(_band_attn_kernel, r=r),
        grid=(b, l * r // BAND_SPAN),
        in_specs=[cur(0), cur(1), cur(2), prev(1), prev(2), pl.BlockSpec(bias.shape, lambda bi, n: (0, 0, 0, 0))],
        out_specs=[out_spec, out_spec],
        out_shape=[jax.ShapeDtypeStruct((b, HEADS_A, l * r, HEAD_DIM), F32)] * 2,
        compiler_params=_cparams(("arbitrary", "arbitrary")),
        name=name,
    )(pg, pg, pg, pg, pg, bias)


HGRN_TILE = 256
_N_CHUNK = HGRN_TILE // CHUNK_B


def _dynamic_row(ref, r, ls):
    blk = ref[pl.ds(pl.multiple_of((r // 8) * 8, 8), 8), ls]
    sub = lax.broadcasted_iota(jnp.int32, blk.shape, 0)
    return jnp.sum(jnp.where(sub == r % 8, blk, 0.0), axis=0, keepdims=True)


def _sigmoid(z):
    return 0.5 * jnp.tanh(0.5 * z) + 0.5


_PROJ_PIECE = 512


def _hgrn_kernel(x_ref, xn_ref, w_ref, lbl_ref, ng_ref, o_ref, proj_ref, xb_ref,
                 st_ref, qs_ref, ks_ref, kh_ref, vs_ref, bs_ref, ebl_ref, a_ref, os_ref, *, tiles_per_seq):
    i = pl.program_id(0)
    slot = lax.rem(i, 2)
    pr = proj_ref.at[slot]
    pn = proj_ref.at[1 - slot]

    def piece(src_ref, dst_ref, k):
        cols = slice(k * _PROJ_PIECE, (k + 1) * _PROJ_PIECE)
        dst_ref[:, cols] = jnp.dot(src_ref[...], w_ref[:, cols], preferred_element_type=F32)

    @pl.when(i == 0)
    def _():
        xb_ref[...] = x_ref[...].astype(BF16)
        for k in range(COLS_B // _PROJ_PIECE):
            piece(xb_ref, pr, k)

    @pl.when(lax.rem(i, tiles_per_seq) == 0)
    def _():
        st_ref[...] = jnp.zeros_like(st_ref)

    xb_ref[...] = xn_ref[...].astype(BF16)
    pieces = iter(range(COLS_B // _PROJ_PIECE))
    next_piece = lambda: piece(xb_ref, pn, next(pieces))
    q_cols, f_cols, i_cols, g_cols = (slice(p * WIDTH_B, (p + 1) * WIDTH_B) for p in range(4))

    lg = lbl_ref[...]
    e = jnp.exp(lg - jnp.max(lg, axis=0, keepdims=True))
    lb = e[0:1, :] / jnp.sum(e, axis=0, keepdims=True)

    row = lax.broadcasted_iota(jnp.int32, (CHUNK_B, CHUNK_B), 0)
    colm = lax.broadcasted_iota(jnp.int32, (CHUNK_B, CHUNK_B), 1)
    causal = row >= colm
    tri = causal.astype(BF16)

    next_piece()
    f = lb + (1.0 - lb) * _sigmoid(pr[:, f_cols])
    logf = jnp.log(f)
    kf = 1.0 - f
    l_hi = logf.astype(BF16)
    rem = logf - l_hi.astype(F32)
    l_mid = rem.astype(BF16)
    l_lo = (rem - l_mid.astype(F32)).astype(BF16)
    min_b = None
    for c in range(_N_CHUNK):
        rs = slice(c * CHUNK_B, (c + 1) * CHUNK_B)
        b = (jnp.dot(tri, l_hi[rs], preferred_element_type=F32) + jnp.dot(tri, l_mid[rs], preferred_element_type=F32)
             + jnp.dot(tri, l_lo[rs], preferred_element_type=F32))
        bs_ref[rs, :] = b
        cm = jnp.min(b[CHUNK_B - 1:CHUNK_B, :])
        min_b = cm if min_b is None else jnp.minimum(min_b, cm)
        if c % 2 == 1:
            next_piece()
    safe = min_b > -SAFE_LOG_DECAY

    qv = pr[:, q_cols]
    qsil = qv * _sigmoid(qv)
    for c in range(_N_CHUNK):
        rs = slice(c * CHUNK_B, (c + 1) * CHUNK_B)
        b = bs_ref[rs, :]
        eb_last = jnp.exp(b[CHUNK_B - 1:CHUNK_B, :])
        ebl_ref[c:c + 1, :] = eb_last
        qs_ref[rs, :] = (qsil[rs] * jnp.exp(b)).astype(BF16)
        k_grown = kf[rs] * jnp.exp(-b)
        ks_ref[rs, :] = k_grown.astype(BF16)
        kh_ref[rs, :] = (k_grown * eb_last).astype(BF16)
    vs_ref[...] = pr[:, i_cols].astype(BF16)

    @pl.when(safe)
    def _():
        for c in range(_N_CHUNK):
            rs = slice(c * CHUNK_B, (c + 1) * CHUNK_B)
            for h in range(N_HEADS_B):
                ls = slice(h * HEAD_DIM, (h + 1) * HEAD_DIM)
                a = lax.dot_general(qs_ref[rs, ls], ks_ref[rs, ls], _NT, preferred_element_type=F32)
                a_ref[c * N_HEADS_B + h] = jnp.where(causal, a, 0.0)

    @pl.when(jnp.logical_not(safe))
    def _():
        lane = lax.broadcasted_iota(jnp.int32, (1, CHUNK_B), 1)
        trow = lax.broadcasted_iota(jnp.int32, (CHUNK_B, 1), 0)
        for c in range(_N_CHUNK):
            rs = slice(c * CHUNK_B, (c + 1) * CHUNK_B)
            b = bs_ref[rs, :]
            f_c = lb + (1.0 - lb) * _sigmoid(pr[rs, f_cols])
            kh_ref[rs, :] = ((1.0 - f_c) * jnp.exp(b[CHUNK_B - 1:CHUNK_B, :] - b)).astype(BF16)
            for h in range(N_HEADS_B):
                ls = slice(h * HEAD_DIM, (h + 1) * HEAD_DIM)
                fls = slice(WIDTH_B + h * HEAD_DIM, WIDTH_B + (h + 1) * HEAD_DIM)
                bq = bs_ref[rs, ls]
                qh = pr[rs, ls]
                qh = qh * _sigmoid(qh)

                def body(s, acc, c=c, ls=ls, fls=fls, bq=bq, qh=qh):
                    b_s = _dynamic_row(bs_ref, c * CHUNK_B + s, ls)
                    f_s = lb[:, ls] + (1.0 - lb[:, ls]) * _sigmoid(_dynamic_row(pr, c * CHUNK_B + s, fls))
                    w = jnp.exp(jnp.minimum(bq - b_s, 0.0))
                    colv = jnp.sum(qh * (1.0 - f_s) * w, axis=-1, keepdims=True)
                    colv = jnp.where(trow >= s, colv, 0.0)
                    return acc + colv * (lane == s).astype(F32)

                a_ref[c * N_HEADS_B + h] = lax.fori_loop(0, CHUNK_B, body, jnp.zeros((CHUNK_B, CHUNK_B), F32))

    for c in range(_N_CHUNK):
        rs = slice(c * CHUNK_B, (c + 1) * CHUNK_B)
        for h in range(N_HEADS_B):
            ls = slice(h * HEAD_DIM, (h + 1) * HEAD_DIM)
            st = st_ref[h]
            vh = vs_ref[rs, ls]
            o = lax.dot_general(qs_ref[rs, ls], st.astype(BF16), _NT, preferred_element_type=F32)
            o = o + jnp.dot(a_ref[c * N_HEADS_B + h].astype(BF16), vh, preferred_element_type=F32)
            os_ref[rs, ls] = o
            st_ref[h] = st * ebl_ref[c:c + 1, ls] + lax.dot_general(vh, kh_ref[rs, ls], _TN, preferred_element_type=F32)
            if h == N_HEADS_B - 1:
                next_piece()
    next_piece()
    assert next(pieces, None) is None

    ng = ng_ref[...]
    for h in range(N_HEADS_B):
        ls = slice(h * HEAD_DIM, (h + 1) * HEAD_DIM)
        gls = slice(3 * WIDTH_B + h * HEAD_DIM, 3 * WIDTH_B + (h + 1) * HEAD_DIM)
        o = os_ref[:, ls]
        o = o * lax.rsqrt(jnp.mean(o * o, axis=-1, keepdims=True) + RMS_EPS)
        gv = pr[:, gls]
        o_ref[:, ls] = (o * ng[:, ls] * (gv * _sigmoid(gv))).astype(o_ref.dtype)


def _hgrn2(x, w_b, lb_logits, norm_g):
    b, s, d = x.shape
    tb = HGRN_TILE
    n = b * s // tb
    n_slots = lb_logits.shape[0]
    out = pl.pallas_call(
        functools.partial(_hgrn_kernel, tiles_per_seq=s // tb),
        grid=(n,),
        in_specs=[pl.BlockSpec((tb, d), lambda i: (i, 0)),
                  pl.BlockSpec((tb, d), lambda i: (jnp.minimum(i + 1, n - 1), 0)),
                  pl.BlockSpec(w_b.shape, lambda i: (0, 0)),
                  pl.BlockSpec((n_slots, WIDTH_B), lambda i: (0, 0)),
                  pl.BlockSpec((1, WIDTH_B), lambda i: (0, 0))],
        out_specs=pl.BlockSpec((tb, WIDTH_B), lambda i: (i, 0)),
        out_shape=jax.ShapeDtypeStruct((b * s, WIDTH_B), BF16),
        scratch_shapes=[
            pltpu.VMEM((2, tb, COLS_B), F32),
            pltpu.VMEM((tb, d), BF16),
            pltpu.VMEM((N_HEADS_B, HEAD_DIM, HEAD_DIM), F32),
            pltpu.VMEM((tb, WIDTH_B), BF16),
            pltpu.VMEM((tb, WIDTH_B), BF16),
            pltpu.VMEM((tb, WIDTH_B), BF16),
            pltpu.VMEM((tb, WIDTH_B), BF16),
            pltpu.VMEM((tb, WIDTH_B), F32),
            pltpu.VMEM((8, WIDTH_B), F32),
            pltpu.VMEM((_N_CHUNK * N_HEADS_B, CHUNK_B, CHUNK_B), F32),
            pltpu.VMEM((tb, WIDTH_B), F32),
        ],
        compiler_params=_cparams(("arbitrary",)),
        name="hgrn2",
    )(x.reshape(b * s, d), x.reshape(b * s, d), w_b, lb_logits, norm_g)
    return out.reshape(b, s, WIDTH_B)


MERGE_TILE = 256


def _layer_norm(y, g, b):
    mu = jnp.mean(y, axis=-1, keepdims=True)
    d = y - mu
    var = jnp.mean(d * d, axis=-1, keepdims=True)
    return d * lax.rsqrt(var + LN_EPS) * g + b


def _merge_kernel(x_ref, o1_ref, l1_ref, o2_ref, l2_ref, o3_ref, l3_ref, ob_ref, mk_ref, mv_ref,
                  wc_ref, wg_ref, wa_ref, wb_ref, wcc_ref, wo_ref, g_ref, b_ref, h_ref, oc_ref):
    x = x_ref[...]
    xb = x.astype(BF16)

    qc = jnp.dot(xb, wc_ref[...], preferred_element_type=F32).astype(BF16)
    heads_c = [slice(h * HEAD_DIM, (h + 1) * HEAD_DIM) for h in range(N_HEADS_C)]
    s_all = [lax.dot_general(qc[:, ls], mk_ref[:, ls], _NT, preferred_element_type=F32) for ls in heads_c]
    p_all = [jnp.exp(s - jnp.max(s, axis=-1, keepdims=True)).astype(BF16) for s in s_all]
    ones = jnp.ones((mk_ref.shape[0], HEAD_DIM), BF16)
    for ls, p in zip(heads_c, p_all):
        l = jnp.dot(p, ones, preferred_element_type=F32)
        oc_ref[:, ls] = (jnp.dot(p, mv_ref[:, ls], preferred_element_type=F32) / l).astype(BF16)

    oa_heads = []
    for h in range(HEADS_A):
        l1, l2, l3 = l1_ref[h], l2_ref[h], l3_ref[h]
        m = jnp.maximum(jnp.maximum(l1, l2), l3)
        e1, e2, e3 = jnp.exp(l1 - m), jnp.exp(l2 - m), jnp.exp(l3 - m)
        oa_heads.append(((e1 * o1_ref[h] + e2 * o2_ref[h] + e3 * o3_ref[h]) / (e1 + e2 + e3)).astype(BF16))
    oa = jnp.concatenate(oa_heads, axis=-1)

    gates = jax.nn.sigmoid(jnp.dot(xb, wg_ref[...], preferred_element_type=F32))
    merged = gates[:, :D_MODEL] * jnp.dot(oa, wa_ref[...], preferred_element_type=F32)
    merged = merged + gates[:, D_MODEL:2 * D_MODEL] * jnp.dot(ob_ref[...], wb_ref[...], preferred_element_type=F32)
    merged = merged + gates[:, 2 * D_MODEL:] * jnp.dot(oc_ref[...], wcc_ref[...], preferred_element_type=F32)
    y = DN_ALPHA * x + jnp.dot(merged.astype(BF16), wo_ref[...], preferred_element_type=F32)
    h_ref[...] = _layer_norm(y, g_ref[...], b_ref[...])


def _merge(x, oa_parts, ob, mk, mv, wc, wg, wa, wb, wcc, wo, ln_g, ln_b):
    b, s, d = x.shape
    tm = MERGE_TILE
    tok = lambda w: pl.BlockSpec((None, tm, w), lambda bi, t: (bi, t, 0))
    full = lambda a: pl.BlockSpec(a.shape, lambda bi, t: (0,) * a.ndim)
    mem = pl.BlockSpec((None,) + mk.shape[1:], lambda bi, t: (bi, 0, 0))
    flat = [a for pair in oa_parts for a in pair]
    return pl.pallas_call(
        _merge_kernel,
        grid=(b, s // tm),
        in_specs=[tok(d)] + [pl.BlockSpec((None, HEADS_A, tm, HEAD_DIM), lambda bi, t: (bi, 0, t, 0))] * 6 + [tok(WIDTH_B), mem, mem,
                  full(wc), full(wg), full(wa), full(wb), full(wcc), full(wo), full(ln_g), full(ln_b)],
        out_specs=tok(d),
        out_shape=jax.ShapeDtypeStruct((b, s, d), F32),
        scratch_shapes=[pltpu.VMEM((tm, WIDTH_C), BF16)],
        compiler_params=_cparams(("arbitrary", "arbitrary")),
        name="merge",
    )(x, *flat, ob, mk, mv, wc, wg, wa, wb, wcc, wo, ln_g, ln_b)


ROUTER_TILE = 512
_ROUTER_ROWS = 8 + N_EXPERTS


def _router_kernel(h_ref, wr_ref, br_ref, dest_ref, wrow_ref, cnt_ref, cnt_sc, *, group_capacity):
    tm = h_ref.shape[0]

    @pl.when(pl.program_id(0) == 0)
    def _():
        cnt_sc[...] = jnp.zeros_like(cnt_sc)

    hv = h_ref[...]
    h_hi = hv.astype(BF16)
    h_lo = (hv - h_hi.astype(F32)).astype(BF16)
    wv = wr_ref[...]
    w_hi = wv.astype(BF16)
    w_lo = (wv - w_hi.astype(F32)).astype(BF16)
    both = jnp.dot(h_hi, jnp.concatenate([w_hi, w_lo], axis=1), preferred_element_type=F32)
    logits = both[:, :LANES] + both[:, LANES:] + jnp.dot(h_lo, w_hi, preferred_element_type=F32)
    logits = logits.T[:_ROUTER_ROWS, :] + br_ref[:, 0:1]
    g = [logits[i:i + 1, :] for i in range(N_EXPERT_GROUPS)]
    gmax = functools.reduce(jnp.maximum, g)
    gsel = jnp.full_like(gmax, N_EXPERT_GROUPS - 1).astype(jnp.int32)
    for i in range(N_EXPERT_GROUPS - 2, -1, -1):
        gsel = jnp.where(g[i] == gmax, i, gsel)
    gprob = 1.0 / functools.reduce(lambda a, c: a + c, [jnp.exp(gi - gmax) for gi in g])

    esel = logits[8 + (N_EXPERT_GROUPS - 1) * EXPERTS_PER_GROUP:8 + N_EXPERTS, :]
    for i in range(N_EXPERT_GROUPS - 2, -1, -1):
        esel = jnp.where(gsel == i, logits[8 + i * EXPERTS_PER_GROUP:8 + (i + 1) * EXPERTS_PER_GROUP, :], esel)
    ridx = lax.broadcasted_iota(jnp.int32, (EXPERTS_PER_GROUP, tm), 0)
    v1 = jnp.max(esel, axis=0, keepdims=True)
    i1 = jnp.min(jnp.where(esel == v1, ridx, EXPERTS_PER_GROUP), axis=0, keepdims=True)
    rest = jnp.where(ridx == i1, -jnp.inf, esel)
    v2 = jnp.max(rest, axis=0, keepdims=True)
    i2 = jnp.min(jnp.where(rest == v2, ridx, EXPERTS_PER_GROUP), axis=0, keepdims=True)
    t = jnp.exp(v2 - v1)
    w1 = gprob / (1.0 + t)
    w2 = gprob * t / (1.0 + t)
    wt = jnp.where(ridx == i1, w1, jnp.where(ridx == i2, w2, 0.0))
    eye = (lax.broadcasted_iota(jnp.int32, (EXPERTS_PER_GROUP, LANES), 0)
           == lax.broadcasted_iota(jnp.int32, (EXPERTS_PER_GROUP, LANES), 1)).astype(F32)
    wrow_ref[...] = lax.dot_general(wt, eye, _TN, preferred_element_type=F32, precision=lax.Precision.HIGHEST)

    grow = lax.broadcasted_iota(jnp.int32, (8, tm), 0)
    onehot = (grow == gsel).astype(F32)
    before = (lax.broadcasted_iota(jnp.int32, (tm, tm), 0) < lax.broadcasted_iota(jnp.int32, (tm, tm), 1))
    excl = jnp.dot(onehot.astype(BF16), before.astype(BF16), preferred_element_type=F32)
    pos = cnt_sc[:, 0:1] + excl
    rank = jnp.sum(onehot * pos, axis=0, keepdims=True).astype(jnp.int32)
    dest_ref[...] = gsel * group_capacity + rank
    cnt_sc[...] = cnt_sc[...] + jnp.sum(onehot, axis=1, keepdims=True)
    cnt_ref[...] = cnt_sc[...].astype(jnp.int32)


def _router(h2d, wr, br, group_capacity):
    t = h2d.shape[0]
    tm = ROUTER_TILE
    return pl.pallas_call(
        functools.partial(_router_kernel, group_capacity=group_capacity),
        grid=(t // tm,),
        in_specs=[pl.BlockSpec((tm, D_MODEL), lambda i: (i, 0)),
                  pl.BlockSpec(wr.shape, lambda i: (0, 0)),
                  pl.BlockSpec(br.shape, lambda i: (0, 0))],
        out_specs=[pl.BlockSpec((1, tm), lambda i: (0, i)), pl.BlockSpec((tm, LANES), lambda i: (i, 0)),
                   pl.BlockSpec((8, LANES), lambda i: (0, 0))],
        out_shape=[jax.ShapeDtypeStruct((1, t), jnp.int32), jax.ShapeDtypeStruct((t, LANES), F32),
                   jax.ShapeDtypeStruct((8, LANES), jnp.int32)],
        scratch_shapes=[pltpu.VMEM((8, LANES), F32)],
        compiler_params=_cparams(("arbitrary",)),
        name="router",
    )(h2d, wr, br)


DISPATCH_TILE = 512


SUBLANES = 8
_HALF = D_MODEL // 2
_HI_MASK = 0xFFFF0000


def _tile_copy(src_ref, src_tok, dst_ref, dst_tok, sem):
    src = src_ref.at[pl.ds(pl.multiple_of(src_tok * SUBLANES, SUBLANES), SUBLANES)]
    dst = dst_ref.at[pl.ds(pl.multiple_of(dst_tok * SUBLANES, SUBLANES), SUBLANES)]
    return pltpu.make_async_copy(src, dst, sem)


def _tile_rows(ref, j, n):
    return ref.at[pl.ds(j, n, stride=SUBLANES), :]


def _all_tiles_wait(src_ref, dst_ref, sem):
    pltpu.make_async_copy(src_ref, dst_ref.at[pl.ds(0, src_ref.shape[0])], sem).wait()


def _dispatch_kernel(zblk_ref, dest_ref, h_ref, w_ref, xbuf_ref, pay_ref, zero_ref, sems, zsem):
    i = pl.program_id(0)
    n_steps = pl.num_programs(0)
    slot = lax.rem(i, 2)
    tm = h_ref.shape[0]
    pay = pay_ref.at[slot]

    @pl.when(i == 0)
    def _():
        zero_ref[...] = jnp.zeros_like(zero_ref)
        for k in range(zblk_ref.shape[0]):
            @pl.when(zblk_ref[k] >= 0)
            def _():
                start = pl.multiple_of(zblk_ref[k] * zero_ref.shape[0], zero_ref.shape[0])
                cp = pltpu.make_async_copy(zero_ref, xbuf_ref.at[pl.ds(start, zero_ref.shape[0])], zsem)
                cp.start()
                cp.wait()

    @pl.when(i >= 2)
    def _():
        _all_tiles_wait(pay, xbuf_ref, sems.at[slot])

    bits = pltpu.bitcast(h_ref[...].astype(BF16).astype(F32), jnp.uint32)
    for j in range(_HALF // LANES):
        lo = bits[:, j * LANES:(j + 1) * LANES] >> 16
        hi = bits[:, _HALF + j * LANES:_HALF + (j + 1) * LANES] & jnp.uint32(_HI_MASK)
        _tile_rows(pay, j, tm)[...] = lo | hi
    _tile_rows(pay, _HALF // LANES, tm)[...] = pltpu.bitcast(w_ref[...], jnp.uint32)
    for j in range(_HALF // LANES + 1, SUBLANES):
        _tile_rows(pay, j, tm)[...] = jnp.zeros((tm, LANES), jnp.uint32)

    def issue(t, carry):
        _tile_copy(pay, t, xbuf_ref, dest_ref[0, t], sems.at[slot]).start()
        return carry

    lax.fori_loop(0, tm, issue, 0, unroll=8)

    @pl.when(i == n_steps - 1)
    def _():
        @pl.when(i >= 1)
        def _():
            _all_tiles_wait(pay_ref.at[1 - slot], xbuf_ref, sems.at[1 - slot])

        _all_tiles_wait(pay, xbuf_ref, sems.at[slot])


def _dispatch(h2d, wrow, dest, zero_blocks, n_blocks):
    t, d = h2d.shape
    tm = DISPATCH_TILE
    grid_spec = pltpu.PrefetchScalarGridSpec(
        num_scalar_prefetch=1,
        grid=(t // tm,),
        in_specs=[pl.BlockSpec((1, tm), lambda i, zb: (0, i), memory_space=pltpu.SMEM),
                  pl.BlockSpec((tm, d), lambda i, zb: (i, 0)),
                  pl.BlockSpec((tm, LANES), lambda i, zb: (i, 0))],
        out_specs=pl.BlockSpec(memory_space=pl.ANY),
        scratch_shapes=[pltpu.VMEM((2, tm * SUBLANES, LANES), jnp.uint32),
                        pltpu.VMEM((EXPERT_BLOCK * SUBLANES, LANES), jnp.uint32),
                        pltpu.SemaphoreType.DMA((2,)), pltpu.SemaphoreType.DMA(())],
    )
    return pl.pallas_call(
        _dispatch_kernel,
        grid_spec=grid_spec,
        out_shape=jax.ShapeDtypeStruct((n_blocks * EXPERT_BLOCK * SUBLANES, LANES), jnp.uint32),
        compiler_params=_cparams(("arbitrary",)),
        name="dispatch",
    )(zero_blocks, dest, h2d, wrow)


def _expert_kernel(bg_ref, nact_ref, x_ref, wg_ref, wu_ref, wd_ref, y_ref):
    active = pl.program_id(0) < nact_ref[0]
    pl.when(active)(functools.partial(_expert_block, x_ref, wg_ref, wu_ref, wd_ref, y_ref))

    @pl.when(jnp.logical_not(active))
    def _():
        y_ref[...] = jnp.zeros_like(y_ref)


def _expert_block(x_ref, wg_ref, wu_ref, wd_ref, y_ref):
    eb = x_ref.shape[0] // SUBLANES
    lo, hi = [], []
    for j in range(_HALF // LANES):
        word = _tile_rows(x_ref, j, eb)[...]
        lo.append(pltpu.bitcast(word << 16, F32))
        hi.append(pltpu.bitcast(word & jnp.uint32(_HI_MASK), F32))
    x = jnp.concatenate(lo + hi, axis=-1).astype(BF16)
    w8 = pltpu.bitcast(_tile_rows(x_ref, _HALF // LANES, eb)[...], F32)
    y = jnp.zeros((eb, D_MODEL), F32)

    def gate_up(e):
        return jnp.dot(x, wg_ref[e], preferred_element_type=F32), jnp.dot(x, wu_ref[e], preferred_element_type=F32)

    nxt = gate_up(0)
    for e in range(EXPERTS_PER_GROUP):
        a, u = nxt
        if e + 1 < EXPERTS_PER_GROUP:
            nxt = gate_up(e + 1)
        we = w8[:, e:e + 1]
        hid = jnp.where(we != 0.0, jax.nn.silu(a) * u * we, 0.0).astype(BF16)
        y = y + jnp.dot(hid, wd_ref[e], preferred_element_type=F32)
    for j in range(SUBLANES):
        _tile_rows(y_ref, j, eb)[...] = y[:, j * LANES:(j + 1) * LANES]


def _experts(xbuf, block_g, n_active, wg, wu, wd):
    eb = EXPERT_BLOCK
    blk = pl.BlockSpec((eb * SUBLANES, LANES), lambda i, bg, nact: (i, 0))
    gw = lambda a: pl.BlockSpec((EXPERTS_PER_GROUP,) + a.shape[1:], lambda i, bg, nact: (bg[i], 0, 0))
    grid_spec = pltpu.PrefetchScalarGridSpec(
        num_scalar_prefetch=2,
        grid=(block_g.shape[0],),
        in_specs=[blk, gw(wg), gw(wu), gw(wd)],
        out_specs=blk,
    )
    return pl.pallas_call(
        _expert_kernel,
        grid_spec=grid_spec,
        out_shape=jax.ShapeDtypeStruct(xbuf.shape, F32),
        compiler_params=_cparams(("arbitrary",)),
        name="experts",
    )(block_g, n_active, xbuf, wg, wu, wd)


FINAL_TILE = 512


def _final_kernel(dest_ref, dest_next_ref, h_ref, g_ref, b_ref, ybuf_ref, o_ref, ys_ref, sems):
    i = pl.program_id(0)
    slot = lax.rem(i, 2)
    tm = h_ref.shape[0]

    def gather(d_ref, s):
        def issue(t, carry):
            _tile_copy(ybuf_ref, d_ref[0, t], ys_ref.at[s], t, sems.at[s]).start()
            return carry

        lax.fori_loop(0, tm, issue, 0, unroll=8)

    @pl.when(i == 0)
    def _():
        gather(dest_ref, slot)

    @pl.when(i + 1 < pl.num_programs(0))
    def _():
        gather(dest_next_ref, 1 - slot)

    ys = ys_ref.at[slot]
    pltpu.make_async_copy(ybuf_ref.at[pl.ds(0, ys.shape[0])], ys, sems.at[slot]).wait()
    y = jnp.concatenate([_tile_rows(ys, j, tm)[...] for j in range(SUBLANES)], axis=-1)
    o_ref[...] = _layer_norm(DN_ALPHA * h_ref[...] + y, g_ref[...], b_ref[...])


def _final(h2d, ybuf, dest, ln_g, ln_b):
    t, d = h2d.shape
    tm = FINAL_TILE
    n = t // tm
    row = pl.BlockSpec((tm, d), lambda i: (i, 0))
    vec = pl.BlockSpec((1, d), lambda i: (0, 0))
    return pl.pallas_call(
        _final_kernel,
        grid=(n,),
        in_specs=[pl.BlockSpec((1, tm), lambda i: (0, i), memory_space=pltpu.SMEM),
                  pl.BlockSpec((1, tm), lambda i: (0, jnp.minimum(i + 1, n - 1)), memory_space=pltpu.SMEM),
                  row, vec, vec, pl.BlockSpec(memory_space=pl.ANY)],
        out_specs=row,
        out_shape=jax.ShapeDtypeStruct((t, d), F32),
        scratch_shapes=[pltpu.VMEM((2, tm * SUBLANES, LANES), F32), pltpu.SemaphoreType.DMA((2,))],
        compiler_params=_cparams(("arbitrary",)),
        name="final",
    )(dest, dest, h2d, ln_g, ln_b, ybuf)


def kernel(x, mem, rel_bias, hgrn_lb_logits, w_in, w_mem_kv, hgrn_norm_g, w_branch_a, w_branch_b, w_branch_c, w_out, ln1_g, ln1_b, w_router_group, b_router_group, w_router_expert, b_router_expert, w_exp_gate, w_exp_up, w_exp_down, ln2_g, ln2_b):
    b, s, d = x.shape
    t = b * s
    assert d == D_MODEL and w_in.shape[0] == DEPTH == 1
    assert s % (BAND * DILATED_GROUPS[-1][1]) == 0 and s % HGRN_TILE == 0 and t % ROUTER_TILE == 0
    li = 0
    scale = HEAD_DIM ** -0.5
    x2d = x.reshape(t, d)

    w = w_in[li]
    col_scale = np.ones((COLS_A,), np.float32).reshape(N_GROUPS_A, 3, WIDTH_A)
    col_scale[:, 0] = scale
    w_a = (w[:, :COLS_A] * jnp.asarray(col_scale.reshape(-1))).astype(BF16)
    w_b = w[:, COLS_A:COLS_A + COLS_B].astype(BF16)
    w_c = (w[:, COLS_A + COLS_B:COLS_A + COLS_B + COLS_C] * scale).astype(BF16)
    w_g = w[:, COLS_A + COLS_B + COLS_C:].astype(BF16)

    mkv = _matmul(mem.reshape(-1, d), w_mem_kv[li].astype(BF16), BF16, 256, 2 * WIDTH_C, "mem_kv")
    mkv = mkv.reshape(b, -1, 2 * WIDTH_C)
    mk, mv = mkv[..., :WIDTH_C], mkv[..., WIDTH_C:]

    oa_parts = []
    for gi, (_, dilation) in enumerate(DILATED_GROUPS):
        bias = _band_bias(rel_bias[:, gi * HEADS_A:(gi + 1) * HEADS_A], dilation)
        pg = _proj_classes(x, w_a[:, gi * 3 * WIDTH_A:(gi + 1) * 3 * WIDTH_A], dilation, f"proj_a_{gi}")
        oa_parts.append(_band_attention(pg, bias, f"band_attn_{gi}"))

    ob = _hgrn2(x, w_b, hgrn_lb_logits.astype(F32), hgrn_norm_g[li].reshape(1, WIDTH_B))

    h = _merge(x, oa_parts, ob, mk, mv, w_c, w_g, w_branch_a[li].astype(BF16), w_branch_b[li].astype(BF16),
               w_branch_c[li].astype(BF16), w_out[li].astype(BF16), ln1_g[li].reshape(1, d), ln1_b[li].reshape(1, d))
    h2d = h.reshape(t, d)

    wr = jnp.zeros((d, LANES), F32)
    wr = wr.at[:, :N_EXPERT_GROUPS].set(w_router_group[li]).at[:, 8:_ROUTER_ROWS].set(w_router_expert[li])
    br = jnp.zeros((_ROUTER_ROWS,), F32).at[:N_EXPERT_GROUPS].set(b_router_group[li]).at[8:].set(b_router_expert[li])
    br = jnp.broadcast_to(br[:, None], (_ROUTER_ROWS, 128))
    eb = EXPERT_BLOCK
    code, wrow, counts = _router(h2d, wr, br, t)
    counts = counts[:N_EXPERT_GROUPS, 0]

    blocks_g = (counts + eb - 1) // eb
    ends = jnp.cumsum(blocks_g)
    starts = ends - blocks_g
    n_active = ends[-1:].astype(jnp.int32)
    n_blocks = t // eb + N_EXPERT_GROUPS
    blk = jnp.arange(n_blocks, dtype=jnp.int32)
    block_g = jnp.minimum(jnp.sum(ends[None, :] <= blk[:, None], axis=1), N_EXPERT_GROUPS - 1).astype(jnp.int32)
    g_of = code // t
    is_g = g_of == jnp.arange(N_EXPERT_GROUPS, dtype=jnp.int32)[:, None]
    dest = (code - g_of * t + jnp.sum(jnp.where(is_g, starts[:, None] * eb, 0), axis=0, keepdims=True)).astype(jnp.int32)
    trailing = n_active + jnp.arange(N_EXPERT_GROUPS, dtype=jnp.int32)
    zero_blocks = jnp.concatenate([jnp.where(blocks_g > 0, ends - 1, -1),
                                   jnp.where(trailing < n_blocks, trailing, -1)]).astype(jnp.int32)

    xbuf = _dispatch(h2d, wrow, dest, zero_blocks, n_blocks)
    ybuf = _experts(xbuf, block_g, n_active,
                    w_exp_gate[li].astype(BF16), w_exp_up[li].astype(BF16), w_exp_down[li].astype(BF16))
    out = _final(h2d, ybuf, dest, ln2_g[li].reshape(1, d), ln2_b[li].reshape(1, d))
    return out.reshape(b, s, d)
```

```python
import functools
import math

import numpy as np
import jax
import jax.numpy as jnp
from jax import lax
from jax.experimental import pallas as pl
from jax.experimental.pallas import tpu as pltpu

F32 = jnp.float32
BF16 = jnp.bfloat16

D_MODEL = 1024
DEPTH = 1
DILATED_GROUPS = ((128, 1), (512, 4), (2048, 16))
N_GROUPS_A = len(DILATED_GROUPS)
HEADS_A = 4
HEAD_DIM = 128
WIDTH_A = HEADS_A * HEAD_DIM
BAND = 128
NUM_BUCKETS = 32
MAX_DISTANCE = 2048
N_HEADS_B = 8
WIDTH_B = N_HEADS_B * HEAD_DIM
CHUNK_B = 64
N_HEADS_C = 4
WIDTH_C = N_HEADS_C * HEAD_DIM
COLS_A = N_GROUPS_A * 3 * WIDTH_A
COLS_B = 4 * WIDTH_B
COLS_C = WIDTH_C
COLS_GATE = 3 * D_MODEL
N_EXPERT_GROUPS = 4
EXPERTS_PER_GROUP = 8
N_EXPERTS = N_EXPERT_GROUPS * EXPERTS_PER_GROUP
TOP_K = 2
D_EXPERT = D_MODEL // 4
DN_ALPHA = (2 * DEPTH) ** 0.25
LN_EPS = 1e-5
RMS_EPS = 1e-6

NEG_BIG = -1e30
SAFE_LOG_DECAY = 80.0
VMEM_LIMIT = 56 * 1024 * 1024
EXPERT_BLOCK = 512

_NT = (((1,), (1,)), ((), ()))
_TN = (((0,), (0,)), ((), ()))


def _cparams(sem):
    return pltpu.CompilerParams(dimension_semantics=sem, vmem_limit_bytes=VMEM_LIMIT)


def _matmul_kernel(x_ref, w_ref, o_ref):
    o_ref[...] = jnp.dot(x_ref[...].astype(BF16), w_ref[...], preferred_element_type=F32).astype(o_ref.dtype)


def _matmul(x, w, out_dtype, tm, tn, name):
    t, k = x.shape
    n = w.shape[1]
    return pl.pallas_call(
        _matmul_kernel,
        grid=(n // tn, t // tm),
        in_specs=[pl.BlockSpec((tm, k), lambda j, i: (i, 0)), pl.BlockSpec((k, tn), lambda j, i: (0, j))],
        out_specs=pl.BlockSpec((tm, tn), lambda j, i: (i, j)),
        out_shape=jax.ShapeDtypeStruct((t, n), out_dtype),
        compiler_params=_cparams(("arbitrary", "arbitrary")),
        name=name,
    )(x, w)


def _t5_bucket_np(dist):
    dist = np.asarray(dist, np.int32)
    max_exact = NUM_BUCKETS // 2
    d = np.maximum(dist, 1).astype(np.float32)
    large = max_exact + (np.log(d / max_exact) / math.log(MAX_DISTANCE / max_exact) * (NUM_BUCKETS - max_exact)).astype(np.int32)
    large = np.minimum(large, NUM_BUCKETS - 1)
    return np.where(dist < max_exact, dist, large).astype(np.int32)


def _band_bias(bias_tab, dilation):
    i = np.arange(BAND)[:, None]
    j = np.arange(2 * BAND)[None, :]
    u = i + BAND - j
    in_band = (u >= 0) & (u <= BAND)
    bucket = _t5_bucket_np(np.clip(u, 0, BAND) * dilation)
    onehot = np.asarray(bucket[:, :, None] == np.arange(NUM_BUCKETS)[None, None, :], np.float32)
    bias = jnp.einsum('pqb,bh->hpq', jnp.asarray(onehot), bias_tab.astype(F32), precision=lax.Precision.HIGHEST)
    general = jnp.where(in_band[None], bias, NEG_BIG)
    first = jnp.where((in_band & (j >= BAND))[None], bias, NEG_BIG)
    return jnp.stack([first, general])


LANES = 128


def _proj_perm_kernel(*refs, r):
    n_slab = D_MODEL // LANES
    x_refs, (w_ref, o_ref, xp_ref) = refs[:n_slab], refs[n_slab:]
    tm = xp_ref.shape[0]
    n = tm // r
    for j, x_ref in enumerate(x_refs):
        for c in range(r):
            rows = x_ref[pl.ds(c, n, stride=r), :] if r > 1 else x_ref[...]
            xp_ref[c * n:(c + 1) * n, j * LANES:(j + 1) * LANES] = rows.astype(BF16)
    res = jnp.dot(xp_ref[...], w_ref[...], preferred_element_type=F32).astype(o_ref.dtype)
    for c in range(r):
        o_ref[c] = res[c * n:(c + 1) * n]


PROJ_A_TILE = 1024


def _proj_classes(x, w, r, name):
    b, s, d = x.shape
    n = w.shape[1]
    tm = PROJ_A_TILE
    n_slab = d // LANES
    slabs = [pl.BlockSpec((None, tm, LANES), lambda bi, t, j=j: (bi, t, j)) for j in range(n_slab)]
    return pl.pallas_call(
        functools.partial(_proj_perm_kernel, r=r),
        grid=(b, s // tm),
        in_specs=slabs + [pl.BlockSpec((d, n), lambda bi, t: (0, 0))],
        out_specs=pl.BlockSpec((None, r, tm // r, n), lambda bi, t: (bi, 0, t, 0)),
        out_shape=jax.ShapeDtypeStruct((b, r, s // r, n), BF16),
        scratch_shapes=[pltpu.VMEM((tm, d), BF16)],
        compiler_params=_cparams(("arbitrary", "arbitrary")),
        name=name,
    )(*([x] * n_slab), w)


BAND_SPAN = 2048
_HI_MASK = 0xFFFF0000
_LSE_LANES = LANES // HEADS_A


def _pack_bf16_pair(a, b):
    abits = pltpu.bitcast(a.astype(BF16).astype(F32), jnp.uint32)
    bbits = pltpu.bitcast(b.astype(BF16).astype(F32), jnp.uint32)
    return (abits >> 16) | (bbits & jnp.uint32(_HI_MASK))


def _unpack_bf16_pair(word):
    return pltpu.bitcast(word << 16, F32), pltpu.bitcast(word & jnp.uint32(_HI_MASK), F32)


def _band_attn_kernel(q_ref, k_ref, v_ref, kp_ref, vp_ref, bias_ref, o_ref, lse_ref, *, r):
    nq = q_ref.shape[1] // BAND
    variant = jnp.minimum(pl.program_id(1), 1)
    ones = jnp.ones((BAND, HEAD_DIM), BF16)
    lane = lax.broadcasted_iota(jnp.int32, (BAND, LANES), 1)
    heads = [slice(h * HEAD_DIM, (h + 1) * HEAD_DIM) for h in range(HEADS_A)]
    units = [(c, j) for c in range(r) for j in range(nq)]

    def prev_of(k_or_v_ref, prev_ref, c, j, sl):
        return prev_ref[c, :, sl] if j == 0 else k_or_v_ref[c, (j - 1) * BAND:j * BAND, sl]

    def scores(c, j):
        cur = slice(j * BAND, (j + 1) * BAND)
        out = []
        for h, sl in enumerate(heads):
            q = q_ref[c, cur, sl]
            bias = bias_ref[variant, h] if j == 0 else bias_ref[1, h]
            sp = lax.dot_general(q, prev_of(k_ref, kp_ref, c, j, sl), _NT, preferred_element_type=F32)
            sc = lax.dot_general(q, k_ref[c, cur, sl], _NT, preferred_element_type=F32)
            out.append((sp + bias[:, :BAND], sc + bias[:, BAND:]))
        return out

    nxt = scores(*units[0])
    for u, (c, j) in enumerate(units):
        s_all = nxt
        if u + 1 < len(units):
            nxt = scores(*units[u + 1])
        cur = slice(j * BAND, (j + 1) * BAND)
        rows = pl.ds(j * BAND * r + c, BAND, stride=r) if r > 1 else cur
        probs = []
        for sp, sc in s_all:
            m = jnp.max(jnp.maximum(sp, sc), axis=-1, keepdims=True)
            probs.append((m, jnp.exp(sp - m).astype(BF16), jnp.exp(sc - m).astype(BF16)))
        outs, lses = [], []
        for h, sl in enumerate(heads):
            m, pp, pc = probs[h]
            l = jnp.dot(pp, ones, preferred_element_type=F32) + jnp.dot(pc, ones, preferred_element_type=F32)
            o = jnp.dot(pp, prev_of(v_ref, vp_ref, c, j, sl), preferred_element_type=F32)
            o = o + jnp.dot(pc, v_ref[c, cur, sl], preferred_element_type=F32)
            outs.append(o / l)
            lses.append(m + jnp.log(l))
        for p in range(HEADS_A // 2):
            o_ref[p, rows, :] = _pack_bf16_pair(outs[2 * p], outs[2 * p + 1])
        lse = lses[-1]
        for h in range(HEADS_A - 2, -1, -1):
            lse = jnp.where(lane < (h + 1) * _LSE_LANES, lses[h], lse)
        lse_ref[rows, :] = lse


def _band_attention(pg, bias, name):
    b, r, l, _ = pg.shape
    nq = BAND_SPAN // (BAND * r)
    cur = lambda which: pl.BlockSpec((None, r, nq * BAND, WIDTH_A), lambda bi, n: (bi, 0, n, which))
    prev = lambda which: pl.BlockSpec((None, r, BAND, WIDTH_A), lambda bi, n: (bi, 0, jnp.maximum(n * nq - 1, 0), which))
    return pl.pallas_call(
        functools.partial(_band_attn_kernel, r=r),
        grid=(b, l * r // BAND_SPAN),
        in_specs=[cur(0), cur(1), cur(2), prev(1), prev(2), pl.BlockSpec(bias.shape, lambda bi, n: (0, 0, 0, 0))],
        out_specs=[pl.BlockSpec((None, HEADS_A // 2, BAND_SPAN, HEAD_DIM), lambda bi, n: (bi, 0, n, 0)),
                   pl.BlockSpec((None, BAND_SPAN, LANES), lambda bi, n: (bi, n, 0))],
        out_shape=[jax.ShapeDtypeStruct((b, HEADS_A // 2, l * r, HEAD_DIM), jnp.uint32),
                   jax.ShapeDtypeStruct((b, l * r, LANES), F32)],
        compiler_params=_cparams(("arbitrary", "arbitrary")),
        name=name,
    )(pg, pg, pg, pg, pg, bias)


HGRN_TILE = 512
_N_CHUNK = HGRN_TILE // CHUNK_B


def _dynamic_row(ref, r, ls):
    blk = ref[pl.ds(pl.multiple_of((r // 8) * 8, 8), 8), ls]
    sub = lax.broadcasted_iota(jnp.int32, blk.shape, 0)
    return jnp.sum(jnp.where(sub == r % 8, blk, 0.0), axis=0, keepdims=True)


def _sigmoid(z):
    return 0.5 * jnp.tanh(0.5 * z) + 0.5


_PROJ_PIECE = 256


def _hgrn_kernel(x_ref, xn_ref, w_ref, lbl_ref, ng_ref, o_ref, proj_ref, xb_ref,
                 st_ref, qs_ref, ks_ref, kh_ref, vs_ref, bs_ref, ebl_ref, a_ref, os_ref, *, tiles_per_seq):
    i = pl.program_id(0)
    slot = lax.rem(i, 2)
    pr = proj_ref.at[slot]
    pn = proj_ref.at[1 - slot]

    def piece(src_ref, dst_ref, k):
        cols = slice(k * _PROJ_PIECE, (k + 1) * _PROJ_PIECE)
        dst_ref[:, cols] = jnp.dot(src_ref[...], w_ref[:, cols], preferred_element_type=F32)

    @pl.when(i == 0)
    def _():
        xb_ref[...] = x_ref[...].astype(BF16)
        for k in range(COLS_B // _PROJ_PIECE):
            piece(xb_ref, pr, k)

    @pl.when(lax.rem(i, tiles_per_seq) == 0)
    def _():
        st_ref[...] = jnp.zeros_like(st_ref)

    xb_ref[...] = xn_ref[...].astype(BF16)
    pieces = list(range(COLS_B // _PROJ_PIECE))
    next_piece = lambda: piece(xb_ref, pn, pieces.pop(0)) if pieces else None
    q_cols, f_cols, i_cols, g_cols = (slice(p * WIDTH_B, (p + 1) * WIDTH_B) for p in range(4))

    lg = lbl_ref[...]
    e = jnp.exp(lg - jnp.max(lg, axis=0, keepdims=True))
    lb = e[0:1, :] / jnp.sum(e, axis=0, keepdims=True)

    row = lax.broadcasted_iota(jnp.int32, (CHUNK_B, CHUNK_B), 0)
    colm = lax.broadcasted_iota(jnp.int32, (CHUNK_B, CHUNK_B), 1)
    causal = row >= colm
    tri = causal.astype(BF16)

    next_piece()
    f = lb + (1.0 - lb) * _sigmoid(pr[:, f_cols])
    logf = jnp.log(f)
    kf = 1.0 - f
    l_hi = logf.astype(BF16)
    rem = logf - l_hi.astype(F32)
    l_mid = rem.astype(BF16)
    l_lo = (rem - l_mid.astype(F32)).astype(BF16)
    min_b = None
    for c in range(_N_CHUNK):
        rs = slice(c * CHUNK_B, (c + 1) * CHUNK_B)
        b = (jnp.dot(tri, l_hi[rs], preferred_element_type=F32) + jnp.dot(tri, l_mid[rs], preferred_element_type=F32)
             + jnp.dot(tri, l_lo[rs], preferred_element_type=F32))
        bs_ref[rs, :] = b
        cm = jnp.min(b[CHUNK_B - 1:CHUNK_B, :])
        min_b = cm if min_b is None else jnp.minimum(min_b, cm)
        if c % 2 == 1:
            next_piece()
    safe = min_b > -SAFE_LOG_DECAY

    qv = pr[:, q_cols]
    qsil = qv * _sigmoid(qv)
    for c in range(_N_CHUNK):
        rs = slice(c * CHUNK_B, (c + 1) * CHUNK_B)
        b = bs_ref[rs, :]
        eb_last = jnp.exp(b[CHUNK_B - 1:CHUNK_B, :])
        ebl_ref[c:c + 1, :] = eb_last
        qs_ref[rs, :] = (qsil[rs] * jnp.exp(b)).astype(BF16)
        k_grown = kf[rs] * jnp.exp(-b)
        ks_ref[rs, :] = k_grown.astype(BF16)
        kh_ref[rs, :] = (k_grown * eb_last).astype(BF16)
    vs_ref[...] = pr[:, i_cols].astype(BF16)

    @pl.when(safe)
    def _():
        for c in range(_N_CHUNK):
            rs = slice(c * CHUNK_B, (c + 1) * CHUNK_B)
            for h in range(N_HEADS_B):
                ls = slice(h * HEAD_DIM, (h + 1) * HEAD_DIM)
                a = lax.dot_general(qs_ref[rs, ls], ks_ref[rs, ls], _NT, preferred_element_type=F32)
                a_ref[c * N_HEADS_B + h] = jnp.where(causal, a, 0.0)

    @pl.when(jnp.logical_not(safe))
    def _():
        lane = lax.broadcasted_iota(jnp.int32, (1, CHUNK_B), 1)
        trow = lax.broadcasted_iota(jnp.int32, (CHUNK_B, 1), 0)
        for c in range(_N_CHUNK):
            rs = slice(c * CHUNK_B, (c + 1) * CHUNK_B)
            b = bs_ref[rs, :]
            f_c = lb + (1.0 - lb) * _sigmoid(pr[rs, f_cols])
            kh_ref[rs, :] = ((1.0 - f_c) * jnp.exp(b[CHUNK_B - 1:CHUNK_B, :] - b)).astype(BF16)
            for h in range(N_HEADS_B):
                ls = slice(h * HEAD_DIM, (h + 1) * HEAD_DIM)
                fls = slice(WIDTH_B + h * HEAD_DIM, WIDTH_B + (h + 1) * HEAD_DIM)
                bq = bs_ref[rs, ls]
                qh = pr[rs, ls]
                qh = qh * _sigmoid(qh)

                def body(s, acc, c=c, ls=ls, fls=fls, bq=bq, qh=qh):
                    b_s = _dynamic_row(bs_ref, c * CHUNK_B + s, ls)
                    f_s = lb[:, ls] + (1.0 - lb[:, ls]) * _sigmoid(_dynamic_row(pr, c * CHUNK_B + s, fls))
                    w = jnp.exp(jnp.minimum(bq - b_s, 0.0))
                    colv = jnp.sum(qh * (1.0 - f_s) * w, axis=-1, keepdims=True)
                    colv = jnp.where(trow >= s, colv, 0.0)
                    return acc + colv * (lane == s).astype(F32)

                a_ref[c * N_HEADS_B + h] = lax.fori_loop(0, CHUNK_B, body, jnp.zeros((CHUNK_B, CHUNK_B), F32))

    for c in range(_N_CHUNK):
        rs = slice(c * CHUNK_B, (c + 1) * CHUNK_B)
        for h in range(N_HEADS_B):
            ls = slice(h * HEAD_DIM, (h + 1) * HEAD_DIM)
            st = st_ref[h]
            vh = vs_ref[rs, ls]
            o = lax.dot_general(qs_ref[rs, ls], st.astype(BF16), _NT, preferred_element_type=F32)
            o = o + jnp.dot(a_ref[c * N_HEADS_B + h].astype(BF16), vh, preferred_element_type=F32)
            os_ref[rs, ls] = o
            st_ref[h] = st * ebl_ref[c:c + 1, ls] + lax.dot_general(vh, kh_ref[rs, ls], _TN, preferred_element_type=F32)
            if h == N_HEADS_B - 1:
                next_piece()
    while pieces:
        next_piece()

    ng = ng_ref[...]
    for h in range(N_HEADS_B):
        ls = slice(h * HEAD_DIM, (h + 1) * HEAD_DIM)
        gls = slice(3 * WIDTH_B + h * HEAD_DIM, 3 * WIDTH_B + (h + 1) * HEAD_DIM)
        o = os_ref[:, ls]
        o = o * lax.rsqrt(jnp.mean(o * o, axis=-1, keepdims=True) + RMS_EPS)
        gv = pr[:, gls]
        o_ref[:, ls] = (o * ng[:, ls] * (gv * _sigmoid(gv))).astype(o_ref.dtype)


def _hgrn2(x, w_b, lb_logits, norm_g):
    b, s, d = x.shape
    tb = HGRN_TILE
    n = b * s // tb
    n_slots = lb_logits.shape[0]
    out = pl.pallas_call(
        functools.partial(_hgrn_kernel, tiles_per_seq=s // tb),
        grid=(n,),
        in_specs=[pl.BlockSpec((tb, d), lambda i: (i, 0)),
                  pl.BlockSpec((tb, d), lambda i: (jnp.minimum(i + 1, n - 1), 0)),
                  pl.BlockSpec(w_b.shape, lambda i: (0, 0), pipeline_mode=pl.Buffered(1)),
                  pl.BlockSpec((n_slots, WIDTH_B), lambda i: (0, 0)),
                  pl.BlockSpec((1, WIDTH_B), lambda i: (0, 0))],
        out_specs=pl.BlockSpec((tb, WIDTH_B), lambda i: (i, 0)),
        out_shape=jax.ShapeDtypeStruct((b * s, WIDTH_B), BF16),
        scratch_shapes=[
            pltpu.VMEM((2, tb, COLS_B), F32),
            pltpu.VMEM((tb, d), BF16),
            pltpu.VMEM((N_HEADS_B, HEAD_DIM, HEAD_DIM), F32),
            pltpu.VMEM((tb, WIDTH_B), BF16),
            pltpu.VMEM((tb, WIDTH_B), BF16),
            pltpu.VMEM((tb, WIDTH_B), BF16),
            pltpu.VMEM((tb, WIDTH_B), BF16),
            pltpu.VMEM((tb, WIDTH_B), F32),
            pltpu.VMEM((8, WIDTH_B), F32),
            pltpu.VMEM((_N_CHUNK * N_HEADS_B, CHUNK_B, CHUNK_B), F32),
            pltpu.VMEM((tb, WIDTH_B), F32),
        ],
        compiler_params=_cparams(("arbitrary",)),
        name="hgrn2",
    )(x.reshape(b * s, d), x.reshape(b * s, d), w_b, lb_logits, norm_g)
    return out.reshape(b, s, WIDTH_B)


MERGE_TILE = 512


def _layer_norm(y, g, b):
    mu = jnp.mean(y, axis=-1, keepdims=True)
    d = y - mu
    var = jnp.mean(d * d, axis=-1, keepdims=True)
    return d * lax.rsqrt(var + LN_EPS) * g + b


def _merge_kernel(x_ref, o1_ref, l1_ref, o2_ref, l2_ref, o3_ref, l3_ref, ob_ref, mk_ref, mv_ref,
                  wc_ref, wg_ref, wa_ref, wb_ref, wcc_ref, wo_ref, g_ref, b_ref, wr_ref, br_ref,
                  h_ref, code_ref, wrow_ref, cnt_ref, oc_ref, cnt_sc, *, n_tokens):
    x = x_ref[...]
    xb = x.astype(BF16)

    qc = jnp.dot(xb, wc_ref[...], preferred_element_type=F32).astype(BF16)
    heads_c = [slice(h * HEAD_DIM, (h + 1) * HEAD_DIM) for h in range(N_HEADS_C)]
    s_all = [lax.dot_general(qc[:, ls], mk_ref[:, ls], _NT, preferred_element_type=F32) for ls in heads_c]
    p_all = [jnp.exp(s - jnp.max(s, axis=-1, keepdims=True)).astype(BF16) for s in s_all]
    ones = jnp.ones((mk_ref.shape[0], HEAD_DIM), BF16)
    for ls, p in zip(heads_c, p_all):
        l = jnp.dot(p, ones, preferred_element_type=F32)
        oc_ref[:, ls] = (jnp.dot(p, mv_ref[:, ls], preferred_element_type=F32) / l).astype(BF16)

    lses = [l1_ref[...], l2_ref[...], l3_ref[...]]
    m = functools.reduce(jnp.maximum, lses)
    es = [jnp.exp(l - m) for l in lses]
    inv = 1.0 / functools.reduce(lambda a, c: a + c, es)
    mix = [e * inv for e in es]
    o_pairs = [[_unpack_bf16_pair(o_ref[p]) for p in range(HEADS_A // 2)] for o_ref in (o1_ref, o2_ref, o3_ref)]
    oa_heads = []
    for h in range(HEADS_A):
        acc = None
        for gi in range(N_GROUPS_A):
            w = jnp.broadcast_to(mix[gi][:, h * _LSE_LANES:h * _LSE_LANES + 1], (x.shape[0], HEAD_DIM))
            term = w * o_pairs[gi][h // 2][h % 2]
            acc = term if acc is None else acc + term
        oa_heads.append(acc.astype(BF16))
    oa = jnp.concatenate(oa_heads, axis=-1)

    gates = jax.nn.sigmoid(jnp.dot(xb, wg_ref[...], preferred_element_type=F32))
    merged = gates[:, :D_MODEL] * jnp.dot(oa, wa_ref[...], preferred_element_type=F32)
    merged = merged + gates[:, D_MODEL:2 * D_MODEL] * jnp.dot(ob_ref[...], wb_ref[...], preferred_element_type=F32)
    merged = merged + gates[:, 2 * D_MODEL:] * jnp.dot(oc_ref[...], wcc_ref[...], preferred_element_type=F32)
    y = DN_ALPHA * x + jnp.dot(merged.astype(BF16), wo_ref[...], preferred_element_type=F32)
    hv = _layer_norm(y, g_ref[...], b_ref[...])
    h_ref[...] = hv

    first = jnp.logical_and(pl.program_id(0) == 0, pl.program_id(1) == 0)
    _route(hv, wr_ref, br_ref, code_ref, wrow_ref, cnt_ref, cnt_sc, first, n_tokens)


def _merge(x, oa_parts, ob, mk, mv, wc, wg, wa, wb, wcc, wo, ln_g, ln_b, wr, br):
    b, s, d = x.shape
    tm = MERGE_TILE
    nt = s // tm
    tok = lambda w: pl.BlockSpec((None, tm, w), lambda bi, t: (bi, t, 0))
    full = lambda a: pl.BlockSpec(a.shape, lambda bi, t: (0,) * a.ndim, pipeline_mode=pl.Buffered(1))
    mem = pl.BlockSpec((None,) + mk.shape[1:], lambda bi, t: (bi, 0, 0))
    o_spec = pl.BlockSpec((None, HEADS_A // 2, tm, HEAD_DIM), lambda bi, t: (bi, 0, t, 0))
    flat = [a for pair in oa_parts for a in pair]
    return pl.pallas_call(
        functools.partial(_merge_kernel, n_tokens=b * s),
        grid=(b, nt),
        in_specs=[tok(d)] + [o_spec, tok(LANES)] * N_GROUPS_A + [tok(WIDTH_B), mem, mem,
                  full(wc), full(wg), full(wa), full(wb), full(wcc), full(wo), full(ln_g), full(ln_b), full(wr), full(br)],
        out_specs=[tok(d), pl.BlockSpec((1, tm), lambda bi, t: (0, bi * nt + t)),
                   pl.BlockSpec((tm, LANES), lambda bi, t: (bi * nt + t, 0)),
                   pl.BlockSpec((8, LANES), lambda bi, t: (0, 0))],
        out_shape=[jax.ShapeDtypeStruct((b, s, d), F32), jax.ShapeDtypeStruct((1, b * s), jnp.int32),
                   jax.ShapeDtypeStruct((b * s, LANES), F32), jax.ShapeDtypeStruct((8, LANES), jnp.int32)],
        scratch_shapes=[pltpu.VMEM((tm, WIDTH_C), BF16), pltpu.VMEM((8, LANES), F32)],
        compiler_params=_cparams(("arbitrary", "arbitrary")),
        name="merge",
    )(x, *flat, ob, mk, mv, wc, wg, wa, wb, wcc, wo, ln_g, ln_b, wr, br)


_ROUTER_ROWS = 8 + N_EXPERTS


def _route(hv, wr_ref, br_ref, dest_ref, wrow_ref, cnt_ref, cnt_sc, first_step, group_capacity):
    tm = hv.shape[0]

    @pl.when(first_step)
    def _():
        cnt_sc[...] = jnp.zeros_like(cnt_sc)

    h_hi = hv.astype(BF16)
    h_lo = (hv - h_hi.astype(F32)).astype(BF16)
    wv = wr_ref[...]
    w_hi = wv.astype(BF16)
    w_lo = (wv - w_hi.astype(F32)).astype(BF16)
    both = jnp.dot(h_hi, jnp.concatenate([w_hi, w_lo], axis=1), preferred_element_type=F32)
    logits = both[:, :LANES] + both[:, LANES:] + jnp.dot(h_lo, w_hi, preferred_element_type=F32)
    logits = logits.T[:_ROUTER_ROWS, :] + br_ref[:, 0:1]
    g = [logits[i:i + 1, :] for i in range(N_EXPERT_GROUPS)]
    gmax = functools.reduce(jnp.maximum, g)
    gsel = jnp.full_like(gmax, N_EXPERT_GROUPS - 1).astype(jnp.int32)
    for i in range(N_EXPERT_GROUPS - 2, -1, -1):
        gsel = jnp.where(g[i] == gmax, i, gsel)
    gprob = 1.0 / functools.reduce(lambda a, c: a + c, [jnp.exp(gi - gmax) for gi in g])

    esel = logits[8 + (N_EXPERT_GROUPS - 1) * EXPERTS_PER_GROUP:8 + N_EXPERTS, :]
    for i in range(N_EXPERT_GROUPS - 2, -1, -1):
        esel = jnp.where(gsel == i, logits[8 + i * EXPERTS_PER_GROUP:8 + (i + 1) * EXPERTS_PER_GROUP, :], esel)
    ridx = lax.broadcasted_iota(jnp.int32, (EXPERTS_PER_GROUP, tm), 0)
    v1 = jnp.max(esel, axis=0, keepdims=True)
    i1 = jnp.min(jnp.where(esel == v1, ridx, EXPERTS_PER_GROUP), axis=0, keepdims=True)
    rest = jnp.where(ridx == i1, -jnp.inf, esel)
    v2 = jnp.max(rest, axis=0, keepdims=True)
    i2 = jnp.min(jnp.where(rest == v2, ridx, EXPERTS_PER_GROUP), axis=0, keepdims=True)
    t = jnp.exp(v2 - v1)
    w1 = gprob / (1.0 + t)
    w2 = gprob * t / (1.0 + t)
    wt = jnp.where(ridx == i1, w1, jnp.where(ridx == i2, w2, 0.0))
    eye = (lax.broadcasted_iota(jnp.int32, (EXPERTS_PER_GROUP, LANES), 0)
           == lax.broadcasted_iota(jnp.int32, (EXPERTS_PER_GROUP, LANES), 1)).astype(F32)
    wrow_ref[...] = lax.dot_general(wt, eye, _TN, preferred_element_type=F32, precision=lax.Precision.HIGHEST)

    grow = lax.broadcasted_iota(jnp.int32, (8, tm), 0)
    onehot = (grow == gsel).astype(F32)
    before = (lax.broadcasted_iota(jnp.int32, (tm, tm), 0) < lax.broadcasted_iota(jnp.int32, (tm, tm), 1))
    excl = jnp.dot(onehot.astype(BF16), before.astype(BF16), preferred_element_type=F32)
    pos = cnt_sc[:, 0:1] + excl
    rank = jnp.sum(onehot * pos, axis=0, keepdims=True).astype(jnp.int32)
    dest_ref[...] = gsel * group_capacity + rank
    cnt_sc[...] = cnt_sc[...] + jnp.sum(onehot, axis=1, keepdims=True)
    cnt_ref[...] = cnt_sc[...].astype(jnp.int32)


DISPATCH_TILE = 512


SUBLANES = 8
_HALF = D_MODEL // 2


def _tile_copy(src_ref, src_tok, dst_ref, dst_tok, sem):
    src = src_ref.at[pl.ds(pl.multiple_of(src_tok * SUBLANES, SUBLANES), SUBLANES)]
    dst = dst_ref.at[pl.ds(pl.multiple_of(dst_tok * SUBLANES, SUBLANES), SUBLANES)]
    return pltpu.make_async_copy(src, dst, sem)


def _tile_rows(ref, j, n):
    return ref.at[pl.ds(j, n, stride=SUBLANES), :]


def _all_tiles_wait(src_ref, dst_ref, sem):
    pltpu.make_async_copy(src_ref, dst_ref.at[pl.ds(0, src_ref.shape[0])], sem).wait()


def _dispatch_kernel(zblk_ref, dest_ref, h_ref, w_ref, xbuf_ref, pay_ref, zero_ref, sems, zsem):
    i = pl.program_id(0)
    n_steps = pl.num_programs(0)
    slot = lax.rem(i, 2)
    tm = h_ref.shape[0]
    pay = pay_ref.at[slot]

    @pl.when(i == 0)
    def _():
        zero_ref[...] = jnp.zeros_like(zero_ref)
        for k in range(zblk_ref.shape[0]):
            @pl.when(zblk_ref[k] >= 0)
            def _():
                start = pl.multiple_of(zblk_ref[k] * zero_ref.shape[0], zero_ref.shape[0])
                cp = pltpu.make_async_copy(zero_ref, xbuf_ref.at[pl.ds(start, zero_ref.shape[0])], zsem)
                cp.start()
                cp.wait()

    @pl.when(i >= 2)
    def _():
        _all_tiles_wait(pay, xbuf_ref, sems.at[slot])

    hv = h_ref[...]
    for j in range(_HALF // LANES):
        lo, hi = hv[:, j * LANES:(j + 1) * LANES], hv[:, _HALF + j * LANES:_HALF + (j + 1) * LANES]
        _tile_rows(pay, j, tm)[...] = _pack_bf16_pair(lo, hi)
    _tile_rows(pay, _HALF // LANES, tm)[...] = pltpu.bitcast(w_ref[...], jnp.uint32)
    for j in range(_HALF // LANES + 1, SUBLANES):
        _tile_rows(pay, j, tm)[...] = jnp.zeros((tm, LANES), jnp.uint32)

    def issue(t, carry):
        _tile_copy(pay, t, xbuf_ref, dest_ref[0, t], sems.at[slot]).start()
        return carry

    lax.fori_loop(0, tm, issue, 0, unroll=8)

    @pl.when(i == n_steps - 1)
    def _():
        @pl.when(i >= 1)
        def _():
            _all_tiles_wait(pay_ref.at[1 - slot], xbuf_ref, sems.at[1 - slot])

        _all_tiles_wait(pay, xbuf_ref, sems.at[slot])


def _dispatch(h2d, wrow, dest, zero_blocks, n_blocks):
    t, d = h2d.shape
    tm = DISPATCH_TILE
    grid_spec = pltpu.PrefetchScalarGridSpec(
        num_scalar_prefetch=1,
        grid=(t // tm,),
        in_specs=[pl.BlockSpec((1, tm), lambda i, zb: (0, i), memory_space=pltpu.SMEM),
                  pl.BlockSpec((tm, d), lambda i, zb: (i, 0)),
                  pl.BlockSpec((tm, LANES), lambda i, zb: (i, 0))],
        out_specs=pl.BlockSpec(memory_space=pl.ANY),
        scratch_shapes=[pltpu.VMEM((2, tm * SUBLANES, LANES), jnp.uint32),
                        pltpu.VMEM((EXPERT_BLOCK * SUBLANES, LANES), jnp.uint32),
                        pltpu.SemaphoreType.DMA((2,)), pltpu.SemaphoreType.DMA(())],
    )
    return pl.pallas_call(
        _dispatch_kernel,
        grid_spec=grid_spec,
        out_shape=jax.ShapeDtypeStruct((n_blocks * EXPERT_BLOCK * SUBLANES, LANES), jnp.uint32),
        compiler_params=_cparams(("arbitrary",)),
        name="dispatch",
    )(zero_blocks, dest, h2d, wrow)


def _expert_kernel(bg_ref, nact_ref, x_ref, wg_ref, wu_ref, wd_ref, y_ref):
    active = pl.program_id(0) < nact_ref[0]
    pl.when(active)(functools.partial(_expert_block, x_ref, wg_ref, wu_ref, wd_ref, y_ref))

    @pl.when(jnp.logical_not(active))
    def _():
        y_ref[...] = jnp.zeros_like(y_ref)


def _expert_block(x_ref, wg_ref, wu_ref, wd_ref, y_ref):
    eb = x_ref.shape[0] // SUBLANES
    lo, hi = [], []
    for j in range(_HALF // LANES):
        lo_j, hi_j = _unpack_bf16_pair(_tile_rows(x_ref, j, eb)[...])
        lo.append(lo_j)
        hi.append(hi_j)
    x = jnp.concatenate(lo + hi, axis=-1).astype(BF16)
    w8 = pltpu.bitcast(_tile_rows(x_ref, _HALF // LANES, eb)[...], F32)
    y = jnp.zeros((eb, D_MODEL), F32)

    def gate_up(e):
        return jnp.dot(x, wg_ref[e], preferred_element_type=F32), jnp.dot(x, wu_ref[e], preferred_element_type=F32)

    nxt = gate_up(0)
    for e in range(EXPERTS_PER_GROUP):
        a, u = nxt
        if e + 1 < EXPERTS_PER_GROUP:
            nxt = gate_up(e + 1)
        we = w8[:, e:e + 1]
        hid = jnp.where(we != 0.0, jax.nn.silu(a) * u * we, 0.0).astype(BF16)
        y = y + jnp.dot(hid, wd_ref[e], preferred_element_type=F32)
    for j in range(SUBLANES):
        _tile_rows(y_ref, j, eb)[...] = y[:, j * LANES:(j + 1) * LANES]


def _experts(xbuf, block_g, n_active, wg, wu, wd):
    eb = EXPERT_BLOCK
    blk = pl.BlockSpec((eb * SUBLANES, LANES), lambda i, bg, nact: (i, 0))
    gw = lambda a: pl.BlockSpec((EXPERTS_PER_GROUP,) + a.shape[1:], lambda i, bg, nact: (bg[i], 0, 0))
    grid_spec = pltpu.PrefetchScalarGridSpec(
        num_scalar_prefetch=2,
        grid=(block_g.shape[0],),
        in_specs=[blk, gw(wg), gw(wu), gw(wd)],
        out_specs=blk,
    )
    return pl.pallas_call(
        _expert_kernel,
        grid_spec=grid_spec,
        out_shape=jax.ShapeDtypeStruct(xbuf.shape, F32),
        compiler_params=_cparams(("arbitrary",)),
        name="experts",
    )(block_g, n_active, xbuf, wg, wu, wd)


FINAL_TILE = 512


def _final_kernel(dest_ref, dest_next_ref, h_ref, g_ref, b_ref, ybuf_ref, o_ref, ys_ref, sems):
    i = pl.program_id(0)
    slot = lax.rem(i, 2)
    tm = h_ref.shape[0]

    def gather(d_ref, s):
        def issue(t, carry):
            _tile_copy(ybuf_ref, d_ref[0, t], ys_ref.at[s], t, sems.at[s]).start()
            return carry

        lax.fori_loop(0, tm, issue, 0, unroll=8)

    @pl.when(i == 0)
    def _():
        gather(dest_ref, slot)

    @pl.when(i + 1 < pl.num_programs(0))
    def _():
        gather(dest_next_ref, 1 - slot)

    ys = ys_ref.at[slot]
    pltpu.make_async_copy(ybuf_ref.at[pl.ds(0, ys.shape[0])], ys, sems.at[slot]).wait()
    y = jnp.concatenate([_tile_rows(ys, j, tm)[...] for j in range(SUBLANES)], axis=-1)
    o_ref[...] = _layer_norm(DN_ALPHA * h_ref[...] + y, g_ref[...], b_ref[...])


def _final(h2d, ybuf, dest, ln_g, ln_b):
    t, d = h2d.shape
    tm = FINAL_TILE
    n = t // tm
    row = pl.BlockSpec((tm, d), lambda i: (i, 0))
    vec = pl.BlockSpec((1, d), lambda i: (0, 0))
    return pl.pallas_call(
        _final_kernel,
        grid=(n,),
        in_specs=[pl.BlockSpec((1, tm), lambda i: (0, i), memory_space=pltpu.SMEM),
                  pl.BlockSpec((1, tm), lambda i: (0, jnp.minimum(i + 1, n - 1)), memory_space=pltpu.SMEM),
                  row, vec, vec, pl.BlockSpec(memory_space=pl.ANY)],
        out_specs=row,
        out_shape=jax.ShapeDtypeStruct((t, d), F32),
        scratch_shapes=[pltpu.VMEM((2, tm * SUBLANES, LANES), F32), pltpu.SemaphoreType.DMA((2,))],
        compiler_params=_cparams(("arbitrary",)),
        name="final",
    )(dest, dest, h2d, ln_g, ln_b, ybuf)


def kernel(x, mem, rel_bias, hgrn_lb_logits, w_in, w_mem_kv, hgrn_norm_g, w_branch_a, w_branch_b, w_branch_c, w_out, ln1_g, ln1_b, w_router_group, b_router_group, w_router_expert, b_router_expert, w_exp_gate, w_exp_up, w_exp_down, ln2_g, ln2_b):
    b, s, d = x.shape
    t = b * s
    assert d == D_MODEL and w_in.shape[0] == DEPTH == 1
    assert s % (BAND * DILATED_GROUPS[-1][1]) == 0 and s % HGRN_TILE == 0 and s % MERGE_TILE == 0 and t % EXPERT_BLOCK == 0
    li = 0
    scale = HEAD_DIM ** -0.5
    x2d = x.reshape(t, d)

    w = w_in[li]
    col_scale = np.ones((COLS_A,), np.float32).reshape(N_GROUPS_A, 3, WIDTH_A)
    col_scale[:, 0] = scale
    w_a = (w[:, :COLS_A] * jnp.asarray(col_scale.reshape(-1))).astype(BF16)
    w_b = w[:, COLS_A:COLS_A + COLS_B].astype(BF16)
    w_c = (w[:, COLS_A + COLS_B:COLS_A + COLS_B + COLS_C] * scale).astype(BF16)
    w_g = w[:, COLS_A + COLS_B + COLS_C:].astype(BF16)

    mkv = _matmul(mem.reshape(-1, d), w_mem_kv[li].astype(BF16), BF16, 256, 2 * WIDTH_C, "mem_kv")
    mkv = mkv.reshape(b, -1, 2 * WIDTH_C)
    mk, mv = mkv[..., :WIDTH_C], mkv[..., WIDTH_C:]

    oa_parts = []
    for gi, (_, dilation) in enumerate(DILATED_GROUPS):
        bias = _band_bias(rel_bias[:, gi * HEADS_A:(gi + 1) * HEADS_A], dilation)
        pg = _proj_classes(x, w_a[:, gi * 3 * WIDTH_A:(gi + 1) * 3 * WIDTH_A], dilation, f"proj_a_{gi}")
        oa_parts.append(_band_attention(pg, bias, f"band_attn_{gi}"))

    ob = _hgrn2(x, w_b, hgrn_lb_logits.astype(F32), hgrn_norm_g[li].reshape(1, WIDTH_B))

    wr = jnp.zeros((d, LANES), F32)
    wr = wr.at[:, :N_EXPERT_GROUPS].set(w_router_group[li]).at[:, 8:_ROUTER_ROWS].set(w_router_expert[li])
    br = jnp.zeros((_ROUTER_ROWS,), F32).at[:N_EXPERT_GROUPS].set(b_router_group[li]).at[8:].set(b_router_expert[li])
    br = jnp.broadcast_to(br[:, None], (_ROUTER_ROWS, 128))

    h, code, wrow, counts = _merge(
        x, oa_parts, ob, mk, mv, w_c, w_g, w_branch_a[li].astype(BF16), w_branch_b[li].astype(BF16),
        w_branch_c[li].astype(BF16), w_out[li].astype(BF16), ln1_g[li].reshape(1, d), ln1_b[li].reshape(1, d), wr, br)
    h2d = h.reshape(t, d)
    eb = EXPERT_BLOCK
    counts = counts[:N_EXPERT_GROUPS, 0]

    blocks_g = (counts + eb - 1) // eb
    ends = jnp.cumsum(blocks_g)
    starts = ends - blocks_g
    n_active = ends[-1:].astype(jnp.int32)
    n_blocks = t // eb + N_EXPERT_GROUPS
    blk = jnp.arange(n_blocks, dtype=jnp.int32)
    block_g = jnp.minimum(jnp.sum(ends[None, :] <= blk[:, None], axis=1), N_EXPERT_GROUPS - 1).astype(jnp.int32)
    g_of = code // t
    is_g = g_of == jnp.arange(N_EXPERT_GROUPS, dtype=jnp.int32)[:, None]
    dest = (code - g_of * t + jnp.sum(jnp.where(is_g, starts[:, None] * eb, 0), axis=0, keepdims=True)).astype(jnp.int32)
    trailing = n_active + jnp.arange(N_EXPERT_GROUPS, dtype=jnp.int32)
    zero_blocks = jnp.concatenate([jnp.where(blocks_g > 0, ends - 1, -1),
                                   jnp.where(trailing < n_blocks, trailing, -1)]).astype(jnp.int32)

    xbuf = _dispatch(h2d, wrow, dest, zero_blocks, n_blocks)
    ybuf = _experts(xbuf, block_g, n_active,
                    w_exp_gate[li].astype(BF16), w_exp_up[li].astype(BF16), w_exp_down[li].astype(BF16))
    out = _final(h2d, ybuf, dest, ln2_g[li].reshape(1, d), ln2_b[li].reshape(1, d))
    return out.reshape(b, s, d)
```

```python
import functools
import math

import numpy as np
import jax
import jax.numpy as jnp
from jax import lax
from jax.experimental import pallas as pl
from jax.experimental.pallas import tpu as pltpu

F32 = jnp.float32
BF16 = jnp.bfloat16

D_MODEL = 1024
DEPTH = 1
DILATED_GROUPS = ((128, 1), (512, 4), (2048, 16))
N_GROUPS_A = len(DILATED_GROUPS)
HEADS_A = 4
HEAD_DIM = 128
WIDTH_A = HEADS_A * HEAD_DIM
BAND = 128
NUM_BUCKETS = 32
MAX_DISTANCE = 2048
N_HEADS_B = 8
WIDTH_B = N_HEADS_B * HEAD_DIM
CHUNK_B = 64
N_HEADS_C = 4
WIDTH_C = N_HEADS_C * HEAD_DIM
COLS_A = N_GROUPS_A * 3 * WIDTH_A
COLS_B = 4 * WIDTH_B
COLS_C = WIDTH_C
COLS_GATE = 3 * D_MODEL
N_EXPERT_GROUPS = 4
EXPERTS_PER_GROUP = 8
N_EXPERTS = N_EXPERT_GROUPS * EXPERTS_PER_GROUP
TOP_K = 2
D_EXPERT = D_MODEL // 4
DN_ALPHA = (2 * DEPTH) ** 0.25
LN_EPS = 1e-5
RMS_EPS = 1e-6

NEG_BIG = -1e30
SAFE_LOG_DECAY = 80.0
VMEM_LIMIT = 56 * 1024 * 1024
EXPERT_BLOCK = 512

_NT = (((1,), (1,)), ((), ()))
_TN = (((0,), (0,)), ((), ()))


def _cparams(sem):
    return pltpu.CompilerParams(dimension_semantics=sem, vmem_limit_bytes=VMEM_LIMIT)


def _matmul_kernel(x_ref, w_ref, o_ref):
    o_ref[...] = jnp.dot(x_ref[...].astype(BF16), w_ref[...], preferred_element_type=F32).astype(o_ref.dtype)


def _matmul(x, w, out_dtype, tm, tn, name):
    t, k = x.shape
    n = w.shape[1]
    return pl.pallas_call(
        _matmul_kernel,
        grid=(n // tn, t // tm),
        in_specs=[pl.BlockSpec((tm, k), lambda j, i: (i, 0)), pl.BlockSpec((k, tn), lambda j, i: (0, j))],
        out_specs=pl.BlockSpec((tm, tn), lambda j, i: (i, j)),
        out_shape=jax.ShapeDtypeStruct((t, n), out_dtype),
        compiler_params=_cparams(("arbitrary", "arbitrary")),
        name=name,
    )(x, w)


def _t5_bucket_np(dist):
    dist = np.asarray(dist, np.int32)
    max_exact = NUM_BUCKETS // 2
    d = np.maximum(dist, 1).astype(np.float32)
    large = max_exact + (np.log(d / max_exact) / math.log(MAX_DISTANCE / max_exact) * (NUM_BUCKETS - max_exact)).astype(np.int32)
    large = np.minimum(large, NUM_BUCKETS - 1)
    return np.where(dist < max_exact, dist, large).astype(np.int32)


def _band_bias(bias_tab, dilation):
    i = np.arange(BAND)[:, None]
    j = np.arange(2 * BAND)[None, :]
    u = i + BAND - j
    in_band = (u >= 0) & (u <= BAND)
    bucket = _t5_bucket_np(np.clip(u, 0, BAND) * dilation)
    onehot = np.asarray(bucket[:, :, None] == np.arange(NUM_BUCKETS)[None, None, :], np.float32)
    bias = jnp.einsum('pqb,bh->hpq', jnp.asarray(onehot), bias_tab.astype(F32), precision=lax.Precision.HIGHEST)
    general = jnp.where(in_band[None], bias, NEG_BIG)
    first = jnp.where((in_band & (j >= BAND))[None], bias, NEG_BIG)
    return jnp.stack([first, general])


LANES = 128


def _proj_perm_kernel(*refs, r):
    n_slab = D_MODEL // LANES
    x_refs, (w_ref, o_ref, xp_ref) = refs[:n_slab], refs[n_slab:]
    tm = xp_ref.shape[0]
    n = tm // r
    for j, x_ref in enumerate(x_refs):
        for c in range(r):
            rows = x_ref[pl.ds(c, n, stride=r), :] if r > 1 else x_ref[...]
            xp_ref[c * n:(c + 1) * n, j * LANES:(j + 1) * LANES] = rows.astype(BF16)
    res = jnp.dot(xp_ref[...], w_ref[...], preferred_element_type=F32).astype(o_ref.dtype)
    for c in range(r):
        o_ref[c] = res[c * n:(c + 1) * n]


PROJ_A_TILE = 1024


def _proj_classes(x, w, r, name):
    b, s, d = x.shape
    n = w.shape[1]
    tm = PROJ_A_TILE
    n_slab = d // LANES
    slabs = [pl.BlockSpec((None, tm, LANES), lambda bi, t, j=j: (bi, t, j)) for j in range(n_slab)]
    return pl.pallas_call(
        functools.partial(_proj_perm_kernel, r=r),
        grid=(b, s // tm),
        in_specs=slabs + [pl.BlockSpec((d, n), lambda bi, t: (0, 0))],
        out_specs=pl.BlockSpec((None, r, tm // r, n), lambda bi, t: (bi, 0, t, 0)),
        out_shape=jax.ShapeDtypeStruct((b, r, s // r, n), BF16),
        scratch_shapes=[pltpu.VMEM((tm, d), BF16)],
        compiler_params=_cparams(("arbitrary", "arbitrary")),
        name=name,
    )(*([x] * n_slab), w)


BAND_SPAN = 2048
_HI_MASK = 0xFFFF0000
_LSE_LANES = LANES // HEADS_A


def _pack_bf16_pair(a, b):
    abits = pltpu.bitcast(a.astype(BF16).astype(F32), jnp.uint32)
    bbits = pltpu.bitcast(b.astype(BF16).astype(F32), jnp.uint32)
    return (abits >> 16) | (bbits & jnp.uint32(_HI_MASK))


def _unpack_bf16_pair(word):
    return pltpu.bitcast(word << 16, F32), pltpu.bitcast(word & jnp.uint32(_HI_MASK), F32)


def _band_attn_kernel(q_ref, k_ref, v_ref, kp_ref, vp_ref, bias_ref, o_ref, lse_ref, *, r):
    nq = q_ref.shape[1] // BAND
    variant = jnp.minimum(pl.program_id(1), 1)
    ones = jnp.ones((BAND, HEAD_DIM), BF16)
    lane = lax.broadcasted_iota(jnp.int32, (BAND, LANES), 1)
    heads = [slice(h * HEAD_DIM, (h + 1) * HEAD_DIM) for h in range(HEADS_A)]
    units = [(c, j) for c in range(r) for j in range(nq)]

    def prev_of(k_or_v_ref, prev_ref, c, j, sl):
        return prev_ref[c, :, sl] if j == 0 else k_or_v_ref[c, (j - 1) * BAND:j * BAND, sl]

    def scores(c, j):
        cur = slice(j * BAND, (j + 1) * BAND)
        out = []
        for h, sl in enumerate(heads):
            q = q_ref[c, cur, sl]
            bias = bias_ref[variant, h] if j == 0 else bias_ref[1, h]
            sp = lax.dot_general(q, prev_of(k_ref, kp_ref, c, j, sl), _NT, preferred_element_type=F32)
            sc = lax.dot_general(q, k_ref[c, cur, sl], _NT, preferred_element_type=F32)
            out.append((sp + bias[:, :BAND], sc + bias[:, BAND:]))
        return out

    nxt = scores(*units[0])
    for u, (c, j) in enumerate(units):
        s_all = nxt
        if u + 1 < len(units):
            nxt = scores(*units[u + 1])
        cur = slice(j * BAND, (j + 1) * BAND)
        rows = pl.ds(j * BAND * r + c, BAND, stride=r) if r > 1 else cur
        probs = []
        for sp, sc in s_all:
            m = jnp.max(jnp.maximum(sp, sc), axis=-1, keepdims=True)
            probs.append((m, jnp.exp(sp - m).astype(BF16), jnp.exp(sc - m).astype(BF16)))
        outs, lses = [], []
        for h, sl in enumerate(heads):
            m, pp, pc = probs[h]
            l = jnp.dot(pp, ones, preferred_element_type=F32) + jnp.dot(pc, ones, preferred_element_type=F32)
            o = jnp.dot(pp, prev_of(v_ref, vp_ref, c, j, sl), preferred_element_type=F32)
            o = o + jnp.dot(pc, v_ref[c, cur, sl], preferred_element_type=F32)
            outs.append(o / l)
            lses.append(m + jnp.log(l))
        for p in range(HEADS_A // 2):
            o_ref[p, rows, :] = _pack_bf16_pair(outs[2 * p], outs[2 * p + 1])
        lse = lses[-1]
        for h in range(HEADS_A - 2, -1, -1):
            lse = jnp.where(lane < (h + 1) * _LSE_LANES, lses[h], lse)
        lse_ref[rows, :] = lse


def _band_attention(pg, bias, name):
    b, r, l, _ = pg.shape
    nq = BAND_SPAN // (BAND * r)
    cur = lambda which: pl.BlockSpec((None, r, nq * BAND, WIDTH_A), lambda bi, n: (bi, 0, n, which))
    prev = lambda which: pl.BlockSpec((None, r, BAND, WIDTH_A), lambda bi, n: (bi, 0, jnp.maximum(n * nq - 1, 0), which))
    return pl.pallas_call(
        functools.partial(_band_attn_kernel, r=r),
        grid=(b, l * r // BAND_SPAN),
        in_specs=[cur(0), cur(1), cur(2), prev(1), prev(2), pl.BlockSpec(bias.shape, lambda bi, n: (0, 0, 0, 0))],
        out_specs=[pl.BlockSpec((None, HEADS_A // 2, BAND_SPAN, HEAD_DIM), lambda bi, n: (bi, 0, n, 0)),
                   pl.BlockSpec((None, BAND_SPAN, LANES), lambda bi, n: (bi, n, 0))],
        out_shape=[jax.ShapeDtypeStruct((b, HEADS_A // 2, l * r, HEAD_DIM), jnp.uint32),
                   jax.ShapeDtypeStruct((b, l * r, LANES), F32)],
        compiler_params=_cparams(("arbitrary", "arbitrary")),
        name=name,
    )(pg, pg, pg, pg, pg, bias)


HGRN_TILE = 512
_N_CHUNK = HGRN_TILE // CHUNK_B


def _dynamic_row(ref, r, ls):
    blk = ref[pl.ds(pl.multiple_of((r // 8) * 8, 8), 8), ls]
    sub = lax.broadcasted_iota(jnp.int32, blk.shape, 0)
    return jnp.sum(jnp.where(sub == r % 8, blk, 0.0), axis=0, keepdims=True)


def _sigmoid(z):
    return 0.5 * jnp.tanh(0.5 * z) + 0.5


_PROJ_PIECE = 256


def _hgrn_kernel(x_ref, xn_ref, w_ref, lbl_ref, ng_ref, o_ref, proj_ref, xb_ref,
                 st_ref, qs_ref, ks_ref, kh_ref, vs_ref, bs_ref, ebl_ref, a_ref, os_ref, *, tiles_per_seq):
    i = pl.program_id(0)
    slot = lax.rem(i, 2)
    pr = proj_ref.at[slot]
    pn = proj_ref.at[1 - slot]

    def piece(src_ref, dst_ref, k):
        cols = slice(k * _PROJ_PIECE, (k + 1) * _PROJ_PIECE)
        dst_ref[:, cols] = jnp.dot(src_ref[...], w_ref[:, cols], preferred_element_type=F32)

    @pl.when(i == 0)
    def _():
        xb_ref[...] = x_ref[...].astype(BF16)
        for k in range(COLS_B // _PROJ_PIECE):
            piece(xb_ref, pr, k)

    @pl.when(lax.rem(i, tiles_per_seq) == 0)
    def _():
        st_ref[...] = jnp.zeros_like(st_ref)

    xb_ref[...] = xn_ref[...].astype(BF16)
    pieces = list(range(COLS_B // _PROJ_PIECE))
    next_piece = lambda: piece(xb_ref, pn, pieces.pop(0)) if pieces else None
    q_cols, f_cols, i_cols, g_cols = (slice(p * WIDTH_B, (p + 1) * WIDTH_B) for p in range(4))

    lg = lbl_ref[...]
    e = jnp.exp(lg - jnp.max(lg, axis=0, keepdims=True))
    lb = e[0:1, :] / jnp.sum(e, axis=0, keepdims=True)

    row = lax.broadcasted_iota(jnp.int32, (CHUNK_B, CHUNK_B), 0)
    colm = lax.broadcasted_iota(jnp.int32, (CHUNK_B, CHUNK_B), 1)
    causal = row >= colm
    tri = causal.astype(BF16)

    next_piece()
    f = lb + (1.0 - lb) * _sigmoid(pr[:, f_cols])
    logf = jnp.log(f)
    kf = 1.0 - f
    l_hi = logf.astype(BF16)
    rem = logf - l_hi.astype(F32)
    l_mid = rem.astype(BF16)
    l_lo = (rem - l_mid.astype(F32)).astype(BF16)
    min_b = None
    for c in range(_N_CHUNK):
        rs = slice(c * CHUNK_B, (c + 1) * CHUNK_B)
        b = (jnp.dot(tri, l_hi[rs], preferred_element_type=F32) + jnp.dot(tri, l_mid[rs], preferred_element_type=F32)
             + jnp.dot(tri, l_lo[rs], preferred_element_type=F32))
        bs_ref[rs, :] = b
        cm = jnp.min(b[CHUNK_B - 1:CHUNK_B, :])
        min_b = cm if min_b is None else jnp.minimum(min_b, cm)
        if c % 2 == 1:
            next_piece()
    safe = min_b > -SAFE_LOG_DECAY

    qv = pr[:, q_cols]
    qsil = qv * _sigmoid(qv)
    for c in range(_N_CHUNK):
        rs = slice(c * CHUNK_B, (c + 1) * CHUNK_B)
        b = bs_ref[rs, :]
        eb_last = jnp.exp(b[CHUNK_B - 1:CHUNK_B, :])
        ebl_ref[c:c + 1, :] = eb_last
        qs_ref[rs, :] = (qsil[rs] * jnp.exp(b)).astype(BF16)
        k_grown = kf[rs] * jnp.exp(-b)
        ks_ref[rs, :] = k_grown.astype(BF16)
        kh_ref[rs, :] = (k_grown * eb_last).astype(BF16)
        next_piece()
    vs_ref[...] = pr[:, i_cols].astype(BF16)

    @pl.when(safe)
    def _():
        for c in range(_N_CHUNK):
            rs = slice(c * CHUNK_B, (c + 1) * CHUNK_B)
            for h in range(N_HEADS_B):
                ls = slice(h * HEAD_DIM, (h + 1) * HEAD_DIM)
                a = lax.dot_general(qs_ref[rs, ls], ks_ref[rs, ls], _NT, preferred_element_type=F32)
                a_ref[c * N_HEADS_B + h] = jnp.where(causal, a, 0.0)

    @pl.when(jnp.logical_not(safe))
    def _():
        lane = lax.broadcasted_iota(jnp.int32, (1, CHUNK_B), 1)
        trow = lax.broadcasted_iota(jnp.int32, (CHUNK_B, 1), 0)
        for c in range(_N_CHUNK):
            rs = slice(c * CHUNK_B, (c + 1) * CHUNK_B)
            b = bs_ref[rs, :]
            f_c = lb + (1.0 - lb) * _sigmoid(pr[rs, f_cols])
            kh_ref[rs, :] = ((1.0 - f_c) * jnp.exp(b[CHUNK_B - 1:CHUNK_B, :] - b)).astype(BF16)
            for h in range(N_HEADS_B):
                ls = slice(h * HEAD_DIM, (h + 1) * HEAD_DIM)
                fls = slice(WIDTH_B + h * HEAD_DIM, WIDTH_B + (h + 1) * HEAD_DIM)
                bq = bs_ref[rs, ls]
                qh = pr[rs, ls]
                qh = qh * _sigmoid(qh)

                def body(s, acc, c=c, ls=ls, fls=fls, bq=bq, qh=qh):
                    b_s = _dynamic_row(bs_ref, c * CHUNK_B + s, ls)
                    f_s = lb[:, ls] + (1.0 - lb[:, ls]) * _sigmoid(_dynamic_row(pr, c * CHUNK_B + s, fls))
                    w = jnp.exp(jnp.minimum(bq - b_s, 0.0))
                    colv = jnp.sum(qh * (1.0 - f_s) * w, axis=-1, keepdims=True)
                    colv = jnp.where(trow >= s, colv, 0.0)
                    return acc + colv * (lane == s).astype(F32)

                a_ref[c * N_HEADS_B + h] = lax.fori_loop(0, CHUNK_B, body, jnp.zeros((CHUNK_B, CHUNK_B), F32))

    for c in range(_N_CHUNK):
        rs = slice(c * CHUNK_B, (c + 1) * CHUNK_B)
        for h in range(N_HEADS_B):
            ls = slice(h * HEAD_DIM, (h + 1) * HEAD_DIM)
            st = st_ref[h]
            vh = vs_ref[rs, ls]
            o = lax.dot_general(qs_ref[rs, ls], st.astype(BF16), _NT, preferred_element_type=F32)
            o = o + jnp.dot(a_ref[c * N_HEADS_B + h].astype(BF16), vh, preferred_element_type=F32)
            os_ref[rs, ls] = o
            st_ref[h] = st * ebl_ref[c:c + 1, ls] + lax.dot_general(vh, kh_ref[rs, ls], _TN, preferred_element_type=F32)
            if h == N_HEADS_B - 1:
                next_piece()
    while pieces:
        next_piece()

    ng = ng_ref[...]
    for h in range(N_HEADS_B):
        ls = slice(h * HEAD_DIM, (h + 1) * HEAD_DIM)
        gls = slice(3 * WIDTH_B + h * HEAD_DIM, 3 * WIDTH_B + (h + 1) * HEAD_DIM)
        o = os_ref[:, ls]
        o = o * lax.rsqrt(jnp.mean(o * o, axis=-1, keepdims=True) + RMS_EPS)
        gv = pr[:, gls]
        o_ref[:, ls] = (o * ng[:, ls] * (gv * _sigmoid(gv))).astype(o_ref.dtype)


def _hgrn2(x, w_b, lb_logits, norm_g):
    b, s, d = x.shape
    tb = HGRN_TILE
    n = b * s // tb
    n_slots = lb_logits.shape[0]
    out = pl.pallas_call(
        functools.partial(_hgrn_kernel, tiles_per_seq=s // tb),
        grid=(n,),
        in_specs=[pl.BlockSpec((tb, d), lambda i: (i, 0)),
                  pl.BlockSpec((tb, d), lambda i: (jnp.minimum(i + 1, n - 1), 0)),
                  pl.BlockSpec(w_b.shape, lambda i: (0, 0), pipeline_mode=pl.Buffered(1)),
                  pl.BlockSpec((n_slots, WIDTH_B), lambda i: (0, 0)),
                  pl.BlockSpec((1, WIDTH_B), lambda i: (0, 0))],
        out_specs=pl.BlockSpec((tb, WIDTH_B), lambda i: (i, 0)),
        out_shape=jax.ShapeDtypeStruct((b * s, WIDTH_B), BF16),
        scratch_shapes=[
            pltpu.VMEM((2, tb, COLS_B), F32),
            pltpu.VMEM((tb, d), BF16),
            pltpu.VMEM((N_HEADS_B, HEAD_DIM, HEAD_DIM), F32),
            pltpu.VMEM((tb, WIDTH_B), BF16),
            pltpu.VMEM((tb, WIDTH_B), BF16),
            pltpu.VMEM((tb, WIDTH_B), BF16),
            pltpu.VMEM((tb, WIDTH_B), BF16),
            pltpu.VMEM((tb, WIDTH_B), F32),
            pltpu.VMEM((8, WIDTH_B), F32),
            pltpu.VMEM((_N_CHUNK * N_HEADS_B, CHUNK_B, CHUNK_B), F32),
            pltpu.VMEM((tb, WIDTH_B), F32),
        ],
        compiler_params=_cparams(("arbitrary",)),
        name="hgrn2",
    )(x.reshape(b * s, d), x.reshape(b * s, d), w_b, lb_logits, norm_g)
    return out.reshape(b, s, WIDTH_B)


MERGE_TILE = 512


def _layer_norm(y, g, b):
    mu = jnp.mean(y, axis=-1, keepdims=True)
    d = y - mu
    var = jnp.mean(d * d, axis=-1, keepdims=True)
    return d * lax.rsqrt(var + LN_EPS) * g + b


def _merge_kernel(x_ref, o1_ref, l1_ref, o2_ref, l2_ref, o3_ref, l3_ref, ob_ref, mk_ref, mv_ref,
                  wc_ref, wg_ref, wa_ref, wb_ref, wcc_ref, wo_ref, g_ref, b_ref, wr_ref, br_ref,
                  h_ref, code_ref, wrow_ref, cnt_ref, oc_ref, cnt_sc, *, n_tokens):
    x = x_ref[...]
    xb = x.astype(BF16)

    qc = jnp.dot(xb, wc_ref[...], preferred_element_type=F32).astype(BF16)
    heads_c = [slice(h * HEAD_DIM, (h + 1) * HEAD_DIM) for h in range(N_HEADS_C)]
    s_all = [lax.dot_general(qc[:, ls], mk_ref[:, ls], _NT, preferred_element_type=F32) for ls in heads_c]
    p_all = [jnp.exp(s - jnp.max(s, axis=-1, keepdims=True)).astype(BF16) for s in s_all]
    ones = jnp.ones((mk_ref.shape[0], HEAD_DIM), BF16)
    for ls, p in zip(heads_c, p_all):
        l = jnp.dot(p, ones, preferred_element_type=F32)
        oc_ref[:, ls] = (jnp.dot(p, mv_ref[:, ls], preferred_element_type=F32) / l).astype(BF16)

    lses = [l1_ref[...], l2_ref[...], l3_ref[...]]
    m = functools.reduce(jnp.maximum, lses)
    es = [jnp.exp(l - m) for l in lses]
    inv = 1.0 / functools.reduce(lambda a, c: a + c, es)
    mix = [e * inv for e in es]
    o_pairs = [[_unpack_bf16_pair(o_ref[p]) for p in range(HEADS_A // 2)] for o_ref in (o1_ref, o2_ref, o3_ref)]
    oa_heads = []
    for h in range(HEADS_A):
        acc = None
        for gi in range(N_GROUPS_A):
            w = jnp.broadcast_to(mix[gi][:, h * _LSE_LANES:h * _LSE_LANES + 1], (x.shape[0], HEAD_DIM))
            term = w * o_pairs[gi][h // 2][h % 2]
            acc = term if acc is None else acc + term
        oa_heads.append(acc.astype(BF16))
    oa = jnp.concatenate(oa_heads, axis=-1)

    gates = jax.nn.sigmoid(jnp.dot(xb, wg_ref[...], preferred_element_type=F32))
    merged = gates[:, :D_MODEL] * jnp.dot(oa, wa_ref[...], preferred_element_type=F32)
    merged = merged + gates[:, D_MODEL:2 * D_MODEL] * jnp.dot(ob_ref[...], wb_ref[...], preferred_element_type=F32)
    merged = merged + gates[:, 2 * D_MODEL:] * jnp.dot(oc_ref[...], wcc_ref[...], preferred_element_type=F32)
    y = DN_ALPHA * x + jnp.dot(merged.astype(BF16), wo_ref[...], preferred_element_type=F32)
    hv = _layer_norm(y, g_ref[...], b_ref[...])
    h_ref[...] = hv

    first = jnp.logical_and(pl.program_id(0) == 0, pl.program_id(1) == 0)
    _route(hv, wr_ref, br_ref, code_ref, wrow_ref, cnt_ref, cnt_sc, first, n_tokens)


def _merge(x, oa_parts, ob, mk, mv, wc, wg, wa, wb, wcc, wo, ln_g, ln_b, wr, br):
    b, s, d = x.shape
    tm = MERGE_TILE
    nt = s // tm
    tok = lambda w: pl.BlockSpec((None, tm, w), lambda bi, t: (bi, t, 0))
    full = lambda a: pl.BlockSpec(a.shape, lambda bi, t: (0,) * a.ndim, pipeline_mode=pl.Buffered(1))
    mem = pl.BlockSpec((None,) + mk.shape[1:], lambda bi, t: (bi, 0, 0))
    o_spec = pl.BlockSpec((None, HEADS_A // 2, tm, HEAD_DIM), lambda bi, t: (bi, 0, t, 0))
    flat = [a for pair in oa_parts for a in pair]
    return pl.pallas_call(
        functools.partial(_merge_kernel, n_tokens=b * s),
        grid=(b, nt),
        in_specs=[tok(d)] + [o_spec, tok(LANES)] * N_GROUPS_A + [tok(WIDTH_B), mem, mem,
                  full(wc), full(wg), full(wa), full(wb), full(wcc), full(wo), full(ln_g), full(ln_b), full(wr), full(br)],
        out_specs=[tok(d), pl.BlockSpec((1, tm), lambda bi, t: (0, bi * nt + t)),
                   pl.BlockSpec((tm, LANES), lambda bi, t: (bi * nt + t, 0)),
                   pl.BlockSpec((N_ROUTE_CLASSES, LANES), lambda bi, t: (0, 0))],
        out_shape=[jax.ShapeDtypeStruct((b, s, d), F32), jax.ShapeDtypeStruct((1, b * s), jnp.int32),
                   jax.ShapeDtypeStruct((b * s, LANES), F32), jax.ShapeDtypeStruct((N_ROUTE_CLASSES, LANES), jnp.int32)],
        scratch_shapes=[pltpu.VMEM((tm, WIDTH_C), BF16), pltpu.VMEM((N_ROUTE_CLASSES, LANES), F32)],
        compiler_params=_cparams(("arbitrary", "arbitrary")),
        name="merge",
    )(x, *flat, ob, mk, mv, wc, wg, wa, wb, wcc, wo, ln_g, ln_b, wr, br)


N_ROUTE_CLASSES = N_EXPERT_GROUPS * EXPERTS_PER_GROUP * EXPERTS_PER_GROUP
_ROUTER_ROWS = 8 + N_EXPERTS


def _route(hv, wr_ref, br_ref, dest_ref, wrow_ref, cnt_ref, cnt_sc, first_step, group_capacity):
    tm = hv.shape[0]

    @pl.when(first_step)
    def _():
        cnt_sc[...] = jnp.zeros_like(cnt_sc)

    h_hi = hv.astype(BF16)
    h_lo = (hv - h_hi.astype(F32)).astype(BF16)
    wv = wr_ref[...]
    w_hi = wv.astype(BF16)
    w_lo = (wv - w_hi.astype(F32)).astype(BF16)
    both = jnp.dot(h_hi, jnp.concatenate([w_hi, w_lo], axis=1), preferred_element_type=F32)
    logits = both[:, :LANES] + both[:, LANES:] + jnp.dot(h_lo, w_hi, preferred_element_type=F32)
    logits = logits.T[:_ROUTER_ROWS, :] + br_ref[:, 0:1]
    g = [logits[i:i + 1, :] for i in range(N_EXPERT_GROUPS)]
    gmax = functools.reduce(jnp.maximum, g)
    gsel = jnp.full_like(gmax, N_EXPERT_GROUPS - 1).astype(jnp.int32)
    for i in range(N_EXPERT_GROUPS - 2, -1, -1):
        gsel = jnp.where(g[i] == gmax, i, gsel)
    gprob = 1.0 / functools.reduce(lambda a, c: a + c, [jnp.exp(gi - gmax) for gi in g])

    esel = logits[8 + (N_EXPERT_GROUPS - 1) * EXPERTS_PER_GROUP:8 + N_EXPERTS, :]
    for i in range(N_EXPERT_GROUPS - 2, -1, -1):
        esel = jnp.where(gsel == i, logits[8 + i * EXPERTS_PER_GROUP:8 + (i + 1) * EXPERTS_PER_GROUP, :], esel)
    ridx = lax.broadcasted_iota(jnp.int32, (EXPERTS_PER_GROUP, tm), 0)
    v1 = jnp.max(esel, axis=0, keepdims=True)
    i1 = jnp.min(jnp.where(esel == v1, ridx, EXPERTS_PER_GROUP), axis=0, keepdims=True)
    rest = jnp.where(ridx == i1, -jnp.inf, esel)
    v2 = jnp.max(rest, axis=0, keepdims=True)
    i2 = jnp.min(jnp.where(rest == v2, ridx, EXPERTS_PER_GROUP), axis=0, keepdims=True)
    t = jnp.exp(v2 - v1)
    w1 = gprob / (1.0 + t)
    w2 = gprob * t / (1.0 + t)
    wt = jnp.where(ridx == i1, w1, jnp.where(ridx == i2, w2, 0.0))
    eye = (lax.broadcasted_iota(jnp.int32, (EXPERTS_PER_GROUP, LANES), 0)
           == lax.broadcasted_iota(jnp.int32, (EXPERTS_PER_GROUP, LANES), 1)).astype(F32)
    wrow_ref[...] = lax.dot_general(wt, eye, _TN, preferred_element_type=F32, precision=lax.Precision.HIGHEST)

    cls = (gsel * EXPERTS_PER_GROUP + jnp.minimum(i1, i2)) * EXPERTS_PER_GROUP + jnp.maximum(i1, i2)
    crow = lax.broadcasted_iota(jnp.int32, (N_ROUTE_CLASSES, tm), 0)
    onehot = (crow == cls).astype(F32)
    before = (lax.broadcasted_iota(jnp.int32, (tm, tm), 0) < lax.broadcasted_iota(jnp.int32, (tm, tm), 1))
    excl = jnp.dot(onehot.astype(BF16), before.astype(BF16), preferred_element_type=F32)
    pos = cnt_sc[:, 0:1] + excl
    rank = jnp.sum(onehot * pos, axis=0, keepdims=True).astype(jnp.int32)
    dest_ref[...] = cls * group_capacity + rank
    cnt_sc[...] = cnt_sc[...] + jnp.sum(onehot, axis=1, keepdims=True)
    cnt_ref[...] = cnt_sc[...].astype(jnp.int32)


DISPATCH_TILE = 512


SUBLANES = 8
_HALF = D_MODEL // 2


def _tile_copy(src_ref, src_tok, dst_ref, dst_tok, sem):
    src = src_ref.at[pl.ds(pl.multiple_of(src_tok * SUBLANES, SUBLANES), SUBLANES)]
    dst = dst_ref.at[pl.ds(pl.multiple_of(dst_tok * SUBLANES, SUBLANES), SUBLANES)]
    return pltpu.make_async_copy(src, dst, sem)


def _tile_rows(ref, j, n):
    return ref.at[pl.ds(j, n, stride=SUBLANES), :]


def _all_tiles_wait(src_ref, dst_ref, sem):
    pltpu.make_async_copy(src_ref, dst_ref.at[pl.ds(0, src_ref.shape[0])], sem).wait()


def _dispatch_kernel(zblk_ref, dest_ref, h_ref, w_ref, xbuf_ref, pay_ref, zero_ref, sems, zsem):
    i = pl.program_id(0)
    n_steps = pl.num_programs(0)
    slot = lax.rem(i, 2)
    tm = h_ref.shape[0]
    pay = pay_ref.at[slot]

    @pl.when(i == 0)
    def _():
        zero_ref[...] = jnp.zeros_like(zero_ref)
        for k in range(zblk_ref.shape[0]):
            @pl.when(zblk_ref[k] >= 0)
            def _():
                start = pl.multiple_of(zblk_ref[k] * zero_ref.shape[0], zero_ref.shape[0])
                cp = pltpu.make_async_copy(zero_ref, xbuf_ref.at[pl.ds(start, zero_ref.shape[0])], zsem)
                cp.start()
                cp.wait()

    @pl.when(i >= 2)
    def _():
        _all_tiles_wait(pay, xbuf_ref, sems.at[slot])

    hv = h_ref[...]
    for j in range(_HALF // LANES):
        lo, hi = hv[:, j * LANES:(j + 1) * LANES], hv[:, _HALF + j * LANES:_HALF + (j + 1) * LANES]
        _tile_rows(pay, j, tm)[...] = _pack_bf16_pair(lo, hi)
    _tile_rows(pay, _HALF // LANES, tm)[...] = pltpu.bitcast(w_ref[...], jnp.uint32)
    for j in range(_HALF // LANES + 1, SUBLANES):
        _tile_rows(pay, j, tm)[...] = jnp.zeros((tm, LANES), jnp.uint32)

    def issue(t, carry):
        _tile_copy(pay, t, xbuf_ref, dest_ref[0, t], sems.at[slot]).start()
        return carry

    lax.fori_loop(0, tm, issue, 0, unroll=8)

    @pl.when(i == n_steps - 1)
    def _():
        @pl.when(i >= 1)
        def _():
            _all_tiles_wait(pay_ref.at[1 - slot], xbuf_ref, sems.at[1 - slot])

        _all_tiles_wait(pay, xbuf_ref, sems.at[slot])


def _dispatch(h2d, wrow, dest, zero_blocks, n_blocks):
    t, d = h2d.shape
    tm = DISPATCH_TILE
    grid_spec = pltpu.PrefetchScalarGridSpec(
        num_scalar_prefetch=1,
        grid=(t // tm,),
        in_specs=[pl.BlockSpec((1, tm), lambda i, zb: (0, i), memory_space=pltpu.SMEM),
                  pl.BlockSpec((tm, d), lambda i, zb: (i, 0)),
                  pl.BlockSpec((tm, LANES), lambda i, zb: (i, 0))],
        out_specs=pl.BlockSpec(memory_space=pl.ANY),
        scratch_shapes=[pltpu.VMEM((2, tm * SUBLANES, LANES), jnp.uint32),
                        pltpu.VMEM((EXPERT_BLOCK * SUBLANES, LANES), jnp.uint32),
                        pltpu.SemaphoreType.DMA((2,)), pltpu.SemaphoreType.DMA(())],
    )
    return pl.pallas_call(
        _dispatch_kernel,
        grid_spec=grid_spec,
        out_shape=jax.ShapeDtypeStruct((n_blocks * EXPERT_BLOCK * SUBLANES, LANES), jnp.uint32),
        compiler_params=_cparams(("arbitrary",)),
        name="dispatch",
    )(zero_blocks, dest, h2d, wrow)


def _expert_kernel(bg_ref, pres_ref, x_ref, wg_ref, wu_ref, wd_ref, y_ref, xs_ref, ws_ref, ys_ref):
    i = pl.program_id(0)
    eb = xs_ref.shape[0]
    lo, hi = [], []
    for j in range(_HALF // LANES):
        lo_j, hi_j = _unpack_bf16_pair(_tile_rows(x_ref, j, eb)[...])
        lo.append(lo_j)
        hi.append(hi_j)
    xs_ref[...] = jnp.concatenate(lo + hi, axis=-1).astype(BF16)
    ws_ref[...] = pltpu.bitcast(_tile_rows(x_ref, _HALF // LANES, eb)[...], F32)
    ys_ref[...] = jnp.zeros_like(ys_ref)
    halves = [slice(0, eb // 2), slice(eb // 2, eb)]

    for e in range(EXPERTS_PER_GROUP):
        @pl.when(pres_ref[i * EXPERTS_PER_GROUP + e] > 0)
        def _(e=e):
            gu = [(jnp.dot(xs_ref[p, :], wg_ref[e], preferred_element_type=F32),
                   jnp.dot(xs_ref[p, :], wu_ref[e], preferred_element_type=F32)) for p in halves]
            for p, (a, u) in zip(halves, gu):
                we = ws_ref[p, e:e + 1]
                hid = jnp.where(we != 0.0, a * _sigmoid(a) * u * we, 0.0).astype(BF16)
                ys_ref[p, :] += jnp.dot(hid, wd_ref[e], preferred_element_type=F32)

    for j in range(SUBLANES):
        _tile_rows(y_ref, j, eb)[...] = ys_ref[:, j * LANES:(j + 1) * LANES]


def _experts(xbuf, block_g, present, wg, wu, wd):
    eb = EXPERT_BLOCK
    blk = pl.BlockSpec((eb * SUBLANES, LANES), lambda i, bg, pres: (i, 0))
    gw = lambda a: pl.BlockSpec((EXPERTS_PER_GROUP,) + a.shape[1:], lambda i, bg, pres: (bg[i], 0, 0))
    grid_spec = pltpu.PrefetchScalarGridSpec(
        num_scalar_prefetch=2,
        grid=(block_g.shape[0],),
        in_specs=[blk, gw(wg), gw(wu), gw(wd)],
        out_specs=blk,
        scratch_shapes=[pltpu.VMEM((eb, D_MODEL), BF16), pltpu.VMEM((eb, LANES), F32), pltpu.VMEM((eb, D_MODEL), F32)],
    )
    return pl.pallas_call(
        _expert_kernel,
        grid_spec=grid_spec,
        out_shape=jax.ShapeDtypeStruct(xbuf.shape, F32),
        compiler_params=_cparams(("arbitrary",)),
        name="experts",
    )(block_g, present, xbuf, wg, wu, wd)


FINAL_TILE = 512


def _final_kernel(dest_ref, dest_next_ref, h_ref, g_ref, b_ref, ybuf_ref, o_ref, ys_ref, sems):
    i = pl.program_id(0)
    slot = lax.rem(i, 2)
    tm = h_ref.shape[0]

    def gather(d_ref, s):
        def issue(t, carry):
            _tile_copy(ybuf_ref, d_ref[0, t], ys_ref.at[s], t, sems.at[s]).start()
            return carry

        lax.fori_loop(0, tm, issue, 0, unroll=8)

    @pl.when(i == 0)
    def _():
        gather(dest_ref, slot)

    @pl.when(i + 1 < pl.num_programs(0))
    def _():
        gather(dest_next_ref, 1 - slot)

    ys = ys_ref.at[slot]
    pltpu.make_async_copy(ybuf_ref.at[pl.ds(0, ys.shape[0])], ys, sems.at[slot]).wait()
    y = jnp.concatenate([_tile_rows(ys, j, tm)[...] for j in range(SUBLANES)], axis=-1)
    o_ref[...] = _layer_norm(DN_ALPHA * h_ref[...] + y, g_ref[...], b_ref[...])


def _final(h2d, ybuf, dest, ln_g, ln_b):
    t, d = h2d.shape
    tm = FINAL_TILE
    n = t // tm
    row = pl.BlockSpec((tm, d), lambda i: (i, 0))
    vec = pl.BlockSpec((1, d), lambda i: (0, 0))
    return pl.pallas_call(
        _final_kernel,
        grid=(n,),
        in_specs=[pl.BlockSpec((1, tm), lambda i: (0, i), memory_space=pltpu.SMEM),
                  pl.BlockSpec((1, tm), lambda i: (0, jnp.minimum(i + 1, n - 1)), memory_space=pltpu.SMEM),
                  row, vec, vec, pl.BlockSpec(memory_space=pl.ANY)],
        out_specs=row,
        out_shape=jax.ShapeDtypeStruct((t, d), F32),
        scratch_shapes=[pltpu.VMEM((2, tm * SUBLANES, LANES), F32), pltpu.SemaphoreType.DMA((2,))],
        compiler_params=_cparams(("arbitrary",)),
        name="final",
    )(dest, dest, h2d, ln_g, ln_b, ybuf)


def kernel(x, mem, rel_bias, hgrn_lb_logits, w_in, w_mem_kv, hgrn_norm_g, w_branch_a, w_branch_b, w_branch_c, w_out, ln1_g, ln1_b, w_router_group, b_router_group, w_router_expert, b_router_expert, w_exp_gate, w_exp_up, w_exp_down, ln2_g, ln2_b):
    b, s, d = x.shape
    t = b * s
    assert d == D_MODEL and w_in.shape[0] == DEPTH == 1
    assert s % (BAND * DILATED_GROUPS[-1][1]) == 0 and s % HGRN_TILE == 0 and s % MERGE_TILE == 0 and t % EXPERT_BLOCK == 0
    li = 0
    scale = HEAD_DIM ** -0.5
    x2d = x.reshape(t, d)

    w = w_in[li]
    col_scale = np.ones((COLS_A,), np.float32).reshape(N_GROUPS_A, 3, WIDTH_A)
    col_scale[:, 0] = scale
    w_a = (w[:, :COLS_A] * jnp.asarray(col_scale.reshape(-1))).astype(BF16)
    w_b = w[:, COLS_A:COLS_A + COLS_B].astype(BF16)
    w_c = (w[:, COLS_A + COLS_B:COLS_A + COLS_B + COLS_C] * scale).astype(BF16)
    w_g = w[:, COLS_A + COLS_B + COLS_C:].astype(BF16)

    mkv = _matmul(mem.reshape(-1, d), w_mem_kv[li].astype(BF16), BF16, 256, 2 * WIDTH_C, "mem_kv")
    mkv = mkv.reshape(b, -1, 2 * WIDTH_C)
    mk, mv = mkv[..., :WIDTH_C], mkv[..., WIDTH_C:]

    oa_parts = []
    for gi, (_, dilation) in enumerate(DILATED_GROUPS):
        bias = _band_bias(rel_bias[:, gi * HEADS_A:(gi + 1) * HEADS_A], dilation)
        pg = _proj_classes(x, w_a[:, gi * 3 * WIDTH_A:(gi + 1) * 3 * WIDTH_A], dilation, f"proj_a_{gi}")
        oa_parts.append(_band_attention(pg, bias, f"band_attn_{gi}"))

    ob = _hgrn2(x, w_b, hgrn_lb_logits.astype(F32), hgrn_norm_g[li].reshape(1, WIDTH_B))

    wr = jnp.zeros((d, LANES), F32)
    wr = wr.at[:, :N_EXPERT_GROUPS].set(w_router_group[li]).at[:, 8:_ROUTER_ROWS].set(w_router_expert[li])
    br = jnp.zeros((_ROUTER_ROWS,), F32).at[:N_EXPERT_GROUPS].set(b_router_group[li]).at[8:].set(b_router_expert[li])
    br = jnp.broadcast_to(br[:, None], (_ROUTER_ROWS, 128))

    h, code, wrow, counts = _merge(
        x, oa_parts, ob, mk, mv, w_c, w_g, w_branch_a[li].astype(BF16), w_branch_b[li].astype(BF16),
        w_branch_c[li].astype(BF16), w_out[li].astype(BF16), ln1_g[li].reshape(1, d), ln1_b[li].reshape(1, d), wr, br)
    h2d = h.reshape(t, d)
    eb = EXPERT_BLOCK
    counts = counts[:, 0]
    per_group = EXPERTS_PER_GROUP * EXPERTS_PER_GROUP
    group_counts = jnp.sum(counts.reshape(N_EXPERT_GROUPS, per_group), axis=1)

    blocks_g = (group_counts + eb - 1) // eb
    ends = jnp.cumsum(blocks_g)
    starts = ends - blocks_g
    n_active = ends[-1:].astype(jnp.int32)
    n_blocks = t // eb + N_EXPERT_GROUPS
    blk = jnp.arange(n_blocks, dtype=jnp.int32)
    block_g = jnp.minimum(jnp.sum(ends[None, :] <= blk[:, None], axis=1), N_EXPERT_GROUPS - 1).astype(jnp.int32)
    in_group = counts.reshape(N_EXPERT_GROUPS, per_group)
    class_start = ((starts * eb)[:, None] + jnp.cumsum(in_group, axis=1) - in_group).reshape(-1)
    cls = code // t
    is_c = cls == jnp.arange(N_ROUTE_CLASSES, dtype=jnp.int32)[:, None]
    dest = (code - cls * t + jnp.sum(jnp.where(is_c, class_start[:, None], 0), axis=0, keepdims=True)).astype(jnp.int32)
    trailing = n_active + jnp.arange(N_EXPERT_GROUPS, dtype=jnp.int32)
    zero_blocks = jnp.concatenate([jnp.where(blocks_g > 0, ends - 1, -1),
                                   jnp.where(trailing < n_blocks, trailing, -1)]).astype(jnp.int32)
    cidx = np.arange(N_ROUTE_CLASSES)
    pair_lo, pair_hi = (cidx // EXPERTS_PER_GROUP) % EXPERTS_PER_GROUP, cidx % EXPERTS_PER_GROUP
    member = np.asarray((np.arange(EXPERTS_PER_GROUP)[None, :] == pair_lo[:, None])
                        | (np.arange(EXPERTS_PER_GROUP)[None, :] == pair_hi[:, None]), np.int32)
    overlap = ((class_start[None, :] < (blk[:, None] + 1) * eb) & (class_start[None, :] + counts[None, :] > blk[:, None] * eb)
               & (counts[None, :] > 0))
    present = (jnp.sum(overlap[:, :, None].astype(jnp.int32) * jnp.asarray(member)[None], axis=1) > 0).astype(jnp.int32)

    xbuf = _dispatch(h2d, wrow, dest, zero_blocks, n_blocks)
    ybuf = _experts(xbuf, block_g, present.reshape(-1),
                    w_exp_gate[li].astype(BF16), w_exp_up[li].astype(BF16), w_exp_down[li].astype(BF16))
    out = _final(h2d, ybuf, dest, ln2_g[li].reshape(1, d), ln2_b[li].reshape(1, d))
    return out.reshape(b, s, d)
```

```python
import functools
import math

import numpy as np
import jax
import jax.numpy as jnp
from jax import lax
from jax.experimental import pallas as pl
from jax.experimental.pallas import tpu as pltpu

F32 = jnp.float32
BF16 = jnp.bfloat16

D_MODEL = 1024
DEPTH = 1
DILATED_GROUPS = ((128, 1), (512, 4), (2048, 16))
N_GROUPS_A = len(DILATED_GROUPS)
HEADS_A = 4
HEAD_DIM = 128
WIDTH_A = HEADS_A * HEAD_DIM
BAND = 128
NUM_BUCKETS = 32
MAX_DISTANCE = 2048
N_HEADS_B = 8
WIDTH_B = N_HEADS_B * HEAD_DIM
CHUNK_B = 64
N_HEADS_C = 4
WIDTH_C = N_HEADS_C * HEAD_DIM
COLS_A = N_GROUPS_A * 3 * WIDTH_A
COLS_B = 4 * WIDTH_B
COLS_C = WIDTH_C
COLS_GATE = 3 * D_MODEL
N_EXPERT_GROUPS = 4
EXPERTS_PER_GROUP = 8
N_EXPERTS = N_EXPERT_GROUPS * EXPERTS_PER_GROUP
TOP_K = 2
D_EXPERT = D_MODEL // 4
DN_ALPHA = (2 * DEPTH) ** 0.25
LN_EPS = 1e-5
RMS_EPS = 1e-6

NEG_BIG = -1e30
SAFE_LOG_DECAY = 80.0
VMEM_LIMIT = 56 * 1024 * 1024
EXPERT_BLOCK = 512

_NT = (((1,), (1,)), ((), ()))
_TN = (((0,), (0,)), ((), ()))


def _cparams(sem):
    return pltpu.CompilerParams(dimension_semantics=sem, vmem_limit_bytes=VMEM_LIMIT)


def _matmul_kernel(x_ref, w_ref, o_ref):
    o_ref[...] = jnp.dot(x_ref[...].astype(BF16), w_ref[...], preferred_element_type=F32).astype(o_ref.dtype)


def _matmul(x, w, out_dtype, tm, tn, name):
    t, k = x.shape
    n = w.shape[1]
    return pl.pallas_call(
        _matmul_kernel,
        grid=(n // tn, t // tm),
        in_specs=[pl.BlockSpec((tm, k), lambda j, i: (i, 0)), pl.BlockSpec((k, tn), lambda j, i: (0, j))],
        out_specs=pl.BlockSpec((tm, tn), lambda j, i: (i, j)),
        out_shape=jax.ShapeDtypeStruct((t, n), out_dtype),
        compiler_params=_cparams(("arbitrary", "arbitrary")),
        name=name,
    )(x, w)


def _t5_bucket_np(dist):
    dist = np.asarray(dist, np.int32)
    max_exact = NUM_BUCKETS // 2
    d = np.maximum(dist, 1).astype(np.float32)
    large = max_exact + (np.log(d / max_exact) / math.log(MAX_DISTANCE / max_exact) * (NUM_BUCKETS - max_exact)).astype(np.int32)
    large = np.minimum(large, NUM_BUCKETS - 1)
    return np.where(dist < max_exact, dist, large).astype(np.int32)


def _band_bias(bias_tab, dilation):
    i = np.arange(BAND)[:, None]
    j = np.arange(2 * BAND)[None, :]
    u = i + BAND - j
    in_band = (u >= 0) & (u <= BAND)
    bucket = _t5_bucket_np(np.clip(u, 0, BAND) * dilation)
    onehot = np.asarray(bucket[:, :, None] == np.arange(NUM_BUCKETS)[None, None, :], np.float32)
    bias = jnp.einsum('pqb,bh->hpq', jnp.asarray(onehot), bias_tab.astype(F32), precision=lax.Precision.HIGHEST)
    general = jnp.where(in_band[None], bias, NEG_BIG)
    first = jnp.where((in_band & (j >= BAND))[None], bias, NEG_BIG)
    return jnp.stack([first, general])


LANES = 128


def _proj_perm_kernel(*refs, r):
    n_slab = D_MODEL // LANES
    x_refs, (w_ref, o_ref, xp_ref) = refs[:n_slab], refs[n_slab:]
    tm = xp_ref.shape[0]
    n = tm // r
    for j, x_ref in enumerate(x_refs):
        for c in range(r):
            rows = x_ref[pl.ds(c, n, stride=r), :] if r > 1 else x_ref[...]
            xp_ref[c * n:(c + 1) * n, j * LANES:(j + 1) * LANES] = rows.astype(BF16)
    res = jnp.dot(xp_ref[...], w_ref[...], preferred_element_type=F32).astype(o_ref.dtype)
    for c in range(r):
        o_ref[c] = res[c * n:(c + 1) * n]


PROJ_A_TILE = 1024


def _proj_classes(x, w, r, name):
    b, s, d = x.shape
    n = w.shape[1]
    tm = PROJ_A_TILE
    n_slab = d // LANES
    slabs = [pl.BlockSpec((None, tm, LANES), lambda bi, t, j=j: (bi, t, j)) for j in range(n_slab)]
    return pl.pallas_call(
        functools.partial(_proj_perm_kernel, r=r),
        grid=(b, s // tm),
        in_specs=slabs + [pl.BlockSpec((d, n), lambda bi, t: (0, 0))],
        out_specs=pl.BlockSpec((None, r, tm // r, n), lambda bi, t: (bi, 0, t, 0)),
        out_shape=jax.ShapeDtypeStruct((b, r, s // r, n), BF16),
        scratch_shapes=[pltpu.VMEM((tm, d), BF16)],
        compiler_params=_cparams(("arbitrary", "arbitrary")),
        name=name,
    )(*([x] * n_slab), w)


BAND_SPAN = 2048
_HI_MASK = 0xFFFF0000
_LSE_LANES = LANES // HEADS_A


def _pack_bf16_pair(a, b):
    abits = pltpu.bitcast(a.astype(BF16).astype(F32), jnp.uint32)
    bbits = pltpu.bitcast(b.astype(BF16).astype(F32), jnp.uint32)
    return (abits >> 16) | (bbits & jnp.uint32(_HI_MASK))


def _unpack_bf16_pair(word):
    return pltpu.bitcast(word << 16, F32), pltpu.bitcast(word & jnp.uint32(_HI_MASK), F32)


def _band_attn_kernel(q_ref, k_ref, v_ref, kp_ref, vp_ref, bias_ref, o_ref, lse_ref, *, r):
    nq = q_ref.shape[1] // BAND
    variant = jnp.minimum(pl.program_id(1), 1)
    ones = jnp.ones((BAND, HEAD_DIM), BF16)
    lane = lax.broadcasted_iota(jnp.int32, (BAND, LANES), 1)
    heads = [slice(h * HEAD_DIM, (h + 1) * HEAD_DIM) for h in range(HEADS_A)]
    units = [(c, j) for c in range(r) for j in range(nq)]

    def prev_of(k_or_v_ref, prev_ref, c, j, sl):
        return prev_ref[c, :, sl] if j == 0 else k_or_v_ref[c, (j - 1) * BAND:j * BAND, sl]

    def scores(c, j):
        cur = slice(j * BAND, (j + 1) * BAND)
        out = []
        for h, sl in enumerate(heads):
            q = q_ref[c, cur, sl]
            bias = bias_ref[variant, h] if j == 0 else bias_ref[1, h]
            sp = lax.dot_general(q, prev_of(k_ref, kp_ref, c, j, sl), _NT, preferred_element_type=F32)
            sc = lax.dot_general(q, k_ref[c, cur, sl], _NT, preferred_element_type=F32)
            out.append((sp + bias[:, :BAND], sc + bias[:, BAND:]))
        return out

    nxt = scores(*units[0])
    for u, (c, j) in enumerate(units):
        s_all = nxt
        if u + 1 < len(units):
            nxt = scores(*units[u + 1])
        cur = slice(j * BAND, (j + 1) * BAND)
        rows = pl.ds(j * BAND * r + c, BAND, stride=r) if r > 1 else cur
        probs = []
        for sp, sc in s_all:
            m = jnp.max(jnp.maximum(sp, sc), axis=-1, keepdims=True)
            probs.append((m, jnp.exp(sp - m).astype(BF16), jnp.exp(sc - m).astype(BF16)))
        outs, lses = [], []
        for h, sl in enumerate(heads):
            m, pp, pc = probs[h]
            l = jnp.dot(pp, ones, preferred_element_type=F32) + jnp.dot(pc, ones, preferred_element_type=F32)
            o = jnp.dot(pp, prev_of(v_ref, vp_ref, c, j, sl), preferred_element_type=F32)
            o = o + jnp.dot(pc, v_ref[c, cur, sl], preferred_element_type=F32)
            outs.append(o / l)
            lses.append(m + jnp.log(l))
        for p in range(HEADS_A // 2):
            o_ref[p, rows, :] = _pack_bf16_pair(outs[2 * p], outs[2 * p + 1])
        lse = lses[-1]
        for h in range(HEADS_A - 2, -1, -1):
            lse = jnp.where(lane < (h + 1) * _LSE_LANES, lses[h], lse)
        lse_ref[rows, :] = lse


def _band_attention(pg, bias, name):
    b, r, l, _ = pg.shape
    nq = BAND_SPAN // (BAND * r)
    cur = lambda which: pl.BlockSpec((None, r, nq * BAND, WIDTH_A), lambda bi, n: (bi, 0, n, which))
    prev = lambda which: pl.BlockSpec((None, r, BAND, WIDTH_A), lambda bi, n: (bi, 0, jnp.maximum(n * nq - 1, 0), which))
    return pl.pallas_call(
        functools.partial(_band_attn_kernel, r=r),
        grid=(b, l * r // BAND_SPAN),
        in_specs=[cur(0), cur(1), cur(2), prev(1), prev(2), pl.BlockSpec(bias.shape, lambda bi, n: (0, 0, 0, 0))],
        out_specs=[pl.BlockSpec((None, HEADS_A // 2, BAND_SPAN, HEAD_DIM), lambda bi, n: (bi, 0, n, 0)),
                   pl.BlockSpec((None, BAND_SPAN, LANES), lambda bi, n: (bi, n, 0))],
        out_shape=[jax.ShapeDtypeStruct((b, HEADS_A // 2, l * r, HEAD_DIM), jnp.uint32),
                   jax.ShapeDtypeStruct((b, l * r, LANES), F32)],
        compiler_params=_cparams(("arbitrary", "arbitrary")),
        name=name,
    )(pg, pg, pg, pg, pg, bias)


HGRN_TILE = 512
_N_CHUNK = HGRN_TILE // CHUNK_B


def _dynamic_row(ref, r, ls):
    blk = ref[pl.ds(pl.multiple_of((r // 8) * 8, 8), 8), ls]
    sub = lax.broadcasted_iota(jnp.int32, blk.shape, 0)
    return jnp.sum(jnp.where(sub == r % 8, blk, 0.0), axis=0, keepdims=True)


def _sigmoid(z):
    return 0.5 * jnp.tanh(0.5 * z) + 0.5


_PROJ_PIECE = 256


def _hgrn_kernel(x_ref, xn_ref, w_ref, lbl_ref, ng_ref, o_ref, pq_ref, pf_ref, pi_ref, pg_ref, xb_ref,
                 st_ref, qs_ref, ks_ref, kh_ref, vs_ref, bs_ref, ebl_ref, a_ref, os_ref, gs_ref, kf_ref, qsil_ref,
                 *, tiles_per_seq):
    i = pl.program_id(0)
    parts = (pq_ref, pf_ref, pi_ref, pg_ref)
    per_part = WIDTH_B // _PROJ_PIECE

    def piece(k):
        cols = slice(k * _PROJ_PIECE, (k + 1) * _PROJ_PIECE)
        local = slice((k % per_part) * _PROJ_PIECE, (k % per_part + 1) * _PROJ_PIECE)
        parts[k // per_part][:, local] = jnp.dot(xb_ref[...], w_ref[:, cols], preferred_element_type=F32)

    @pl.when(i == 0)
    def _():
        xb_ref[...] = x_ref[...].astype(BF16)
        for k in range(COLS_B // _PROJ_PIECE):
            piece(k)

    @pl.when(lax.rem(i, tiles_per_seq) == 0)
    def _():
        st_ref[...] = jnp.zeros_like(st_ref)

    xb_ref[...] = xn_ref[...].astype(BF16)
    pieces = [k for part in (2, 3, 1, 0) for k in range(part * per_part, (part + 1) * per_part)]
    next_piece = lambda: piece(pieces.pop(0)) if pieces else None
    vs_ref[...] = pi_ref[...].astype(BF16)
    gv = pg_ref[...]
    gs_ref[...] = gv * _sigmoid(gv) * ng_ref[...]

    lg = lbl_ref[...]
    e = jnp.exp(lg - jnp.max(lg, axis=0, keepdims=True))
    lb = e[0:1, :] / jnp.sum(e, axis=0, keepdims=True)

    row = lax.broadcasted_iota(jnp.int32, (CHUNK_B, CHUNK_B), 0)
    colm = lax.broadcasted_iota(jnp.int32, (CHUNK_B, CHUNK_B), 1)
    causal = row >= colm
    tri = causal.astype(BF16)

    for _ in range(2):
        next_piece()
    f = lb + (1.0 - lb) * _sigmoid(pf_ref[...])
    logf = jnp.log(f)
    kf = 1.0 - f
    kf_ref[...] = kf
    l_hi = logf.astype(BF16)
    rem = logf - l_hi.astype(F32)
    l_mid = rem.astype(BF16)
    l_lo = (rem - l_mid.astype(F32)).astype(BF16)
    min_b = None
    for c in range(_N_CHUNK):
        rs = slice(c * CHUNK_B, (c + 1) * CHUNK_B)
        b = (jnp.dot(tri, l_hi[rs], preferred_element_type=F32) + jnp.dot(tri, l_mid[rs], preferred_element_type=F32)
             + jnp.dot(tri, l_lo[rs], preferred_element_type=F32))
        bs_ref[rs, :] = b
        cm = jnp.min(b[CHUNK_B - 1:CHUNK_B, :])
        min_b = cm if min_b is None else jnp.minimum(min_b, cm)
        if c % 2 == 1:
            next_piece()
    safe = min_b > -SAFE_LOG_DECAY

    qv = pq_ref[...]
    qsil_ref[...] = qv * _sigmoid(qv)
    for c in range(_N_CHUNK):
        rs = slice(c * CHUNK_B, (c + 1) * CHUNK_B)
        b = bs_ref[rs, :]
        eb_last = jnp.exp(b[CHUNK_B - 1:CHUNK_B, :])
        ebl_ref[c:c + 1, :] = eb_last
        qs_ref[rs, :] = (qsil_ref[rs, :] * jnp.exp(b)).astype(BF16)
        k_grown = kf_ref[rs, :] * jnp.exp(-b)
        ks_ref[rs, :] = k_grown.astype(BF16)
        kh_ref[rs, :] = (k_grown * eb_last).astype(BF16)
        next_piece()

    @pl.when(safe)
    def _():
        for c in range(_N_CHUNK):
            rs = slice(c * CHUNK_B, (c + 1) * CHUNK_B)
            for h in range(N_HEADS_B):
                ls = slice(h * HEAD_DIM, (h + 1) * HEAD_DIM)
                a = lax.dot_general(qs_ref[rs, ls], ks_ref[rs, ls], _NT, preferred_element_type=F32)
                a_ref[c * N_HEADS_B + h] = jnp.where(causal, a, 0.0)

    @pl.when(jnp.logical_not(safe))
    def _():
        lane = lax.broadcasted_iota(jnp.int32, (1, CHUNK_B), 1)
        trow = lax.broadcasted_iota(jnp.int32, (CHUNK_B, 1), 0)
        for c in range(_N_CHUNK):
            rs = slice(c * CHUNK_B, (c + 1) * CHUNK_B)
            b = bs_ref[rs, :]
            kh_ref[rs, :] = (kf_ref[rs, :] * jnp.exp(b[CHUNK_B - 1:CHUNK_B, :] - b)).astype(BF16)
            for h in range(N_HEADS_B):
                ls = slice(h * HEAD_DIM, (h + 1) * HEAD_DIM)
                bq = bs_ref[rs, ls]
                qh = qsil_ref[rs, ls]

                def body(s, acc, c=c, ls=ls, bq=bq, qh=qh):
                    b_s = _dynamic_row(bs_ref, c * CHUNK_B + s, ls)
                    k_s = _dynamic_row(kf_ref, c * CHUNK_B + s, ls)
                    w = jnp.exp(jnp.minimum(bq - b_s, 0.0))
                    colv = jnp.sum(qh * k_s * w, axis=-1, keepdims=True)
                    colv = jnp.where(trow >= s, colv, 0.0)
                    return acc + colv * (lane == s).astype(F32)

                a_ref[c * N_HEADS_B + h] = lax.fori_loop(0, CHUNK_B, body, jnp.zeros((CHUNK_B, CHUNK_B), F32))

    for c in range(_N_CHUNK):
        rs = slice(c * CHUNK_B, (c + 1) * CHUNK_B)
        for h in range(N_HEADS_B):
            ls = slice(h * HEAD_DIM, (h + 1) * HEAD_DIM)
            st = st_ref[h]
            vh = vs_ref[rs, ls]
            o = lax.dot_general(qs_ref[rs, ls], st.astype(BF16), _NT, preferred_element_type=F32)
            o = o + jnp.dot(a_ref[c * N_HEADS_B + h].astype(BF16), vh, preferred_element_type=F32)
            os_ref[rs, ls] = o
            st_ref[h] = st * ebl_ref[c:c + 1, ls] + lax.dot_general(vh, kh_ref[rs, ls], _TN, preferred_element_type=F32)
            if h == N_HEADS_B - 1:
                next_piece()
    while pieces:
        next_piece()

    for h in range(N_HEADS_B):
        ls = slice(h * HEAD_DIM, (h + 1) * HEAD_DIM)
        o = os_ref[:, ls]
        o = o * lax.rsqrt(jnp.mean(o * o, axis=-1, keepdims=True) + RMS_EPS)
        o_ref[:, ls] = (o * gs_ref[:, ls]).astype(o_ref.dtype)


def _hgrn2(x, w_b, lb_logits, norm_g):
    b, s, d = x.shape
    tb = HGRN_TILE
    n = b * s // tb
    n_slots = lb_logits.shape[0]
    out = pl.pallas_call(
        functools.partial(_hgrn_kernel, tiles_per_seq=s // tb),
        grid=(n,),
        in_specs=[pl.BlockSpec((tb, d), lambda i: (i, 0)),
                  pl.BlockSpec((tb, d), lambda i: (jnp.minimum(i + 1, n - 1), 0)),
                  pl.BlockSpec(w_b.shape, lambda i: (0, 0), pipeline_mode=pl.Buffered(1)),
                  pl.BlockSpec((n_slots, WIDTH_B), lambda i: (0, 0)),
                  pl.BlockSpec((1, WIDTH_B), lambda i: (0, 0))],
        out_specs=pl.BlockSpec((tb, WIDTH_B), lambda i: (i, 0)),
        out_shape=jax.ShapeDtypeStruct((b * s, WIDTH_B), BF16),
        scratch_shapes=[
            pltpu.VMEM((tb, WIDTH_B), F32),
            pltpu.VMEM((tb, WIDTH_B), F32),
            pltpu.VMEM((tb, WIDTH_B), F32),
            pltpu.VMEM((tb, WIDTH_B), F32),
            pltpu.VMEM((tb, d), BF16),
            pltpu.VMEM((N_HEADS_B, HEAD_DIM, HEAD_DIM), F32),
            pltpu.VMEM((tb, WIDTH_B), BF16),
            pltpu.VMEM((tb, WIDTH_B), BF16),
            pltpu.VMEM((tb, WIDTH_B), BF16),
            pltpu.VMEM((tb, WIDTH_B), BF16),
            pltpu.VMEM((tb, WIDTH_B), F32),
            pltpu.VMEM((8, WIDTH_B), F32),
            pltpu.VMEM((_N_CHUNK * N_HEADS_B, CHUNK_B, CHUNK_B), F32),
            pltpu.VMEM((tb, WIDTH_B), F32),
            pltpu.VMEM((tb, WIDTH_B), F32),
            pltpu.VMEM((tb, WIDTH_B), F32),
            pltpu.VMEM((tb, WIDTH_B), F32),
        ],
        compiler_params=_cparams(("arbitrary",)),
        name="hgrn2",
    )(x.reshape(b * s, d), x.reshape(b * s, d), w_b, lb_logits, norm_g)
    return out.reshape(b, s, WIDTH_B)


MERGE_TILE = 512


def _layer_norm(y, g, b):
    mu = jnp.mean(y, axis=-1, keepdims=True)
    d = y - mu
    var = jnp.mean(d * d, axis=-1, keepdims=True)
    return d * lax.rsqrt(var + LN_EPS) * g + b


def _merge_kernel(x_ref, o1_ref, l1_ref, o2_ref, l2_ref, o3_ref, l3_ref, ob_ref, mk_ref, mv_ref,
                  wc_ref, wg_ref, wa_ref, wb_ref, wcc_ref, wo_ref, g_ref, b_ref, wr_ref, br_ref,
                  h_ref, code_ref, wrow_ref, cnt_ref, oc_ref, cnt_sc, *, n_tokens):
    x = x_ref[...]
    xb = x.astype(BF16)

    qc = jnp.dot(xb, wc_ref[...], preferred_element_type=F32).astype(BF16)
    heads_c = [slice(h * HEAD_DIM, (h + 1) * HEAD_DIM) for h in range(N_HEADS_C)]
    s_all = [lax.dot_general(qc[:, ls], mk_ref[:, ls], _NT, preferred_element_type=F32) for ls in heads_c]
    p_all = [jnp.exp(s - jnp.max(s, axis=-1, keepdims=True)).astype(BF16) for s in s_all]
    ones = jnp.ones((mk_ref.shape[0], HEAD_DIM), BF16)
    for ls, p in zip(heads_c, p_all):
        l = jnp.dot(p, ones, preferred_element_type=F32)
        oc_ref[:, ls] = (jnp.dot(p, mv_ref[:, ls], preferred_element_type=F32) / l).astype(BF16)

    lses = [l1_ref[...], l2_ref[...], l3_ref[...]]
    m = functools.reduce(jnp.maximum, lses)
    es = [jnp.exp(l - m) for l in lses]
    inv = 1.0 / functools.reduce(lambda a, c: a + c, es)
    mix = [e * inv for e in es]
    o_pairs = [[_unpack_bf16_pair(o_ref[p]) for p in range(HEADS_A // 2)] for o_ref in (o1_ref, o2_ref, o3_ref)]
    oa_heads = []
    for h in range(HEADS_A):
        acc = None
        for gi in range(N_GROUPS_A):
            w = jnp.broadcast_to(mix[gi][:, h * _LSE_LANES:h * _LSE_LANES + 1], (x.shape[0], HEAD_DIM))
            term = w * o_pairs[gi][h // 2][h % 2]
            acc = term if acc is None else acc + term
        oa_heads.append(acc.astype(BF16))
    oa = jnp.concatenate(oa_heads, axis=-1)

    gates = jax.nn.sigmoid(jnp.dot(xb, wg_ref[...], preferred_element_type=F32))
    merged = gates[:, :D_MODEL] * jnp.dot(oa, wa_ref[...], preferred_element_type=F32)
    merged = merged + gates[:, D_MODEL:2 * D_MODEL] * jnp.dot(ob_ref[...], wb_ref[...], preferred_element_type=F32)
    merged = merged + gates[:, 2 * D_MODEL:] * jnp.dot(oc_ref[...], wcc_ref[...], preferred_element_type=F32)
    y = DN_ALPHA * x + jnp.dot(merged.astype(BF16), wo_ref[...], preferred_element_type=F32)
    hv = _layer_norm(y, g_ref[...], b_ref[...])
    h_ref[...] = hv

    first = jnp.logical_and(pl.program_id(0) == 0, pl.program_id(1) == 0)
    _route(hv, wr_ref, br_ref, code_ref, wrow_ref, cnt_ref, cnt_sc, first, n_tokens)


def _merge(x, oa_parts, ob, mk, mv, wc, wg, wa, wb, wcc, wo, ln_g, ln_b, wr, br):
    b, s, d = x.shape
    tm = MERGE_TILE
    nt = s // tm
    tok = lambda w: pl.BlockSpec((None, tm, w), lambda bi, t: (bi, t, 0))
    full = lambda a: pl.BlockSpec(a.shape, lambda bi, t: (0,) * a.ndim, pipeline_mode=pl.Buffered(1))
    mem = pl.BlockSpec((None,) + mk.shape[1:], lambda bi, t: (bi, 0, 0))
    o_spec = pl.BlockSpec((None, HEADS_A // 2, tm, HEAD_DIM), lambda bi, t: (bi, 0, t, 0))
    flat = [a for pair in oa_parts for a in pair]
    return pl.pallas_call(
        functools.partial(_merge_kernel, n_tokens=b * s),
        grid=(b, nt),
        in_specs=[tok(d)] + [o_spec, tok(LANES)] * N_GROUPS_A + [tok(WIDTH_B), mem, mem,
                  full(wc), full(wg), full(wa), full(wb), full(wcc), full(wo), full(ln_g), full(ln_b), full(wr), full(br)],
        out_specs=[tok(d), pl.BlockSpec((1, tm), lambda bi, t: (0, bi * nt + t)),
                   pl.BlockSpec((tm, LANES), lambda bi, t: (bi * nt + t, 0)),
                   pl.BlockSpec((N_ROUTE_CLASSES, LANES), lambda bi, t: (0, 0))],
        out_shape=[jax.ShapeDtypeStruct((b, s, d), F32), jax.ShapeDtypeStruct((1, b * s), jnp.int32),
                   jax.ShapeDtypeStruct((b * s, LANES), F32), jax.ShapeDtypeStruct((N_ROUTE_CLASSES, LANES), jnp.int32)],
        scratch_shapes=[pltpu.VMEM((tm, WIDTH_C), BF16), pltpu.VMEM((N_ROUTE_CLASSES, LANES), F32)],
        compiler_params=_cparams(("arbitrary", "arbitrary")),
        name="merge",
    )(x, *flat, ob, mk, mv, wc, wg, wa, wb, wcc, wo, ln_g, ln_b, wr, br)


N_ROUTE_CLASSES = N_EXPERT_GROUPS * EXPERTS_PER_GROUP * EXPERTS_PER_GROUP
_ROUTER_ROWS = 8 + N_EXPERTS


def _route(hv, wr_ref, br_ref, dest_ref, wrow_ref, cnt_ref, cnt_sc, first_step, group_capacity):
    tm = hv.shape[0]

    @pl.when(first_step)
    def _():
        cnt_sc[...] = jnp.zeros_like(cnt_sc)

    h_hi = hv.astype(BF16)
    h_lo = (hv - h_hi.astype(F32)).astype(BF16)
    wv = wr_ref[...]
    w_hi = wv.astype(BF16)
    w_lo = (wv - w_hi.astype(F32)).astype(BF16)
    both = jnp.dot(h_hi, jnp.concatenate([w_hi, w_lo], axis=1), preferred_element_type=F32)
    logits = both[:, :LANES] + both[:, LANES:] + jnp.dot(h_lo, w_hi, preferred_element_type=F32)
    logits = logits.T[:_ROUTER_ROWS, :] + br_ref[:, 0:1]
    g = [logits[i:i + 1, :] for i in range(N_EXPERT_GROUPS)]
    gmax = functools.reduce(jnp.maximum, g)
    gsel = jnp.full_like(gmax, N_EXPERT_GROUPS - 1).astype(jnp.int32)
    for i in range(N_EXPERT_GROUPS - 2, -1, -1):
        gsel = jnp.where(g[i] == gmax, i, gsel)
    gprob = 1.0 / functools.reduce(lambda a, c: a + c, [jnp.exp(gi - gmax) for gi in g])

    esel = logits[8 + (N_EXPERT_GROUPS - 1) * EXPERTS_PER_GROUP:8 + N_EXPERTS, :]
    for i in range(N_EXPERT_GROUPS - 2, -1, -1):
        esel = jnp.where(gsel == i, logits[8 + i * EXPERTS_PER_GROUP:8 + (i + 1) * EXPERTS_PER_GROUP, :], esel)
    ridx = lax.broadcasted_iota(jnp.int32, (EXPERTS_PER_GROUP, tm), 0)
    v1 = jnp.max(esel, axis=0, keepdims=True)
    i1 = jnp.min(jnp.where(esel == v1, ridx, EXPERTS_PER_GROUP), axis=0, keepdims=True)
    rest = jnp.where(ridx == i1, -jnp.inf, esel)
    v2 = jnp.max(rest, axis=0, keepdims=True)
    i2 = jnp.min(jnp.where(rest == v2, ridx, EXPERTS_PER_GROUP), axis=0, keepdims=True)
    t = jnp.exp(v2 - v1)
    w1 = gprob / (1.0 + t)
    w2 = gprob * t / (1.0 + t)
    wt = jnp.where(ridx == i1, w1, jnp.where(ridx == i2, w2, 0.0))
    eye = (lax.broadcasted_iota(jnp.int32, (EXPERTS_PER_GROUP, LANES), 0)
           == lax.broadcasted_iota(jnp.int32, (EXPERTS_PER_GROUP, LANES), 1)).astype(F32)
    wrow_ref[...] = lax.dot_general(wt, eye, _TN, preferred_element_type=F32, precision=lax.Precision.HIGHEST)

    cls = (gsel * EXPERTS_PER_GROUP + jnp.minimum(i1, i2)) * EXPERTS_PER_GROUP + jnp.maximum(i1, i2)
    crow = lax.broadcasted_iota(jnp.int32, (N_ROUTE_CLASSES, tm), 0)
    onehot = (crow == cls).astype(F32)
    before = (lax.broadcasted_iota(jnp.int32, (tm, tm), 0) < lax.broadcasted_iota(jnp.int32, (tm, tm), 1))
    excl = jnp.dot(onehot.astype(BF16), before.astype(BF16), preferred_element_type=F32)
    pos = cnt_sc[:, 0:1] + excl
    rank = jnp.sum(onehot * pos, axis=0, keepdims=True).astype(jnp.int32)
    dest_ref[...] = cls * group_capacity + rank
    cnt_sc[...] = cnt_sc[...] + jnp.sum(onehot, axis=1, keepdims=True)
    cnt_ref[...] = cnt_sc[...].astype(jnp.int32)


DISPATCH_TILE = 512


SUBLANES = 8
_HALF = D_MODEL // 2


def _tile_copy(src_ref, src_tok, dst_ref, dst_tok, sem):
    src = src_ref.at[pl.ds(pl.multiple_of(src_tok * SUBLANES, SUBLANES), SUBLANES)]
    dst = dst_ref.at[pl.ds(pl.multiple_of(dst_tok * SUBLANES, SUBLANES), SUBLANES)]
    return pltpu.make_async_copy(src, dst, sem)


def _tile_rows(ref, j, n):
    return ref.at[pl.ds(j, n, stride=SUBLANES), :]


def _all_tiles_wait(src_ref, dst_ref, sem):
    pltpu.make_async_copy(src_ref, dst_ref.at[pl.ds(0, src_ref.shape[0])], sem).wait()


def _dispatch_kernel(zblk_ref, dest_ref, h_ref, w_ref, xbuf_ref, pay_ref, zero_ref, sems, zsem):
    i = pl.program_id(0)
    n_steps = pl.num_programs(0)
    slot = lax.rem(i, 2)
    tm = h_ref.shape[0]
    pay = pay_ref.at[slot]

    @pl.when(i == 0)
    def _():
        zero_ref[...] = jnp.zeros_like(zero_ref)
        for k in range(zblk_ref.shape[0]):
            @pl.when(zblk_ref[k] >= 0)
            def _():
                start = pl.multiple_of(zblk_ref[k] * zero_ref.shape[0], zero_ref.shape[0])
                cp = pltpu.make_async_copy(zero_ref, xbuf_ref.at[pl.ds(start, zero_ref.shape[0])], zsem)
                cp.start()
                cp.wait()

    @pl.when(i >= 2)
    def _():
        _all_tiles_wait(pay, xbuf_ref, sems.at[slot])

    hv = h_ref[...]
    for j in range(_HALF // LANES):
        lo, hi = hv[:, j * LANES:(j + 1) * LANES], hv[:, _HALF + j * LANES:_HALF + (j + 1) * LANES]
        _tile_rows(pay, j, tm)[...] = _pack_bf16_pair(lo, hi)
    _tile_rows(pay, _HALF // LANES, tm)[...] = pltpu.bitcast(w_ref[...], jnp.uint32)
    for j in range(_HALF // LANES + 1, SUBLANES):
        _tile_rows(pay, j, tm)[...] = jnp.zeros((tm, LANES), jnp.uint32)

    def issue(t, carry):
        _tile_copy(pay, t, xbuf_ref, dest_ref[0, t], sems.at[slot]).start()
        return carry

    lax.fori_loop(0, tm, issue, 0, unroll=8)

    @pl.when(i == n_steps - 1)
    def _():
        @pl.when(i >= 1)
        def _():
            _all_tiles_wait(pay_ref.at[1 - slot], xbuf_ref, sems.at[1 - slot])

        _all_tiles_wait(pay, xbuf_ref, sems.at[slot])


def _dispatch(h2d, wrow, dest, zero_blocks, n_blocks):
    t, d = h2d.shape
    tm = DISPATCH_TILE
    grid_spec = pltpu.PrefetchScalarGridSpec(
        num_scalar_prefetch=1,
        grid=(t // tm,),
        in_specs=[pl.BlockSpec((1, tm), lambda i, zb: (0, i), memory_space=pltpu.SMEM),
                  pl.BlockSpec((tm, d), lambda i, zb: (i, 0)),
                  pl.BlockSpec((tm, LANES), lambda i, zb: (i, 0))],
        out_specs=pl.BlockSpec(memory_space=pl.ANY),
        scratch_shapes=[pltpu.VMEM((2, tm * SUBLANES, LANES), jnp.uint32),
                        pltpu.VMEM((EXPERT_BLOCK * SUBLANES, LANES), jnp.uint32),
                        pltpu.SemaphoreType.DMA((2,)), pltpu.SemaphoreType.DMA(())],
    )
    return pl.pallas_call(
        _dispatch_kernel,
        grid_spec=grid_spec,
        out_shape=jax.ShapeDtypeStruct((n_blocks * EXPERT_BLOCK * SUBLANES, LANES), jnp.uint32),
        compiler_params=_cparams(("arbitrary",)),
        name="dispatch",
    )(zero_blocks, dest, h2d, wrow)


def _expert_kernel(bg_ref, pres_ref, x_ref, wg_ref, wu_ref, wd_ref, y_ref, xs_ref, ws_ref, ys_ref):
    i = pl.program_id(0)
    eb = xs_ref.shape[0]
    lo, hi = [], []
    for j in range(_HALF // LANES):
        lo_j, hi_j = _unpack_bf16_pair(_tile_rows(x_ref, j, eb)[...])
        lo.append(lo_j)
        hi.append(hi_j)
    xs_ref[...] = jnp.concatenate(lo + hi, axis=-1).astype(BF16)
    ws_ref[...] = pltpu.bitcast(_tile_rows(x_ref, _HALF // LANES, eb)[...], F32)
    ys_ref[...] = jnp.zeros_like(ys_ref)
    halves = [slice(0, eb // 2), slice(eb // 2, eb)]

    for e in range(EXPERTS_PER_GROUP):
        @pl.when(pres_ref[i * EXPERTS_PER_GROUP + e] > 0)
        def _(e=e):
            gu = [(jnp.dot(xs_ref[p, :], wg_ref[e], preferred_element_type=F32),
                   jnp.dot(xs_ref[p, :], wu_ref[e], preferred_element_type=F32)) for p in halves]
            for p, (a, u) in zip(halves, gu):
                we = ws_ref[p, e:e + 1]
                hid = jnp.where(we != 0.0, a * _sigmoid(a) * u * we, 0.0).astype(BF16)
                ys_ref[p, :] += jnp.dot(hid, wd_ref[e], preferred_element_type=F32)

    for j in range(SUBLANES):
        _tile_rows(y_ref, j, eb)[...] = ys_ref[:, j * LANES:(j + 1) * LANES]


def _experts(xbuf, block_g, present, wg, wu, wd):
    eb = EXPERT_BLOCK
    blk = pl.BlockSpec((eb * SUBLANES, LANES), lambda i, bg, pres: (i, 0))
    gw = lambda a: pl.BlockSpec((EXPERTS_PER_GROUP,) + a.shape[1:], lambda i, bg, pres: (bg[i], 0, 0))
    grid_spec = pltpu.PrefetchScalarGridSpec(
        num_scalar_prefetch=2,
        grid=(block_g.shape[0],),
        in_specs=[blk, gw(wg), gw(wu), gw(wd)],
        out_specs=blk,
        scratch_shapes=[pltpu.VMEM((eb, D_MODEL), BF16), pltpu.VMEM((eb, LANES), F32), pltpu.VMEM((eb, D_MODEL), F32)],
    )
    return pl.pallas_call(
        _expert_kernel,
        grid_spec=grid_spec,
        out_shape=jax.ShapeDtypeStruct(xbuf.shape, F32),
        compiler_params=_cparams(("arbitrary",)),
        name="experts",
    )(block_g, present, xbuf, wg, wu, wd)


FINAL_TILE = 512


def _final_kernel(dest_ref, dest_next_ref, h_ref, g_ref, b_ref, ybuf_ref, o_ref, ys_ref, sems):
    i = pl.program_id(0)
    slot = lax.rem(i, 2)
    tm = h_ref.shape[0]

    def gather(d_ref, s):
        def issue(t, carry):
            _tile_copy(ybuf_ref, d_ref[0, t], ys_ref.at[s], t, sems.at[s]).start()
            return carry

        lax.fori_loop(0, tm, issue, 0, unroll=8)

    @pl.when(i == 0)
    def _():
        gather(dest_ref, slot)

    @pl.when(i + 1 < pl.num_programs(0))
    def _():
        gather(dest_next_ref, 1 - slot)

    ys = ys_ref.at[slot]
    pltpu.make_async_copy(ybuf_ref.at[pl.ds(0, ys.shape[0])], ys, sems.at[slot]).wait()
    y = jnp.concatenate([_tile_rows(ys, j, tm)[...] for j in range(SUBLANES)], axis=-1)
    o_ref[...] = _layer_norm(DN_ALPHA * h_ref[...] + y, g_ref[...], b_ref[...])


def _final(h2d, ybuf, dest, ln_g, ln_b):
    t, d = h2d.shape
    tm = FINAL_TILE
    n = t // tm
    row = pl.BlockSpec((tm, d), lambda i: (i, 0))
    vec = pl.BlockSpec((1, d), lambda i: (0, 0))
    return pl.pallas_call(
        _final_kernel,
        grid=(n,),
        in_specs=[pl.BlockSpec((1, tm), lambda i: (0, i), memory_space=pltpu.SMEM),
                  pl.BlockSpec((1, tm), lambda i: (0, jnp.minimum(i + 1, n - 1)), memory_space=pltpu.SMEM),
                  row, vec, vec, pl.BlockSpec(memory_space=pl.ANY)],
        out_specs=row,
        out_shape=jax.ShapeDtypeStruct((t, d), F32),
        scratch_shapes=[pltpu.VMEM((2, tm * SUBLANES, LANES), F32), pltpu.SemaphoreType.DMA((2,))],
        compiler_params=_cparams(("arbitrary",)),
        name="final",
    )(dest, dest, h2d, ln_g, ln_b, ybuf)


def kernel(x, mem, rel_bias, hgrn_lb_logits, w_in, w_mem_kv, hgrn_norm_g, w_branch_a, w_branch_b, w_branch_c, w_out, ln1_g, ln1_b, w_router_group, b_router_group, w_router_expert, b_router_expert, w_exp_gate, w_exp_up, w_exp_down, ln2_g, ln2_b):
    b, s, d = x.shape
    t = b * s
    assert d == D_MODEL and w_in.shape[0] == DEPTH == 1
    assert s % (BAND * DILATED_GROUPS[-1][1]) == 0 and s % HGRN_TILE == 0 and s % MERGE_TILE == 0 and t % EXPERT_BLOCK == 0
    li = 0
    scale = HEAD_DIM ** -0.5
    x2d = x.reshape(t, d)

    w = w_in[li]
    col_scale = np.ones((COLS_A,), np.float32).reshape(N_GROUPS_A, 3, WIDTH_A)
    col_scale[:, 0] = scale
    w_a = (w[:, :COLS_A] * jnp.asarray(col_scale.reshape(-1))).astype(BF16)
    w_b = w[:, COLS_A:COLS_A + COLS_B].astype(BF16)
    w_c = (w[:, COLS_A + COLS_B:COLS_A + COLS_B + COLS_C] * scale).astype(BF16)
    w_g = w[:, COLS_A + COLS_B + COLS_C:].astype(BF16)

    mkv = _matmul(mem.reshape(-1, d), w_mem_kv[li].astype(BF16), BF16, 256, 2 * WIDTH_C, "mem_kv")
    mkv = mkv.reshape(b, -1, 2 * WIDTH_C)
    mk, mv = mkv[..., :WIDTH_C], mkv[..., WIDTH_C:]

    oa_parts = []
    for gi, (_, dilation) in enumerate(DILATED_GROUPS):
        bias = _band_bias(rel_bias[:, gi * HEADS_A:(gi + 1) * HEADS_A], dilation)
        pg = _proj_classes(x, w_a[:, gi * 3 * WIDTH_A:(gi + 1) * 3 * WIDTH_A], dilation, f"proj_a_{gi}")
        oa_parts.append(_band_attention(pg, bias, f"band_attn_{gi}"))

    ob = _hgrn2(x, w_b, hgrn_lb_logits.astype(F32), hgrn_norm_g[li].reshape(1, WIDTH_B))

    wr = jnp.zeros((d, LANES), F32)
    wr = wr.at[:, :N_EXPERT_GROUPS].set(w_router_group[li]).at[:, 8:_ROUTER_ROWS].set(w_router_expert[li])
    br = jnp.zeros((_ROUTER_ROWS,), F32).at[:N_EXPERT_GROUPS].set(b_router_group[li]).at[8:].set(b_router_expert[li])
    br = jnp.broadcast_to(br[:, None], (_ROUTER_ROWS, 128))

    h, code, wrow, counts = _merge(
        x, oa_parts, ob, mk, mv, w_c, w_g, w_branch_a[li].astype(BF16), w_branch_b[li].astype(BF16),
        w_branch_c[li].astype(BF16), w_out[li].astype(BF16), ln1_g[li].reshape(1, d), ln1_b[li].reshape(1, d), wr, br)
    h2d = h.reshape(t, d)
    eb = EXPERT_BLOCK
    counts = counts[:, 0]
    per_group = EXPERTS_PER_GROUP * EXPERTS_PER_GROUP
    group_counts = jnp.sum(counts.reshape(N_EXPERT_GROUPS, per_group), axis=1)

    blocks_g = (group_counts + eb - 1) // eb
    ends = jnp.cumsum(blocks_g)
    starts = ends - blocks_g
    n_active = ends[-1:].astype(jnp.int32)
    n_blocks = t // eb + N_EXPERT_GROUPS
    blk = jnp.arange(n_blocks, dtype=jnp.int32)
    block_g = jnp.minimum(jnp.sum(ends[None, :] <= blk[:, None], axis=1), N_EXPERT_GROUPS - 1).astype(jnp.int32)
    in_group = counts.reshape(N_EXPERT_GROUPS, per_group)
    class_start = ((starts * eb)[:, None] + jnp.cumsum(in_group, axis=1) - in_group).reshape(-1)
    cls = code // t
    is_c = cls == jnp.arange(N_ROUTE_CLASSES, dtype=jnp.int32)[:, None]
    dest = (code - cls * t + jnp.sum(jnp.where(is_c, class_start[:, None], 0), axis=0, keepdims=True)).astype(jnp.int32)
    trailing = n_active + jnp.arange(N_EXPERT_GROUPS, dtype=jnp.int32)
    zero_blocks = jnp.concatenate([jnp.where(blocks_g > 0, ends - 1, -1),
                                   jnp.where(trailing < n_blocks, trailing, -1)]).astype(jnp.int32)
    cidx = np.arange(N_ROUTE_CLASSES)
    pair_lo, pair_hi = (cidx // EXPERTS_PER_GROUP) % EXPERTS_PER_GROUP, cidx % EXPERTS_PER_GROUP
    member = np.asarray((np.arange(EXPERTS_PER_GROUP)[None, :] == pair_lo[:, None])
                        | (np.arange(EXPERTS_PER_GROUP)[None, :] == pair_hi[:, None]), np.int32)
    overlap = ((class_start[None, :] < (blk[:, None] + 1) * eb) & (class_start[None, :] + counts[None, :] > blk[:, None] * eb)
               & (counts[None, :] > 0))
    present = (jnp.sum(overlap[:, :, None].astype(jnp.int32) * jnp.asarray(member)[None], axis=1) > 0).astype(jnp.int32)

    xbuf = _dispatch(h2d, wrow, dest, zero_blocks, n_blocks)
    ybuf = _experts(xbuf, block_g, present.reshape(-1),
                    w_exp_gate[li].astype(BF16), w_exp_up[li].astype(BF16), w_exp_down[li].astype(BF16))
    out = _final(h2d, ybuf, dest, ln2_g[li].reshape(1, d), ln2_b[li].reshape(1, d))
    return out.reshape(b, s, d)
```

```python
import functools
import math

import numpy as np
import jax
import jax.numpy as jnp
from jax import lax
from jax.experimental import pallas as pl
from jax.experimental.pallas import tpu as pltpu

F32 = jnp.float32
BF16 = jnp.bfloat16

D_MODEL = 1024
DEPTH = 1
DILATED_GROUPS = ((128, 1), (512, 4), (2048, 16))
N_GROUPS_A = len(DILATED_GROUPS)
HEADS_A = 4
HEAD_DIM = 128
WIDTH_A = HEADS_A * HEAD_DIM
BAND = 128
NUM_BUCKETS = 32
MAX_DISTANCE = 2048
N_HEADS_B = 8
WIDTH_B = N_HEADS_B * HEAD_DIM
CHUNK_B = 64
N_HEADS_C = 4
WIDTH_C = N_HEADS_C * HEAD_DIM
COLS_A = N_GROUPS_A * 3 * WIDTH_A
COLS_B = 4 * WIDTH_B
COLS_C = WIDTH_C
COLS_GATE = 3 * D_MODEL
N_EXPERT_GROUPS = 4
EXPERTS_PER_GROUP = 8
N_EXPERTS = N_EXPERT_GROUPS * EXPERTS_PER_GROUP
TOP_K = 2
D_EXPERT = D_MODEL // 4
DN_ALPHA = (2 * DEPTH) ** 0.25
LN_EPS = 1e-5
RMS_EPS = 1e-6

NEG_BIG = -1e30
SAFE_LOG_DECAY = 80.0
VMEM_LIMIT = 56 * 1024 * 1024
EXPERT_BLOCK = 512

_NT = (((1,), (1,)), ((), ()))
_TN = (((0,), (0,)), ((), ()))


def _cparams(sem):
    return pltpu.CompilerParams(dimension_semantics=sem, vmem_limit_bytes=VMEM_LIMIT)


def _matmul_kernel(x_ref, w_ref, o_ref):
    o_ref[...] = jnp.dot(x_ref[...].astype(BF16), w_ref[...], preferred_element_type=F32).astype(o_ref.dtype)


def _matmul(x, w, out_dtype, tm, tn, name):
    t, k = x.shape
    n = w.shape[1]
    return pl.pallas_call(
        _matmul_kernel,
        grid=(n // tn, t // tm),
        in_specs=[pl.BlockSpec((tm, k), lambda j, i: (i, 0)), pl.BlockSpec((k, tn), lambda j, i: (0, j))],
        out_specs=pl.BlockSpec((tm, tn), lambda j, i: (i, j)),
        out_shape=jax.ShapeDtypeStruct((t, n), out_dtype),
        compiler_params=_cparams(("arbitrary", "arbitrary")),
        name=name,
    )(x, w)


def _t5_bucket_np(dist):
    dist = np.asarray(dist, np.int32)
    max_exact = NUM_BUCKETS // 2
    d = np.maximum(dist, 1).astype(np.float32)
    large = max_exact + (np.log(d / max_exact) / math.log(MAX_DISTANCE / max_exact) * (NUM_BUCKETS - max_exact)).astype(np.int32)
    large = np.minimum(large, NUM_BUCKETS - 1)
    return np.where(dist < max_exact, dist, large).astype(np.int32)


def _band_bias(bias_tab, dilation):
    i = np.arange(BAND)[:, None]
    j = np.arange(2 * BAND)[None, :]
    u = i + BAND - j
    in_band = (u >= 0) & (u <= BAND)
    bucket = _t5_bucket_np(np.clip(u, 0, BAND) * dilation)
    onehot = np.asarray(bucket[:, :, None] == np.arange(NUM_BUCKETS)[None, None, :], np.float32)
    bias = jnp.einsum('pqb,bh->hpq', jnp.asarray(onehot), bias_tab.astype(F32), precision=lax.Precision.HIGHEST)
    general = jnp.where(in_band[None], bias, NEG_BIG)
    first = jnp.where((in_band & (j >= BAND))[None], bias, NEG_BIG)
    return jnp.stack([first, general])


LANES = 128


def _proj_perm_kernel(*refs, r):
    n_slab = D_MODEL // LANES
    x_refs, (w_ref, o_ref, xp_ref) = refs[:n_slab], refs[n_slab:]
    tm = xp_ref.shape[0]
    n = tm // r
    for j, x_ref in enumerate(x_refs):
        for c in range(r):
            rows = x_ref[pl.ds(c, n, stride=r), :] if r > 1 else x_ref[...]
            xp_ref[c * n:(c + 1) * n, j * LANES:(j + 1) * LANES] = rows.astype(BF16)
    res = jnp.dot(xp_ref[...], w_ref[...], preferred_element_type=F32).astype(o_ref.dtype)
    for c in range(r):
        o_ref[c] = res[c * n:(c + 1) * n]


PROJ_A_TILE = 1024


def _proj_classes(x, w, r, name):
    b, s, d = x.shape
    n = w.shape[1]
    tm = PROJ_A_TILE
    n_slab = d // LANES
    slabs = [pl.BlockSpec((None, tm, LANES), lambda bi, t, j=j: (bi, t, j)) for j in range(n_slab)]
    return pl.pallas_call(
        functools.partial(_proj_perm_kernel, r=r),
        grid=(b, s // tm),
        in_specs=slabs + [pl.BlockSpec((d, n), lambda bi, t: (0, 0))],
        out_specs=pl.BlockSpec((None, r, tm // r, n), lambda bi, t: (bi, 0, t, 0)),
        out_shape=jax.ShapeDtypeStruct((b, r, s // r, n), BF16),
        scratch_shapes=[pltpu.VMEM((tm, d), BF16)],
        compiler_params=_cparams(("arbitrary", "arbitrary")),
        name=name,
    )(*([x] * n_slab), w)


BAND_SPAN = 2048
_HI_MASK = 0xFFFF0000
_LSE_LANES = LANES // HEADS_A


def _pack_bf16_pair(a, b):
    abits = pltpu.bitcast(a.astype(BF16).astype(F32), jnp.uint32)
    bbits = pltpu.bitcast(b.astype(BF16).astype(F32), jnp.uint32)
    return (abits >> 16) | (bbits & jnp.uint32(_HI_MASK))


def _unpack_bf16_pair(word):
    return pltpu.bitcast(word << 16, F32), pltpu.bitcast(word & jnp.uint32(_HI_MASK), F32)


def _band_attn_kernel(q_ref, k_ref, v_ref, kp_ref, vp_ref, bias_ref, o_ref, lse_ref, *, r):
    nq = q_ref.shape[1] // BAND
    variant = jnp.minimum(pl.program_id(1), 1)
    ones = jnp.ones((BAND, HEAD_DIM), BF16)
    lane = lax.broadcasted_iota(jnp.int32, (BAND, LANES), 1)
    heads = [slice(h * HEAD_DIM, (h + 1) * HEAD_DIM) for h in range(HEADS_A)]
    units = [(c, j) for c in range(r) for j in range(nq)]

    def prev_of(k_or_v_ref, prev_ref, c, j, sl):
        return prev_ref[c, :, sl] if j == 0 else k_or_v_ref[c, (j - 1) * BAND:j * BAND, sl]

    def scores(c, j):
        cur = slice(j * BAND, (j + 1) * BAND)
        out = []
        for h, sl in enumerate(heads):
            q = q_ref[c, cur, sl]
            bias = bias_ref[variant, h] if j == 0 else bias_ref[1, h]
            sp = lax.dot_general(q, prev_of(k_ref, kp_ref, c, j, sl), _NT, preferred_element_type=F32)
            sc = lax.dot_general(q, k_ref[c, cur, sl], _NT, preferred_element_type=F32)
            out.append((sp + bias[:, :BAND], sc + bias[:, BAND:]))
        return out

    nxt = scores(*units[0])
    for u, (c, j) in enumerate(units):
        s_all = nxt
        if u + 1 < len(units):
            nxt = scores(*units[u + 1])
        cur = slice(j * BAND, (j + 1) * BAND)
        rows = pl.ds(j * BAND * r + c, BAND, stride=r) if r > 1 else cur
        probs = []
        for sp, sc in s_all:
            m = jnp.max(jnp.maximum(sp, sc), axis=-1, keepdims=True)
            probs.append((m, jnp.exp(sp - m).astype(BF16), jnp.exp(sc - m).astype(BF16)))
        outs, lses = [], []
        for h, sl in enumerate(heads):
            m, pp, pc = probs[h]
            l = jnp.dot(pp, ones, preferred_element_type=F32) + jnp.dot(pc, ones, preferred_element_type=F32)
            o = jnp.dot(pp, prev_of(v_ref, vp_ref, c, j, sl), preferred_element_type=F32)
            o = o + jnp.dot(pc, v_ref[c, cur, sl], preferred_element_type=F32)
            outs.append(o / l)
            lses.append(m + jnp.log(l))
        for p in range(HEADS_A // 2):
            o_ref[p, rows, :] = _pack_bf16_pair(outs[2 * p], outs[2 * p + 1])
        lse = lses[-1]
        for h in range(HEADS_A - 2, -1, -1):
            lse = jnp.where(lane < (h + 1) * _LSE_LANES, lses[h], lse)
        lse_ref[rows, :] = lse


def _band_attention(pg, bias, name):
    b, r, l, _ = pg.shape
    nq = BAND_SPAN // (BAND * r)
    cur = lambda which: pl.BlockSpec((None, r, nq * BAND, WIDTH_A), lambda bi, n: (bi, 0, n, which))
    prev = lambda which: pl.BlockSpec((None, r, BAND, WIDTH_A), lambda bi, n: (bi, 0, jnp.maximum(n * nq - 1, 0), which))
    return pl.pallas_call(
        functools.partial(_band_attn_kernel, r=r),
        grid=(b, l * r // BAND_SPAN),
        in_specs=[cur(0), cur(1), cur(2), prev(1), prev(2), pl.BlockSpec(bias.shape, lambda bi, n: (0, 0, 0, 0))],
        out_specs=[pl.BlockSpec((None, HEADS_A // 2, BAND_SPAN, HEAD_DIM), lambda bi, n: (bi, 0, n, 0)),
                   pl.BlockSpec((None, BAND_SPAN, LANES), lambda bi, n: (bi, n, 0))],
        out_shape=[jax.ShapeDtypeStruct((b, HEADS_A // 2, l * r, HEAD_DIM), jnp.uint32),
                   jax.ShapeDtypeStruct((b, l * r, LANES), F32)],
        compiler_params=_cparams(("arbitrary", "arbitrary")),
        name=name,
    )(pg, pg, pg, pg, pg, bias)


HGRN_TILE = 512
_N_CHUNK = HGRN_TILE // CHUNK_B


def _dynamic_row(ref, r, ls):
    blk = ref[pl.ds(pl.multiple_of((r // 8) * 8, 8), 8), ls]
    sub = lax.broadcasted_iota(jnp.int32, blk.shape, 0)
    return jnp.sum(jnp.where(sub == r % 8, blk, 0.0), axis=0, keepdims=True)


def _sigmoid(z):
    return 0.5 * jnp.tanh(0.5 * z) + 0.5


_PROJ_PIECE = 256


def _hgrn_kernel(x_ref, xn_ref, w_ref, lbl_ref, ng_ref, o_ref, pq_ref, pf_ref, pi_ref, pg_ref, xb_ref,
                 st_ref, qs_ref, ks_ref, kh_ref, vs_ref, bs_ref, ebl_ref, a_ref, os_ref, gs_ref, kf_ref, qsil_ref,
                 *, tiles_per_seq):
    i = pl.program_id(0)
    parts = (pq_ref, pf_ref, pi_ref, pg_ref)
    per_part = WIDTH_B // _PROJ_PIECE

    def piece(k):
        cols = slice(k * _PROJ_PIECE, (k + 1) * _PROJ_PIECE)
        local = slice((k % per_part) * _PROJ_PIECE, (k % per_part + 1) * _PROJ_PIECE)
        parts[k // per_part][:, local] = jnp.dot(xb_ref[...], w_ref[:, cols], preferred_element_type=F32)

    @pl.when(i == 0)
    def _():
        xb_ref[...] = x_ref[...].astype(BF16)
        for k in range(COLS_B // _PROJ_PIECE):
            piece(k)

    @pl.when(lax.rem(i, tiles_per_seq) == 0)
    def _():
        st_ref[...] = jnp.zeros_like(st_ref)

    xb_ref[...] = xn_ref[...].astype(BF16)
    pieces = [k for part in (2, 3, 1, 0) for k in range(part * per_part, (part + 1) * per_part)]
    next_piece = lambda: piece(pieces.pop(0)) if pieces else None
    vs_ref[...] = pi_ref[...].astype(BF16)
    gv = pg_ref[...]
    gs_ref[...] = gv * _sigmoid(gv) * ng_ref[...]

    lg = lbl_ref[...]
    e = jnp.exp(lg - jnp.max(lg, axis=0, keepdims=True))
    lb = e[0:1, :] / jnp.sum(e, axis=0, keepdims=True)

    row = lax.broadcasted_iota(jnp.int32, (CHUNK_B, CHUNK_B), 0)
    colm = lax.broadcasted_iota(jnp.int32, (CHUNK_B, CHUNK_B), 1)
    causal = row >= colm
    tri = causal.astype(BF16)

    for _ in range(2):
        next_piece()
    f = lb + (1.0 - lb) * _sigmoid(pf_ref[...])
    logf = jnp.log(f)
    kf = 1.0 - f
    kf_ref[...] = kf
    l_hi = logf.astype(BF16)
    rem = logf - l_hi.astype(F32)
    l_mid = rem.astype(BF16)
    l_lo = (rem - l_mid.astype(F32)).astype(BF16)
    min_b = None
    for c in range(_N_CHUNK):
        rs = slice(c * CHUNK_B, (c + 1) * CHUNK_B)
        b = (jnp.dot(tri, l_hi[rs], preferred_element_type=F32) + jnp.dot(tri, l_mid[rs], preferred_element_type=F32)
             + jnp.dot(tri, l_lo[rs], preferred_element_type=F32))
        bs_ref[rs, :] = b
        cm = jnp.min(b[CHUNK_B - 1:CHUNK_B, :])
        min_b = cm if min_b is None else jnp.minimum(min_b, cm)
        if c % 2 == 1:
            next_piece()
    safe = min_b > -SAFE_LOG_DECAY

    qv = pq_ref[...]
    qsil_ref[...] = qv * _sigmoid(qv)
    for c in range(_N_CHUNK):
        rs = slice(c * CHUNK_B, (c + 1) * CHUNK_B)
        b = bs_ref[rs, :]
        eb_last = jnp.exp(b[CHUNK_B - 1:CHUNK_B, :])
        ebl_ref[c:c + 1, :] = eb_last
        qs_ref[rs, :] = (qsil_ref[rs, :] * jnp.exp(b)).astype(BF16)
        k_grown = kf_ref[rs, :] * jnp.exp(-b)
        ks_ref[rs, :] = k_grown.astype(BF16)
        kh_ref[rs, :] = (k_grown * eb_last).astype(BF16)
        next_piece()

    @pl.when(safe)
    def _():
        for c in range(_N_CHUNK):
            rs = slice(c * CHUNK_B, (c + 1) * CHUNK_B)
            for h in range(N_HEADS_B):
                ls = slice(h * HEAD_DIM, (h + 1) * HEAD_DIM)
                a = lax.dot_general(qs_ref[rs, ls], ks_ref[rs, ls], _NT, preferred_element_type=F32)
                a_ref[c * N_HEADS_B + h] = jnp.where(causal, a, 0.0)

    @pl.when(jnp.logical_not(safe))
    def _():
        lane = lax.broadcasted_iota(jnp.int32, (1, CHUNK_B), 1)
        trow = lax.broadcasted_iota(jnp.int32, (CHUNK_B, 1), 0)
        for c in range(_N_CHUNK):
            rs = slice(c * CHUNK_B, (c + 1) * CHUNK_B)
            b = bs_ref[rs, :]
            kh_ref[rs, :] = (kf_ref[rs, :] * jnp.exp(b[CHUNK_B - 1:CHUNK_B, :] - b)).astype(BF16)
            for h in range(N_HEADS_B):
                ls = slice(h * HEAD_DIM, (h + 1) * HEAD_DIM)
                bq = bs_ref[rs, ls]
                qh = qsil_ref[rs, ls]

                def body(s, acc, c=c, ls=ls, bq=bq, qh=qh):
                    b_s = _dynamic_row(bs_ref, c * CHUNK_B + s, ls)
                    k_s = _dynamic_row(kf_ref, c * CHUNK_B + s, ls)
                    w = jnp.exp(jnp.minimum(bq - b_s, 0.0))
                    colv = jnp.sum(qh * k_s * w, axis=-1, keepdims=True)
                    colv = jnp.where(trow >= s, colv, 0.0)
                    return acc + colv * (lane == s).astype(F32)

                a_ref[c * N_HEADS_B + h] = lax.fori_loop(0, CHUNK_B, body, jnp.zeros((CHUNK_B, CHUNK_B), F32))

    for c in range(_N_CHUNK):
        rs = slice(c * CHUNK_B, (c + 1) * CHUNK_B)
        for h in range(N_HEADS_B):
            ls = slice(h * HEAD_DIM, (h + 1) * HEAD_DIM)
            st = st_ref[h]
            vh = vs_ref[rs, ls]
            o = lax.dot_general(qs_ref[rs, ls], st.astype(BF16), _NT, preferred_element_type=F32)
            o = o + jnp.dot(a_ref[c * N_HEADS_B + h].astype(BF16), vh, preferred_element_type=F32)
            os_ref[rs, ls] = o
            st_ref[h] = st * ebl_ref[c:c + 1, ls] + lax.dot_general(vh, kh_ref[rs, ls], _TN, preferred_element_type=F32)
            if h == N_HEADS_B - 1:
                next_piece()
    while pieces:
        next_piece()

    for h in range(N_HEADS_B):
        ls = slice(h * HEAD_DIM, (h + 1) * HEAD_DIM)
        o = os_ref[:, ls]
        o = o * lax.rsqrt(jnp.mean(o * o, axis=-1, keepdims=True) + RMS_EPS)
        o_ref[:, ls] = (o * gs_ref[:, ls]).astype(o_ref.dtype)


def _hgrn2(x, w_b, lb_logits, norm_g):
    b, s, d = x.shape
    tb = HGRN_TILE
    n = b * s // tb
    n_slots = lb_logits.shape[0]
    out = pl.pallas_call(
        functools.partial(_hgrn_kernel, tiles_per_seq=s // tb),
        grid=(n,),
        in_specs=[pl.BlockSpec((tb, d), lambda i: (i, 0)),
                  pl.BlockSpec((tb, d), lambda i: (jnp.minimum(i + 1, n - 1), 0)),
                  pl.BlockSpec(w_b.shape, lambda i: (0, 0), pipeline_mode=pl.Buffered(1)),
                  pl.BlockSpec((n_slots, WIDTH_B), lambda i: (0, 0)),
                  pl.BlockSpec((1, WIDTH_B), lambda i: (0, 0))],
        out_specs=pl.BlockSpec((tb, WIDTH_B), lambda i: (i, 0)),
        out_shape=jax.ShapeDtypeStruct((b * s, WIDTH_B), BF16),
        scratch_shapes=[
            pltpu.VMEM((tb, WIDTH_B), F32),
            pltpu.VMEM((tb, WIDTH_B), F32),
            pltpu.VMEM((tb, WIDTH_B), F32),
            pltpu.VMEM((tb, WIDTH_B), F32),
            pltpu.VMEM((tb, d), BF16),
            pltpu.VMEM((N_HEADS_B, HEAD_DIM, HEAD_DIM), F32),
            pltpu.VMEM((tb, WIDTH_B), BF16),
            pltpu.VMEM((tb, WIDTH_B), BF16),
            pltpu.VMEM((tb, WIDTH_B), BF16),
            pltpu.VMEM((tb, WIDTH_B), BF16),
            pltpu.VMEM((tb, WIDTH_B), F32),
            pltpu.VMEM((8, WIDTH_B), F32),
            pltpu.VMEM((_N_CHUNK * N_HEADS_B, CHUNK_B, CHUNK_B), F32),
            pltpu.VMEM((tb, WIDTH_B), F32),
            pltpu.VMEM((tb, WIDTH_B), F32),
            pltpu.VMEM((tb, WIDTH_B), F32),
            pltpu.VMEM((tb, WIDTH_B), F32),
        ],
        compiler_params=_cparams(("arbitrary",)),
        name="hgrn2",
    )(x.reshape(b * s, d), x.reshape(b * s, d), w_b, lb_logits, norm_g)
    return out.reshape(b, s, WIDTH_B)


MERGE_TILE = 512


def _layer_norm(y, g, b):
    mu = jnp.mean(y, axis=-1, keepdims=True)
    d = y - mu
    var = jnp.mean(d * d, axis=-1, keepdims=True)
    return d * lax.rsqrt(var + LN_EPS) * g + b


def _merge_kernel(x_ref, o1_ref, l1_ref, o2_ref, l2_ref, o3_ref, l3_ref, ob_ref, mk_ref, mv_ref,
                  wc_ref, wg_ref, wa_ref, wb_ref, wcc_ref, wo_ref, g_ref, b_ref, wr_ref, br_ref,
                  h_ref, code_ref, wrow_ref, cnt_ref, oc_ref, cnt_sc, *, n_tokens):
    x = x_ref[...]
    xb = x.astype(BF16)

    qc = jnp.dot(xb, wc_ref[...], preferred_element_type=F32).astype(BF16)
    heads_c = [slice(h * HEAD_DIM, (h + 1) * HEAD_DIM) for h in range(N_HEADS_C)]
    s_all = [lax.dot_general(qc[:, ls], mk_ref[:, ls], _NT, preferred_element_type=F32) for ls in heads_c]
    gate_pre = [None] * 3
    gate_pre[0] = jnp.dot(xb, wg_ref[:, :D_MODEL], preferred_element_type=F32)
    p_all = [jnp.exp(s - jnp.max(s, axis=-1, keepdims=True)).astype(BF16) for s in s_all]
    ones = jnp.ones((mk_ref.shape[0], HEAD_DIM), BF16)
    for ls, p in zip(heads_c, p_all):
        l = jnp.dot(p, ones, preferred_element_type=F32)
        oc_ref[:, ls] = (jnp.dot(p, mv_ref[:, ls], preferred_element_type=F32) / l).astype(BF16)
    gate_pre[1] = jnp.dot(xb, wg_ref[:, D_MODEL:2 * D_MODEL], preferred_element_type=F32)
    branch_b = jnp.dot(ob_ref[...], wb_ref[...], preferred_element_type=F32)
    gate_pre[2] = jnp.dot(xb, wg_ref[:, 2 * D_MODEL:], preferred_element_type=F32)

    lses = [l1_ref[...], l2_ref[...], l3_ref[...]]
    m = functools.reduce(jnp.maximum, lses)
    es = [jnp.exp(l - m) for l in lses]
    inv = 1.0 / functools.reduce(lambda a, c: a + c, es)
    mix = [e * inv for e in es]
    o_pairs = [[_unpack_bf16_pair(o_ref[p]) for p in range(HEADS_A // 2)] for o_ref in (o1_ref, o2_ref, o3_ref)]
    oa_heads = []
    for h in range(HEADS_A):
        acc = None
        for gi in range(N_GROUPS_A):
            w = jnp.broadcast_to(mix[gi][:, h * _LSE_LANES:h * _LSE_LANES + 1], (x.shape[0], HEAD_DIM))
            term = w * o_pairs[gi][h // 2][h % 2]
            acc = term if acc is None else acc + term
        oa_heads.append(acc.astype(BF16))
    oa = jnp.concatenate(oa_heads, axis=-1)

    branch_a = jnp.dot(oa, wa_ref[...], preferred_element_type=F32)
    branch_c = jnp.dot(oc_ref[...], wcc_ref[...], preferred_element_type=F32)
    merged = _sigmoid(gate_pre[1]) * branch_b
    merged = merged + _sigmoid(gate_pre[0]) * branch_a
    merged = merged + _sigmoid(gate_pre[2]) * branch_c
    y = DN_ALPHA * x + jnp.dot(merged.astype(BF16), wo_ref[...], preferred_element_type=F32)
    hv = _layer_norm(y, g_ref[...], b_ref[...])
    h_ref[...] = hv

    first = jnp.logical_and(pl.program_id(0) == 0, pl.program_id(1) == 0)
    _route(hv, wr_ref, br_ref, code_ref, wrow_ref, cnt_ref, cnt_sc, first, n_tokens)


def _merge(x, oa_parts, ob, mk, mv, wc, wg, wa, wb, wcc, wo, ln_g, ln_b, wr, br):
    b, s, d = x.shape
    tm = MERGE_TILE
    nt = s // tm
    tok = lambda w: pl.BlockSpec((None, tm, w), lambda bi, t: (bi, t, 0))
    full = lambda a: pl.BlockSpec(a.shape, lambda bi, t: (0,) * a.ndim, pipeline_mode=pl.Buffered(1))
    mem = pl.BlockSpec((None,) + mk.shape[1:], lambda bi, t: (bi, 0, 0))
    o_spec = pl.BlockSpec((None, HEADS_A // 2, tm, HEAD_DIM), lambda bi, t: (bi, 0, t, 0))
    flat = [a for pair in oa_parts for a in pair]
    return pl.pallas_call(
        functools.partial(_merge_kernel, n_tokens=b * s),
        grid=(b, nt),
        in_specs=[tok(d)] + [o_spec, tok(LANES)] * N_GROUPS_A + [tok(WIDTH_B), mem, mem,
                  full(wc), full(wg), full(wa), full(wb), full(wcc), full(wo), full(ln_g), full(ln_b), full(wr), full(br)],
        out_specs=[tok(d), pl.BlockSpec((1, tm), lambda bi, t: (0, bi * nt + t)),
                   pl.BlockSpec((tm, LANES), lambda bi, t: (bi * nt + t, 0)),
                   pl.BlockSpec((N_ROUTE_CLASSES, LANES), lambda bi, t: (0, 0))],
        out_shape=[jax.ShapeDtypeStruct((b, s, d), F32), jax.ShapeDtypeStruct((1, b * s), jnp.int32),
                   jax.ShapeDtypeStruct((b * s, LANES), F32), jax.ShapeDtypeStruct((N_ROUTE_CLASSES, LANES), jnp.int32)],
        scratch_shapes=[pltpu.VMEM((tm, WIDTH_C), BF16), pltpu.VMEM((N_ROUTE_CLASSES, LANES), F32)],
        compiler_params=_cparams(("arbitrary", "arbitrary")),
        name="merge",
    )(x, *flat, ob, mk, mv, wc, wg, wa, wb, wcc, wo, ln_g, ln_b, wr, br)


N_ROUTE_CLASSES = N_EXPERT_GROUPS * EXPERTS_PER_GROUP * EXPERTS_PER_GROUP
_ROUTER_ROWS = 8 + N_EXPERTS


def _route(hv, wr_ref, br_ref, dest_ref, wrow_ref, cnt_ref, cnt_sc, first_step, group_capacity):
    tm = hv.shape[0]

    @pl.when(first_step)
    def _():
        cnt_sc[...] = jnp.zeros_like(cnt_sc)

    h_hi = hv.astype(BF16)
    h_lo = (hv - h_hi.astype(F32)).astype(BF16)
    wv = wr_ref[...]
    w_hi = wv.astype(BF16)
    w_lo = (wv - w_hi.astype(F32)).astype(BF16)
    both = jnp.dot(h_hi, jnp.concatenate([w_hi, w_lo], axis=1), preferred_element_type=F32)
    logits = both[:, :LANES] + both[:, LANES:] + jnp.dot(h_lo, w_hi, preferred_element_type=F32)
    logits = logits.T[:_ROUTER_ROWS, :] + br_ref[:, 0:1]
    g = [logits[i:i + 1, :] for i in range(N_EXPERT_GROUPS)]
    gmax = functools.reduce(jnp.maximum, g)
    gsel = jnp.full_like(gmax, N_EXPERT_GROUPS - 1).astype(jnp.int32)
    for i in range(N_EXPERT_GROUPS - 2, -1, -1):
        gsel = jnp.where(g[i] == gmax, i, gsel)
    gprob = 1.0 / functools.reduce(lambda a, c: a + c, [jnp.exp(gi - gmax) for gi in g])

    esel = logits[8 + (N_EXPERT_GROUPS - 1) * EXPERTS_PER_GROUP:8 + N_EXPERTS, :]
    for i in range(N_EXPERT_GROUPS - 2, -1, -1):
        esel = jnp.where(gsel == i, logits[8 + i * EXPERTS_PER_GROUP:8 + (i + 1) * EXPERTS_PER_GROUP, :], esel)
    ridx = lax.broadcasted_iota(jnp.int32, (EXPERTS_PER_GROUP, tm), 0)
    v1 = jnp.max(esel, axis=0, keepdims=True)
    i1 = jnp.min(jnp.where(esel == v1, ridx, EXPERTS_PER_GROUP), axis=0, keepdims=True)
    rest = jnp.where(ridx == i1, -jnp.inf, esel)
    v2 = jnp.max(rest, axis=0, keepdims=True)
    i2 = jnp.min(jnp.where(rest == v2, ridx, EXPERTS_PER_GROUP), axis=0, keepdims=True)
    t = jnp.exp(v2 - v1)
    w1 = gprob / (1.0 + t)
    w2 = gprob * t / (1.0 + t)
    wt = jnp.where(ridx == i1, w1, jnp.where(ridx == i2, w2, 0.0))
    eye = (lax.broadcasted_iota(jnp.int32, (EXPERTS_PER_GROUP, LANES), 0)
           == lax.broadcasted_iota(jnp.int32, (EXPERTS_PER_GROUP, LANES), 1)).astype(F32)
    wrow_ref[...] = lax.dot_general(wt, eye, _TN, preferred_element_type=F32, precision=lax.Precision.HIGHEST)

    cls = (gsel * EXPERTS_PER_GROUP + jnp.minimum(i1, i2)) * EXPERTS_PER_GROUP + jnp.maximum(i1, i2)
    crow = lax.broadcasted_iota(jnp.int32, (N_ROUTE_CLASSES, tm), 0)
    onehot = (crow == cls).astype(F32)
    before = (lax.broadcasted_iota(jnp.int32, (tm, tm), 0) < lax.broadcasted_iota(jnp.int32, (tm, tm), 1))
    excl = jnp.dot(onehot.astype(BF16), before.astype(BF16), preferred_element_type=F32)
    pos = cnt_sc[:, 0:1] + excl
    rank = jnp.sum(onehot * pos, axis=0, keepdims=True).astype(jnp.int32)
    dest_ref[...] = cls * group_capacity + rank
    cnt_sc[...] = cnt_sc[...] + jnp.sum(onehot, axis=1, keepdims=True)
    cnt_ref[...] = cnt_sc[...].astype(jnp.int32)


DISPATCH_TILE = 512


SUBLANES = 8
_HALF = D_MODEL // 2


def _tile_copy(src_ref, src_tok, dst_ref, dst_tok, sem):
    src = src_ref.at[pl.ds(pl.multiple_of(src_tok * SUBLANES, SUBLANES), SUBLANES)]
    dst = dst_ref.at[pl.ds(pl.multiple_of(dst_tok * SUBLANES, SUBLANES), SUBLANES)]
    return pltpu.make_async_copy(src, dst, sem)


def _tile_rows(ref, j, n):
    return ref.at[pl.ds(j, n, stride=SUBLANES), :]


def _all_tiles_wait(src_ref, dst_ref, sem):
    pltpu.make_async_copy(src_ref, dst_ref.at[pl.ds(0, src_ref.shape[0])], sem).wait()


def _dispatch_kernel(zblk_ref, dest_ref, dest_prev_ref, h_ref, w_ref, xbuf_ref, pay0_ref, pay1_ref, zero_ref,
                     sems, zsem):
    i = pl.program_id(0)
    n_steps = pl.num_programs(0)
    tm = h_ref.shape[0]

    @pl.when(i == 0)
    def _():
        zero_ref[...] = jnp.zeros_like(zero_ref)
        for k in range(zblk_ref.shape[0]):
            @pl.when(zblk_ref[k] >= 0)
            def _():
                start = pl.multiple_of(zblk_ref[k] * zero_ref.shape[0], zero_ref.shape[0])
                cp = pltpu.make_async_copy(zero_ref, xbuf_ref.at[pl.ds(start, zero_ref.shape[0])], zsem)
                cp.start()
                cp.wait()

    def build(pay):
        hv = h_ref[...]
        for j in range(_HALF // LANES):
            lo, hi = hv[:, j * LANES:(j + 1) * LANES], hv[:, _HALF + j * LANES:_HALF + (j + 1) * LANES]
            _tile_rows(pay, j, tm)[...] = _pack_bf16_pair(lo, hi)
        _tile_rows(pay, _HALF // LANES, tm)[...] = pltpu.bitcast(w_ref[...], jnp.uint32)
        for j in range(_HALF // LANES + 1, SUBLANES):
            _tile_rows(pay, j, tm)[...] = jnp.zeros((tm, LANES), jnp.uint32)

    def send_own_and_drain(pay, sem, pay_prev, sem_prev):
        def issue(t, carry):
            _tile_copy(pay, t, xbuf_ref, dest_ref[0, t], sem).start()
            return carry

        lax.fori_loop(0, tm, issue, 0, unroll=8)
        _all_tiles_wait(pay, xbuf_ref, sem)
        if pay_prev is not None:
            _all_tiles_wait(pay_prev, xbuf_ref, sem_prev)

    @pl.when(i == 0)
    def _():
        build(pay0_ref)
        pl.when(n_steps == 1)(functools.partial(send_own_and_drain, pay0_ref, sems.at[0], None, None))

    def step(pay, sem, pay_prev, sem_prev):
        pl.when(i >= 2)(functools.partial(_all_tiles_wait, pay, xbuf_ref, sem))
        for t in range(tm):
            _tile_copy(pay_prev, t, xbuf_ref, dest_prev_ref[0, t], sem_prev).start()
        build(pay)
        pl.when(i == n_steps - 1)(functools.partial(send_own_and_drain, pay, sem, pay_prev, sem_prev))

    odd = lax.rem(i, 2) == 1
    pl.when(odd)(functools.partial(step, pay1_ref, sems.at[1], pay0_ref, sems.at[0]))
    pl.when(jnp.logical_and(jnp.logical_not(odd), i >= 2))(
        functools.partial(step, pay0_ref, sems.at[0], pay1_ref, sems.at[1]))


def _dispatch(h2d, wrow, dest, zero_blocks, n_blocks):
    t, d = h2d.shape
    tm = DISPATCH_TILE
    grid_spec = pltpu.PrefetchScalarGridSpec(
        num_scalar_prefetch=1,
        grid=(t // tm,),
        in_specs=[pl.BlockSpec((1, tm), lambda i, zb: (0, i), memory_space=pltpu.SMEM),
                  pl.BlockSpec((1, tm), lambda i, zb: (0, jnp.maximum(i - 1, 0)), memory_space=pltpu.SMEM),
                  pl.BlockSpec((tm, d), lambda i, zb: (i, 0)),
                  pl.BlockSpec((tm, LANES), lambda i, zb: (i, 0))],
        out_specs=pl.BlockSpec(memory_space=pl.ANY),
        scratch_shapes=[pltpu.VMEM((tm * SUBLANES, LANES), jnp.uint32),
                        pltpu.VMEM((tm * SUBLANES, LANES), jnp.uint32),
                        pltpu.VMEM((EXPERT_BLOCK * SUBLANES, LANES), jnp.uint32),
                        pltpu.SemaphoreType.DMA((2,)), pltpu.SemaphoreType.DMA(())],
    )
    return pl.pallas_call(
        _dispatch_kernel,
        grid_spec=grid_spec,
        out_shape=jax.ShapeDtypeStruct((n_blocks * EXPERT_BLOCK * SUBLANES, LANES), jnp.uint32),
        compiler_params=_cparams(("arbitrary",)),
        name="dispatch",
    )(zero_blocks, dest, dest, h2d, wrow)


def _expert_kernel(bg_ref, pres_ref, x_ref, wg_ref, wu_ref, wd_ref, y_ref, xs_ref, ws_ref, ys_ref):
    i = pl.program_id(0)
    eb = xs_ref.shape[0]
    lo, hi = [], []
    for j in range(_HALF // LANES):
        lo_j, hi_j = _unpack_bf16_pair(_tile_rows(x_ref, j, eb)[...])
        lo.append(lo_j)
        hi.append(hi_j)
    xs_ref[...] = jnp.concatenate(lo + hi, axis=-1).astype(BF16)
    ws_ref[...] = pltpu.bitcast(_tile_rows(x_ref, _HALF // LANES, eb)[...], F32)
    ys_ref[...] = jnp.zeros_like(ys_ref)
    halves = [slice(0, eb // 2), slice(eb // 2, eb)]

    for e in range(EXPERTS_PER_GROUP):
        @pl.when(pres_ref[i * EXPERTS_PER_GROUP + e] > 0)
        def _(e=e):
            gu = [(jnp.dot(xs_ref[p, :], wg_ref[e], preferred_element_type=F32),
                   jnp.dot(xs_ref[p, :], wu_ref[e], preferred_element_type=F32)) for p in halves]
            for p, (a, u) in zip(halves, gu):
                we = ws_ref[p, e:e + 1]
                hid = jnp.where(we != 0.0, a * _sigmoid(a) * u * we, 0.0).astype(BF16)
                ys_ref[p, :] += jnp.dot(hid, wd_ref[e], preferred_element_type=F32)

    for j in range(SUBLANES):
        _tile_rows(y_ref, j, eb)[...] = ys_ref[:, j * LANES:(j + 1) * LANES]


def _experts(xbuf, block_g, present, wg, wu, wd):
    eb = EXPERT_BLOCK
    blk = pl.BlockSpec((eb * SUBLANES, LANES), lambda i, bg, pres: (i, 0))
    gw = lambda a: pl.BlockSpec((EXPERTS_PER_GROUP,) + a.shape[1:], lambda i, bg, pres: (bg[i], 0, 0))
    grid_spec = pltpu.PrefetchScalarGridSpec(
        num_scalar_prefetch=2,
        grid=(block_g.shape[0],),
        in_specs=[blk, gw(wg), gw(wu), gw(wd)],
        out_specs=blk,
        scratch_shapes=[pltpu.VMEM((eb, D_MODEL), BF16), pltpu.VMEM((eb, LANES), F32), pltpu.VMEM((eb, D_MODEL), F32)],
    )
    return pl.pallas_call(
        _expert_kernel,
        grid_spec=grid_spec,
        out_shape=jax.ShapeDtypeStruct(xbuf.shape, F32),
        compiler_params=_cparams(("arbitrary",)),
        name="experts",
    )(block_g, present, xbuf, wg, wu, wd)


FINAL_TILE = 512


def _final_kernel(dest_ref, dest_next_ref, h_ref, g_ref, b_ref, ybuf_ref, o_ref, ys0_ref, ys1_ref, sems):
    i = pl.program_id(0)
    n_steps = pl.num_programs(0)
    tm = h_ref.shape[0]

    def wait_all(ys, sem):
        pltpu.make_async_copy(ybuf_ref.at[pl.ds(0, ys.shape[0])], ys, sem).wait()

    @pl.when(i == 0)
    def _():
        def issue(t, carry):
            _tile_copy(ybuf_ref, dest_ref[0, t], ys0_ref, t, sems.at[0]).start()
            return carry

        lax.fori_loop(0, tm, issue, 0, unroll=8)

    def step(ys, sem, ys_next, sem_next):
        wait_all(ys, sem)
        for t in range(tm):
            _tile_copy(ybuf_ref, dest_next_ref[0, t], ys_next, t, sem_next).start()
        y = jnp.concatenate([_tile_rows(ys, j, tm)[...] for j in range(SUBLANES)], axis=-1)
        o_ref[...] = _layer_norm(DN_ALPHA * h_ref[...] + y, g_ref[...], b_ref[...])

        @pl.when(i == n_steps - 1)
        def _():
            wait_all(ys_next, sem_next)

    even = lax.rem(i, 2) == 0
    pl.when(even)(functools.partial(step, ys0_ref, sems.at[0], ys1_ref, sems.at[1]))
    pl.when(jnp.logical_not(even))(functools.partial(step, ys1_ref, sems.at[1], ys0_ref, sems.at[0]))


def _final(h2d, ybuf, dest, ln_g, ln_b):
    t, d = h2d.shape
    tm = FINAL_TILE
    n = t // tm
    row = pl.BlockSpec((tm, d), lambda i: (i, 0))
    vec = pl.BlockSpec((1, d), lambda i: (0, 0))
    return pl.pallas_call(
        _final_kernel,
        grid=(n,),
        in_specs=[pl.BlockSpec((1, tm), lambda i: (0, i), memory_space=pltpu.SMEM),
                  pl.BlockSpec((1, tm), lambda i: (0, jnp.minimum(i + 1, n - 1)), memory_space=pltpu.SMEM),
                  row, vec, vec, pl.BlockSpec(memory_space=pl.ANY)],
        out_specs=row,
        out_shape=jax.ShapeDtypeStruct((t, d), F32),
        scratch_shapes=[pltpu.VMEM((tm * SUBLANES, LANES), F32), pltpu.VMEM((tm * SUBLANES, LANES), F32),
                        pltpu.SemaphoreType.DMA((2,))],
        compiler_params=_cparams(("arbitrary",)),
        name="final",
    )(dest, dest, h2d, ln_g, ln_b, ybuf)


def kernel(x, mem, rel_bias, hgrn_lb_logits, w_in, w_mem_kv, hgrn_norm_g, w_branch_a, w_branch_b, w_branch_c, w_out, ln1_g, ln1_b, w_router_group, b_router_group, w_router_expert, b_router_expert, w_exp_gate, w_exp_up, w_exp_down, ln2_g, ln2_b):
    b, s, d = x.shape
    t = b * s
    assert d == D_MODEL and w_in.shape[0] == DEPTH == 1
    assert s % (BAND * DILATED_GROUPS[-1][1]) == 0 and s % HGRN_TILE == 0 and s % MERGE_TILE == 0 and t % EXPERT_BLOCK == 0
    li = 0
    scale = HEAD_DIM ** -0.5
    x2d = x.reshape(t, d)

    w = w_in[li]
    col_scale = np.ones((COLS_A,), np.float32).reshape(N_GROUPS_A, 3, WIDTH_A)
    col_scale[:, 0] = scale
    w_a = (w[:, :COLS_A] * jnp.asarray(col_scale.reshape(-1))).astype(BF16)
    w_b = w[:, COLS_A:COLS_A + COLS_B].astype(BF16)
    w_c = (w[:, COLS_A + COLS_B:COLS_A + COLS_B + COLS_C] * scale).astype(BF16)
    w_g = w[:, COLS_A + COLS_B + COLS_C:].astype(BF16)

    mkv = _matmul(mem.reshape(-1, d), w_mem_kv[li].astype(BF16), BF16, 256, 2 * WIDTH_C, "mem_kv")
    mkv = mkv.reshape(b, -1, 2 * WIDTH_C)
    mk, mv = mkv[..., :WIDTH_C], mkv[..., WIDTH_C:]

    oa_parts = []
    for gi, (_, dilation) in enumerate(DILATED_GROUPS):
        bias = _band_bias(rel_bias[:, gi * HEADS_A:(gi + 1) * HEADS_A], dilation)
        pg = _proj_classes(x, w_a[:, gi * 3 * WIDTH_A:(gi + 1) * 3 * WIDTH_A], dilation, f"proj_a_{gi}")
        oa_parts.append(_band_attention(pg, bias, f"band_attn_{gi}"))

    ob = _hgrn2(x, w_b, hgrn_lb_logits.astype(F32), hgrn_norm_g[li].reshape(1, WIDTH_B))

    wr = jnp.zeros((d, LANES), F32)
    wr = wr.at[:, :N_EXPERT_GROUPS].set(w_router_group[li]).at[:, 8:_ROUTER_ROWS].set(w_router_expert[li])
    br = jnp.zeros((_ROUTER_ROWS,), F32).at[:N_EXPERT_GROUPS].set(b_router_group[li]).at[8:].set(b_router_expert[li])
    br = jnp.broadcast_to(br[:, None], (_ROUTER_ROWS, 128))

    h, code, wrow, counts = _merge(
        x, oa_parts, ob, mk, mv, w_c, w_g, w_branch_a[li].astype(BF16), w_branch_b[li].astype(BF16),
        w_branch_c[li].astype(BF16), w_out[li].astype(BF16), ln1_g[li].reshape(1, d), ln1_b[li].reshape(1, d), wr, br)
    h2d = h.reshape(t, d)
    eb = EXPERT_BLOCK
    counts = counts[:, 0]
    per_group = EXPERTS_PER_GROUP * EXPERTS_PER_GROUP
    group_counts = jnp.sum(counts.reshape(N_EXPERT_GROUPS, per_group), axis=1)

    blocks_g = (group_counts + eb - 1) // eb
    ends = jnp.cumsum(blocks_g)
    starts = ends - blocks_g
    n_active = ends[-1:].astype(jnp.int32)
    n_blocks = t // eb + N_EXPERT_GROUPS
    blk = jnp.arange(n_blocks, dtype=jnp.int32)
    block_g = jnp.minimum(jnp.sum(ends[None, :] <= blk[:, None], axis=1), N_EXPERT_GROUPS - 1).astype(jnp.int32)
    in_group = counts.reshape(N_EXPERT_GROUPS, per_group)
    class_start = ((starts * eb)[:, None] + jnp.cumsum(in_group, axis=1) - in_group).reshape(-1)
    cls = code // t
    is_c = cls == jnp.arange(N_ROUTE_CLASSES, dtype=jnp.int32)[:, None]
    dest = (code - cls * t + jnp.sum(jnp.where(is_c, class_start[:, None], 0), axis=0, keepdims=True)).astype(jnp.int32)
    trailing = n_active + jnp.arange(N_EXPERT_GROUPS, dtype=jnp.int32)
    zero_blocks = jnp.concatenate([jnp.where(blocks_g > 0, ends - 1, -1),
                                   jnp.where(trailing < n_blocks, trailing, -1)]).astype(jnp.int32)
    cidx = np.arange(N_ROUTE_CLASSES)
    pair_lo, pair_hi = (cidx // EXPERTS_PER_GROUP) % EXPERTS_PER_GROUP, cidx % EXPERTS_PER_GROUP
    member = np.asarray((np.arange(EXPERTS_PER_GROUP)[None, :] == pair_lo[:, None])
                        | (np.arange(EXPERTS_PER_GROUP)[None, :] == pair_hi[:, None]), np.int32)
    overlap = ((class_start[None, :] < (blk[:, None] + 1) * eb) & (class_start[None, :] + counts[None, :] > blk[:, None] * eb)
               & (counts[None, :] > 0))
    present = (jnp.sum(overlap[:, :, None].astype(jnp.int32) * jnp.asarray(member)[None], axis=1) > 0).astype(jnp.int32)

    xbuf = _dispatch(h2d, wrow, dest, zero_blocks, n_blocks)
    ybuf = _experts(xbuf, block_g, present.reshape(-1),
                    w_exp_gate[li].astype(BF16), w_exp_up[li].astype(BF16), w_exp_down[li].astype(BF16))
    out = _final(h2d, ybuf, dest, ln2_g[li].reshape(1, d), ln2_b[li].reshape(1, d))
    return out.reshape(b, s, d)
```

```python
import functools
import math

import numpy as np
import jax
import jax.numpy as jnp
from jax import lax
from jax.experimental import pallas as pl
from jax.experimental.pallas import tpu as pltpu

F32 = jnp.float32
BF16 = jnp.bfloat16

D_MODEL = 1024
DEPTH = 1
DILATED_GROUPS = ((128, 1), (512, 4), (2048, 16))
N_GROUPS_A = len(DILATED_GROUPS)
HEADS_A = 4
HEAD_DIM = 128
WIDTH_A = HEADS_A * HEAD_DIM
BAND = 128
NUM_BUCKETS = 32
MAX_DISTANCE = 2048
N_HEADS_B = 8
WIDTH_B = N_HEADS_B * HEAD_DIM
CHUNK_B = 64
N_HEADS_C = 4
WIDTH_C = N_HEADS_C * HEAD_DIM
COLS_A = N_GROUPS_A * 3 * WIDTH_A
COLS_B = 4 * WIDTH_B
COLS_C = WIDTH_C
COLS_GATE = 3 * D_MODEL
N_EXPERT_GROUPS = 4
EXPERTS_PER_GROUP = 8
N_EXPERTS = N_EXPERT_GROUPS * EXPERTS_PER_GROUP
TOP_K = 2
D_EXPERT = D_MODEL // 4
DN_ALPHA = (2 * DEPTH) ** 0.25
LN_EPS = 1e-5
RMS_EPS = 1e-6

NEG_BIG = -1e30
SAFE_LOG_DECAY = 80.0
VMEM_LIMIT = 56 * 1024 * 1024
EXPERT_BLOCK = 512

_NT = (((1,), (1,)), ((), ()))
_TN = (((0,), (0,)), ((), ()))


def _cparams(sem):
    return pltpu.CompilerParams(dimension_semantics=sem, vmem_limit_bytes=VMEM_LIMIT)


def _matmul_kernel(x_ref, w_ref, o_ref):
    o_ref[...] = jnp.dot(x_ref[...].astype(BF16), w_ref[...], preferred_element_type=F32).astype(o_ref.dtype)


def _matmul(x, w, out_dtype, tm, tn, name):
    t, k = x.shape
    n = w.shape[1]
    return pl.pallas_call(
        _matmul_kernel,
        grid=(n // tn, t // tm),
        in_specs=[pl.BlockSpec((tm, k), lambda j, i: (i, 0)), pl.BlockSpec((k, tn), lambda j, i: (0, j))],
        out_specs=pl.BlockSpec((tm, tn), lambda j, i: (i, j)),
        out_shape=jax.ShapeDtypeStruct((t, n), out_dtype),
        compiler_params=_cparams(("arbitrary", "arbitrary")),
        name=name,
    )(x, w)


def _t5_bucket_np(dist):
    dist = np.asarray(dist, np.int32)
    max_exact = NUM_BUCKETS // 2
    d = np.maximum(dist, 1).astype(np.float32)
    large = max_exact + (np.log(d / max_exact) / math.log(MAX_DISTANCE / max_exact) * (NUM_BUCKETS - max_exact)).astype(np.int32)
    large = np.minimum(large, NUM_BUCKETS - 1)
    return np.where(dist < max_exact, dist, large).astype(np.int32)


def _band_bias(bias_tab, dilation):
    i = np.arange(BAND)[:, None]
    j = np.arange(2 * BAND)[None, :]
    u = i + BAND - j
    in_band = (u >= 0) & (u <= BAND)
    bucket = _t5_bucket_np(np.clip(u, 0, BAND) * dilation)
    onehot = np.asarray(bucket[:, :, None] == np.arange(NUM_BUCKETS)[None, None, :], np.float32)
    bias = jnp.einsum('pqb,bh->hpq', jnp.asarray(onehot), bias_tab.astype(F32), precision=lax.Precision.HIGHEST)
    general = jnp.where(in_band[None], bias, NEG_BIG)
    first = jnp.where((in_band & (j >= BAND))[None], bias, NEG_BIG)
    return jnp.stack([first, general])


LANES = 128


def _proj_perm_kernel(*refs, r):
    n_slab = D_MODEL // LANES
    x_refs, (w_ref, o_ref, xp_ref) = refs[:n_slab], refs[n_slab:]
    tm = xp_ref.shape[0]
    n = tm // r
    for j, x_ref in enumerate(x_refs):
        for c in range(r):
            rows = x_ref[pl.ds(c, n, stride=r), :] if r > 1 else x_ref[...]
            xp_ref[c * n:(c + 1) * n, j * LANES:(j + 1) * LANES] = rows.astype(BF16)
    res = jnp.dot(xp_ref[...], w_ref[...], preferred_element_type=F32).astype(o_ref.dtype)
    for c in range(r):
        o_ref[c] = res[c * n:(c + 1) * n]


PROJ_A_TILE = 1024


def _proj_classes(x, w, col_block, n, r, name):
    b, s, d = x.shape
    tm = PROJ_A_TILE
    n_slab = d // LANES
    slabs = [pl.BlockSpec((None, tm, LANES), lambda bi, t, j=j: (bi, t, j)) for j in range(n_slab)]
    return pl.pallas_call(
        functools.partial(_proj_perm_kernel, r=r),
        grid=(b, s // tm),
        in_specs=slabs + [pl.BlockSpec((d, n), lambda bi, t: (0, col_block))],
        out_specs=pl.BlockSpec((None, r, tm // r, n), lambda bi, t: (bi, 0, t, 0)),
        out_shape=jax.ShapeDtypeStruct((b, r, s // r, n), BF16),
        scratch_shapes=[pltpu.VMEM((tm, d), BF16)],
        compiler_params=_cparams(("arbitrary", "arbitrary")),
        name=name,
    )(*([x] * n_slab), w)


BAND_SPAN = 2048
_HI_MASK = 0xFFFF0000
_LSE_LANES = LANES // HEADS_A


def _pack_bf16_pair(a, b):
    abits = pltpu.bitcast(a.astype(BF16).astype(F32), jnp.uint32)
    bbits = pltpu.bitcast(b.astype(BF16).astype(F32), jnp.uint32)
    return (abits >> 16) | (bbits & jnp.uint32(_HI_MASK))


def _unpack_bf16_pair(word):
    return pltpu.bitcast(word << 16, F32), pltpu.bitcast(word & jnp.uint32(_HI_MASK), F32)


def _band_attn_kernel(q_ref, k_ref, v_ref, kp_ref, vp_ref, bias_ref, o_ref, lse_ref, *, r):
    nq = q_ref.shape[1] // BAND
    variant = jnp.minimum(pl.program_id(1), 1)
    ones = jnp.ones((BAND, HEAD_DIM), BF16)
    lane = lax.broadcasted_iota(jnp.int32, (BAND, LANES), 1)
    heads = [slice(h * HEAD_DIM, (h + 1) * HEAD_DIM) for h in range(HEADS_A)]
    units = [(c, j) for c in range(r) for j in range(nq)]

    def prev_of(k_or_v_ref, prev_ref, c, j, sl):
        return prev_ref[c, :, sl] if j == 0 else k_or_v_ref[c, (j - 1) * BAND:j * BAND, sl]

    def scores(c, j):
        cur = slice(j * BAND, (j + 1) * BAND)
        out = []
        for h, sl in enumerate(heads):
            q = q_ref[c, cur, sl]
            bias = bias_ref[variant, h] if j == 0 else bias_ref[1, h]
            sp = lax.dot_general(q, prev_of(k_ref, kp_ref, c, j, sl), _NT, preferred_element_type=F32)
            sc = lax.dot_general(q, k_ref[c, cur, sl], _NT, preferred_element_type=F32)
            out.append((sp + bias[:, :BAND], sc + bias[:, BAND:]))
        return out

    nxt = scores(*units[0])
    for u, (c, j) in enumerate(units):
        s_all = nxt
        if u + 1 < len(units):
            nxt = scores(*units[u + 1])
        cur = slice(j * BAND, (j + 1) * BAND)
        rows = pl.ds(j * BAND * r + c, BAND, stride=r) if r > 1 else cur
        probs = []
        for sp, sc in s_all:
            m = jnp.max(jnp.maximum(sp, sc), axis=-1, keepdims=True)
            probs.append((m, jnp.exp(sp - m).astype(BF16), jnp.exp(sc - m).astype(BF16)))
        outs, lses = [], []
        for h, sl in enumerate(heads):
            m, pp, pc = probs[h]
            l = jnp.dot(pp, ones, preferred_element_type=F32) + jnp.dot(pc, ones, preferred_element_type=F32)
            o = jnp.dot(pp, prev_of(v_ref, vp_ref, c, j, sl), preferred_element_type=F32)
            o = o + jnp.dot(pc, v_ref[c, cur, sl], preferred_element_type=F32)
            outs.append(o / l)
            lses.append(m + jnp.log(l))
        for p in range(HEADS_A // 2):
            o_ref[p, rows, :] = _pack_bf16_pair(outs[2 * p], outs[2 * p + 1])
        lse = lses[-1]
        for h in range(HEADS_A - 2, -1, -1):
            lse = jnp.where(lane < (h + 1) * _LSE_LANES, lses[h], lse)
        lse_ref[rows, :] = lse


def _band_attention(pg, bias, name):
    b, r, l, _ = pg.shape
    nq = BAND_SPAN // (BAND * r)
    cur = lambda which: pl.BlockSpec((None, r, nq * BAND, WIDTH_A), lambda bi, n: (bi, 0, n, which))
    prev = lambda which: pl.BlockSpec((None, r, BAND, WIDTH_A), lambda bi, n: (bi, 0, jnp.maximum(n * nq - 1, 0), which))
    return pl.pallas_call(
        functools.partial(_band_attn_kernel, r=r),
        grid=(b, l * r // BAND_SPAN),
        in_specs=[cur(0), cur(1), cur(2), prev(1), prev(2), pl.BlockSpec(bias.shape, lambda bi, n: (0, 0, 0, 0))],
        out_specs=[pl.BlockSpec((None, HEADS_A // 2, BAND_SPAN, HEAD_DIM), lambda bi, n: (bi, 0, n, 0)),
                   pl.BlockSpec((None, BAND_SPAN, LANES), lambda bi, n: (bi, n, 0))],
        out_shape=[jax.ShapeDtypeStruct((b, HEADS_A // 2, l * r, HEAD_DIM), jnp.uint32),
                   jax.ShapeDtypeStruct((b, l * r, LANES), F32)],
        compiler_params=_cparams(("arbitrary", "arbitrary")),
        name=name,
    )(pg, pg, pg, pg, pg, bias)


HGRN_TILE = 512
_N_CHUNK = HGRN_TILE // CHUNK_B


def _dynamic_row(ref, r, ls):
    blk = ref[pl.ds(pl.multiple_of((r // 8) * 8, 8), 8), ls]
    sub = lax.broadcasted_iota(jnp.int32, blk.shape, 0)
    return jnp.sum(jnp.where(sub == r % 8, blk, 0.0), axis=0, keepdims=True)


def _sigmoid(z):
    return 0.5 * jnp.tanh(0.5 * z) + 0.5


_PROJ_PIECE = 256


def _hgrn_kernel(x_ref, xn_ref, w_ref, lbl_ref, ng_ref, o_ref, pq_ref, pf_ref, pi_ref, pg_ref, xb_ref,
                 st_ref, qs_ref, ks_ref, kh_ref, vs_ref, bs_ref, ebl_ref, a_ref, os_ref, gs_ref, kf_ref, qsil_ref,
                 *, tiles_per_seq):
    i = pl.program_id(0)
    parts = (pq_ref, pf_ref, pi_ref, pg_ref)
    per_part = WIDTH_B // _PROJ_PIECE

    def piece(k):
        cols = slice(k * _PROJ_PIECE, (k + 1) * _PROJ_PIECE)
        local = slice((k % per_part) * _PROJ_PIECE, (k % per_part + 1) * _PROJ_PIECE)
        parts[k // per_part][:, local] = jnp.dot(xb_ref[...], w_ref[:, cols], preferred_element_type=F32)

    @pl.when(i == 0)
    def _():
        xb_ref[...] = x_ref[...].astype(BF16)
        for k in range(COLS_B // _PROJ_PIECE):
            piece(k)

    @pl.when(lax.rem(i, tiles_per_seq) == 0)
    def _():
        st_ref[...] = jnp.zeros_like(st_ref)

    xb_ref[...] = xn_ref[...].astype(BF16)
    pieces = [k for part in (2, 3, 1, 0) for k in range(part * per_part, (part + 1) * per_part)]
    next_piece = lambda: piece(pieces.pop(0)) if pieces else None
    vs_ref[...] = pi_ref[...].astype(BF16)
    gv = pg_ref[...]
    gs_ref[...] = gv * _sigmoid(gv) * ng_ref[...]

    lg = lbl_ref[...]
    e = jnp.exp(lg - jnp.max(lg, axis=0, keepdims=True))
    lb = e[0:1, :] / jnp.sum(e, axis=0, keepdims=True)

    row = lax.broadcasted_iota(jnp.int32, (CHUNK_B, CHUNK_B), 0)
    colm = lax.broadcasted_iota(jnp.int32, (CHUNK_B, CHUNK_B), 1)
    causal = row >= colm
    tri = causal.astype(BF16)

    for _ in range(2):
        next_piece()
    f = lb + (1.0 - lb) * _sigmoid(pf_ref[...])
    logf = jnp.log(f)
    kf = 1.0 - f
    kf_ref[...] = kf
    l_hi = logf.astype(BF16)
    rem = logf - l_hi.astype(F32)
    l_mid = rem.astype(BF16)
    l_lo = (rem - l_mid.astype(F32)).astype(BF16)
    min_b = None
    for c in range(_N_CHUNK):
        rs = slice(c * CHUNK_B, (c + 1) * CHUNK_B)
        b = (jnp.dot(tri, l_hi[rs], preferred_element_type=F32) + jnp.dot(tri, l_mid[rs], preferred_element_type=F32)
             + jnp.dot(tri, l_lo[rs], preferred_element_type=F32))
        bs_ref[rs, :] = b
        cm = jnp.min(b[CHUNK_B - 1:CHUNK_B, :])
        min_b = cm if min_b is None else jnp.minimum(min_b, cm)
        if c % 2 == 1:
            next_piece()
    safe = min_b > -SAFE_LOG_DECAY

    qv = pq_ref[...]
    qsil_ref[...] = qv * _sigmoid(qv)
    for c in range(_N_CHUNK):
        rs = slice(c * CHUNK_B, (c + 1) * CHUNK_B)
        b = bs_ref[rs, :]
        eb_last = jnp.exp(b[CHUNK_B - 1:CHUNK_B, :])
        ebl_ref[c:c + 1, :] = eb_last
        qs_ref[rs, :] = (qsil_ref[rs, :] * jnp.exp(b)).astype(BF16)
        k_grown = kf_ref[rs, :] * jnp.exp(-b)
        ks_ref[rs, :] = k_grown.astype(BF16)
        kh_ref[rs, :] = (k_grown * eb_last).astype(BF16)
        next_piece()

    @pl.when(safe)
    def _():
        for c in range(_N_CHUNK):
            rs = slice(c * CHUNK_B, (c + 1) * CHUNK_B)
            for h in range(N_HEADS_B):
                ls = slice(h * HEAD_DIM, (h + 1) * HEAD_DIM)
                a = lax.dot_general(qs_ref[rs, ls], ks_ref[rs, ls], _NT, preferred_element_type=F32)
                a_ref[c * N_HEADS_B + h] = jnp.where(causal, a, 0.0)

    @pl.when(jnp.logical_not(safe))
    def _():
        lane = lax.broadcasted_iota(jnp.int32, (1, CHUNK_B), 1)
        trow = lax.broadcasted_iota(jnp.int32, (CHUNK_B, 1), 0)
        for c in range(_N_CHUNK):
            rs = slice(c * CHUNK_B, (c + 1) * CHUNK_B)
            b = bs_ref[rs, :]
            kh_ref[rs, :] = (kf_ref[rs, :] * jnp.exp(b[CHUNK_B - 1:CHUNK_B, :] - b)).astype(BF16)
            for h in range(N_HEADS_B):
                ls = slice(h * HEAD_DIM, (h + 1) * HEAD_DIM)
                bq = bs_ref[rs, ls]
                qh = qsil_ref[rs, ls]

                def body(s, acc, c=c, ls=ls, bq=bq, qh=qh):
                    b_s = _dynamic_row(bs_ref, c * CHUNK_B + s, ls)
                    k_s = _dynamic_row(kf_ref, c * CHUNK_B + s, ls)
                    w = jnp.exp(jnp.minimum(bq - b_s, 0.0))
                    colv = jnp.sum(qh * k_s * w, axis=-1, keepdims=True)
                    colv = jnp.where(trow >= s, colv, 0.0)
                    return acc + colv * (lane == s).astype(F32)

                a_ref[c * N_HEADS_B + h] = lax.fori_loop(0, CHUNK_B, body, jnp.zeros((CHUNK_B, CHUNK_B), F32))

    for c in range(_N_CHUNK):
        rs = slice(c * CHUNK_B, (c + 1) * CHUNK_B)
        for h in range(N_HEADS_B):
            ls = slice(h * HEAD_DIM, (h + 1) * HEAD_DIM)
            st = st_ref[h]
            vh = vs_ref[rs, ls]
            o = lax.dot_general(qs_ref[rs, ls], st.astype(BF16), _NT, preferred_element_type=F32)
            o = o + jnp.dot(a_ref[c * N_HEADS_B + h].astype(BF16), vh, preferred_element_type=F32)
            os_ref[rs, ls] = o
            st_ref[h] = st * ebl_ref[c:c + 1, ls] + lax.dot_general(vh, kh_ref[rs, ls], _TN, preferred_element_type=F32)
            if h == N_HEADS_B - 1:
                next_piece()
    while pieces:
        next_piece()

    for h in range(N_HEADS_B):
        ls = slice(h * HEAD_DIM, (h + 1) * HEAD_DIM)
        o = os_ref[:, ls]
        o = o * lax.rsqrt(jnp.mean(o * o, axis=-1, keepdims=True) + RMS_EPS)
        o_ref[:, ls] = (o * gs_ref[:, ls]).astype(o_ref.dtype)


def _hgrn2(x, w_b, lb_logits, norm_g):
    b, s, d = x.shape
    tb = HGRN_TILE
    n = b * s // tb
    n_slots = lb_logits.shape[0]
    out = pl.pallas_call(
        functools.partial(_hgrn_kernel, tiles_per_seq=s // tb),
        grid=(n,),
        in_specs=[pl.BlockSpec((tb, d), lambda i: (i, 0)),
                  pl.BlockSpec((tb, d), lambda i: (jnp.minimum(i + 1, n - 1), 0)),
                  pl.BlockSpec(w_b.shape, lambda i: (0, 0), pipeline_mode=pl.Buffered(1)),
                  pl.BlockSpec((n_slots, WIDTH_B), lambda i: (0, 0)),
                  pl.BlockSpec((1, WIDTH_B), lambda i: (0, 0))],
        out_specs=pl.BlockSpec((tb, WIDTH_B), lambda i: (i, 0)),
        out_shape=jax.ShapeDtypeStruct((b * s, WIDTH_B), BF16),
        scratch_shapes=[
            pltpu.VMEM((tb, WIDTH_B), F32),
            pltpu.VMEM((tb, WIDTH_B), F32),
            pltpu.VMEM((tb, WIDTH_B), F32),
            pltpu.VMEM((tb, WIDTH_B), F32),
            pltpu.VMEM((tb, d), BF16),
            pltpu.VMEM((N_HEADS_B, HEAD_DIM, HEAD_DIM), F32),
            pltpu.VMEM((tb, WIDTH_B), BF16),
            pltpu.VMEM((tb, WIDTH_B), BF16),
            pltpu.VMEM((tb, WIDTH_B), BF16),
            pltpu.VMEM((tb, WIDTH_B), BF16),
            pltpu.VMEM((tb, WIDTH_B), F32),
            pltpu.VMEM((8, WIDTH_B), F32),
            pltpu.VMEM((_N_CHUNK * N_HEADS_B, CHUNK_B, CHUNK_B), F32),
            pltpu.VMEM((tb, WIDTH_B), F32),
            pltpu.VMEM((tb, WIDTH_B), F32),
            pltpu.VMEM((tb, WIDTH_B), F32),
            pltpu.VMEM((tb, WIDTH_B), F32),
        ],
        compiler_params=_cparams(("arbitrary",)),
        name="hgrn2",
    )(x.reshape(b * s, d), x.reshape(b * s, d), w_b, lb_logits, norm_g)
    return out.reshape(b, s, WIDTH_B)


MERGE_TILE = 512


def _layer_norm(y, g, b):
    mu = jnp.mean(y, axis=-1, keepdims=True)
    d = y - mu
    var = jnp.mean(d * d, axis=-1, keepdims=True)
    return d * lax.rsqrt(var + LN_EPS) * g + b


def _merge_kernel(x_ref, o1_ref, l1_ref, o2_ref, l2_ref, o3_ref, l3_ref, ob_ref, mk_ref, mv_ref,
                  wc_ref, wg_ref, wa_ref, wb_ref, wcc_ref, wo_ref, g_ref, b_ref, wr_ref, br_ref,
                  h_ref, code_ref, wrow_ref, cnt_ref, oc_ref, cnt_sc, *, n_tokens):
    x = x_ref[...]
    xb = x.astype(BF16)

    qc = jnp.dot(xb, wc_ref[...], preferred_element_type=F32).astype(BF16)
    heads_c = [slice(h * HEAD_DIM, (h + 1) * HEAD_DIM) for h in range(N_HEADS_C)]
    s_all = [lax.dot_general(qc[:, ls], mk_ref[:, ls], _NT, preferred_element_type=F32) for ls in heads_c]
    gate_pre = [None] * 3
    gate_pre[0] = jnp.dot(xb, wg_ref[:, :D_MODEL], preferred_element_type=F32)
    p_all = [jnp.exp(s - jnp.max(s, axis=-1, keepdims=True)).astype(BF16) for s in s_all]
    ones = jnp.ones((mk_ref.shape[0], HEAD_DIM), BF16)
    for ls, p in zip(heads_c, p_all):
        l = jnp.dot(p, ones, preferred_element_type=F32)
        oc_ref[:, ls] = (jnp.dot(p, mv_ref[:, ls], preferred_element_type=F32) / l).astype(BF16)
    gate_pre[1] = jnp.dot(xb, wg_ref[:, D_MODEL:2 * D_MODEL], preferred_element_type=F32)
    branch_b = jnp.dot(ob_ref[...], wb_ref[...], preferred_element_type=F32)
    gate_pre[2] = jnp.dot(xb, wg_ref[:, 2 * D_MODEL:], preferred_element_type=F32)

    lses = [l1_ref[...], l2_ref[...], l3_ref[...]]
    m = functools.reduce(jnp.maximum, lses)
    es = [jnp.exp(l - m) for l in lses]
    inv = 1.0 / functools.reduce(lambda a, c: a + c, es)
    mix = [e * inv for e in es]
    o_pairs = [[_unpack_bf16_pair(o_ref[p]) for p in range(HEADS_A // 2)] for o_ref in (o1_ref, o2_ref, o3_ref)]
    oa_heads = []
    for h in range(HEADS_A):
        acc = None
        for gi in range(N_GROUPS_A):
            w = jnp.broadcast_to(mix[gi][:, h * _LSE_LANES:h * _LSE_LANES + 1], (x.shape[0], HEAD_DIM))
            term = w * o_pairs[gi][h // 2][h % 2]
            acc = term if acc is None else acc + term
        oa_heads.append(acc.astype(BF16))
    oa = jnp.concatenate(oa_heads, axis=-1)

    branch_a = jnp.dot(oa, wa_ref[...], preferred_element_type=F32)
    branch_c = jnp.dot(oc_ref[...], wcc_ref[...], preferred_element_type=F32)
    merged = _sigmoid(gate_pre[1]) * branch_b
    merged = merged + _sigmoid(gate_pre[0]) * branch_a
    merged = merged + _sigmoid(gate_pre[2]) * branch_c
    y = DN_ALPHA * x + jnp.dot(merged.astype(BF16), wo_ref[...], preferred_element_type=F32)
    hv = _layer_norm(y, g_ref[...], b_ref[...])
    h_ref[...] = hv

    first = jnp.logical_and(pl.program_id(0) == 0, pl.program_id(1) == 0)
    _route(hv, wr_ref, br_ref, code_ref, wrow_ref, cnt_ref, cnt_sc, first, n_tokens)


def _merge(x, oa_parts, ob, mkv, wc, wg, wa, wb, wcc, wo, ln_g, ln_b, wr, br):
    b, s, d = x.shape
    tm = MERGE_TILE
    nt = s // tm
    tok = lambda w: pl.BlockSpec((None, tm, w), lambda bi, t: (bi, t, 0))
    full = lambda a: pl.BlockSpec(a.shape, lambda bi, t: (0,) * a.ndim, pipeline_mode=pl.Buffered(1))
    mem = lambda half: pl.BlockSpec((None, mkv.shape[1], WIDTH_C), lambda bi, t: (bi, 0, half))
    o_spec = pl.BlockSpec((None, HEADS_A // 2, tm, HEAD_DIM), lambda bi, t: (bi, 0, t, 0))
    flat = [a for pair in oa_parts for a in pair]
    return pl.pallas_call(
        functools.partial(_merge_kernel, n_tokens=b * s),
        grid=(b, nt),
        in_specs=[tok(d)] + [o_spec, tok(LANES)] * N_GROUPS_A + [tok(WIDTH_B), mem(0), mem(1),
                  full(wc), full(wg), full(wa), full(wb), full(wcc), full(wo), full(ln_g), full(ln_b), full(wr), full(br)],
        out_specs=[tok(d), pl.BlockSpec((1, tm), lambda bi, t: (0, bi * nt + t)),
                   pl.BlockSpec((tm, LANES), lambda bi, t: (bi * nt + t, 0)),
                   pl.BlockSpec((N_ROUTE_CLASSES, LANES), lambda bi, t: (0, 0))],
        out_shape=[jax.ShapeDtypeStruct((b, s, d), F32), jax.ShapeDtypeStruct((1, b * s), jnp.int32),
                   jax.ShapeDtypeStruct((b * s, LANES), F32), jax.ShapeDtypeStruct((N_ROUTE_CLASSES, LANES), jnp.int32)],
        scratch_shapes=[pltpu.VMEM((tm, WIDTH_C), BF16), pltpu.VMEM((N_ROUTE_CLASSES, LANES), F32)],
        compiler_params=_cparams(("arbitrary", "arbitrary")),
        name="merge",
    )(x, *flat, ob, mkv, mkv, wc, wg, wa, wb, wcc, wo, ln_g, ln_b, wr, br)


N_ROUTE_CLASSES = N_EXPERT_GROUPS * EXPERTS_PER_GROUP * EXPERTS_PER_GROUP
_ROUTER_ROWS = 8 + N_EXPERTS


def _route(hv, wr_ref, br_ref, dest_ref, wrow_ref, cnt_ref, cnt_sc, first_step, group_capacity):
    tm = hv.shape[0]

    @pl.when(first_step)
    def _():
        cnt_sc[...] = jnp.zeros_like(cnt_sc)

    h_hi = hv.astype(BF16)
    h_lo = (hv - h_hi.astype(F32)).astype(BF16)
    wv = wr_ref[...]
    w_hi = wv.astype(BF16)
    w_lo = (wv - w_hi.astype(F32)).astype(BF16)
    both = jnp.dot(h_hi, jnp.concatenate([w_hi, w_lo], axis=1), preferred_element_type=F32)
    logits = both[:, :LANES] + both[:, LANES:] + jnp.dot(h_lo, w_hi, preferred_element_type=F32)
    logits = logits.T[:_ROUTER_ROWS, :] + br_ref[:, 0:1]
    g = [logits[i:i + 1, :] for i in range(N_EXPERT_GROUPS)]
    gmax = functools.reduce(jnp.maximum, g)
    gsel = jnp.full_like(gmax, N_EXPERT_GROUPS - 1).astype(jnp.int32)
    for i in range(N_EXPERT_GROUPS - 2, -1, -1):
        gsel = jnp.where(g[i] == gmax, i, gsel)
    gprob = 1.0 / functools.reduce(lambda a, c: a + c, [jnp.exp(gi - gmax) for gi in g])

    esel = logits[8 + (N_EXPERT_GROUPS - 1) * EXPERTS_PER_GROUP:8 + N_EXPERTS, :]
    for i in range(N_EXPERT_GROUPS - 2, -1, -1):
        esel = jnp.where(gsel == i, logits[8 + i * EXPERTS_PER_GROUP:8 + (i + 1) * EXPERTS_PER_GROUP, :], esel)
    ridx = lax.broadcasted_iota(jnp.int32, (EXPERTS_PER_GROUP, tm), 0)
    v1 = jnp.max(esel, axis=0, keepdims=True)
    i1 = jnp.min(jnp.where(esel == v1, ridx, EXPERTS_PER_GROUP), axis=0, keepdims=True)
    rest = jnp.where(ridx == i1, -jnp.inf, esel)
    v2 = jnp.max(rest, axis=0, keepdims=True)
    i2 = jnp.min(jnp.where(rest == v2, ridx, EXPERTS_PER_GROUP), axis=0, keepdims=True)
    t = jnp.exp(v2 - v1)
    w1 = gprob / (1.0 + t)
    w2 = gprob * t / (1.0 + t)
    wt = jnp.where(ridx == i1, w1, jnp.where(ridx == i2, w2, 0.0))
    eye = (lax.broadcasted_iota(jnp.int32, (EXPERTS_PER_GROUP, LANES), 0)
           == lax.broadcasted_iota(jnp.int32, (EXPERTS_PER_GROUP, LANES), 1)).astype(F32)
    wrow_ref[...] = lax.dot_general(wt, eye, _TN, preferred_element_type=F32, precision=lax.Precision.HIGHEST)

    cls = (gsel * EXPERTS_PER_GROUP + jnp.minimum(i1, i2)) * EXPERTS_PER_GROUP + jnp.maximum(i1, i2)
    crow = lax.broadcasted_iota(jnp.int32, (N_ROUTE_CLASSES, tm), 0)
    onehot = (crow == cls).astype(F32)
    before = (lax.broadcasted_iota(jnp.int32, (tm, tm), 0) < lax.broadcasted_iota(jnp.int32, (tm, tm), 1))
    excl = jnp.dot(onehot.astype(BF16), before.astype(BF16), preferred_element_type=F32)
    pos = cnt_sc[:, 0:1] + excl
    rank = jnp.sum(onehot * pos, axis=0, keepdims=True).astype(jnp.int32)
    dest_ref[...] = cls * group_capacity + rank
    cnt_sc[...] = cnt_sc[...] + jnp.sum(onehot, axis=1, keepdims=True)
    cnt_ref[...] = cnt_sc[...].astype(jnp.int32)


DISPATCH_TILE = 1024


SUBLANES = 8
_HALF = D_MODEL // 2


def _tile_copy(src_ref, src_tok, dst_ref, dst_tok, sem):
    src = src_ref.at[pl.ds(pl.multiple_of(src_tok * SUBLANES, SUBLANES), SUBLANES)]
    dst = dst_ref.at[pl.ds(pl.multiple_of(dst_tok * SUBLANES, SUBLANES), SUBLANES)]
    return pltpu.make_async_copy(src, dst, sem)


def _tile_rows(ref, j, n):
    return ref.at[pl.ds(j, n, stride=SUBLANES), :]


def _all_tiles_wait(src_ref, dst_ref, sem):
    pltpu.make_async_copy(src_ref, dst_ref.at[pl.ds(0, src_ref.shape[0])], sem).wait()


def _dispatch_kernel(zblk_ref, dest_ref, h_ref, w_ref, xbuf_ref, pay_ref, zero_ref, sems, zsem):
    i = pl.program_id(0)
    n_steps = pl.num_programs(0)
    slot = lax.rem(i, 2)
    tm = h_ref.shape[0]
    pay = pay_ref.at[slot]

    @pl.when(i == 0)
    def _():
        zero_ref[...] = jnp.zeros_like(zero_ref)
        for k in range(zblk_ref.shape[0]):
            @pl.when(zblk_ref[k] >= 0)
            def _():
                start = pl.multiple_of(zblk_ref[k] * zero_ref.shape[0], zero_ref.shape[0])
                cp = pltpu.make_async_copy(zero_ref, xbuf_ref.at[pl.ds(start, zero_ref.shape[0])], zsem)
                cp.start()
                cp.wait()

    @pl.when(i >= 2)
    def _():
        _all_tiles_wait(pay, xbuf_ref, sems.at[slot])

    hv = h_ref[...]
    for j in range(_HALF // LANES):
        lo, hi = hv[:, j * LANES:(j + 1) * LANES], hv[:, _HALF + j * LANES:_HALF + (j + 1) * LANES]
        _tile_rows(pay, j, tm)[...] = _pack_bf16_pair(lo, hi)
    _tile_rows(pay, _HALF // LANES, tm)[...] = pltpu.bitcast(w_ref[...], jnp.uint32)
    for j in range(_HALF // LANES + 1, SUBLANES):
        _tile_rows(pay, j, tm)[...] = jnp.zeros((tm, LANES), jnp.uint32)

    def issue(t, carry):
        _tile_copy(pay, t, xbuf_ref, dest_ref[0, t], sems.at[slot]).start()
        return carry

    lax.fori_loop(0, tm, issue, 0, unroll=8)

    @pl.when(i == n_steps - 1)
    def _():
        @pl.when(i >= 1)
        def _():
            _all_tiles_wait(pay_ref.at[1 - slot], xbuf_ref, sems.at[1 - slot])

        _all_tiles_wait(pay, xbuf_ref, sems.at[slot])


def _dispatch(h2d, wrow, dest, zero_blocks, n_blocks):
    t, d = h2d.shape
    tm = DISPATCH_TILE
    grid_spec = pltpu.PrefetchScalarGridSpec(
        num_scalar_prefetch=1,
        grid=(t // tm,),
        in_specs=[pl.BlockSpec((1, tm), lambda i, zb: (0, i), memory_space=pltpu.SMEM),
                  pl.BlockSpec((tm, d), lambda i, zb: (i, 0)),
                  pl.BlockSpec((tm, LANES), lambda i, zb: (i, 0))],
        out_specs=pl.BlockSpec(memory_space=pl.ANY),
        scratch_shapes=[pltpu.VMEM((2, tm * SUBLANES, LANES), jnp.uint32),
                        pltpu.VMEM((EXPERT_BLOCK * SUBLANES, LANES), jnp.uint32),
                        pltpu.SemaphoreType.DMA((2,)), pltpu.SemaphoreType.DMA(())],
    )
    return pl.pallas_call(
        _dispatch_kernel,
        grid_spec=grid_spec,
        out_shape=jax.ShapeDtypeStruct((n_blocks * EXPERT_BLOCK * SUBLANES, LANES), jnp.uint32),
        compiler_params=_cparams(("arbitrary",)),
        name="dispatch",
    )(zero_blocks, dest, h2d, wrow)


def _expert_kernel(bg_ref, pres_ref, x_ref, wg_ref, wu_ref, wd_ref, y_ref, xs_ref, ws_ref, ys_ref):
    i = pl.program_id(0)
    eb = xs_ref.shape[0]
    lo, hi = [], []
    for j in range(_HALF // LANES):
        lo_j, hi_j = _unpack_bf16_pair(_tile_rows(x_ref, j, eb)[...])
        lo.append(lo_j)
        hi.append(hi_j)
    xs_ref[...] = jnp.concatenate(lo + hi, axis=-1).astype(BF16)
    ws_ref[...] = pltpu.bitcast(_tile_rows(x_ref, _HALF // LANES, eb)[...], F32)
    ys_ref[...] = jnp.zeros_like(ys_ref)
    halves = [slice(0, eb // 2), slice(eb // 2, eb)]

    for e in range(EXPERTS_PER_GROUP):
        @pl.when(pres_ref[i * EXPERTS_PER_GROUP + e] > 0)
        def _(e=e):
            gu = [(jnp.dot(xs_ref[p, :], wg_ref[e], preferred_element_type=F32),
                   jnp.dot(xs_ref[p, :], wu_ref[e], preferred_element_type=F32)) for p in halves]
            for p, (a, u) in zip(halves, gu):
                we = ws_ref[p, e:e + 1]
                hid = jnp.where(we != 0.0, a * _sigmoid(a) * u * we, 0.0).astype(BF16)
                ys_ref[p, :] += jnp.dot(hid, wd_ref[e], preferred_element_type=F32)

    for j in range(SUBLANES):
        _tile_rows(y_ref, j, eb)[...] = ys_ref[:, j * LANES:(j + 1) * LANES]


def _experts(xbuf, block_g, present, wg, wu, wd):
    eb = EXPERT_BLOCK
    blk = pl.BlockSpec((eb * SUBLANES, LANES), lambda i, bg, pres: (i, 0))
    gw = lambda a: pl.BlockSpec((EXPERTS_PER_GROUP,) + a.shape[1:], lambda i, bg, pres: (bg[i], 0, 0))
    grid_spec = pltpu.PrefetchScalarGridSpec(
        num_scalar_prefetch=2,
        grid=(block_g.shape[0],),
        in_specs=[blk, gw(wg), gw(wu), gw(wd)],
        out_specs=blk,
        scratch_shapes=[pltpu.VMEM((eb, D_MODEL), BF16), pltpu.VMEM((eb, LANES), F32), pltpu.VMEM((eb, D_MODEL), F32)],
    )
    return pl.pallas_call(
        _expert_kernel,
        grid_spec=grid_spec,
        out_shape=jax.ShapeDtypeStruct(xbuf.shape, F32),
        compiler_params=_cparams(("arbitrary",)),
        name="experts",
    )(block_g, present, xbuf, wg, wu, wd)


FINAL_TILE = 1024


def _final_kernel(dest_ref, dest_next_ref, h_ref, g_ref, b_ref, ybuf_ref, o_ref, ys_ref, sems):
    i = pl.program_id(0)
    slot = lax.rem(i, 2)
    tm = h_ref.shape[0]

    def gather(d_ref, s):
        def issue(t, carry):
            _tile_copy(ybuf_ref, d_ref[0, t], ys_ref.at[s], t, sems.at[s]).start()
            return carry

        lax.fori_loop(0, tm, issue, 0, unroll=8)

    @pl.when(i == 0)
    def _():
        gather(dest_ref, slot)

    @pl.when(i + 1 < pl.num_programs(0))
    def _():
        gather(dest_next_ref, 1 - slot)

    ys = ys_ref.at[slot]
    pltpu.make_async_copy(ybuf_ref.at[pl.ds(0, ys.shape[0])], ys, sems.at[slot]).wait()
    y = jnp.concatenate([_tile_rows(ys, j, tm)[...] for j in range(SUBLANES)], axis=-1)
    o_ref[...] = _layer_norm(DN_ALPHA * h_ref[...] + y, g_ref[...], b_ref[...])


def _final(h2d, ybuf, dest, ln_g, ln_b):
    t, d = h2d.shape
    tm = FINAL_TILE
    n = t // tm
    row = pl.BlockSpec((tm, d), lambda i: (i, 0))
    vec = pl.BlockSpec((1, d), lambda i: (0, 0))
    return pl.pallas_call(
        _final_kernel,
        grid=(n,),
        in_specs=[pl.BlockSpec((1, tm), lambda i: (0, i), memory_space=pltpu.SMEM),
                  pl.BlockSpec((1, tm), lambda i: (0, jnp.minimum(i + 1, n - 1)), memory_space=pltpu.SMEM),
                  row, vec, vec, pl.BlockSpec(memory_space=pl.ANY)],
        out_specs=row,
        out_shape=jax.ShapeDtypeStruct((t, d), F32),
        scratch_shapes=[pltpu.VMEM((2, tm * SUBLANES, LANES), F32), pltpu.SemaphoreType.DMA((2,))],
        compiler_params=_cparams(("arbitrary",)),
        name="final",
    )(dest, dest, h2d, ln_g, ln_b, ybuf)


def kernel(x, mem, rel_bias, hgrn_lb_logits, w_in, w_mem_kv, hgrn_norm_g, w_branch_a, w_branch_b, w_branch_c, w_out, ln1_g, ln1_b, w_router_group, b_router_group, w_router_expert, b_router_expert, w_exp_gate, w_exp_up, w_exp_down, ln2_g, ln2_b):
    b, s, d = x.shape
    t = b * s
    assert d == D_MODEL and w_in.shape[0] == DEPTH == 1
    assert s % (BAND * DILATED_GROUPS[-1][1]) == 0 and s % HGRN_TILE == 0 and s % MERGE_TILE == 0 and t % EXPERT_BLOCK == 0
    li = 0
    scale = HEAD_DIM ** -0.5
    x2d = x.reshape(t, d)

    w = w_in[li]
    col_scale = np.ones((COLS_A,), np.float32).reshape(N_GROUPS_A, 3, WIDTH_A)
    col_scale[:, 0] = scale
    w_a = (w[:, :COLS_A] * jnp.asarray(col_scale.reshape(-1))).astype(BF16)
    w_b = w[:, COLS_A:COLS_A + COLS_B].astype(BF16)
    w_c = (w[:, COLS_A + COLS_B:COLS_A + COLS_B + COLS_C] * scale).astype(BF16)
    w_g = w[:, COLS_A + COLS_B + COLS_C:].astype(BF16)

    mkv = _matmul(mem.reshape(-1, d), w_mem_kv[li].astype(BF16), BF16, 256, 2 * WIDTH_C, "mem_kv")
    mkv = mkv.reshape(b, -1, 2 * WIDTH_C)

    oa_parts = []
    for gi, (_, dilation) in enumerate(DILATED_GROUPS):
        bias = _band_bias(rel_bias[:, gi * HEADS_A:(gi + 1) * HEADS_A], dilation)
        pg = _proj_classes(x, w_a, gi, 3 * WIDTH_A, dilation, f"proj_a_{gi}")
        oa_parts.append(_band_attention(pg, bias, f"band_attn_{gi}"))

    ob = _hgrn2(x, w_b, hgrn_lb_logits.astype(F32), hgrn_norm_g[li].reshape(1, WIDTH_B))

    wr = jnp.zeros((d, LANES), F32)
    wr = wr.at[:, :N_EXPERT_GROUPS].set(w_router_group[li]).at[:, 8:_ROUTER_ROWS].set(w_router_expert[li])
    br = jnp.zeros((_ROUTER_ROWS,), F32).at[:N_EXPERT_GROUPS].set(b_router_group[li]).at[8:].set(b_router_expert[li])
    br = jnp.broadcast_to(br[:, None], (_ROUTER_ROWS, 128))

    h, code, wrow, counts = _merge(
        x, oa_parts, ob, mkv, w_c, w_g, w_branch_a[li].astype(BF16), w_branch_b[li].astype(BF16),
        w_branch_c[li].astype(BF16), w_out[li].astype(BF16), ln1_g[li].reshape(1, d), ln1_b[li].reshape(1, d), wr, br)
    h2d = h.reshape(t, d)
    eb = EXPERT_BLOCK
    counts = counts[:, 0]
    per_group = EXPERTS_PER_GROUP * EXPERTS_PER_GROUP
    group_counts = jnp.sum(counts.reshape(N_EXPERT_GROUPS, per_group), axis=1)

    blocks_g = (group_counts + eb - 1) // eb
    ends = jnp.cumsum(blocks_g)
    starts = ends - blocks_g
    n_active = ends[-1:].astype(jnp.int32)
    n_blocks = t // eb + N_EXPERT_GROUPS
    blk = jnp.arange(n_blocks, dtype=jnp.int32)
    block_g = jnp.minimum(jnp.sum(ends[None, :] <= blk[:, None], axis=1), N_EXPERT_GROUPS - 1).astype(jnp.int32)
    in_group = counts.reshape(N_EXPERT_GROUPS, per_group)
    class_start = ((starts * eb)[:, None] + jnp.cumsum(in_group, axis=1) - in_group).reshape(-1)
    cls = code // t
    is_c = cls == jnp.arange(N_ROUTE_CLASSES, dtype=jnp.int32)[:, None]
    dest = (code - cls * t + jnp.sum(jnp.where(is_c, class_start[:, None], 0), axis=0, keepdims=True)).astype(jnp.int32)
    trailing = n_active + jnp.arange(N_EXPERT_GROUPS, dtype=jnp.int32)
    zero_blocks = jnp.concatenate([jnp.where(blocks_g > 0, ends - 1, -1),
                                   jnp.where(trailing < n_blocks, trailing, -1)]).astype(jnp.int32)
    cidx = np.arange(N_ROUTE_CLASSES)
    pair_lo, pair_hi = (cidx // EXPERTS_PER_GROUP) % EXPERTS_PER_GROUP, cidx % EXPERTS_PER_GROUP
    member = np.asarray((np.arange(EXPERTS_PER_GROUP)[None, :] == pair_lo[:, None])
                        | (np.arange(EXPERTS_PER_GROUP)[None, :] == pair_hi[:, None]), np.int32)
    overlap = ((class_start[None, :] < (blk[:, None] + 1) * eb) & (class_start[None, :] + counts[None, :] > blk[:, None] * eb)
               & (counts[None, :] > 0))
    present = (jnp.sum(overlap[:, :, None].astype(jnp.int32) * jnp.asarray(member)[None], axis=1) > 0).astype(jnp.int32)

    xbuf = _dispatch(h2d, wrow, dest, zero_blocks, n_blocks)
    ybuf = _experts(xbuf, block_g, present.reshape(-1),
                    w_exp_gate[li].astype(BF16), w_exp_up[li].astype(BF16), w_exp_down[li].astype(BF16))
    out = _final(h2d, ybuf, dest, ln2_g[li].reshape(1, d), ln2_b[li].reshape(1, d))
    return out.reshape(b, s, d)
```

```python
import functools
import math

import numpy as np
import jax
import jax.numpy as jnp
from jax import lax
from jax.experimental import pallas as pl
from jax.experimental.pallas import tpu as pltpu

F32 = jnp.float32
BF16 = jnp.bfloat16

D_MODEL = 1024
DEPTH = 1
DILATED_GROUPS = ((128, 1), (512, 4), (2048, 16))
N_GROUPS_A = len(DILATED_GROUPS)
HEADS_A = 4
HEAD_DIM = 128
WIDTH_A = HEADS_A * HEAD_DIM
BAND = 128
NUM_BUCKETS = 32
MAX_DISTANCE = 2048
N_HEADS_B = 8
WIDTH_B = N_HEADS_B * HEAD_DIM
CHUNK_B = 64
N_HEADS_C = 4
WIDTH_C = N_HEADS_C * HEAD_DIM
COLS_A = N_GROUPS_A * 3 * WIDTH_A
COLS_B = 4 * WIDTH_B
COLS_C = WIDTH_C
COLS_GATE = 3 * D_MODEL
N_EXPERT_GROUPS = 4
EXPERTS_PER_GROUP = 8
N_EXPERTS = N_EXPERT_GROUPS * EXPERTS_PER_GROUP
TOP_K = 2
D_EXPERT = D_MODEL // 4
DN_ALPHA = (2 * DEPTH) ** 0.25
LN_EPS = 1e-5
RMS_EPS = 1e-6

NEG_BIG = -1e30
SAFE_LOG_DECAY = 80.0
VMEM_LIMIT = 56 * 1024 * 1024
EXPERT_BLOCK = 512

_NT = (((1,), (1,)), ((), ()))
_TN = (((0,), (0,)), ((), ()))


def _cparams(sem):
    return pltpu.CompilerParams(dimension_semantics=sem, vmem_limit_bytes=VMEM_LIMIT)


def _matmul_kernel(x_ref, w_ref, o_ref):
    o_ref[...] = jnp.dot(x_ref[...].astype(BF16), w_ref[...], preferred_element_type=F32).astype(o_ref.dtype)


def _matmul(x, w, out_dtype, tm, tn, name):
    t, k = x.shape
    n = w.shape[1]
    return pl.pallas_call(
        _matmul_kernel,
        grid=(n // tn, t // tm),
        in_specs=[pl.BlockSpec((tm, k), lambda j, i: (i, 0)), pl.BlockSpec((k, tn), lambda j, i: (0, j))],
        out_specs=pl.BlockSpec((tm, tn), lambda j, i: (i, j)),
        out_shape=jax.ShapeDtypeStruct((t, n), out_dtype),
        compiler_params=_cparams(("arbitrary", "arbitrary")),
        name=name,
    )(x, w)


def _t5_bucket_np(dist):
    dist = np.asarray(dist, np.int32)
    max_exact = NUM_BUCKETS // 2
    d = np.maximum(dist, 1).astype(np.float32)
    large = max_exact + (np.log(d / max_exact) / math.log(MAX_DISTANCE / max_exact) * (NUM_BUCKETS - max_exact)).astype(np.int32)
    large = np.minimum(large, NUM_BUCKETS - 1)
    return np.where(dist < max_exact, dist, large).astype(np.int32)


def _band_bias(bias_tab, dilation):
    i = np.arange(BAND)[:, None]
    j = np.arange(2 * BAND)[None, :]
    u = i + BAND - j
    in_band = (u >= 0) & (u <= BAND)
    bucket = _t5_bucket_np(np.clip(u, 0, BAND) * dilation)
    onehot = np.asarray(bucket[:, :, None] == np.arange(NUM_BUCKETS)[None, None, :], np.float32)
    bias = jnp.einsum('pqb,bh->hpq', jnp.asarray(onehot), bias_tab.astype(F32), precision=lax.Precision.HIGHEST)
    general = jnp.where(in_band[None], bias, NEG_BIG)
    first = jnp.where((in_band & (j >= BAND))[None], bias, NEG_BIG)
    return jnp.stack([first, general])


LANES = 128


def _proj_perm_kernel(*refs, r):
    n_slab = D_MODEL // LANES
    x_refs, (w_ref, o_ref, xp_ref) = refs[:n_slab], refs[n_slab:]
    tm = xp_ref.shape[0]
    n = tm // r
    for j, x_ref in enumerate(x_refs):
        for c in range(r):
            rows = x_ref[pl.ds(c, n, stride=r), :] if r > 1 else x_ref[...]
            xp_ref[c * n:(c + 1) * n, j * LANES:(j + 1) * LANES] = rows.astype(BF16)
    res = jnp.dot(xp_ref[...], w_ref[...], preferred_element_type=F32).astype(o_ref.dtype)
    for c in range(r):
        o_ref[c] = res[c * n:(c + 1) * n]


PROJ_A_TILE = 1024


def _proj_classes(x, w, col_block, n, r, name):
    b, s, d = x.shape
    tm = PROJ_A_TILE
    n_slab = d // LANES
    slabs = [pl.BlockSpec((None, tm, LANES), lambda bi, t, j=j: (bi, t, j)) for j in range(n_slab)]
    return pl.pallas_call(
        functools.partial(_proj_perm_kernel, r=r),
        grid=(b, s // tm),
        in_specs=slabs + [pl.BlockSpec((d, n), lambda bi, t: (0, col_block))],
        out_specs=pl.BlockSpec((None, r, tm // r, n), lambda bi, t: (bi, 0, t, 0)),
        out_shape=jax.ShapeDtypeStruct((b, r, s // r, n), BF16),
        scratch_shapes=[pltpu.VMEM((tm, d), BF16)],
        compiler_params=_cparams(("arbitrary", "arbitrary")),
        name=name,
    )(*([x] * n_slab), w)


BAND_SPAN = 2048
_HI_MASK = 0xFFFF0000
_LSE_LANES = LANES // HEADS_A


def _pack_bf16_pair(a, b):
    abits = pltpu.bitcast(a.astype(BF16).astype(F32), jnp.uint32)
    bbits = pltpu.bitcast(b.astype(BF16).astype(F32), jnp.uint32)
    return (abits >> 16) | (bbits & jnp.uint32(_HI_MASK))


def _unpack_bf16_pair(word):
    return pltpu.bitcast(word << 16, F32), pltpu.bitcast(word & jnp.uint32(_HI_MASK), F32)


def _band_attn_kernel(q_ref, k_ref, v_ref, kp_ref, vp_ref, bias_ref, o_ref, lse_ref, *, r):
    nq = q_ref.shape[1] // BAND
    variant = jnp.minimum(pl.program_id(1), 1)
    ones = jnp.ones((BAND, HEAD_DIM), BF16)
    lane = lax.broadcasted_iota(jnp.int32, (BAND, LANES), 1)
    heads = [slice(h * HEAD_DIM, (h + 1) * HEAD_DIM) for h in range(HEADS_A)]
    units = [(c, j) for c in range(r) for j in range(nq)]

    def prev_of(k_or_v_ref, prev_ref, c, j, sl):
        return prev_ref[c, :, sl] if j == 0 else k_or_v_ref[c, (j - 1) * BAND:j * BAND, sl]

    def scores(c, j):
        cur = slice(j * BAND, (j + 1) * BAND)
        out = []
        for h, sl in enumerate(heads):
            q = q_ref[c, cur, sl]
            bias = bias_ref[variant, h] if j == 0 else bias_ref[1, h]
            sp = lax.dot_general(q, prev_of(k_ref, kp_ref, c, j, sl), _NT, preferred_element_type=F32)
            sc = lax.dot_general(q, k_ref[c, cur, sl], _NT, preferred_element_type=F32)
            out.append((sp + bias[:, :BAND], sc + bias[:, BAND:]))
        return out

    nxt = scores(*units[0])
    for u, (c, j) in enumerate(units):
        s_all = nxt
        if u + 1 < len(units):
            nxt = scores(*units[u + 1])
        cur = slice(j * BAND, (j + 1) * BAND)
        rows = pl.ds(j * BAND * r + c, BAND, stride=r) if r > 1 else cur
        probs = []
        for sp, sc in s_all:
            m = jnp.max(jnp.maximum(sp, sc), axis=-1, keepdims=True)
            probs.append((m, jnp.exp(sp - m).astype(BF16), jnp.exp(sc - m).astype(BF16)))
        outs, lses = [], []
        for h, sl in enumerate(heads):
            m, pp, pc = probs[h]
            l = jnp.dot(pp, ones, preferred_element_type=F32) + jnp.dot(pc, ones, preferred_element_type=F32)
            o = jnp.dot(pp, prev_of(v_ref, vp_ref, c, j, sl), preferred_element_type=F32)
            o = o + jnp.dot(pc, v_ref[c, cur, sl], preferred_element_type=F32)
            outs.append(o / l)
            lses.append(m + jnp.log(l))
        for p in range(HEADS_A // 2):
            o_ref[p, rows, :] = _pack_bf16_pair(outs[2 * p], outs[2 * p + 1])
        lse = lses[-1]
        for h in range(HEADS_A - 2, -1, -1):
            lse = jnp.where(lane < (h + 1) * _LSE_LANES, lses[h], lse)
        lse_ref[rows, :] = lse


def _band_attention(pg, bias, name):
    b, r, l, _ = pg.shape
    nq = BAND_SPAN // (BAND * r)
    cur = lambda which: pl.BlockSpec((None, r, nq * BAND, WIDTH_A), lambda bi, n: (bi, 0, n, which))
    prev = lambda which: pl.BlockSpec((None, r, BAND, WIDTH_A), lambda bi, n: (bi, 0, jnp.maximum(n * nq - 1, 0), which))
    return pl.pallas_call(
        functools.partial(_band_attn_kernel, r=r),
        grid=(b, l * r // BAND_SPAN),
        in_specs=[cur(0), cur(1), cur(2), prev(1), prev(2), pl.BlockSpec(bias.shape, lambda bi, n: (0, 0, 0, 0))],
        out_specs=[pl.BlockSpec((None, HEADS_A // 2, BAND_SPAN, HEAD_DIM), lambda bi, n: (bi, 0, n, 0)),
                   pl.BlockSpec((None, BAND_SPAN, LANES), lambda bi, n: (bi, n, 0))],
        out_shape=[jax.ShapeDtypeStruct((b, HEADS_A // 2, l * r, HEAD_DIM), jnp.uint32),
                   jax.ShapeDtypeStruct((b, l * r, LANES), F32)],
        compiler_params=_cparams(("arbitrary", "arbitrary")),
        name=name,
    )(pg, pg, pg, pg, pg, bias)


HGRN_TILE = 512
_N_CHUNK = HGRN_TILE // CHUNK_B


def _dynamic_row(ref, r, ls):
    blk = ref[pl.ds(pl.multiple_of((r // 8) * 8, 8), 8), ls]
    sub = lax.broadcasted_iota(jnp.int32, blk.shape, 0)
    return jnp.sum(jnp.where(sub == r % 8, blk, 0.0), axis=0, keepdims=True)


def _sigmoid(z):
    return 0.5 * jnp.tanh(0.5 * z) + 0.5


_PROJ_PIECE = 256


def _hgrn_kernel(x_ref, xn_ref, w_ref, lbl_ref, ng_ref, o_ref, pq_ref, pf_ref, pi_ref, pg_ref, xb_ref,
                 st_ref, qs_ref, ks_ref, kh_ref, vs_ref, bs_ref, ebl_ref, a_ref, os_ref, gs_ref, kf_ref, qsil_ref,
                 *, tiles_per_seq):
    i = pl.program_id(0)
    parts = (pq_ref, pf_ref, pi_ref, pg_ref)
    per_part = WIDTH_B // _PROJ_PIECE

    def piece(k):
        cols = slice(k * _PROJ_PIECE, (k + 1) * _PROJ_PIECE)
        local = slice((k % per_part) * _PROJ_PIECE, (k % per_part + 1) * _PROJ_PIECE)
        parts[k // per_part][:, local] = jnp.dot(xb_ref[...], w_ref[:, cols], preferred_element_type=F32)

    @pl.when(i == 0)
    def _():
        xb_ref[...] = x_ref[...].astype(BF16)
        for k in range(COLS_B // _PROJ_PIECE):
            piece(k)

    @pl.when(lax.rem(i, tiles_per_seq) == 0)
    def _():
        st_ref[...] = jnp.zeros_like(st_ref)

    xb_ref[...] = xn_ref[...].astype(BF16)
    pieces = [k for part in (2, 3, 1, 0) for k in range(part * per_part, (part + 1) * per_part)]
    next_piece = lambda: piece(pieces.pop(0)) if pieces else None
    vs_ref[...] = pi_ref[...].astype(BF16)
    gv = pg_ref[...]
    gs_ref[...] = gv * _sigmoid(gv) * ng_ref[...]

    lg = lbl_ref[...]
    e = jnp.exp(lg - jnp.max(lg, axis=0, keepdims=True))
    lb = e[0:1, :] / jnp.sum(e, axis=0, keepdims=True)

    row = lax.broadcasted_iota(jnp.int32, (CHUNK_B, CHUNK_B), 0)
    colm = lax.broadcasted_iota(jnp.int32, (CHUNK_B, CHUNK_B), 1)
    causal = row >= colm
    tri = causal.astype(BF16)

    for _ in range(2):
        next_piece()
    f = lb + (1.0 - lb) * _sigmoid(pf_ref[...])
    logf = jnp.log(f)
    kf = 1.0 - f
    kf_ref[...] = kf
    l_hi = logf.astype(BF16)
    rem = logf - l_hi.astype(F32)
    l_mid = rem.astype(BF16)
    l_lo = (rem - l_mid.astype(F32)).astype(BF16)
    min_b = None
    for c in range(_N_CHUNK):
        rs = slice(c * CHUNK_B, (c + 1) * CHUNK_B)
        b = (jnp.dot(tri, l_hi[rs], preferred_element_type=F32) + jnp.dot(tri, l_mid[rs], preferred_element_type=F32)
             + jnp.dot(tri, l_lo[rs], preferred_element_type=F32))
        bs_ref[rs, :] = b
        cm = jnp.min(b[CHUNK_B - 1:CHUNK_B, :])
        min_b = cm if min_b is None else jnp.minimum(min_b, cm)
        if c % 2 == 1:
            next_piece()
    safe = min_b > -SAFE_LOG_DECAY

    qv = pq_ref[...]
    qsil_ref[...] = qv * _sigmoid(qv)
    for c in range(_N_CHUNK):
        rs = slice(c * CHUNK_B, (c + 1) * CHUNK_B)
        b = bs_ref[rs, :]
        eb_last = jnp.exp(b[CHUNK_B - 1:CHUNK_B, :])
        ebl_ref[c:c + 1, :] = eb_last
        qs_ref[rs, :] = (qsil_ref[rs, :] * jnp.exp(b)).astype(BF16)
        k_grown = kf_ref[rs, :] * jnp.exp(-b)
        ks_ref[rs, :] = k_grown.astype(BF16)
        kh_ref[rs, :] = (k_grown * eb_last).astype(BF16)
        next_piece()

    @pl.when(safe)
    def _():
        for c in range(_N_CHUNK):
            rs = slice(c * CHUNK_B, (c + 1) * CHUNK_B)
            for h in range(N_HEADS_B):
                ls = slice(h * HEAD_DIM, (h + 1) * HEAD_DIM)
                a = lax.dot_general(qs_ref[rs, ls], ks_ref[rs, ls], _NT, preferred_element_type=F32)
                a_ref[c * N_HEADS_B + h] = jnp.where(causal, a, 0.0)

    @pl.when(jnp.logical_not(safe))
    def _():
        lane = lax.broadcasted_iota(jnp.int32, (1, CHUNK_B), 1)
        trow = lax.broadcasted_iota(jnp.int32, (CHUNK_B, 1), 0)
        for c in range(_N_CHUNK):
            rs = slice(c * CHUNK_B, (c + 1) * CHUNK_B)
            b = bs_ref[rs, :]
            kh_ref[rs, :] = (kf_ref[rs, :] * jnp.exp(b[CHUNK_B - 1:CHUNK_B, :] - b)).astype(BF16)
            for h in range(N_HEADS_B):
                ls = slice(h * HEAD_DIM, (h + 1) * HEAD_DIM)
                bq = bs_ref[rs, ls]
                qh = qsil_ref[rs, ls]

                def body(s, acc, c=c, ls=ls, bq=bq, qh=qh):
                    b_s = _dynamic_row(bs_ref, c * CHUNK_B + s, ls)
                    k_s = _dynamic_row(kf_ref, c * CHUNK_B + s, ls)
                    w = jnp.exp(jnp.minimum(bq - b_s, 0.0))
                    colv = jnp.sum(qh * k_s * w, axis=-1, keepdims=True)
                    colv = jnp.where(trow >= s, colv, 0.0)
                    return acc + colv * (lane == s).astype(F32)

                a_ref[c * N_HEADS_B + h] = lax.fori_loop(0, CHUNK_B, body, jnp.zeros((CHUNK_B, CHUNK_B), F32))

    for c in range(_N_CHUNK):
        rs = slice(c * CHUNK_B, (c + 1) * CHUNK_B)
        for h in range(N_HEADS_B):
            ls = slice(h * HEAD_DIM, (h + 1) * HEAD_DIM)
            st = st_ref[h]
            vh = vs_ref[rs, ls]
            o = lax.dot_general(qs_ref[rs, ls], st.astype(BF16), _NT, preferred_element_type=F32)
            o = o + jnp.dot(a_ref[c * N_HEADS_B + h].astype(BF16), vh, preferred_element_type=F32)
            os_ref[rs, ls] = o
            st_ref[h] = st * ebl_ref[c:c + 1, ls] + lax.dot_general(vh, kh_ref[rs, ls], _TN, preferred_element_type=F32)
            if h == N_HEADS_B - 1:
                next_piece()
    while pieces:
        next_piece()

    for h in range(N_HEADS_B):
        ls = slice(h * HEAD_DIM, (h + 1) * HEAD_DIM)
        o = os_ref[:, ls]
        o = o * lax.rsqrt(jnp.mean(o * o, axis=-1, keepdims=True) + RMS_EPS)
        o_ref[:, ls] = (o * gs_ref[:, ls]).astype(o_ref.dtype)


def _hgrn2(x, w_b, lb_logits, norm_g):
    b, s, d = x.shape
    tb = HGRN_TILE
    n = b * s // tb
    n_slots = lb_logits.shape[0]
    out = pl.pallas_call(
        functools.partial(_hgrn_kernel, tiles_per_seq=s // tb),
        grid=(n,),
        in_specs=[pl.BlockSpec((tb, d), lambda i: (i, 0)),
                  pl.BlockSpec((tb, d), lambda i: (jnp.minimum(i + 1, n - 1), 0)),
                  pl.BlockSpec(w_b.shape, lambda i: (0, 0), pipeline_mode=pl.Buffered(1)),
                  pl.BlockSpec((n_slots, WIDTH_B), lambda i: (0, 0)),
                  pl.BlockSpec((1, WIDTH_B), lambda i: (0, 0))],
        out_specs=pl.BlockSpec((tb, WIDTH_B), lambda i: (i, 0)),
        out_shape=jax.ShapeDtypeStruct((b * s, WIDTH_B), BF16),
        scratch_shapes=[
            pltpu.VMEM((tb, WIDTH_B), F32),
            pltpu.VMEM((tb, WIDTH_B), F32),
            pltpu.VMEM((tb, WIDTH_B), F32),
            pltpu.VMEM((tb, WIDTH_B), F32),
            pltpu.VMEM((tb, d), BF16),
            pltpu.VMEM((N_HEADS_B, HEAD_DIM, HEAD_DIM), F32),
            pltpu.VMEM((tb, WIDTH_B), BF16),
            pltpu.VMEM((tb, WIDTH_B), BF16),
            pltpu.VMEM((tb, WIDTH_B), BF16),
            pltpu.VMEM((tb, WIDTH_B), BF16),
            pltpu.VMEM((tb, WIDTH_B), F32),
            pltpu.VMEM((8, WIDTH_B), F32),
            pltpu.VMEM((_N_CHUNK * N_HEADS_B, CHUNK_B, CHUNK_B), F32),
            pltpu.VMEM((tb, WIDTH_B), F32),
            pltpu.VMEM((tb, WIDTH_B), F32),
            pltpu.VMEM((tb, WIDTH_B), F32),
            pltpu.VMEM((tb, WIDTH_B), F32),
        ],
        compiler_params=_cparams(("arbitrary",)),
        name="hgrn2",
    )(x.reshape(b * s, d), x.reshape(b * s, d), w_b, lb_logits, norm_g)
    return out.reshape(b, s, WIDTH_B)


MERGE_TILE = 512


def _layer_norm(y, g, b):
    mu = jnp.mean(y, axis=-1, keepdims=True)
    d = y - mu
    var = jnp.mean(d * d, axis=-1, keepdims=True)
    return d * lax.rsqrt(var + LN_EPS) * g + b


def _merge_kernel(x_ref, o1_ref, l1_ref, o2_ref, l2_ref, o3_ref, l3_ref, ob_ref, mk_ref, mv_ref,
                  wc_ref, wg_ref, wa_ref, wb_ref, wcc_ref, wo_ref, g_ref, b_ref, wr_ref, br_ref,
                  h_ref, code_ref, wrow_ref, cnt_ref, oc_ref, cnt_sc, *, n_tokens):
    x = x_ref[...]
    xb = x.astype(BF16)

    qc = jnp.dot(xb, wc_ref[...], preferred_element_type=F32).astype(BF16)
    heads_c = [slice(h * HEAD_DIM, (h + 1) * HEAD_DIM) for h in range(N_HEADS_C)]
    s_all = [lax.dot_general(qc[:, ls], mk_ref[:, ls], _NT, preferred_element_type=F32) for ls in heads_c]
    gate_pre = [None] * 3
    gate_pre[0] = jnp.dot(xb, wg_ref[:, :D_MODEL], preferred_element_type=F32)
    p_all = [jnp.exp(s - jnp.max(s, axis=-1, keepdims=True)).astype(BF16) for s in s_all]
    ones = jnp.ones((mk_ref.shape[0], HEAD_DIM), BF16)
    for ls, p in zip(heads_c, p_all):
        l = jnp.dot(p, ones, preferred_element_type=F32)
        oc_ref[:, ls] = (jnp.dot(p, mv_ref[:, ls], preferred_element_type=F32) / l).astype(BF16)
    gate_pre[1] = jnp.dot(xb, wg_ref[:, D_MODEL:2 * D_MODEL], preferred_element_type=F32)
    branch_b = jnp.dot(ob_ref[...], wb_ref[...], preferred_element_type=F32)
    gate_pre[2] = jnp.dot(xb, wg_ref[:, 2 * D_MODEL:], preferred_element_type=F32)

    lses = [l1_ref[...], l2_ref[...], l3_ref[...]]
    m = functools.reduce(jnp.maximum, lses)
    es = [jnp.exp(l - m) for l in lses]
    inv = 1.0 / functools.reduce(lambda a, c: a + c, es)
    mix = [e * inv for e in es]
    o_pairs = [[_unpack_bf16_pair(o_ref[p]) for p in range(HEADS_A // 2)] for o_ref in (o1_ref, o2_ref, o3_ref)]
    oa_heads = []
    for h in range(HEADS_A):
        acc = None
        for gi in range(N_GROUPS_A):
            w = jnp.broadcast_to(mix[gi][:, h * _LSE_LANES:h * _LSE_LANES + 1], (x.shape[0], HEAD_DIM))
            term = w * o_pairs[gi][h // 2][h % 2]
            acc = term if acc is None else acc + term
        oa_heads.append(acc.astype(BF16))
    oa = jnp.concatenate(oa_heads, axis=-1)

    branch_a = jnp.dot(oa, wa_ref[...], preferred_element_type=F32)
    branch_c = jnp.dot(oc_ref[...], wcc_ref[...], preferred_element_type=F32)
    merged = _sigmoid(gate_pre[1]) * branch_b
    merged = merged + _sigmoid(gate_pre[0]) * branch_a
    merged = merged + _sigmoid(gate_pre[2]) * branch_c
    y = DN_ALPHA * x + jnp.dot(merged.astype(BF16), wo_ref[...], preferred_element_type=F32)
    hv = _layer_norm(y, g_ref[...], b_ref[...])
    h_ref[...] = hv

    first = jnp.logical_and(pl.program_id(0) == 0, pl.program_id(1) == 0)
    _route(hv, wr_ref, br_ref, code_ref, wrow_ref, cnt_ref, cnt_sc, first, n_tokens)


def _merge(x, oa_parts, ob, mkv, wc, wg, wa, wb, wcc, wo, ln_g, ln_b, wr, br):
    b, s, d = x.shape
    tm = MERGE_TILE
    nt = s // tm
    tok = lambda w: pl.BlockSpec((None, tm, w), lambda bi, t: (bi, t, 0))
    full = lambda a: pl.BlockSpec(a.shape, lambda bi, t: (0,) * a.ndim, pipeline_mode=pl.Buffered(1))
    mem = lambda half: pl.BlockSpec((None, mkv.shape[1], WIDTH_C), lambda bi, t: (bi, 0, half))
    o_spec = pl.BlockSpec((None, HEADS_A // 2, tm, HEAD_DIM), lambda bi, t: (bi, 0, t, 0))
    flat = [a for pair in oa_parts for a in pair]
    return pl.pallas_call(
        functools.partial(_merge_kernel, n_tokens=b * s),
        grid=(b, nt),
        in_specs=[tok(d)] + [o_spec, tok(LANES)] * N_GROUPS_A + [tok(WIDTH_B), mem(0), mem(1),
                  full(wc), full(wg), full(wa), full(wb), full(wcc), full(wo), full(ln_g), full(ln_b), full(wr), full(br)],
        out_specs=[tok(d), pl.BlockSpec((1, tm), lambda bi, t: (0, bi * nt + t)),
                   pl.BlockSpec((tm, LANES), lambda bi, t: (bi * nt + t, 0)),
                   pl.BlockSpec((N_ROUTE_CLASSES, LANES), lambda bi, t: (0, 0))],
        out_shape=[jax.ShapeDtypeStruct((b, s, d), F32), jax.ShapeDtypeStruct((1, b * s), jnp.int32),
                   jax.ShapeDtypeStruct((b * s, LANES), F32), jax.ShapeDtypeStruct((N_ROUTE_CLASSES, LANES), jnp.int32)],
        scratch_shapes=[pltpu.VMEM((tm, WIDTH_C), BF16), pltpu.VMEM((N_ROUTE_CLASSES, LANES), F32)],
        compiler_params=_cparams(("arbitrary", "arbitrary")),
        name="merge",
    )(x, *flat, ob, mkv, mkv, wc, wg, wa, wb, wcc, wo, ln_g, ln_b, wr, br)


N_ROUTE_CLASSES = N_EXPERT_GROUPS * EXPERTS_PER_GROUP * EXPERTS_PER_GROUP
_ROUTER_ROWS = 8 + N_EXPERTS


def _route(hv, wr_ref, br_ref, dest_ref, wrow_ref, cnt_ref, cnt_sc, first_step, group_capacity):
    tm = hv.shape[0]

    @pl.when(first_step)
    def _():
        cnt_sc[...] = jnp.zeros_like(cnt_sc)

    h_hi = hv.astype(BF16)
    h_lo = (hv - h_hi.astype(F32)).astype(BF16)
    wv = wr_ref[...]
    w_hi = wv.astype(BF16)
    w_lo = (wv - w_hi.astype(F32)).astype(BF16)
    both = jnp.dot(h_hi, jnp.concatenate([w_hi, w_lo], axis=1), preferred_element_type=F32)
    logits = both[:, :LANES] + both[:, LANES:] + jnp.dot(h_lo, w_hi, preferred_element_type=F32)
    logits = logits.T[:_ROUTER_ROWS, :] + br_ref[:, 0:1]
    g = [logits[i:i + 1, :] for i in range(N_EXPERT_GROUPS)]
    gmax = functools.reduce(jnp.maximum, g)
    gsel = jnp.full_like(gmax, N_EXPERT_GROUPS - 1).astype(jnp.int32)
    for i in range(N_EXPERT_GROUPS - 2, -1, -1):
        gsel = jnp.where(g[i] == gmax, i, gsel)
    gprob = 1.0 / functools.reduce(lambda a, c: a + c, [jnp.exp(gi - gmax) for gi in g])

    esel = logits[8 + (N_EXPERT_GROUPS - 1) * EXPERTS_PER_GROUP:8 + N_EXPERTS, :]
    for i in range(N_EXPERT_GROUPS - 2, -1, -1):
        esel = jnp.where(gsel == i, logits[8 + i * EXPERTS_PER_GROUP:8 + (i + 1) * EXPERTS_PER_GROUP, :], esel)
    ridx = lax.broadcasted_iota(jnp.int32, (EXPERTS_PER_GROUP, tm), 0)
    v1 = jnp.max(esel, axis=0, keepdims=True)
    i1 = jnp.min(jnp.where(esel == v1, ridx, EXPERTS_PER_GROUP), axis=0, keepdims=True)
    rest = jnp.where(ridx == i1, -jnp.inf, esel)
    v2 = jnp.max(rest, axis=0, keepdims=True)
    i2 = jnp.min(jnp.where(rest == v2, ridx, EXPERTS_PER_GROUP), axis=0, keepdims=True)
    t = jnp.exp(v2 - v1)
    w1 = gprob / (1.0 + t)
    w2 = gprob * t / (1.0 + t)
    wt = jnp.where(ridx == i1, w1, jnp.where(ridx == i2, w2, 0.0))
    eye = (lax.broadcasted_iota(jnp.int32, (EXPERTS_PER_GROUP, LANES), 0)
           == lax.broadcasted_iota(jnp.int32, (EXPERTS_PER_GROUP, LANES), 1)).astype(F32)
    wrow_ref[...] = lax.dot_general(wt, eye, _TN, preferred_element_type=F32, precision=lax.Precision.HIGHEST)

    cls = (gsel * EXPERTS_PER_GROUP + jnp.minimum(i1, i2)) * EXPERTS_PER_GROUP + jnp.maximum(i1, i2)
    crow = lax.broadcasted_iota(jnp.int32, (N_ROUTE_CLASSES, tm), 0)
    onehot = (crow == cls).astype(F32)
    before = (lax.broadcasted_iota(jnp.int32, (tm, tm), 0) < lax.broadcasted_iota(jnp.int32, (tm, tm), 1))
    excl = jnp.dot(onehot.astype(BF16), before.astype(BF16), preferred_element_type=F32)
    pos = cnt_sc[:, 0:1] + excl
    rank = jnp.sum(onehot * pos, axis=0, keepdims=True).astype(jnp.int32)
    dest_ref[...] = cls * group_capacity + rank
    cnt_sc[...] = cnt_sc[...] + jnp.sum(onehot, axis=1, keepdims=True)
    cnt_ref[...] = cnt_sc[...].astype(jnp.int32)


DISPATCH_TILE = 1024


SUBLANES = 8
_HALF = D_MODEL // 2


def _tile_copy(src_ref, src_tok, dst_ref, dst_tok, sem):
    src = src_ref.at[pl.ds(pl.multiple_of(src_tok * SUBLANES, SUBLANES), SUBLANES)]
    dst = dst_ref.at[pl.ds(pl.multiple_of(dst_tok * SUBLANES, SUBLANES), SUBLANES)]
    return pltpu.make_async_copy(src, dst, sem)


def _tile_rows(ref, j, n):
    return ref.at[pl.ds(j, n, stride=SUBLANES), :]


def _all_tiles_wait(src_ref, dst_ref, sem):
    pltpu.make_async_copy(src_ref, dst_ref.at[pl.ds(0, src_ref.shape[0])], sem).wait()


def _dispatch_kernel(zblk_ref, dest_ref, h_ref, w_ref, xbuf_ref, pay_ref, zero_ref, sems, zsem):
    i = pl.program_id(0)
    n_steps = pl.num_programs(0)
    slot = lax.rem(i, 2)
    tm = h_ref.shape[0]
    pay = pay_ref.at[slot]

    @pl.when(i == 0)
    def _():
        zero_ref[...] = jnp.zeros_like(zero_ref)
        for k in range(zblk_ref.shape[0]):
            @pl.when(zblk_ref[k] >= 0)
            def _():
                start = pl.multiple_of(zblk_ref[k] * zero_ref.shape[0], zero_ref.shape[0])
                cp = pltpu.make_async_copy(zero_ref, xbuf_ref.at[pl.ds(start, zero_ref.shape[0])], zsem)
                cp.start()
                cp.wait()

    @pl.when(i >= 2)
    def _():
        _all_tiles_wait(pay, xbuf_ref, sems.at[slot])

    hv = h_ref[...]
    for j in range(_HALF // LANES):
        lo, hi = hv[:, j * LANES:(j + 1) * LANES], hv[:, _HALF + j * LANES:_HALF + (j + 1) * LANES]
        _tile_rows(pay, j, tm)[...] = _pack_bf16_pair(lo, hi)
    _tile_rows(pay, _HALF // LANES, tm)[...] = pltpu.bitcast(w_ref[...], jnp.uint32)
    for j in range(_HALF // LANES + 1, SUBLANES):
        _tile_rows(pay, j, tm)[...] = jnp.zeros((tm, LANES), jnp.uint32)

    def issue(t, carry):
        _tile_copy(pay, t, xbuf_ref, dest_ref[0, t], sems.at[slot]).start()
        return carry

    lax.fori_loop(0, tm, issue, 0, unroll=8)

    @pl.when(i == n_steps - 1)
    def _():
        @pl.when(i >= 1)
        def _():
            _all_tiles_wait(pay_ref.at[1 - slot], xbuf_ref, sems.at[1 - slot])

        _all_tiles_wait(pay, xbuf_ref, sems.at[slot])


def _dispatch(h2d, wrow, dest, zero_blocks, n_blocks):
    t, d = h2d.shape
    tm = DISPATCH_TILE
    grid_spec = pltpu.PrefetchScalarGridSpec(
        num_scalar_prefetch=1,
        grid=(t // tm,),
        in_specs=[pl.BlockSpec((1, tm), lambda i, zb: (0, i), memory_space=pltpu.SMEM),
                  pl.BlockSpec((tm, d), lambda i, zb: (i, 0)),
                  pl.BlockSpec((tm, LANES), lambda i, zb: (i, 0))],
        out_specs=pl.BlockSpec(memory_space=pl.ANY),
        scratch_shapes=[pltpu.VMEM((2, tm * SUBLANES, LANES), jnp.uint32),
                        pltpu.VMEM((EXPERT_BLOCK * SUBLANES, LANES), jnp.uint32),
                        pltpu.SemaphoreType.DMA((2,)), pltpu.SemaphoreType.DMA(())],
    )
    return pl.pallas_call(
        _dispatch_kernel,
        grid_spec=grid_spec,
        out_shape=jax.ShapeDtypeStruct((n_blocks * EXPERT_BLOCK * SUBLANES, LANES), jnp.uint32),
        compiler_params=_cparams(("arbitrary",)),
        name="dispatch",
    )(zero_blocks, dest, h2d, wrow)


def _expert_kernel(bg_ref, pres_ref, x_ref, wg_ref, wu_ref, wd_ref, y_ref, xs_ref, ws_ref, ys_ref):
    i = pl.program_id(0)
    eb = xs_ref.shape[0]
    lo, hi = [], []
    for j in range(_HALF // LANES):
        lo_j, hi_j = _unpack_bf16_pair(_tile_rows(x_ref, j, eb)[...])
        lo.append(lo_j)
        hi.append(hi_j)
    xs_ref[...] = jnp.concatenate(lo + hi, axis=-1).astype(BF16)
    ws_ref[...] = pltpu.bitcast(_tile_rows(x_ref, _HALF // LANES, eb)[...], F32)
    ys_ref[...] = jnp.zeros_like(ys_ref)
    halves = [slice(0, eb // 2), slice(eb // 2, eb)]

    for e in range(EXPERTS_PER_GROUP):
        @pl.when(pres_ref[i * EXPERTS_PER_GROUP + e] > 0)
        def _(e=e):
            gu = [(jnp.dot(xs_ref[p, :], wg_ref[e], preferred_element_type=F32),
                   jnp.dot(xs_ref[p, :], wu_ref[e], preferred_element_type=F32)) for p in halves]
            for p, (a, u) in zip(halves, gu):
                we = ws_ref[p, e:e + 1]
                hid = jnp.where(we != 0.0, a * _sigmoid(a) * u * we, 0.0).astype(BF16)
                ys_ref[p, :] += jnp.dot(hid, wd_ref[e], preferred_element_type=F32)

    for j in range(SUBLANES):
        _tile_rows(y_ref, j, eb)[...] = ys_ref[:, j * LANES:(j + 1) * LANES]


def _experts(xbuf, block_g, present, wg, wu, wd):
    eb = EXPERT_BLOCK
    blk = pl.BlockSpec((eb * SUBLANES, LANES), lambda i, bg, pres: (i, 0))
    gw = lambda a: pl.BlockSpec((EXPERTS_PER_GROUP,) + a.shape[1:], lambda i, bg, pres: (bg[i], 0, 0))
    grid_spec = pltpu.PrefetchScalarGridSpec(
        num_scalar_prefetch=2,
        grid=(block_g.shape[0],),
        in_specs=[blk, gw(wg), gw(wu), gw(wd)],
        out_specs=blk,
        scratch_shapes=[pltpu.VMEM((eb, D_MODEL), BF16), pltpu.VMEM((eb, LANES), F32), pltpu.VMEM((eb, D_MODEL), F32)],
    )
    return pl.pallas_call(
        _expert_kernel,
        grid_spec=grid_spec,
        out_shape=jax.ShapeDtypeStruct(xbuf.shape, F32),
        compiler_params=_cparams(("arbitrary",)),
        name="experts",
    )(block_g, present, xbuf, wg, wu, wd)


FINAL_TILE = 512


def _final_kernel(dest_ref, dest_next_ref, h_ref, g_ref, b_ref, ybuf_ref, o_ref, ys_ref, sems):
    i = pl.program_id(0)
    slot = lax.rem(i, 2)
    tm = h_ref.shape[0]

    def gather(d_ref, s):
        def issue(t, carry):
            _tile_copy(ybuf_ref, d_ref[0, t], ys_ref.at[s], t, sems.at[s]).start()
            return carry

        lax.fori_loop(0, tm, issue, 0, unroll=8)

    @pl.when(i == 0)
    def _():
        gather(dest_ref, slot)

    @pl.when(i + 1 < pl.num_programs(0))
    def _():
        gather(dest_next_ref, 1 - slot)

    ys = ys_ref.at[slot]
    pltpu.make_async_copy(ybuf_ref.at[pl.ds(0, ys.shape[0])], ys, sems.at[slot]).wait()
    y = jnp.concatenate([_tile_rows(ys, j, tm)[...] for j in range(SUBLANES)], axis=-1)
    o_ref[...] = _layer_norm(DN_ALPHA * h_ref[...] + y, g_ref[...], b_ref[...])


def _final(h2d, ybuf, dest, ln_g, ln_b):
    t, d = h2d.shape
    tm = FINAL_TILE
    n = t // tm
    row = pl.BlockSpec((tm, d), lambda i: (i, 0))
    vec = pl.BlockSpec((1, d), lambda i: (0, 0))
    return pl.pallas_call(
        _final_kernel,
        grid=(n,),
        in_specs=[pl.BlockSpec((1, tm), lambda i: (0, i), memory_space=pltpu.SMEM),
                  pl.BlockSpec((1, tm), lambda i: (0, jnp.minimum(i + 1, n - 1)), memory_space=pltpu.SMEM),
                  row, vec, vec, pl.BlockSpec(memory_space=pl.ANY)],
        out_specs=row,
        out_shape=jax.ShapeDtypeStruct((t, d), F32),
        scratch_shapes=[pltpu.VMEM((2, tm * SUBLANES, LANES), F32), pltpu.SemaphoreType.DMA((2,))],
        compiler_params=_cparams(("arbitrary",)),
        name="final",
    )(dest, dest, h2d, ln_g, ln_b, ybuf)


def kernel(x, mem, rel_bias, hgrn_lb_logits, w_in, w_mem_kv, hgrn_norm_g, w_branch_a, w_branch_b, w_branch_c, w_out, ln1_g, ln1_b, w_router_group, b_router_group, w_router_expert, b_router_expert, w_exp_gate, w_exp_up, w_exp_down, ln2_g, ln2_b):
    b, s, d = x.shape
    t = b * s
    assert d == D_MODEL and w_in.shape[0] == DEPTH == 1
    assert s % (BAND * DILATED_GROUPS[-1][1]) == 0 and s % HGRN_TILE == 0 and s % MERGE_TILE == 0 and t % EXPERT_BLOCK == 0
    li = 0
    scale = HEAD_DIM ** -0.5
    x2d = x.reshape(t, d)

    w = w_in[li]
    col_scale = np.ones((COLS_A,), np.float32).reshape(N_GROUPS_A, 3, WIDTH_A)
    col_scale[:, 0] = scale
    w_a = (w[:, :COLS_A] * jnp.asarray(col_scale.reshape(-1))).astype(BF16)
    w_b = w[:, COLS_A:COLS_A + COLS_B].astype(BF16)
    w_c = (w[:, COLS_A + COLS_B:COLS_A + COLS_B + COLS_C] * scale).astype(BF16)
    w_g = w[:, COLS_A + COLS_B + COLS_C:].astype(BF16)

    mkv = _matmul(mem.reshape(-1, d), w_mem_kv[li].astype(BF16), BF16, 256, 2 * WIDTH_C, "mem_kv")
    mkv = mkv.reshape(b, -1, 2 * WIDTH_C)

    oa_parts = []
    for gi, (_, dilation) in enumerate(DILATED_GROUPS):
        bias = _band_bias(rel_bias[:, gi * HEADS_A:(gi + 1) * HEADS_A], dilation)
        pg = _proj_classes(x, w_a, gi, 3 * WIDTH_A, dilation, f"proj_a_{gi}")
        oa_parts.append(_band_attention(pg, bias, f"band_attn_{gi}"))

    ob = _hgrn2(x, w_b, hgrn_lb_logits.astype(F32), hgrn_norm_g[li].reshape(1, WIDTH_B))

    wr = jnp.zeros((d, LANES), F32)
    wr = wr.at[:, :N_EXPERT_GROUPS].set(w_router_group[li]).at[:, 8:_ROUTER_ROWS].set(w_router_expert[li])
    br = jnp.zeros((_ROUTER_ROWS,), F32).at[:N_EXPERT_GROUPS].set(b_router_group[li]).at[8:].set(b_router_expert[li])
    br = jnp.broadcast_to(br[:, None], (_ROUTER_ROWS, 128))

    h, code, wrow, counts = _merge(
        x, oa_parts, ob, mkv, w_c, w_g, w_branch_a[li].astype(BF16), w_branch_b[li].astype(BF16),
        w_branch_c[li].astype(BF16), w_out[li].astype(BF16), ln1_g[li].reshape(1, d), ln1_b[li].reshape(1, d), wr, br)
    h2d = h.reshape(t, d)
    eb = EXPERT_BLOCK
    counts = counts[:, 0]
    per_group = EXPERTS_PER_GROUP * EXPERTS_PER_GROUP
    group_counts = jnp.sum(counts.reshape(N_EXPERT_GROUPS, per_group), axis=1)

    blocks_g = (group_counts + eb - 1) // eb
    ends = jnp.cumsum(blocks_g)
    starts = ends - blocks_g
    n_active = ends[-1:].astype(jnp.int32)
    n_blocks = t // eb + N_EXPERT_GROUPS
    blk = jnp.arange(n_blocks, dtype=jnp.int32)
    block_g = jnp.minimum(jnp.sum(ends[None, :] <= blk[:, None], axis=1), N_EXPERT_GROUPS - 1).astype(jnp.int32)
    in_group = counts.reshape(N_EXPERT_GROUPS, per_group)
    class_start = ((starts * eb)[:, None] + jnp.cumsum(in_group, axis=1) - in_group).reshape(-1)
    cls = code // t
    is_c = cls == jnp.arange(N_ROUTE_CLASSES, dtype=jnp.int32)[:, None]
    dest = (code - cls * t + jnp.sum(jnp.where(is_c, class_start[:, None], 0), axis=0, keepdims=True)).astype(jnp.int32)
    trailing = n_active + jnp.arange(N_EXPERT_GROUPS, dtype=jnp.int32)
    zero_blocks = jnp.concatenate([jnp.where(blocks_g > 0, ends - 1, -1),
                                   jnp.where(trailing < n_blocks, trailing, -1)]).astype(jnp.int32)
    cidx = np.arange(N_ROUTE_CLASSES)
    pair_lo, pair_hi = (cidx // EXPERTS_PER_GROUP) % EXPERTS_PER_GROUP, cidx % EXPERTS_PER_GROUP
    member = np.asarray((np.arange(EXPERTS_PER_GROUP)[None, :] == pair_lo[:, None])
                        | (np.arange(EXPERTS_PER_GROUP)[None, :] == pair_hi[:, None]), np.int32)
    overlap = ((class_start[None, :] < (blk[:, None] + 1) * eb) & (class_start[None, :] + counts[None, :] > blk[:, None] * eb)
               & (counts[None, :] > 0))
    present = (jnp.sum(overlap[:, :, None].astype(jnp.int32) * jnp.asarray(member)[None], axis=1) > 0).astype(jnp.int32)

    xbuf = _dispatch(h2d, wrow, dest, zero_blocks, n_blocks)
    ybuf = _experts(xbuf, block_g, present.reshape(-1),
                    w_exp_gate[li].astype(BF16), w_exp_up[li].astype(BF16), w_exp_down[li].astype(BF16))
    out = _final(h2d, ybuf, dest, ln2_g[li].reshape(1, d), ln2_b[li].reshape(1, d))
    return out.reshape(b, s, d)
```

```python
import functools
import math

import numpy as np
import jax
import jax.numpy as jnp
from jax import lax
from jax.experimental import pallas as pl
from jax.experimental.pallas import tpu as pltpu

F32 = jnp.float32
BF16 = jnp.bfloat16

D_MODEL = 1024
DEPTH = 1
DILATED_GROUPS = ((128, 1), (512, 4), (2048, 16))
N_GROUPS_A = len(DILATED_GROUPS)
HEADS_A = 4
HEAD_DIM = 128
WIDTH_A = HEADS_A * HEAD_DIM
BAND = 128
NUM_BUCKETS = 32
MAX_DISTANCE = 2048
N_HEADS_B = 8
WIDTH_B = N_HEADS_B * HEAD_DIM
CHUNK_B = 64
N_HEADS_C = 4
WIDTH_C = N_HEADS_C * HEAD_DIM
COLS_A = N_GROUPS_A * 3 * WIDTH_A
COLS_B = 4 * WIDTH_B
COLS_C = WIDTH_C
COLS_GATE = 3 * D_MODEL
N_EXPERT_GROUPS = 4
EXPERTS_PER_GROUP = 8
N_EXPERTS = N_EXPERT_GROUPS * EXPERTS_PER_GROUP
TOP_K = 2
D_EXPERT = D_MODEL // 4
DN_ALPHA = (2 * DEPTH) ** 0.25
LN_EPS = 1e-5
RMS_EPS = 1e-6

NEG_BIG = -1e30
SAFE_LOG_DECAY = 80.0
VMEM_LIMIT = 56 * 1024 * 1024
EXPERT_BLOCK = 256

_NT = (((1,), (1,)), ((), ()))
_TN = (((0,), (0,)), ((), ()))


def _cparams(sem):
    return pltpu.CompilerParams(dimension_semantics=sem, vmem_limit_bytes=VMEM_LIMIT)


def _matmul_kernel(x_ref, w_ref, o_ref):
    o_ref[...] = jnp.dot(x_ref[...].astype(BF16), w_ref[...], preferred_element_type=F32).astype(o_ref.dtype)


def _matmul(x, w, out_dtype, tm, tn, name):
    t, k = x.shape
    n = w.shape[1]
    return pl.pallas_call(
        _matmul_kernel,
        grid=(n // tn, t // tm),
        in_specs=[pl.BlockSpec((tm, k), lambda j, i: (i, 0)), pl.BlockSpec((k, tn), lambda j, i: (0, j))],
        out_specs=pl.BlockSpec((tm, tn), lambda j, i: (i, j)),
        out_shape=jax.ShapeDtypeStruct((t, n), out_dtype),
        compiler_params=_cparams(("arbitrary", "arbitrary")),
        name=name,
    )(x, w)


def _t5_bucket_np(dist):
    dist = np.asarray(dist, np.int32)
    max_exact = NUM_BUCKETS // 2
    d = np.maximum(dist, 1).astype(np.float32)
    large = max_exact + (np.log(d / max_exact) / math.log(MAX_DISTANCE / max_exact) * (NUM_BUCKETS - max_exact)).astype(np.int32)
    large = np.minimum(large, NUM_BUCKETS - 1)
    return np.where(dist < max_exact, dist, large).astype(np.int32)


def _band_bias(bias_tab, dilation):
    i = np.arange(BAND)[:, None]
    j = np.arange(2 * BAND)[None, :]
    u = i + BAND - j
    in_band = (u >= 0) & (u <= BAND)
    bucket = _t5_bucket_np(np.clip(u, 0, BAND) * dilation)
    onehot = np.asarray(bucket[:, :, None] == np.arange(NUM_BUCKETS)[None, None, :], np.float32)
    bias = jnp.einsum('pqb,bh->hpq', jnp.asarray(onehot), bias_tab.astype(F32), precision=lax.Precision.HIGHEST)
    general = jnp.where(in_band[None], bias, NEG_BIG)
    first = jnp.where((in_band & (j >= BAND))[None], bias, NEG_BIG)
    return jnp.stack([first, general])


LANES = 128


def _proj_perm_kernel(*refs, r):
    n_slab = D_MODEL // LANES
    x_refs, (w_ref, o_ref, xp_ref) = refs[:n_slab], refs[n_slab:]
    tm = xp_ref.shape[0]
    n = tm // r
    for j, x_ref in enumerate(x_refs):
        for c in range(r):
            rows = x_ref[pl.ds(c, n, stride=r), :] if r > 1 else x_ref[...]
            xp_ref[c * n:(c + 1) * n, j * LANES:(j + 1) * LANES] = rows.astype(BF16)
    res = jnp.dot(xp_ref[...], w_ref[...], preferred_element_type=F32).astype(o_ref.dtype)
    for c in range(r):
        o_ref[c] = res[c * n:(c + 1) * n]


PROJ_A_TILE = 1024


def _proj_classes(x, w, col_block, n, r, name):
    b, s, d = x.shape
    tm = PROJ_A_TILE
    n_slab = d // LANES
    slabs = [pl.BlockSpec((None, tm, LANES), lambda bi, t, j=j: (bi, t, j)) for j in range(n_slab)]
    return pl.pallas_call(
        functools.partial(_proj_perm_kernel, r=r),
        grid=(b, s // tm),
        in_specs=slabs + [pl.BlockSpec((d, n), lambda bi, t: (0, col_block))],
        out_specs=pl.BlockSpec((None, r, tm // r, n), lambda bi, t: (bi, 0, t, 0)),
        out_shape=jax.ShapeDtypeStruct((b, r, s // r, n), BF16),
        scratch_shapes=[pltpu.VMEM((tm, d), BF16)],
        compiler_params=_cparams(("arbitrary", "arbitrary")),
        name=name,
    )(*([x] * n_slab), w)


BAND_SPAN = 2048
_HI_MASK = 0xFFFF0000
_LSE_LANES = LANES // HEADS_A


def _pack_bf16_pair(a, b):
    abits = pltpu.bitcast(a.astype(BF16).astype(F32), jnp.uint32)
    bbits = pltpu.bitcast(b.astype(BF16).astype(F32), jnp.uint32)
    return (abits >> 16) | (bbits & jnp.uint32(_HI_MASK))


def _unpack_bf16_pair(word):
    return pltpu.bitcast(word << 16, F32), pltpu.bitcast(word & jnp.uint32(_HI_MASK), F32)


def _band_attn_kernel(q_ref, k_ref, v_ref, kp_ref, vp_ref, bias_ref, o_ref, lse_ref, *, r):
    nq = q_ref.shape[1] // BAND
    variant = jnp.minimum(pl.program_id(1), 1)
    ones = jnp.ones((BAND, HEAD_DIM), BF16)
    lane = lax.broadcasted_iota(jnp.int32, (BAND, LANES), 1)
    heads = [slice(h * HEAD_DIM, (h + 1) * HEAD_DIM) for h in range(HEADS_A)]
    units = [(c, j) for c in range(r) for j in range(nq)]

    def prev_of(k_or_v_ref, prev_ref, c, j, sl):
        return prev_ref[c, :, sl] if j == 0 else k_or_v_ref[c, (j - 1) * BAND:j * BAND, sl]

    def scores(c, j):
        cur = slice(j * BAND, (j + 1) * BAND)
        out = []
        for h, sl in enumerate(heads):
            q = q_ref[c, cur, sl]
            bias = bias_ref[variant, h] if j == 0 else bias_ref[1, h]
            sp = lax.dot_general(q, prev_of(k_ref, kp_ref, c, j, sl), _NT, preferred_element_type=F32)
            sc = lax.dot_general(q, k_ref[c, cur, sl], _NT, preferred_element_type=F32)
            out.append((sp + bias[:, :BAND], sc + bias[:, BAND:]))
        return out

    nxt = scores(*units[0])
    for u, (c, j) in enumerate(units):
        s_all = nxt
        if u + 1 < len(units):
            nxt = scores(*units[u + 1])
        cur = slice(j * BAND, (j + 1) * BAND)
        rows = pl.ds(j * BAND * r + c, BAND, stride=r) if r > 1 else cur
        probs = []
        for sp, sc in s_all:
            m = jnp.max(jnp.maximum(sp, sc), axis=-1, keepdims=True)
            probs.append((m, jnp.exp(sp - m).astype(BF16), jnp.exp(sc - m).astype(BF16)))
        outs, lses = [], []
        for h, sl in enumerate(heads):
            m, pp, pc = probs[h]
            l = jnp.dot(pp, ones, preferred_element_type=F32) + jnp.dot(pc, ones, preferred_element_type=F32)
            o = jnp.dot(pp, prev_of(v_ref, vp_ref, c, j, sl), preferred_element_type=F32)
            o = o + jnp.dot(pc, v_ref[c, cur, sl], preferred_element_type=F32)
            outs.append(o / l)
            lses.append(m + jnp.log(l))
        for p in range(HEADS_A // 2):
            o_ref[p, rows, :] = _pack_bf16_pair(outs[2 * p], outs[2 * p + 1])
        lse = lses[-1]
        for h in range(HEADS_A - 2, -1, -1):
            lse = jnp.where(lane < (h + 1) * _LSE_LANES, lses[h], lse)
        lse_ref[rows, :] = lse


def _band_attention(pg, bias, name):
    b, r, l, _ = pg.shape
    nq = BAND_SPAN // (BAND * r)
    cur = lambda which: pl.BlockSpec((None, r, nq * BAND, WIDTH_A), lambda bi, n: (bi, 0, n, which))
    prev = lambda which: pl.BlockSpec((None, r, BAND, WIDTH_A), lambda bi, n: (bi, 0, jnp.maximum(n * nq - 1, 0), which))
    return pl.pallas_call(
        functools.partial(_band_attn_kernel, r=r),
        grid=(b, l * r // BAND_SPAN),
        in_specs=[cur(0), cur(1), cur(2), prev(1), prev(2), pl.BlockSpec(bias.shape, lambda bi, n: (0, 0, 0, 0))],
        out_specs=[pl.BlockSpec((None, HEADS_A // 2, BAND_SPAN, HEAD_DIM), lambda bi, n: (bi, 0, n, 0)),
                   pl.BlockSpec((None, BAND_SPAN, LANES), lambda bi, n: (bi, n, 0))],
        out_shape=[jax.ShapeDtypeStruct((b, HEADS_A // 2, l * r, HEAD_DIM), jnp.uint32),
                   jax.ShapeDtypeStruct((b, l * r, LANES), F32)],
        compiler_params=_cparams(("arbitrary", "arbitrary")),
        name=name,
    )(pg, pg, pg, pg, pg, bias)


HGRN_TILE = 512
_N_CHUNK = HGRN_TILE // CHUNK_B


def _dynamic_row(ref, r, ls):
    blk = ref[pl.ds(pl.multiple_of((r // 8) * 8, 8), 8), ls]
    sub = lax.broadcasted_iota(jnp.int32, blk.shape, 0)
    return jnp.sum(jnp.where(sub == r % 8, blk, 0.0), axis=0, keepdims=True)


def _sigmoid(z):
    return 0.5 * jnp.tanh(0.5 * z) + 0.5


_PROJ_PIECE = 256


def _hgrn_kernel(x_ref, xn_ref, w_ref, lbl_ref, ng_ref, o_ref, pq_ref, pf_ref, pi_ref, pg_ref, xb_ref,
                 st_ref, qs_ref, ks_ref, kh_ref, vs_ref, bs_ref, ebl_ref, a_ref, os_ref, gs_ref, kf_ref, qsil_ref,
                 *, tiles_per_seq):
    i = pl.program_id(0)
    parts = (pq_ref, pf_ref, pi_ref, pg_ref)
    per_part = WIDTH_B // _PROJ_PIECE

    def piece(k):
        cols = slice(k * _PROJ_PIECE, (k + 1) * _PROJ_PIECE)
        local = slice((k % per_part) * _PROJ_PIECE, (k % per_part + 1) * _PROJ_PIECE)
        parts[k // per_part][:, local] = jnp.dot(xb_ref[...], w_ref[:, cols], preferred_element_type=F32)

    @pl.when(i == 0)
    def _():
        xb_ref[...] = x_ref[...].astype(BF16)
        for k in range(COLS_B // _PROJ_PIECE):
            piece(k)

    @pl.when(lax.rem(i, tiles_per_seq) == 0)
    def _():
        st_ref[...] = jnp.zeros_like(st_ref)

    xb_ref[...] = xn_ref[...].astype(BF16)
    pieces = [k for part in (2, 3, 1, 0) for k in range(part * per_part, (part + 1) * per_part)]
    next_piece = lambda: piece(pieces.pop(0)) if pieces else None
    vs_ref[...] = pi_ref[...].astype(BF16)
    gv = pg_ref[...]
    gs_ref[...] = gv * _sigmoid(gv) * ng_ref[...]

    lg = lbl_ref[...]
    e = jnp.exp(lg - jnp.max(lg, axis=0, keepdims=True))
    lb = e[0:1, :] / jnp.sum(e, axis=0, keepdims=True)

    row = lax.broadcasted_iota(jnp.int32, (CHUNK_B, CHUNK_B), 0)
    colm = lax.broadcasted_iota(jnp.int32, (CHUNK_B, CHUNK_B), 1)
    causal = row >= colm
    tri = causal.astype(BF16)

    for _ in range(2):
        next_piece()
    f = lb + (1.0 - lb) * _sigmoid(pf_ref[...])
    logf = jnp.log(f)
    kf = 1.0 - f
    kf_ref[...] = kf
    l_hi = logf.astype(BF16)
    rem = logf - l_hi.astype(F32)
    l_mid = rem.astype(BF16)
    l_lo = (rem - l_mid.astype(F32)).astype(BF16)
    min_b = None
    for c in range(_N_CHUNK):
        rs = slice(c * CHUNK_B, (c + 1) * CHUNK_B)
        b = (jnp.dot(tri, l_hi[rs], preferred_element_type=F32) + jnp.dot(tri, l_mid[rs], preferred_element_type=F32)
             + jnp.dot(tri, l_lo[rs], preferred_element_type=F32))
        bs_ref[rs, :] = b
        cm = jnp.min(b[CHUNK_B - 1:CHUNK_B, :])
        min_b = cm if min_b is None else jnp.minimum(min_b, cm)
        if c % 2 == 1:
            next_piece()
    safe = min_b > -SAFE_LOG_DECAY

    qv = pq_ref[...]
    qsil_ref[...] = qv * _sigmoid(qv)
    for c in range(_N_CHUNK):
        rs = slice(c * CHUNK_B, (c + 1) * CHUNK_B)
        b = bs_ref[rs, :]
        eb_last = jnp.exp(b[CHUNK_B - 1:CHUNK_B, :])
        ebl_ref[c:c + 1, :] = eb_last
        qs_ref[rs, :] = (qsil_ref[rs, :] * jnp.exp(b)).astype(BF16)
        k_grown = kf_ref[rs, :] * jnp.exp(-b)
        ks_ref[rs, :] = k_grown.astype(BF16)
        kh_ref[rs, :] = (k_grown * eb_last).astype(BF16)
        next_piece()

    @pl.when(safe)
    def _():
        for c in range(_N_CHUNK):
            rs = slice(c * CHUNK_B, (c + 1) * CHUNK_B)
            for h in range(N_HEADS_B):
                ls = slice(h * HEAD_DIM, (h + 1) * HEAD_DIM)
                a = lax.dot_general(qs_ref[rs, ls], ks_ref[rs, ls], _NT, preferred_element_type=F32)
                a_ref[c * N_HEADS_B + h] = jnp.where(causal, a, 0.0)

    @pl.when(jnp.logical_not(safe))
    def _():
        lane = lax.broadcasted_iota(jnp.int32, (1, CHUNK_B), 1)
        trow = lax.broadcasted_iota(jnp.int32, (CHUNK_B, 1), 0)
        for c in range(_N_CHUNK):
            rs = slice(c * CHUNK_B, (c + 1) * CHUNK_B)
            b = bs_ref[rs, :]
            kh_ref[rs, :] = (kf_ref[rs, :] * jnp.exp(b[CHUNK_B - 1:CHUNK_B, :] - b)).astype(BF16)
            for h in range(N_HEADS_B):
                ls = slice(h * HEAD_DIM, (h + 1) * HEAD_DIM)
                bq = bs_ref[rs, ls]
                qh = qsil_ref[rs, ls]

                def body(s, acc, c=c, ls=ls, bq=bq, qh=qh):
                    b_s = _dynamic_row(bs_ref, c * CHUNK_B + s, ls)
                    k_s = _dynamic_row(kf_ref, c * CHUNK_B + s, ls)
                    w = jnp.exp(jnp.minimum(bq - b_s, 0.0))
                    colv = jnp.sum(qh * k_s * w, axis=-1, keepdims=True)
                    colv = jnp.where(trow >= s, colv, 0.0)
                    return acc + colv * (lane == s).astype(F32)

                a_ref[c * N_HEADS_B + h] = lax.fori_loop(0, CHUNK_B, body, jnp.zeros((CHUNK_B, CHUNK_B), F32))

    for c in range(_N_CHUNK):
        rs = slice(c * CHUNK_B, (c + 1) * CHUNK_B)
        for h in range(N_HEADS_B):
            ls = slice(h * HEAD_DIM, (h + 1) * HEAD_DIM)
            st = st_ref[h]
            vh = vs_ref[rs, ls]
            o = lax.dot_general(qs_ref[rs, ls], st.astype(BF16), _NT, preferred_element_type=F32)
            o = o + jnp.dot(a_ref[c * N_HEADS_B + h].astype(BF16), vh, preferred_element_type=F32)
            os_ref[rs, ls] = o
            st_ref[h] = st * ebl_ref[c:c + 1, ls] + lax.dot_general(vh, kh_ref[rs, ls], _TN, preferred_element_type=F32)
            if h == N_HEADS_B - 1:
                next_piece()
    while pieces:
        next_piece()

    for h in range(N_HEADS_B):
        ls = slice(h * HEAD_DIM, (h + 1) * HEAD_DIM)
        o = os_ref[:, ls]
        o = o * lax.rsqrt(jnp.mean(o * o, axis=-1, keepdims=True) + RMS_EPS)
        o_ref[:, ls] = (o * gs_ref[:, ls]).astype(o_ref.dtype)


def _hgrn2(x, w_b, lb_logits, norm_g):
    b, s, d = x.shape
    tb = HGRN_TILE
    n = b * s // tb
    n_slots = lb_logits.shape[0]
    out = pl.pallas_call(
        functools.partial(_hgrn_kernel, tiles_per_seq=s // tb),
        grid=(n,),
        in_specs=[pl.BlockSpec((tb, d), lambda i: (i, 0)),
                  pl.BlockSpec((tb, d), lambda i: (jnp.minimum(i + 1, n - 1), 0)),
                  pl.BlockSpec(w_b.shape, lambda i: (0, 0), pipeline_mode=pl.Buffered(1)),
                  pl.BlockSpec((n_slots, WIDTH_B), lambda i: (0, 0)),
                  pl.BlockSpec((1, WIDTH_B), lambda i: (0, 0))],
        out_specs=pl.BlockSpec((tb, WIDTH_B), lambda i: (i, 0)),
        out_shape=jax.ShapeDtypeStruct((b * s, WIDTH_B), BF16),
        scratch_shapes=[
            pltpu.VMEM((tb, WIDTH_B), F32),
            pltpu.VMEM((tb, WIDTH_B), F32),
            pltpu.VMEM((tb, WIDTH_B), F32),
            pltpu.VMEM((tb, WIDTH_B), F32),
            pltpu.VMEM((tb, d), BF16),
            pltpu.VMEM((N_HEADS_B, HEAD_DIM, HEAD_DIM), F32),
            pltpu.VMEM((tb, WIDTH_B), BF16),
            pltpu.VMEM((tb, WIDTH_B), BF16),
            pltpu.VMEM((tb, WIDTH_B), BF16),
            pltpu.VMEM((tb, WIDTH_B), BF16),
            pltpu.VMEM((tb, WIDTH_B), F32),
            pltpu.VMEM((8, WIDTH_B), F32),
            pltpu.VMEM((_N_CHUNK * N_HEADS_B, CHUNK_B, CHUNK_B), F32),
            pltpu.VMEM((tb, WIDTH_B), F32),
            pltpu.VMEM((tb, WIDTH_B), F32),
            pltpu.VMEM((tb, WIDTH_B), F32),
            pltpu.VMEM((tb, WIDTH_B), F32),
        ],
        compiler_params=_cparams(("arbitrary",)),
        name="hgrn2",
    )(x.reshape(b * s, d), x.reshape(b * s, d), w_b, lb_logits, norm_g)
    return out.reshape(b, s, WIDTH_B)


MERGE_TILE = 512


def _layer_norm(y, g, b):
    mu = jnp.mean(y, axis=-1, keepdims=True)
    d = y - mu
    var = jnp.mean(d * d, axis=-1, keepdims=True)
    return d * lax.rsqrt(var + LN_EPS) * g + b


def _merge_kernel(x_ref, o1_ref, l1_ref, o2_ref, l2_ref, o3_ref, l3_ref, ob_ref, mk_ref, mv_ref,
                  wc_ref, wg_ref, wa_ref, wb_ref, wcc_ref, wo_ref, g_ref, b_ref, wr_ref, br_ref,
                  h_ref, code_ref, wrow_ref, cnt_ref, oc_ref, cnt_sc, *, n_tokens):
    x = x_ref[...]
    xb = x.astype(BF16)

    qc = jnp.dot(xb, wc_ref[...], preferred_element_type=F32).astype(BF16)
    heads_c = [slice(h * HEAD_DIM, (h + 1) * HEAD_DIM) for h in range(N_HEADS_C)]
    s_all = [lax.dot_general(qc[:, ls], mk_ref[:, ls], _NT, preferred_element_type=F32) for ls in heads_c]
    gate_pre = [None] * 3
    gate_pre[0] = jnp.dot(xb, wg_ref[:, :D_MODEL], preferred_element_type=F32)
    p_all = [jnp.exp(s - jnp.max(s, axis=-1, keepdims=True)).astype(BF16) for s in s_all]
    ones = jnp.ones((mk_ref.shape[0], HEAD_DIM), BF16)
    for ls, p in zip(heads_c, p_all):
        l = jnp.dot(p, ones, preferred_element_type=F32)
        oc_ref[:, ls] = (jnp.dot(p, mv_ref[:, ls], preferred_element_type=F32) / l).astype(BF16)
    gate_pre[1] = jnp.dot(xb, wg_ref[:, D_MODEL:2 * D_MODEL], preferred_element_type=F32)
    branch_b = jnp.dot(ob_ref[...], wb_ref[...], preferred_element_type=F32)
    gate_pre[2] = jnp.dot(xb, wg_ref[:, 2 * D_MODEL:], preferred_element_type=F32)

    lses = [l1_ref[...], l2_ref[...], l3_ref[...]]
    m = functools.reduce(jnp.maximum, lses)
    es = [jnp.exp(l - m) for l in lses]
    inv = 1.0 / functools.reduce(lambda a, c: a + c, es)
    mix = [e * inv for e in es]
    o_pairs = [[_unpack_bf16_pair(o_ref[p]) for p in range(HEADS_A // 2)] for o_ref in (o1_ref, o2_ref, o3_ref)]
    oa_heads = []
    for h in range(HEADS_A):
        acc = None
        for gi in range(N_GROUPS_A):
            w = jnp.broadcast_to(mix[gi][:, h * _LSE_LANES:h * _LSE_LANES + 1], (x.shape[0], HEAD_DIM))
            term = w * o_pairs[gi][h // 2][h % 2]
            acc = term if acc is None else acc + term
        oa_heads.append(acc.astype(BF16))
    oa = jnp.concatenate(oa_heads, axis=-1)

    branch_a = jnp.dot(oa, wa_ref[...], preferred_element_type=F32)
    branch_c = jnp.dot(oc_ref[...], wcc_ref[...], preferred_element_type=F32)
    merged = _sigmoid(gate_pre[1]) * branch_b
    merged = merged + _sigmoid(gate_pre[0]) * branch_a
    merged = merged + _sigmoid(gate_pre[2]) * branch_c
    y = DN_ALPHA * x + jnp.dot(merged.astype(BF16), wo_ref[...], preferred_element_type=F32)
    hv = _layer_norm(y, g_ref[...], b_ref[...])
    h_ref[...] = hv

    first = jnp.logical_and(pl.program_id(0) == 0, pl.program_id(1) == 0)
    _route(hv, wr_ref, br_ref, code_ref, wrow_ref, cnt_ref, cnt_sc, first, n_tokens)


def _merge(x, oa_parts, ob, mkv, wc, wg, wa, wb, wcc, wo, ln_g, ln_b, wr, br):
    b, s, d = x.shape
    tm = MERGE_TILE
    nt = s // tm
    tok = lambda w: pl.BlockSpec((None, tm, w), lambda bi, t: (bi, t, 0))
    full = lambda a: pl.BlockSpec(a.shape, lambda bi, t: (0,) * a.ndim, pipeline_mode=pl.Buffered(1))
    mem = lambda half: pl.BlockSpec((None, mkv.shape[1], WIDTH_C), lambda bi, t: (bi, 0, half))
    o_spec = pl.BlockSpec((None, HEADS_A // 2, tm, HEAD_DIM), lambda bi, t: (bi, 0, t, 0))
    flat = [a for pair in oa_parts for a in pair]
    return pl.pallas_call(
        functools.partial(_merge_kernel, n_tokens=b * s),
        grid=(b, nt),
        in_specs=[tok(d)] + [o_spec, tok(LANES)] * N_GROUPS_A + [tok(WIDTH_B), mem(0), mem(1),
                  full(wc), full(wg), full(wa), full(wb), full(wcc), full(wo), full(ln_g), full(ln_b), full(wr), full(br)],
        out_specs=[tok(d), pl.BlockSpec((1, tm), lambda bi, t: (0, bi * nt + t)),
                   pl.BlockSpec((tm, LANES), lambda bi, t: (bi * nt + t, 0)),
                   pl.BlockSpec((N_ROUTE_CLASSES, LANES), lambda bi, t: (0, 0))],
        out_shape=[jax.ShapeDtypeStruct((b, s, d), F32), jax.ShapeDtypeStruct((1, b * s), jnp.int32),
                   jax.ShapeDtypeStruct((b * s, LANES), F32), jax.ShapeDtypeStruct((N_ROUTE_CLASSES, LANES), jnp.int32)],
        scratch_shapes=[pltpu.VMEM((tm, WIDTH_C), BF16), pltpu.VMEM((N_ROUTE_CLASSES, LANES), F32)],
        compiler_params=_cparams(("arbitrary", "arbitrary")),
        name="merge",
    )(x, *flat, ob, mkv, mkv, wc, wg, wa, wb, wcc, wo, ln_g, ln_b, wr, br)


N_ROUTE_CLASSES = N_EXPERT_GROUPS * EXPERTS_PER_GROUP * EXPERTS_PER_GROUP
_ROUTER_ROWS = 8 + N_EXPERTS


def _route(hv, wr_ref, br_ref, dest_ref, wrow_ref, cnt_ref, cnt_sc, first_step, group_capacity):
    tm = hv.shape[0]

    @pl.when(first_step)
    def _():
        cnt_sc[...] = jnp.zeros_like(cnt_sc)

    h_hi = hv.astype(BF16)
    h_lo = (hv - h_hi.astype(F32)).astype(BF16)
    wv = wr_ref[...]
    w_hi = wv.astype(BF16)
    w_lo = (wv - w_hi.astype(F32)).astype(BF16)
    both = jnp.dot(h_hi, jnp.concatenate([w_hi, w_lo], axis=1), preferred_element_type=F32)
    logits = both[:, :LANES] + both[:, LANES:] + jnp.dot(h_lo, w_hi, preferred_element_type=F32)
    logits = logits.T[:_ROUTER_ROWS, :] + br_ref[:, 0:1]
    g = [logits[i:i + 1, :] for i in range(N_EXPERT_GROUPS)]
    gmax = functools.reduce(jnp.maximum, g)
    gsel = jnp.full_like(gmax, N_EXPERT_GROUPS - 1).astype(jnp.int32)
    for i in range(N_EXPERT_GROUPS - 2, -1, -1):
        gsel = jnp.where(g[i] == gmax, i, gsel)
    gprob = 1.0 / functools.reduce(lambda a, c: a + c, [jnp.exp(gi - gmax) for gi in g])

    esel = logits[8 + (N_EXPERT_GROUPS - 1) * EXPERTS_PER_GROUP:8 + N_EXPERTS, :]
    for i in range(N_EXPERT_GROUPS - 2, -1, -1):
        esel = jnp.where(gsel == i, logits[8 + i * EXPERTS_PER_GROUP:8 + (i + 1) * EXPERTS_PER_GROUP, :], esel)
    ridx = lax.broadcasted_iota(jnp.int32, (EXPERTS_PER_GROUP, tm), 0)
    v1 = jnp.max(esel, axis=0, keepdims=True)
    i1 = jnp.min(jnp.where(esel == v1, ridx, EXPERTS_PER_GROUP), axis=0, keepdims=True)
    rest = jnp.where(ridx == i1, -jnp.inf, esel)
    v2 = jnp.max(rest, axis=0, keepdims=True)
    i2 = jnp.min(jnp.where(rest == v2, ridx, EXPERTS_PER_GROUP), axis=0, keepdims=True)
    t = jnp.exp(v2 - v1)
    w1 = gprob / (1.0 + t)
    w2 = gprob * t / (1.0 + t)
    wt = jnp.where(ridx == i1, w1, jnp.where(ridx == i2, w2, 0.0))
    eye = (lax.broadcasted_iota(jnp.int32, (EXPERTS_PER_GROUP, LANES), 0)
           == lax.broadcasted_iota(jnp.int32, (EXPERTS_PER_GROUP, LANES), 1)).astype(F32)
    wrow_ref[...] = lax.dot_general(wt, eye, _TN, preferred_element_type=F32, precision=lax.Precision.HIGHEST)

    cls = (gsel * EXPERTS_PER_GROUP + jnp.minimum(i1, i2)) * EXPERTS_PER_GROUP + jnp.maximum(i1, i2)
    crow = lax.broadcasted_iota(jnp.int32, (N_ROUTE_CLASSES, tm), 0)
    onehot = (crow == cls).astype(F32)
    before = (lax.broadcasted_iota(jnp.int32, (tm, tm), 0) < lax.broadcasted_iota(jnp.int32, (tm, tm), 1))
    excl = jnp.dot(onehot.astype(BF16), before.astype(BF16), preferred_element_type=F32)
    pos = cnt_sc[:, 0:1] + excl
    rank = jnp.sum(onehot * pos, axis=0, keepdims=True).astype(jnp.int32)
    dest_ref[...] = cls * group_capacity + rank
    cnt_sc[...] = cnt_sc[...] + jnp.sum(onehot, axis=1, keepdims=True)
    cnt_ref[...] = cnt_sc[...].astype(jnp.int32)


DISPATCH_TILE = 1024


SUBLANES = 8
_HALF = D_MODEL // 2


def _tile_copy(src_ref, src_tok, dst_ref, dst_tok, sem):
    src = src_ref.at[pl.ds(pl.multiple_of(src_tok * SUBLANES, SUBLANES), SUBLANES)]
    dst = dst_ref.at[pl.ds(pl.multiple_of(dst_tok * SUBLANES, SUBLANES), SUBLANES)]
    return pltpu.make_async_copy(src, dst, sem)


def _tile_rows(ref, j, n):
    return ref.at[pl.ds(j, n, stride=SUBLANES), :]


def _all_tiles_wait(src_ref, dst_ref, sem):
    pltpu.make_async_copy(src_ref, dst_ref.at[pl.ds(0, src_ref.shape[0])], sem).wait()


def _dispatch_kernel(zblk_ref, dest_ref, h_ref, w_ref, xbuf_ref, pay_ref, zero_ref, sems, zsem):
    i = pl.program_id(0)
    n_steps = pl.num_programs(0)
    slot = lax.rem(i, 2)
    tm = h_ref.shape[0]
    pay = pay_ref.at[slot]

    @pl.when(i == 0)
    def _():
        zero_ref[...] = jnp.zeros_like(zero_ref)
        for k in range(zblk_ref.shape[0]):
            @pl.when(zblk_ref[k] >= 0)
            def _():
                start = pl.multiple_of(zblk_ref[k] * zero_ref.shape[0], zero_ref.shape[0])
                cp = pltpu.make_async_copy(zero_ref, xbuf_ref.at[pl.ds(start, zero_ref.shape[0])], zsem)
                cp.start()
                cp.wait()

    @pl.when(i >= 2)
    def _():
        _all_tiles_wait(pay, xbuf_ref, sems.at[slot])

    hv = h_ref[...]
    for j in range(_HALF // LANES):
        lo, hi = hv[:, j * LANES:(j + 1) * LANES], hv[:, _HALF + j * LANES:_HALF + (j + 1) * LANES]
        _tile_rows(pay, j, tm)[...] = _pack_bf16_pair(lo, hi)
    _tile_rows(pay, _HALF // LANES, tm)[...] = pltpu.bitcast(w_ref[...], jnp.uint32)
    for j in range(_HALF // LANES + 1, SUBLANES):
        _tile_rows(pay, j, tm)[...] = jnp.zeros((tm, LANES), jnp.uint32)

    def issue(t, carry):
        _tile_copy(pay, t, xbuf_ref, dest_ref[0, t], sems.at[slot]).start()
        return carry

    lax.fori_loop(0, tm, issue, 0, unroll=8)

    @pl.when(i == n_steps - 1)
    def _():
        @pl.when(i >= 1)
        def _():
            _all_tiles_wait(pay_ref.at[1 - slot], xbuf_ref, sems.at[1 - slot])

        _all_tiles_wait(pay, xbuf_ref, sems.at[slot])


def _dispatch(h2d, wrow, dest, zero_blocks, n_blocks):
    t, d = h2d.shape
    tm = DISPATCH_TILE
    grid_spec = pltpu.PrefetchScalarGridSpec(
        num_scalar_prefetch=1,
        grid=(t // tm,),
        in_specs=[pl.BlockSpec((1, tm), lambda i, zb: (0, i), memory_space=pltpu.SMEM),
                  pl.BlockSpec((tm, d), lambda i, zb: (i, 0)),
                  pl.BlockSpec((tm, LANES), lambda i, zb: (i, 0))],
        out_specs=pl.BlockSpec(memory_space=pl.ANY),
        scratch_shapes=[pltpu.VMEM((2, tm * SUBLANES, LANES), jnp.uint32),
                        pltpu.VMEM((EXPERT_BLOCK * SUBLANES, LANES), jnp.uint32),
                        pltpu.SemaphoreType.DMA((2,)), pltpu.SemaphoreType.DMA(())],
    )
    return pl.pallas_call(
        _dispatch_kernel,
        grid_spec=grid_spec,
        out_shape=jax.ShapeDtypeStruct((n_blocks * EXPERT_BLOCK * SUBLANES, LANES), jnp.uint32),
        compiler_params=_cparams(("arbitrary",)),
        name="dispatch",
    )(zero_blocks, dest, h2d, wrow)


def _expert_kernel(bg_ref, pres_ref, x_ref, wg_ref, wu_ref, wd_ref, y_ref, xs_ref, ws_ref, ys_ref):
    i = pl.program_id(0)
    eb = xs_ref.shape[0]
    lo, hi = [], []
    for j in range(_HALF // LANES):
        lo_j, hi_j = _unpack_bf16_pair(_tile_rows(x_ref, j, eb)[...])
        lo.append(lo_j)
        hi.append(hi_j)
    xs_ref[...] = jnp.concatenate(lo + hi, axis=-1).astype(BF16)
    ws_ref[...] = pltpu.bitcast(_tile_rows(x_ref, _HALF // LANES, eb)[...], F32)
    ys_ref[...] = jnp.zeros_like(ys_ref)
    halves = [slice(0, eb // 2), slice(eb // 2, eb)]

    for e in range(EXPERTS_PER_GROUP):
        @pl.when(pres_ref[i * EXPERTS_PER_GROUP + e] > 0)
        def _(e=e):
            gu = [(jnp.dot(xs_ref[p, :], wg_ref[e], preferred_element_type=F32),
                   jnp.dot(xs_ref[p, :], wu_ref[e], preferred_element_type=F32)) for p in halves]
            for p, (a, u) in zip(halves, gu):
                we = ws_ref[p, e:e + 1]
                hid = jnp.where(we != 0.0, a * _sigmoid(a) * u * we, 0.0).astype(BF16)
                ys_ref[p, :] += jnp.dot(hid, wd_ref[e], preferred_element_type=F32)

    for j in range(SUBLANES):
        _tile_rows(y_ref, j, eb)[...] = ys_ref[:, j * LANES:(j + 1) * LANES]


def _experts(xbuf, block_g, present, wg, wu, wd):
    eb = EXPERT_BLOCK
    blk = pl.BlockSpec((eb * SUBLANES, LANES), lambda i, bg, pres: (i, 0))
    gw = lambda a: pl.BlockSpec((EXPERTS_PER_GROUP,) + a.shape[1:], lambda i, bg, pres: (bg[i], 0, 0))
    grid_spec = pltpu.PrefetchScalarGridSpec(
        num_scalar_prefetch=2,
        grid=(block_g.shape[0],),
        in_specs=[blk, gw(wg), gw(wu), gw(wd)],
        out_specs=blk,
        scratch_shapes=[pltpu.VMEM((eb, D_MODEL), BF16), pltpu.VMEM((eb, LANES), F32), pltpu.VMEM((eb, D_MODEL), F32)],
    )
    return pl.pallas_call(
        _expert_kernel,
        grid_spec=grid_spec,
        out_shape=jax.ShapeDtypeStruct(xbuf.shape, F32),
        compiler_params=_cparams(("arbitrary",)),
        name="experts",
    )(block_g, present, xbuf, wg, wu, wd)


FINAL_TILE = 512


def _final_kernel(dest_ref, dest_next_ref, h_ref, g_ref, b_ref, ybuf_ref, o_ref, ys_ref, sems):
    i = pl.program_id(0)
    slot = lax.rem(i, 2)
    tm = h_ref.shape[0]

    def gather(d_ref, s):
        def issue(t, carry):
            _tile_copy(ybuf_ref, d_ref[0, t], ys_ref.at[s], t, sems.at[s]).start()
            return carry

        lax.fori_loop(0, tm, issue, 0, unroll=8)

    @pl.when(i == 0)
    def _():
        gather(dest_ref, slot)

    @pl.when(i + 1 < pl.num_programs(0))
    def _():
        gather(dest_next_ref, 1 - slot)

    ys = ys_ref.at[slot]
    pltpu.make_async_copy(ybuf_ref.at[pl.ds(0, ys.shape[0])], ys, sems.at[slot]).wait()
    y = jnp.concatenate([_tile_rows(ys, j, tm)[...] for j in range(SUBLANES)], axis=-1)
    o_ref[...] = _layer_norm(DN_ALPHA * h_ref[...] + y, g_ref[...], b_ref[...])


def _final(h2d, ybuf, dest, ln_g, ln_b):
    t, d = h2d.shape
    tm = FINAL_TILE
    n = t // tm
    row = pl.BlockSpec((tm, d), lambda i: (i, 0))
    vec = pl.BlockSpec((1, d), lambda i: (0, 0))
    return pl.pallas_call(
        _final_kernel,
        grid=(n,),
        in_specs=[pl.BlockSpec((1, tm), lambda i: (0, i), memory_space=pltpu.SMEM),
                  pl.BlockSpec((1, tm), lambda i: (0, jnp.minimum(i + 1, n - 1)), memory_space=pltpu.SMEM),
                  row, vec, vec, pl.BlockSpec(memory_space=pl.ANY)],
        out_specs=row,
        out_shape=jax.ShapeDtypeStruct((t, d), F32),
        scratch_shapes=[pltpu.VMEM((2, tm * SUBLANES, LANES), F32), pltpu.SemaphoreType.DMA((2,))],
        compiler_params=_cparams(("arbitrary",)),
        name="final",
    )(dest, dest, h2d, ln_g, ln_b, ybuf)


def kernel(x, mem, rel_bias, hgrn_lb_logits, w_in, w_mem_kv, hgrn_norm_g, w_branch_a, w_branch_b, w_branch_c, w_out, ln1_g, ln1_b, w_router_group, b_router_group, w_router_expert, b_router_expert, w_exp_gate, w_exp_up, w_exp_down, ln2_g, ln2_b):
    b, s, d = x.shape
    t = b * s
    assert d == D_MODEL and w_in.shape[0] == DEPTH == 1
    assert s % (BAND * DILATED_GROUPS[-1][1]) == 0 and s % HGRN_TILE == 0 and s % MERGE_TILE == 0 and t % EXPERT_BLOCK == 0
    li = 0
    scale = HEAD_DIM ** -0.5
    x2d = x.reshape(t, d)

    w = w_in[li]
    col_scale = np.ones((COLS_A,), np.float32).reshape(N_GROUPS_A, 3, WIDTH_A)
    col_scale[:, 0] = scale
    w_a = (w[:, :COLS_A] * jnp.asarray(col_scale.reshape(-1))).astype(BF16)
    w_b = w[:, COLS_A:COLS_A + COLS_B].astype(BF16)
    w_c = (w[:, COLS_A + COLS_B:COLS_A + COLS_B + COLS_C] * scale).astype(BF16)
    w_g = w[:, COLS_A + COLS_B + COLS_C:].astype(BF16)

    mkv = _matmul(mem.reshape(-1, d), w_mem_kv[li].astype(BF16), BF16, 256, 2 * WIDTH_C, "mem_kv")
    mkv = mkv.reshape(b, -1, 2 * WIDTH_C)

    oa_parts = []
    for gi, (_, dilation) in enumerate(DILATED_GROUPS):
        bias = _band_bias(rel_bias[:, gi * HEADS_A:(gi + 1) * HEADS_A], dilation)
        pg = _proj_classes(x, w_a, gi, 3 * WIDTH_A, dilation, f"proj_a_{gi}")
        oa_parts.append(_band_attention(pg, bias, f"band_attn_{gi}"))

    ob = _hgrn2(x, w_b, hgrn_lb_logits.astype(F32), hgrn_norm_g[li].reshape(1, WIDTH_B))

    wr = jnp.zeros((d, LANES), F32)
    wr = wr.at[:, :N_EXPERT_GROUPS].set(w_router_group[li]).at[:, 8:_ROUTER_ROWS].set(w_router_expert[li])
    br = jnp.zeros((_ROUTER_ROWS,), F32).at[:N_EXPERT_GROUPS].set(b_router_group[li]).at[8:].set(b_router_expert[li])
    br = jnp.broadcast_to(br[:, None], (_ROUTER_ROWS, 128))

    h, code, wrow, counts = _merge(
        x, oa_parts, ob, mkv, w_c, w_g, w_branch_a[li].astype(BF16), w_branch_b[li].astype(BF16),
        w_branch_c[li].astype(BF16), w_out[li].astype(BF16), ln1_g[li].reshape(1, d), ln1_b[li].reshape(1, d), wr, br)
    h2d = h.reshape(t, d)
    eb = EXPERT_BLOCK
    counts = counts[:, 0]
    per_group = EXPERTS_PER_GROUP * EXPERTS_PER_GROUP
    group_counts = jnp.sum(counts.reshape(N_EXPERT_GROUPS, per_group), axis=1)

    blocks_g = (group_counts + eb - 1) // eb
    ends = jnp.cumsum(blocks_g)
    starts = ends - blocks_g
    n_active = ends[-1:].astype(jnp.int32)
    n_blocks = t // eb + N_EXPERT_GROUPS
    blk = jnp.arange(n_blocks, dtype=jnp.int32)
    block_g = jnp.minimum(jnp.sum(ends[None, :] <= blk[:, None], axis=1), N_EXPERT_GROUPS - 1).astype(jnp.int32)
    in_group = counts.reshape(N_EXPERT_GROUPS, per_group)
    class_start = ((starts * eb)[:, None] + jnp.cumsum(in_group, axis=1) - in_group).reshape(-1)
    cls = code // t
    is_c = cls == jnp.arange(N_ROUTE_CLASSES, dtype=jnp.int32)[:, None]
    dest = (code - cls * t + jnp.sum(jnp.where(is_c, class_start[:, None], 0), axis=0, keepdims=True)).astype(jnp.int32)
    trailing = n_active + jnp.arange(N_EXPERT_GROUPS, dtype=jnp.int32)
    zero_blocks = jnp.concatenate([jnp.where(blocks_g > 0, ends - 1, -1),
                                   jnp.where(trailing < n_blocks, trailing, -1)]).astype(jnp.int32)
    cidx = np.arange(N_ROUTE_CLASSES)
    pair_lo, pair_hi = (cidx // EXPERTS_PER_GROUP) % EXPERTS_PER_GROUP, cidx % EXPERTS_PER_GROUP
    member = np.asarray((np.arange(EXPERTS_PER_GROUP)[None, :] == pair_lo[:, None])
                        | (np.arange(EXPERTS_PER_GROUP)[None, :] == pair_hi[:, None]), np.int32)
    overlap = ((class_start[None, :] < (blk[:, None] + 1) * eb) & (class_start[None, :] + counts[None, :] > blk[:, None] * eb)
               & (counts[None, :] > 0))
    present = (jnp.sum(overlap[:, :, None].astype(jnp.int32) * jnp.asarray(member)[None], axis=1) > 0).astype(jnp.int32)

    xbuf = _dispatch(h2d, wrow, dest, zero_blocks, n_blocks)
    ybuf = _experts(xbuf, block_g, present.reshape(-1),
                    w_exp_gate[li].astype(BF16), w_exp_up[li].astype(BF16), w_exp_down[li].astype(BF16))
    out = _final(h2d, ybuf, dest, ln2_g[li].reshape(1, d), ln2_b[li].reshape(1, d))
    return out.reshape(b, s, d)
```

```python
import functools
import math

import numpy as np
import jax
import jax.numpy as jnp
from jax import lax
from jax.experimental import pallas as pl
from jax.experimental.pallas import tpu as pltpu

F32 = jnp.float32
BF16 = jnp.bfloat16

D_MODEL = 1024
DEPTH = 1
DILATED_GROUPS = ((128, 1), (512, 4), (2048, 16))
N_GROUPS_A = len(DILATED_GROUPS)
HEADS_A = 4
HEAD_DIM = 128
WIDTH_A = HEADS_A * HEAD_DIM
BAND = 128
NUM_BUCKETS = 32
MAX_DISTANCE = 2048
N_HEADS_B = 8
WIDTH_B = N_HEADS_B * HEAD_DIM
CHUNK_B = 64
N_HEADS_C = 4
WIDTH_C = N_HEADS_C * HEAD_DIM
COLS_A = N_GROUPS_A * 3 * WIDTH_A
COLS_B = 4 * WIDTH_B
COLS_C = WIDTH_C
COLS_GATE = 3 * D_MODEL
N_EXPERT_GROUPS = 4
EXPERTS_PER_GROUP = 8
N_EXPERTS = N_EXPERT_GROUPS * EXPERTS_PER_GROUP
TOP_K = 2
D_EXPERT = D_MODEL // 4
DN_ALPHA = (2 * DEPTH) ** 0.25
LN_EPS = 1e-5
RMS_EPS = 1e-6

NEG_BIG = -1e30
SAFE_LOG_DECAY = 80.0
VMEM_LIMIT = 56 * 1024 * 1024
EXPERT_BLOCK = 512

_NT = (((1,), (1,)), ((), ()))
_TN = (((0,), (0,)), ((), ()))


def _cparams(sem):
    return pltpu.CompilerParams(dimension_semantics=sem, vmem_limit_bytes=VMEM_LIMIT)


def _matmul_kernel(x_ref, w_ref, o_ref):
    o_ref[...] = jnp.dot(x_ref[...].astype(BF16), w_ref[...], preferred_element_type=F32).astype(o_ref.dtype)


def _matmul(x, w, out_dtype, tm, tn, name):
    t, k = x.shape
    n = w.shape[1]
    return pl.pallas_call(
        _matmul_kernel,
        grid=(n // tn, t // tm),
        in_specs=[pl.BlockSpec((tm, k), lambda j, i: (i, 0)), pl.BlockSpec((k, tn), lambda j, i: (0, j))],
        out_specs=pl.BlockSpec((tm, tn), lambda j, i: (i, j)),
        out_shape=jax.ShapeDtypeStruct((t, n), out_dtype),
        compiler_params=_cparams(("arbitrary", "arbitrary")),
        name=name,
    )(x, w)


def _t5_bucket_np(dist):
    dist = np.asarray(dist, np.int32)
    max_exact = NUM_BUCKETS // 2
    d = np.maximum(dist, 1).astype(np.float32)
    large = max_exact + (np.log(d / max_exact) / math.log(MAX_DISTANCE / max_exact) * (NUM_BUCKETS - max_exact)).astype(np.int32)
    large = np.minimum(large, NUM_BUCKETS - 1)
    return np.where(dist < max_exact, dist, large).astype(np.int32)


def _band_bias(bias_tab, dilation):
    i = np.arange(BAND)[:, None]
    j = np.arange(2 * BAND)[None, :]
    u = i + BAND - j
    in_band = (u >= 0) & (u <= BAND)
    bucket = _t5_bucket_np(np.clip(u, 0, BAND) * dilation)
    onehot = np.asarray(bucket[:, :, None] == np.arange(NUM_BUCKETS)[None, None, :], np.float32)
    bias = jnp.einsum('pqb,bh->hpq', jnp.asarray(onehot), bias_tab.astype(F32), precision=lax.Precision.HIGHEST)
    general = jnp.where(in_band[None], bias, NEG_BIG)
    first = jnp.where((in_band & (j >= BAND))[None], bias, NEG_BIG)
    return jnp.stack([first, general])


LANES = 128


def _proj_perm_kernel(*refs, r):
    n_slab = D_MODEL // LANES
    x_refs, (w_ref, o_ref, xp_ref) = refs[:n_slab], refs[n_slab:]
    tm = xp_ref.shape[0]
    n = tm // r
    for j, x_ref in enumerate(x_refs):
        for c in range(r):
            rows = x_ref[pl.ds(c, n, stride=r), :] if r > 1 else x_ref[...]
            xp_ref[c * n:(c + 1) * n, j * LANES:(j + 1) * LANES] = rows.astype(BF16)
    res = jnp.dot(xp_ref[...], w_ref[...], preferred_element_type=F32).astype(o_ref.dtype)
    for c in range(r):
        o_ref[c] = res[c * n:(c + 1) * n]


PROJ_A_TILE = 2048


def _proj_classes(x, w, col_block, n, r, name):
    b, s, d = x.shape
    tm = PROJ_A_TILE
    n_slab = d // LANES
    slabs = [pl.BlockSpec((None, tm, LANES), lambda bi, t, j=j: (bi, t, j)) for j in range(n_slab)]
    return pl.pallas_call(
        functools.partial(_proj_perm_kernel, r=r),
        grid=(b, s // tm),
        in_specs=slabs + [pl.BlockSpec((d, n), lambda bi, t: (0, col_block))],
        out_specs=pl.BlockSpec((None, r, tm // r, n), lambda bi, t: (bi, 0, t, 0)),
        out_shape=jax.ShapeDtypeStruct((b, r, s // r, n), BF16),
        scratch_shapes=[pltpu.VMEM((tm, d), BF16)],
        compiler_params=_cparams(("arbitrary", "arbitrary")),
        name=name,
    )(*([x] * n_slab), w)


BAND_SPAN = 2048
_HI_MASK = 0xFFFF0000
_LSE_LANES = LANES // HEADS_A


def _pack_bf16_pair(a, b):
    abits = pltpu.bitcast(a.astype(BF16).astype(F32), jnp.uint32)
    bbits = pltpu.bitcast(b.astype(BF16).astype(F32), jnp.uint32)
    return (abits >> 16) | (bbits & jnp.uint32(_HI_MASK))


def _unpack_bf16_pair(word):
    return pltpu.bitcast(word << 16, F32), pltpu.bitcast(word & jnp.uint32(_HI_MASK), F32)


def _band_attn_kernel(q_ref, k_ref, v_ref, kp_ref, vp_ref, bias_ref, o_ref, lse_ref, *, r):
    nq = q_ref.shape[1] // BAND
    variant = jnp.minimum(pl.program_id(1), 1)
    ones = jnp.ones((BAND, HEAD_DIM), BF16)
    lane = lax.broadcasted_iota(jnp.int32, (BAND, LANES), 1)
    heads = [slice(h * HEAD_DIM, (h + 1) * HEAD_DIM) for h in range(HEADS_A)]
    units = [(c, j) for c in range(r) for j in range(nq)]

    def prev_of(k_or_v_ref, prev_ref, c, j, sl):
        return prev_ref[c, :, sl] if j == 0 else k_or_v_ref[c, (j - 1) * BAND:j * BAND, sl]

    def scores(c, j):
        cur = slice(j * BAND, (j + 1) * BAND)
        out = []
        for h, sl in enumerate(heads):
            q = q_ref[c, cur, sl]
            bias = bias_ref[variant, h] if j == 0 else bias_ref[1, h]
            sp = lax.dot_general(q, prev_of(k_ref, kp_ref, c, j, sl), _NT, preferred_element_type=F32)
            sc = lax.dot_general(q, k_ref[c, cur, sl], _NT, preferred_element_type=F32)
            out.append((sp + bias[:, :BAND], sc + bias[:, BAND:]))
        return out

    nxt = scores(*units[0])
    for u, (c, j) in enumerate(units):
        s_all = nxt
        if u + 1 < len(units):
            nxt = scores(*units[u + 1])
        cur = slice(j * BAND, (j + 1) * BAND)
        rows = pl.ds(j * BAND * r + c, BAND, stride=r) if r > 1 else cur
        probs = []
        for sp, sc in s_all:
            m = jnp.max(jnp.maximum(sp, sc), axis=-1, keepdims=True)
            probs.append((m, jnp.exp(sp - m).astype(BF16), jnp.exp(sc - m).astype(BF16)))
        outs, lses = [], []
        for h, sl in enumerate(heads):
            m, pp, pc = probs[h]
            l = jnp.dot(pp, ones, preferred_element_type=F32) + jnp.dot(pc, ones, preferred_element_type=F32)
            o = jnp.dot(pp, prev_of(v_ref, vp_ref, c, j, sl), preferred_element_type=F32)
            o = o + jnp.dot(pc, v_ref[c, cur, sl], preferred_element_type=F32)
            outs.append(o / l)
            lses.append(m + jnp.log(l))
        for p in range(HEADS_A // 2):
            o_ref[p, rows, :] = _pack_bf16_pair(outs[2 * p], outs[2 * p + 1])
        lse = lses[-1]
        for h in range(HEADS_A - 2, -1, -1):
            lse = jnp.where(lane < (h + 1) * _LSE_LANES, lses[h], lse)
        lse_ref[rows, :] = lse


def _band_attention(pg, bias, name):
    b, r, l, _ = pg.shape
    nq = BAND_SPAN // (BAND * r)
    cur = lambda which: pl.BlockSpec((None, r, nq * BAND, WIDTH_A), lambda bi, n: (bi, 0, n, which))
    prev = lambda which: pl.BlockSpec((None, r, BAND, WIDTH_A), lambda bi, n: (bi, 0, jnp.maximum(n * nq - 1, 0), which))
    return pl.pallas_call(
        functools.partial(_band_attn_kernel, r=r),
        grid=(b, l * r // BAND_SPAN),
        in_specs=[cur(0), cur(1), cur(2), prev(1), prev(2), pl.BlockSpec(bias.shape, lambda bi, n: (0, 0, 0, 0))],
        out_specs=[pl.BlockSpec((None, HEADS_A // 2, BAND_SPAN, HEAD_DIM), lambda bi, n: (bi, 0, n, 0)),
                   pl.BlockSpec((None, BAND_SPAN, LANES), lambda bi, n: (bi, n, 0))],
        out_shape=[jax.ShapeDtypeStruct((b, HEADS_A // 2, l * r, HEAD_DIM), jnp.uint32),
                   jax.ShapeDtypeStruct((b, l * r, LANES), F32)],
        compiler_params=_cparams(("arbitrary", "arbitrary")),
        name=name,
    )(pg, pg, pg, pg, pg, bias)


HGRN_TILE = 512
_N_CHUNK = HGRN_TILE // CHUNK_B


def _dynamic_row(ref, r, ls):
    blk = ref[pl.ds(pl.multiple_of((r // 8) * 8, 8), 8), ls]
    sub = lax.broadcasted_iota(jnp.int32, blk.shape, 0)
    return jnp.sum(jnp.where(sub == r % 8, blk, 0.0), axis=0, keepdims=True)


def _sigmoid(z):
    return 0.5 * jnp.tanh(0.5 * z) + 0.5


_PROJ_PIECE = 256


def _hgrn_kernel(x_ref, xn_ref, w_ref, lbl_ref, ng_ref, o_ref, pq_ref, pf_ref, pi_ref, pg_ref, xb_ref,
                 st_ref, qs_ref, ks_ref, kh_ref, vs_ref, bs_ref, ebl_ref, a_ref, os_ref, gs_ref, kf_ref, qsil_ref,
                 *, tiles_per_seq):
    i = pl.program_id(0)
    parts = (pq_ref, pf_ref, pi_ref, pg_ref)
    per_part = WIDTH_B // _PROJ_PIECE

    def piece(k):
        cols = slice(k * _PROJ_PIECE, (k + 1) * _PROJ_PIECE)
        local = slice((k % per_part) * _PROJ_PIECE, (k % per_part + 1) * _PROJ_PIECE)
        parts[k // per_part][:, local] = jnp.dot(xb_ref[...], w_ref[:, cols], preferred_element_type=F32)

    @pl.when(i == 0)
    def _():
        xb_ref[...] = x_ref[...].astype(BF16)
        for k in range(COLS_B // _PROJ_PIECE):
            piece(k)

    @pl.when(lax.rem(i, tiles_per_seq) == 0)
    def _():
        st_ref[...] = jnp.zeros_like(st_ref)

    xb_ref[...] = xn_ref[...].astype(BF16)
    pieces = [k for part in (2, 3, 1, 0) for k in range(part * per_part, (part + 1) * per_part)]
    next_piece = lambda: piece(pieces.pop(0)) if pieces else None
    vs_ref[...] = pi_ref[...].astype(BF16)
    gv = pg_ref[...]
    gs_ref[...] = gv * _sigmoid(gv) * ng_ref[...]

    lg = lbl_ref[...]
    e = jnp.exp(lg - jnp.max(lg, axis=0, keepdims=True))
    lb = e[0:1, :] / jnp.sum(e, axis=0, keepdims=True)

    row = lax.broadcasted_iota(jnp.int32, (CHUNK_B, CHUNK_B), 0)
    colm = lax.broadcasted_iota(jnp.int32, (CHUNK_B, CHUNK_B), 1)
    causal = row >= colm
    tri = causal.astype(BF16)

    for _ in range(2):
        next_piece()
    f = lb + (1.0 - lb) * _sigmoid(pf_ref[...])
    logf = jnp.log(f)
    kf = 1.0 - f
    kf_ref[...] = kf
    l_hi = logf.astype(BF16)
    rem = logf - l_hi.astype(F32)
    l_mid = rem.astype(BF16)
    l_lo = (rem - l_mid.astype(F32)).astype(BF16)
    min_b = None
    for c in range(_N_CHUNK):
        rs = slice(c * CHUNK_B, (c + 1) * CHUNK_B)
        b = (jnp.dot(tri, l_hi[rs], preferred_element_type=F32) + jnp.dot(tri, l_mid[rs], preferred_element_type=F32)
             + jnp.dot(tri, l_lo[rs], preferred_element_type=F32))
        bs_ref[rs, :] = b
        cm = jnp.min(b[CHUNK_B - 1:CHUNK_B, :])
        min_b = cm if min_b is None else jnp.minimum(min_b, cm)
        if c % 2 == 1:
            next_piece()
    safe = min_b > -SAFE_LOG_DECAY

    qv = pq_ref[...]
    qsil_ref[...] = qv * _sigmoid(qv)
    for c in range(_N_CHUNK):
        rs = slice(c * CHUNK_B, (c + 1) * CHUNK_B)
        b = bs_ref[rs, :]
        eb_last = jnp.exp(b[CHUNK_B - 1:CHUNK_B, :])
        ebl_ref[c:c + 1, :] = eb_last
        qs_ref[rs, :] = (qsil_ref[rs, :] * jnp.exp(b)).astype(BF16)
        k_grown = kf_ref[rs, :] * jnp.exp(-b)
        ks_ref[rs, :] = k_grown.astype(BF16)
        kh_ref[rs, :] = (k_grown * eb_last).astype(BF16)
        next_piece()

    @pl.when(safe)
    def _():
        for c in range(_N_CHUNK):
            rs = slice(c * CHUNK_B, (c + 1) * CHUNK_B)
            for h in range(N_HEADS_B):
                ls = slice(h * HEAD_DIM, (h + 1) * HEAD_DIM)
                a = lax.dot_general(qs_ref[rs, ls], ks_ref[rs, ls], _NT, preferred_element_type=F32)
                a_ref[c * N_HEADS_B + h] = jnp.where(causal, a, 0.0)

    @pl.when(jnp.logical_not(safe))
    def _():
        lane = lax.broadcasted_iota(jnp.int32, (1, CHUNK_B), 1)
        trow = lax.broadcasted_iota(jnp.int32, (CHUNK_B, 1), 0)
        for c in range(_N_CHUNK):
            rs = slice(c * CHUNK_B, (c + 1) * CHUNK_B)
            b = bs_ref[rs, :]
            kh_ref[rs, :] = (kf_ref[rs, :] * jnp.exp(b[CHUNK_B - 1:CHUNK_B, :] - b)).astype(BF16)
            for h in range(N_HEADS_B):
                ls = slice(h * HEAD_DIM, (h + 1) * HEAD_DIM)
                bq = bs_ref[rs, ls]
                qh = qsil_ref[rs, ls]

                def body(s, acc, c=c, ls=ls, bq=bq, qh=qh):
                    b_s = _dynamic_row(bs_ref, c * CHUNK_B + s, ls)
                    k_s = _dynamic_row(kf_ref, c * CHUNK_B + s, ls)
                    w = jnp.exp(jnp.minimum(bq - b_s, 0.0))
                    colv = jnp.sum(qh * k_s * w, axis=-1, keepdims=True)
                    colv = jnp.where(trow >= s, colv, 0.0)
                    return acc + colv * (lane == s).astype(F32)

                a_ref[c * N_HEADS_B + h] = lax.fori_loop(0, CHUNK_B, body, jnp.zeros((CHUNK_B, CHUNK_B), F32))

    for c in range(_N_CHUNK):
        rs = slice(c * CHUNK_B, (c + 1) * CHUNK_B)
        for h in range(N_HEADS_B):
            ls = slice(h * HEAD_DIM, (h + 1) * HEAD_DIM)
            st = st_ref[h]
            vh = vs_ref[rs, ls]
            o = lax.dot_general(qs_ref[rs, ls], st.astype(BF16), _NT, preferred_element_type=F32)
            o = o + jnp.dot(a_ref[c * N_HEADS_B + h].astype(BF16), vh, preferred_element_type=F32)
            os_ref[rs, ls] = o
            st_ref[h] = st * ebl_ref[c:c + 1, ls] + lax.dot_general(vh, kh_ref[rs, ls], _TN, preferred_element_type=F32)
            if h == N_HEADS_B - 1:
                next_piece()
    while pieces:
        next_piece()

    for h in range(N_HEADS_B):
        ls = slice(h * HEAD_DIM, (h + 1) * HEAD_DIM)
        o = os_ref[:, ls]
        o = o * lax.rsqrt(jnp.mean(o * o, axis=-1, keepdims=True) + RMS_EPS)
        o_ref[:, ls] = (o * gs_ref[:, ls]).astype(o_ref.dtype)


def _hgrn2(x, w_b, lb_logits, norm_g):
    b, s, d = x.shape
    tb = HGRN_TILE
    n = b * s // tb
    n_slots = lb_logits.shape[0]
    out = pl.pallas_call(
        functools.partial(_hgrn_kernel, tiles_per_seq=s // tb),
        grid=(n,),
        in_specs=[pl.BlockSpec((tb, d), lambda i: (i, 0)),
                  pl.BlockSpec((tb, d), lambda i: (jnp.minimum(i + 1, n - 1), 0)),
                  pl.BlockSpec(w_b.shape, lambda i: (0, 0), pipeline_mode=pl.Buffered(1)),
                  pl.BlockSpec((n_slots, WIDTH_B), lambda i: (0, 0)),
                  pl.BlockSpec((1, WIDTH_B), lambda i: (0, 0))],
        out_specs=pl.BlockSpec((tb, WIDTH_B), lambda i: (i, 0)),
        out_shape=jax.ShapeDtypeStruct((b * s, WIDTH_B), BF16),
        scratch_shapes=[
            pltpu.VMEM((tb, WIDTH_B), F32),
            pltpu.VMEM((tb, WIDTH_B), F32),
            pltpu.VMEM((tb, WIDTH_B), F32),
            pltpu.VMEM((tb, WIDTH_B), F32),
            pltpu.VMEM((tb, d), BF16),
            pltpu.VMEM((N_HEADS_B, HEAD_DIM, HEAD_DIM), F32),
            pltpu.VMEM((tb, WIDTH_B), BF16),
            pltpu.VMEM((tb, WIDTH_B), BF16),
            pltpu.VMEM((tb, WIDTH_B), BF16),
            pltpu.VMEM((tb, WIDTH_B), BF16),
            pltpu.VMEM((tb, WIDTH_B), F32),
            pltpu.VMEM((8, WIDTH_B), F32),
            pltpu.VMEM((_N_CHUNK * N_HEADS_B, CHUNK_B, CHUNK_B), F32),
            pltpu.VMEM((tb, WIDTH_B), F32),
            pltpu.VMEM((tb, WIDTH_B), F32),
            pltpu.VMEM((tb, WIDTH_B), F32),
            pltpu.VMEM((tb, WIDTH_B), F32),
        ],
        compiler_params=_cparams(("arbitrary",)),
        name="hgrn2",
    )(x.reshape(b * s, d), x.reshape(b * s, d), w_b, lb_logits, norm_g)
    return out.reshape(b, s, WIDTH_B)


MERGE_TILE = 512


def _layer_norm(y, g, b):
    mu = jnp.mean(y, axis=-1, keepdims=True)
    d = y - mu
    var = jnp.mean(d * d, axis=-1, keepdims=True)
    return d * lax.rsqrt(var + LN_EPS) * g + b


def _merge_kernel(x_ref, o1_ref, l1_ref, o2_ref, l2_ref, o3_ref, l3_ref, ob_ref, mk_ref, mv_ref,
                  wc_ref, wg_ref, wa_ref, wb_ref, wcc_ref, wo_ref, g_ref, b_ref, wr_ref, br_ref,
                  h_ref, code_ref, wrow_ref, cnt_ref, oc_ref, cnt_sc, *, n_tokens):
    x = x_ref[...]
    xb = x.astype(BF16)

    qc = jnp.dot(xb, wc_ref[...], preferred_element_type=F32).astype(BF16)
    heads_c = [slice(h * HEAD_DIM, (h + 1) * HEAD_DIM) for h in range(N_HEADS_C)]
    s_all = [lax.dot_general(qc[:, ls], mk_ref[:, ls], _NT, preferred_element_type=F32) for ls in heads_c]
    gate_pre = [None] * 3
    gate_pre[0] = jnp.dot(xb, wg_ref[:, :D_MODEL], preferred_element_type=F32)
    p_all = [jnp.exp(s - jnp.max(s, axis=-1, keepdims=True)).astype(BF16) for s in s_all]
    ones = jnp.ones((mk_ref.shape[0], HEAD_DIM), BF16)
    for ls, p in zip(heads_c, p_all):
        l = jnp.dot(p, ones, preferred_element_type=F32)
        oc_ref[:, ls] = (jnp.dot(p, mv_ref[:, ls], preferred_element_type=F32) / l).astype(BF16)
    gate_pre[1] = jnp.dot(xb, wg_ref[:, D_MODEL:2 * D_MODEL], preferred_element_type=F32)
    branch_b = jnp.dot(ob_ref[...], wb_ref[...], preferred_element_type=F32)
    gate_pre[2] = jnp.dot(xb, wg_ref[:, 2 * D_MODEL:], preferred_element_type=F32)

    lses = [l1_ref[...], l2_ref[...], l3_ref[...]]
    m = functools.reduce(jnp.maximum, lses)
    es = [jnp.exp(l - m) for l in lses]
    inv = 1.0 / functools.reduce(lambda a, c: a + c, es)
    mix = [e * inv for e in es]
    o_pairs = [[_unpack_bf16_pair(o_ref[p]) for p in range(HEADS_A // 2)] for o_ref in (o1_ref, o2_ref, o3_ref)]
    oa_heads = []
    for h in range(HEADS_A):
        acc = None
        for gi in range(N_GROUPS_A):
            w = jnp.broadcast_to(mix[gi][:, h * _LSE_LANES:h * _LSE_LANES + 1], (x.shape[0], HEAD_DIM))
            term = w * o_pairs[gi][h // 2][h % 2]
            acc = term if acc is None else acc + term
        oa_heads.append(acc.astype(BF16))
    oa = jnp.concatenate(oa_heads, axis=-1)

    branch_a = jnp.dot(oa, wa_ref[...], preferred_element_type=F32)
    branch_c = jnp.dot(oc_ref[...], wcc_ref[...], preferred_element_type=F32)
    merged = _sigmoid(gate_pre[1]) * branch_b
    merged = merged + _sigmoid(gate_pre[0]) * branch_a
    merged = merged + _sigmoid(gate_pre[2]) * branch_c
    y = DN_ALPHA * x + jnp.dot(merged.astype(BF16), wo_ref[...], preferred_element_type=F32)
    hv = _layer_norm(y, g_ref[...], b_ref[...])
    h_ref[...] = hv

    first = jnp.logical_and(pl.program_id(0) == 0, pl.program_id(1) == 0)
    _route(hv, wr_ref, br_ref, code_ref, wrow_ref, cnt_ref, cnt_sc, first, n_tokens)


def _merge(x, oa_parts, ob, mkv, wc, wg, wa, wb, wcc, wo, ln_g, ln_b, wr, br):
    b, s, d = x.shape
    tm = MERGE_TILE
    nt = s // tm
    tok = lambda w: pl.BlockSpec((None, tm, w), lambda bi, t: (bi, t, 0))
    full = lambda a: pl.BlockSpec(a.shape, lambda bi, t: (0,) * a.ndim, pipeline_mode=pl.Buffered(1))
    mem = lambda half: pl.BlockSpec((None, mkv.shape[1], WIDTH_C), lambda bi, t: (bi, 0, half))
    o_spec = pl.BlockSpec((None, HEADS_A // 2, tm, HEAD_DIM), lambda bi, t: (bi, 0, t, 0))
    flat = [a for pair in oa_parts for a in pair]
    return pl.pallas_call(
        functools.partial(_merge_kernel, n_tokens=b * s),
        grid=(b, nt),
        in_specs=[tok(d)] + [o_spec, tok(LANES)] * N_GROUPS_A + [tok(WIDTH_B), mem(0), mem(1),
                  full(wc), full(wg), full(wa), full(wb), full(wcc), full(wo), full(ln_g), full(ln_b), full(wr), full(br)],
        out_specs=[tok(d), pl.BlockSpec((1, tm), lambda bi, t: (0, bi * nt + t)),
                   pl.BlockSpec((tm, LANES), lambda bi, t: (bi * nt + t, 0)),
                   pl.BlockSpec((N_ROUTE_CLASSES, LANES), lambda bi, t: (0, 0))],
        out_shape=[jax.ShapeDtypeStruct((b, s, d), F32), jax.ShapeDtypeStruct((1, b * s), jnp.int32),
                   jax.ShapeDtypeStruct((b * s, LANES), F32), jax.ShapeDtypeStruct((N_ROUTE_CLASSES, LANES), jnp.int32)],
        scratch_shapes=[pltpu.VMEM((tm, WIDTH_C), BF16), pltpu.VMEM((N_ROUTE_CLASSES, LANES), F32)],
        compiler_params=_cparams(("arbitrary", "arbitrary")),
        name="merge",
    )(x, *flat, ob, mkv, mkv, wc, wg, wa, wb, wcc, wo, ln_g, ln_b, wr, br)


N_ROUTE_CLASSES = N_EXPERT_GROUPS * EXPERTS_PER_GROUP * EXPERTS_PER_GROUP
_ROUTER_ROWS = 8 + N_EXPERTS


def _route(hv, wr_ref, br_ref, dest_ref, wrow_ref, cnt_ref, cnt_sc, first_step, group_capacity):
    tm = hv.shape[0]

    @pl.when(first_step)
    def _():
        cnt_sc[...] = jnp.zeros_like(cnt_sc)

    h_hi = hv.astype(BF16)
    h_lo = (hv - h_hi.astype(F32)).astype(BF16)
    wv = wr_ref[...]
    w_hi = wv.astype(BF16)
    w_lo = (wv - w_hi.astype(F32)).astype(BF16)
    both = jnp.dot(h_hi, jnp.concatenate([w_hi, w_lo], axis=1), preferred_element_type=F32)
    logits = both[:, :LANES] + both[:, LANES:] + jnp.dot(h_lo, w_hi, preferred_element_type=F32)
    logits = logits.T[:_ROUTER_ROWS, :] + br_ref[:, 0:1]
    g = [logits[i:i + 1, :] for i in range(N_EXPERT_GROUPS)]
    gmax = functools.reduce(jnp.maximum, g)
    gsel = jnp.full_like(gmax, N_EXPERT_GROUPS - 1).astype(jnp.int32)
    for i in range(N_EXPERT_GROUPS - 2, -1, -1):
        gsel = jnp.where(g[i] == gmax, i, gsel)
    gprob = 1.0 / functools.reduce(lambda a, c: a + c, [jnp.exp(gi - gmax) for gi in g])

    esel = logits[8 + (N_EXPERT_GROUPS - 1) * EXPERTS_PER_GROUP:8 + N_EXPERTS, :]
    for i in range(N_EXPERT_GROUPS - 2, -1, -1):
        esel = jnp.where(gsel == i, logits[8 + i * EXPERTS_PER_GROUP:8 + (i + 1) * EXPERTS_PER_GROUP, :], esel)
    ridx = lax.broadcasted_iota(jnp.int32, (EXPERTS_PER_GROUP, tm), 0)
    v1 = jnp.max(esel, axis=0, keepdims=True)
    i1 = jnp.min(jnp.where(esel == v1, ridx, EXPERTS_PER_GROUP), axis=0, keepdims=True)
    rest = jnp.where(ridx == i1, -jnp.inf, esel)
    v2 = jnp.max(rest, axis=0, keepdims=True)
    i2 = jnp.min(jnp.where(rest == v2, ridx, EXPERTS_PER_GROUP), axis=0, keepdims=True)
    t = jnp.exp(v2 - v1)
    w1 = gprob / (1.0 + t)
    w2 = gprob * t / (1.0 + t)
    wt = jnp.where(ridx == i1, w1, jnp.where(ridx == i2, w2, 0.0))
    eye = (lax.broadcasted_iota(jnp.int32, (EXPERTS_PER_GROUP, LANES), 0)
           == lax.broadcasted_iota(jnp.int32, (EXPERTS_PER_GROUP, LANES), 1)).astype(F32)
    wrow_ref[...] = lax.dot_general(wt, eye, _TN, preferred_element_type=F32, precision=lax.Precision.HIGHEST)

    cls = (gsel * EXPERTS_PER_GROUP + jnp.minimum(i1, i2)) * EXPERTS_PER_GROUP + jnp.maximum(i1, i2)
    crow = lax.broadcasted_iota(jnp.int32, (N_ROUTE_CLASSES, tm), 0)
    onehot = (crow == cls).astype(F32)
    before = (lax.broadcasted_iota(jnp.int32, (tm, tm), 0) < lax.broadcasted_iota(jnp.int32, (tm, tm), 1))
    excl = jnp.dot(onehot.astype(BF16), before.astype(BF16), preferred_element_type=F32)
    pos = cnt_sc[:, 0:1] + excl
    rank = jnp.sum(onehot * pos, axis=0, keepdims=True).astype(jnp.int32)
    dest_ref[...] = cls * group_capacity + rank
    cnt_sc[...] = cnt_sc[...] + jnp.sum(onehot, axis=1, keepdims=True)
    cnt_ref[...] = cnt_sc[...].astype(jnp.int32)


DISPATCH_TILE = 1024


SUBLANES = 8
_HALF = D_MODEL // 2


def _tile_copy(src_ref, src_tok, dst_ref, dst_tok, sem):
    src = src_ref.at[pl.ds(pl.multiple_of(src_tok * SUBLANES, SUBLANES), SUBLANES)]
    dst = dst_ref.at[pl.ds(pl.multiple_of(dst_tok * SUBLANES, SUBLANES), SUBLANES)]
    return pltpu.make_async_copy(src, dst, sem)


def _tile_rows(ref, j, n):
    return ref.at[pl.ds(j, n, stride=SUBLANES), :]


def _all_tiles_wait(src_ref, dst_ref, sem):
    pltpu.make_async_copy(src_ref, dst_ref.at[pl.ds(0, src_ref.shape[0])], sem).wait()


def _dispatch_kernel(zblk_ref, dest_ref, h_ref, w_ref, xbuf_ref, pay_ref, zero_ref, sems, zsem):
    i = pl.program_id(0)
    n_steps = pl.num_programs(0)
    slot = lax.rem(i, 2)
    tm = h_ref.shape[0]
    pay = pay_ref.at[slot]

    @pl.when(i == 0)
    def _():
        zero_ref[...] = jnp.zeros_like(zero_ref)
        for k in range(zblk_ref.shape[0]):
            @pl.when(zblk_ref[k] >= 0)
            def _():
                start = pl.multiple_of(zblk_ref[k] * zero_ref.shape[0], zero_ref.shape[0])
                cp = pltpu.make_async_copy(zero_ref, xbuf_ref.at[pl.ds(start, zero_ref.shape[0])], zsem)
                cp.start()
                cp.wait()

    @pl.when(i >= 2)
    def _():
        _all_tiles_wait(pay, xbuf_ref, sems.at[slot])

    hv = h_ref[...]
    for j in range(_HALF // LANES):
        lo, hi = hv[:, j * LANES:(j + 1) * LANES], hv[:, _HALF + j * LANES:_HALF + (j + 1) * LANES]
        _tile_rows(pay, j, tm)[...] = _pack_bf16_pair(lo, hi)
    _tile_rows(pay, _HALF // LANES, tm)[...] = pltpu.bitcast(w_ref[...], jnp.uint32)
    for j in range(_HALF // LANES + 1, SUBLANES):
        _tile_rows(pay, j, tm)[...] = jnp.zeros((tm, LANES), jnp.uint32)

    def issue(t, carry):
        _tile_copy(pay, t, xbuf_ref, dest_ref[0, t], sems.at[slot]).start()
        return carry

    lax.fori_loop(0, tm, issue, 0, unroll=8)

    @pl.when(i == n_steps - 1)
    def _():
        @pl.when(i >= 1)
        def _():
            _all_tiles_wait(pay_ref.at[1 - slot], xbuf_ref, sems.at[1 - slot])

        _all_tiles_wait(pay, xbuf_ref, sems.at[slot])


def _dispatch(h2d, wrow, dest, zero_blocks, n_blocks):
    t, d = h2d.shape
    tm = DISPATCH_TILE
    grid_spec = pltpu.PrefetchScalarGridSpec(
        num_scalar_prefetch=1,
        grid=(t // tm,),
        in_specs=[pl.BlockSpec((1, tm), lambda i, zb: (0, i), memory_space=pltpu.SMEM),
                  pl.BlockSpec((tm, d), lambda i, zb: (i, 0)),
                  pl.BlockSpec((tm, LANES), lambda i, zb: (i, 0))],
        out_specs=pl.BlockSpec(memory_space=pl.ANY),
        scratch_shapes=[pltpu.VMEM((2, tm * SUBLANES, LANES), jnp.uint32),
                        pltpu.VMEM((EXPERT_BLOCK * SUBLANES, LANES), jnp.uint32),
                        pltpu.SemaphoreType.DMA((2,)), pltpu.SemaphoreType.DMA(())],
    )
    return pl.pallas_call(
        _dispatch_kernel,
        grid_spec=grid_spec,
        out_shape=jax.ShapeDtypeStruct((n_blocks * EXPERT_BLOCK * SUBLANES, LANES), jnp.uint32),
        compiler_params=_cparams(("arbitrary",)),
        name="dispatch",
    )(zero_blocks, dest, h2d, wrow)


def _expert_kernel(bg_ref, pres_ref, x_ref, wg_ref, wu_ref, wd_ref, y_ref, xs_ref, ws_ref, ys_ref):
    i = pl.program_id(0)
    eb = xs_ref.shape[0]
    lo, hi = [], []
    for j in range(_HALF // LANES):
        lo_j, hi_j = _unpack_bf16_pair(_tile_rows(x_ref, j, eb)[...])
        lo.append(lo_j)
        hi.append(hi_j)
    xs_ref[...] = jnp.concatenate(lo + hi, axis=-1).astype(BF16)
    ws_ref[...] = pltpu.bitcast(_tile_rows(x_ref, _HALF // LANES, eb)[...], F32)
    ys_ref[...] = jnp.zeros_like(ys_ref)
    halves = [slice(0, eb // 2), slice(eb // 2, eb)]

    for e in range(EXPERTS_PER_GROUP):
        @pl.when(pres_ref[i * EXPERTS_PER_GROUP + e] > 0)
        def _(e=e):
            gu = [(jnp.dot(xs_ref[p, :], wg_ref[e], preferred_element_type=F32),
                   jnp.dot(xs_ref[p, :], wu_ref[e], preferred_element_type=F32)) for p in halves]
            for p, (a, u) in zip(halves, gu):
                we = ws_ref[p, e:e + 1]
                hid = jnp.where(we != 0.0, a * _sigmoid(a) * u * we, 0.0).astype(BF16)
                ys_ref[p, :] += jnp.dot(hid, wd_ref[e], preferred_element_type=F32)

    for j in range(SUBLANES):
        _tile_rows(y_ref, j, eb)[...] = ys_ref[:, j * LANES:(j + 1) * LANES]


def _experts(xbuf, block_g, present, wg, wu, wd):
    eb = EXPERT_BLOCK
    blk = pl.BlockSpec((eb * SUBLANES, LANES), lambda i, bg, pres: (i, 0))
    gw = lambda a: pl.BlockSpec((EXPERTS_PER_GROUP,) + a.shape[1:], lambda i, bg, pres: (bg[i], 0, 0))
    grid_spec = pltpu.PrefetchScalarGridSpec(
        num_scalar_prefetch=2,
        grid=(block_g.shape[0],),
        in_specs=[blk, gw(wg), gw(wu), gw(wd)],
        out_specs=blk,
        scratch_shapes=[pltpu.VMEM((eb, D_MODEL), BF16), pltpu.VMEM((eb, LANES), F32), pltpu.VMEM((eb, D_MODEL), F32)],
    )
    return pl.pallas_call(
        _expert_kernel,
        grid_spec=grid_spec,
        out_shape=jax.ShapeDtypeStruct(xbuf.shape, F32),
        compiler_params=_cparams(("arbitrary",)),
        name="experts",
    )(block_g, present, xbuf, wg, wu, wd)


FINAL_TILE = 512


def _final_kernel(dest_ref, dest_next_ref, h_ref, g_ref, b_ref, ybuf_ref, o_ref, ys_ref, sems):
    i = pl.program_id(0)
    slot = lax.rem(i, 2)
    tm = h_ref.shape[0]

    def gather(d_ref, s):
        def issue(t, carry):
            _tile_copy(ybuf_ref, d_ref[0, t], ys_ref.at[s], t, sems.at[s]).start()
            return carry

        lax.fori_loop(0, tm, issue, 0, unroll=8)

    @pl.when(i == 0)
    def _():
        gather(dest_ref, slot)

    @pl.when(i + 1 < pl.num_programs(0))
    def _():
        gather(dest_next_ref, 1 - slot)

    ys = ys_ref.at[slot]
    pltpu.make_async_copy(ybuf_ref.at[pl.ds(0, ys.shape[0])], ys, sems.at[slot]).wait()
    y = jnp.concatenate([_tile_rows(ys, j, tm)[...] for j in range(SUBLANES)], axis=-1)
    o_ref[...] = _layer_norm(DN_ALPHA * h_ref[...] + y, g_ref[...], b_ref[...])


def _final(h2d, ybuf, dest, ln_g, ln_b):
    t, d = h2d.shape
    tm = FINAL_TILE
    n = t // tm
    row = pl.BlockSpec((tm, d), lambda i: (i, 0))
    vec = pl.BlockSpec((1, d), lambda i: (0, 0))
    return pl.pallas_call(
        _final_kernel,
        grid=(n,),
        in_specs=[pl.BlockSpec((1, tm), lambda i: (0, i), memory_space=pltpu.SMEM),
                  pl.BlockSpec((1, tm), lambda i: (0, jnp.minimum(i + 1, n - 1)), memory_space=pltpu.SMEM),
                  row, vec, vec, pl.BlockSpec(memory_space=pl.ANY)],
        out_specs=row,
        out_shape=jax.ShapeDtypeStruct((t, d), F32),
        scratch_shapes=[pltpu.VMEM((2, tm * SUBLANES, LANES), F32), pltpu.SemaphoreType.DMA((2,))],
        compiler_params=_cparams(("arbitrary",)),
        name="final",
    )(dest, dest, h2d, ln_g, ln_b, ybuf)


def kernel(x, mem, rel_bias, hgrn_lb_logits, w_in, w_mem_kv, hgrn_norm_g, w_branch_a, w_branch_b, w_branch_c, w_out, ln1_g, ln1_b, w_router_group, b_router_group, w_router_expert, b_router_expert, w_exp_gate, w_exp_up, w_exp_down, ln2_g, ln2_b):
    b, s, d = x.shape
    t = b * s
    assert d == D_MODEL and w_in.shape[0] == DEPTH == 1
    assert s % (BAND * DILATED_GROUPS[-1][1]) == 0 and s % HGRN_TILE == 0 and s % MERGE_TILE == 0 and t % EXPERT_BLOCK == 0
    li = 0
    scale = HEAD_DIM ** -0.5
    x2d = x.reshape(t, d)

    w = w_in[li]
    col_scale = np.ones((COLS_A,), np.float32).reshape(N_GROUPS_A, 3, WIDTH_A)
    col_scale[:, 0] = scale
    w_a = (w[:, :COLS_A] * jnp.asarray(col_scale.reshape(-1))).astype(BF16)
    w_b = w[:, COLS_A:COLS_A + COLS_B].astype(BF16)
    w_c = (w[:, COLS_A + COLS_B:COLS_A + COLS_B + COLS_C] * scale).astype(BF16)
    w_g = w[:, COLS_A + COLS_B + COLS_C:].astype(BF16)

    mkv = _matmul(mem.reshape(-1, d), w_mem_kv[li].astype(BF16), BF16, 256, 2 * WIDTH_C, "mem_kv")
    mkv = mkv.reshape(b, -1, 2 * WIDTH_C)

    oa_parts = []
    for gi, (_, dilation) in enumerate(DILATED_GROUPS):
        bias = _band_bias(rel_bias[:, gi * HEADS_A:(gi + 1) * HEADS_A], dilation)
        pg = _proj_classes(x, w_a, gi, 3 * WIDTH_A, dilation, f"proj_a_{gi}")
        oa_parts.append(_band_attention(pg, bias, f"band_attn_{gi}"))

    ob = _hgrn2(x, w_b, hgrn_lb_logits.astype(F32), hgrn_norm_g[li].reshape(1, WIDTH_B))

    wr = jnp.zeros((d, LANES), F32)
    wr = wr.at[:, :N_EXPERT_GROUPS].set(w_router_group[li]).at[:, 8:_ROUTER_ROWS].set(w_router_expert[li])
    br = jnp.zeros((_ROUTER_ROWS,), F32).at[:N_EXPERT_GROUPS].set(b_router_group[li]).at[8:].set(b_router_expert[li])
    br = jnp.broadcast_to(br[:, None], (_ROUTER_ROWS, 128))

    h, code, wrow, counts = _merge(
        x, oa_parts, ob, mkv, w_c, w_g, w_branch_a[li].astype(BF16), w_branch_b[li].astype(BF16),
        w_branch_c[li].astype(BF16), w_out[li].astype(BF16), ln1_g[li].reshape(1, d), ln1_b[li].reshape(1, d), wr, br)
    h2d = h.reshape(t, d)
    eb = EXPERT_BLOCK
    counts = counts[:, 0]
    per_group = EXPERTS_PER_GROUP * EXPERTS_PER_GROUP
    group_counts = jnp.sum(counts.reshape(N_EXPERT_GROUPS, per_group), axis=1)

    blocks_g = (group_counts + eb - 1) // eb
    ends = jnp.cumsum(blocks_g)
    starts = ends - blocks_g
    n_active = ends[-1:].astype(jnp.int32)
    n_blocks = t // eb + N_EXPERT_GROUPS
    blk = jnp.arange(n_blocks, dtype=jnp.int32)
    block_g = jnp.minimum(jnp.sum(ends[None, :] <= blk[:, None], axis=1), N_EXPERT_GROUPS - 1).astype(jnp.int32)
    in_group = counts.reshape(N_EXPERT_GROUPS, per_group)
    class_start = ((starts * eb)[:, None] + jnp.cumsum(in_group, axis=1) - in_group).reshape(-1)
    cls = code // t
    is_c = cls == jnp.arange(N_ROUTE_CLASSES, dtype=jnp.int32)[:, None]
    dest = (code - cls * t + jnp.sum(jnp.where(is_c, class_start[:, None], 0), axis=0, keepdims=True)).astype(jnp.int32)
    trailing = n_active + jnp.arange(N_EXPERT_GROUPS, dtype=jnp.int32)
    zero_blocks = jnp.concatenate([jnp.where(blocks_g > 0, ends - 1, -1),
                                   jnp.where(trailing < n_blocks, trailing, -1)]).astype(jnp.int32)
    cidx = np.arange(N_ROUTE_CLASSES)
    pair_lo, pair_hi = (cidx // EXPERTS_PER_GROUP) % EXPERTS_PER_GROUP, cidx % EXPERTS_PER_GROUP
    member = np.asarray((np.arange(EXPERTS_PER_GROUP)[None, :] == pair_lo[:, None])
                        | (np.arange(EXPERTS_PER_GROUP)[None, :] == pair_hi[:, None]), np.int32)
    overlap = ((class_start[None, :] < (blk[:, None] + 1) * eb) & (class_start[None, :] + counts[None, :] > blk[:, None] * eb)
               & (counts[None, :] > 0))
    present = (jnp.sum(overlap[:, :, None].astype(jnp.int32) * jnp.asarray(member)[None], axis=1) > 0).astype(jnp.int32)

    xbuf = _dispatch(h2d, wrow, dest, zero_blocks, n_blocks)
    ybuf = _experts(xbuf, block_g, present.reshape(-1),
                    w_exp_gate[li].astype(BF16), w_exp_up[li].astype(BF16), w_exp_down[li].astype(BF16))
    out = _final(h2d, ybuf, dest, ln2_g[li].reshape(1, d), ln2_b[li].reshape(1, d))
    return out.reshape(b, s, d)
```

```python
import functools
import math

import numpy as np
import jax
import jax.numpy as jnp
from jax import lax
from jax.experimental import pallas as pl
from jax.experimental.pallas import tpu as pltpu

F32 = jnp.float32
BF16 = jnp.bfloat16

D_MODEL = 1024
DEPTH = 1
DILATED_GROUPS = ((128, 1), (512, 4), (2048, 16))
N_GROUPS_A = len(DILATED_GROUPS)
HEADS_A = 4
HEAD_DIM = 128
WIDTH_A = HEADS_A * HEAD_DIM
BAND = 128
NUM_BUCKETS = 32
MAX_DISTANCE = 2048
N_HEADS_B = 8
WIDTH_B = N_HEADS_B * HEAD_DIM
CHUNK_B = 64
N_HEADS_C = 4
WIDTH_C = N_HEADS_C * HEAD_DIM
COLS_A = N_GROUPS_A * 3 * WIDTH_A
COLS_B = 4 * WIDTH_B
COLS_C = WIDTH_C
COLS_GATE = 3 * D_MODEL
N_EXPERT_GROUPS = 4
EXPERTS_PER_GROUP = 8
N_EXPERTS = N_EXPERT_GROUPS * EXPERTS_PER_GROUP
DN_ALPHA = (2 * DEPTH) ** 0.25
LN_EPS = 1e-5
RMS_EPS = 1e-6

NEG_BIG = -1e30
SAFE_LOG_DECAY = 80.0
VMEM_LIMIT = 56 * 1024 * 1024
EXPERT_BLOCK = 512

_NT = (((1,), (1,)), ((), ()))
_TN = (((0,), (0,)), ((), ()))


def _cparams(sem):
    return pltpu.CompilerParams(dimension_semantics=sem, vmem_limit_bytes=VMEM_LIMIT)


def _matmul_kernel(x_ref, w_ref, o_ref):
    o_ref[...] = jnp.dot(x_ref[...].astype(BF16), w_ref[...], preferred_element_type=F32).astype(o_ref.dtype)


def _matmul(x, w, out_dtype, tm, tn, name):
    t, k = x.shape
    n = w.shape[1]
    return pl.pallas_call(
        _matmul_kernel,
        grid=(n // tn, t // tm),
        in_specs=[pl.BlockSpec((tm, k), lambda j, i: (i, 0)), pl.BlockSpec((k, tn), lambda j, i: (0, j))],
        out_specs=pl.BlockSpec((tm, tn), lambda j, i: (i, j)),
        out_shape=jax.ShapeDtypeStruct((t, n), out_dtype),
        compiler_params=_cparams(("arbitrary", "arbitrary")),
        name=name,
    )(x, w)


def _t5_bucket_np(dist):
    dist = np.asarray(dist, np.int32)
    max_exact = NUM_BUCKETS // 2
    d = np.maximum(dist, 1).astype(np.float32)
    large = max_exact + (np.log(d / max_exact) / math.log(MAX_DISTANCE / max_exact) * (NUM_BUCKETS - max_exact)).astype(np.int32)
    large = np.minimum(large, NUM_BUCKETS - 1)
    return np.where(dist < max_exact, dist, large).astype(np.int32)


def _band_bias(bias_tab, dilation):
    i = np.arange(BAND)[:, None]
    j = np.arange(2 * BAND)[None, :]
    u = i + BAND - j
    in_band = (u >= 0) & (u <= BAND)
    bucket = _t5_bucket_np(np.clip(u, 0, BAND) * dilation)
    onehot = np.asarray(bucket[:, :, None] == np.arange(NUM_BUCKETS)[None, None, :], np.float32)
    bias = jnp.einsum('pqb,bh->hpq', jnp.asarray(onehot), bias_tab.astype(F32), precision=lax.Precision.HIGHEST)
    general = jnp.where(in_band[None], bias, NEG_BIG)
    first = jnp.where((in_band & (j >= BAND))[None], bias, NEG_BIG)
    return jnp.stack([first, general])


LANES = 128


def _proj_perm_kernel(*refs, r):
    n_slab = D_MODEL // LANES
    x_refs, (w_ref, o_ref, xp_ref) = refs[:n_slab], refs[n_slab:]
    tm = xp_ref.shape[0]
    n = tm // r
    for j, x_ref in enumerate(x_refs):
        for c in range(r):
            rows = x_ref[pl.ds(c, n, stride=r), :] if r > 1 else x_ref[...]
            xp_ref[c * n:(c + 1) * n, j * LANES:(j + 1) * LANES] = rows.astype(BF16)
    res = jnp.dot(xp_ref[...], w_ref[...], preferred_element_type=F32).astype(o_ref.dtype)
    for c in range(r):
        o_ref[c] = res[c * n:(c + 1) * n]


PROJ_A_TILE = 1024


def _proj_classes(x, w, col_block, n, r, name):
    b, s, d = x.shape
    tm = PROJ_A_TILE
    n_slab = d // LANES
    slabs = [pl.BlockSpec((None, tm, LANES), lambda bi, t, j=j: (bi, t, j)) for j in range(n_slab)]
    return pl.pallas_call(
        functools.partial(_proj_perm_kernel, r=r),
        grid=(b, s // tm),
        in_specs=slabs + [pl.BlockSpec((d, n), lambda bi, t: (0, col_block))],
        out_specs=pl.BlockSpec((None, r, tm // r, n), lambda bi, t: (bi, 0, t, 0)),
        out_shape=jax.ShapeDtypeStruct((b, r, s // r, n), BF16),
        scratch_shapes=[pltpu.VMEM((tm, d), BF16)],
        compiler_params=_cparams(("arbitrary", "arbitrary")),
        name=name,
    )(*([x] * n_slab), w)


BAND_SPAN = 2048
_HI_MASK = 0xFFFF0000
_LSE_LANES = LANES // HEADS_A


def _pack_bf16_pair(a, b):
    abits = pltpu.bitcast(a.astype(BF16).astype(F32), jnp.uint32)
    bbits = pltpu.bitcast(b.astype(BF16).astype(F32), jnp.uint32)
    return (abits >> 16) | (bbits & jnp.uint32(_HI_MASK))


def _unpack_bf16_pair(word):
    return pltpu.bitcast(word << 16, F32), pltpu.bitcast(word & jnp.uint32(_HI_MASK), F32)


def _band_attn_kernel(q_ref, k_ref, v_ref, kp_ref, vp_ref, bias_ref, o_ref, lse_ref, *, r):
    nq = q_ref.shape[1] // BAND
    variant = jnp.minimum(pl.program_id(1), 1)
    ones = jnp.ones((BAND, HEAD_DIM), BF16)
    lane = lax.broadcasted_iota(jnp.int32, (BAND, LANES), 1)
    heads = [slice(h * HEAD_DIM, (h + 1) * HEAD_DIM) for h in range(HEADS_A)]
    units = [(c, j) for c in range(r) for j in range(nq)]

    def prev_of(k_or_v_ref, prev_ref, c, j, sl):
        return prev_ref[c, :, sl] if j == 0 else k_or_v_ref[c, (j - 1) * BAND:j * BAND, sl]

    def scores(c, j):
        cur = slice(j * BAND, (j + 1) * BAND)
        out = []
        for h, sl in enumerate(heads):
            q = q_ref[c, cur, sl]
            bias = bias_ref[variant, h] if j == 0 else bias_ref[1, h]
            sp = lax.dot_general(q, prev_of(k_ref, kp_ref, c, j, sl), _NT, preferred_element_type=F32)
            sc = lax.dot_general(q, k_ref[c, cur, sl], _NT, preferred_element_type=F32)
            out.append((sp + bias[:, :BAND], sc + bias[:, BAND:]))
        return out

    nxt = scores(*units[0])
    for u, (c, j) in enumerate(units):
        s_all = nxt
        if u + 1 < len(units):
            nxt = scores(*units[u + 1])
        cur = slice(j * BAND, (j + 1) * BAND)
        rows = pl.ds(j * BAND * r + c, BAND, stride=r) if r > 1 else cur
        probs = []
        for sp, sc in s_all:
            m = jnp.max(jnp.maximum(sp, sc), axis=-1, keepdims=True)
            probs.append((m, jnp.exp(sp - m).astype(BF16), jnp.exp(sc - m).astype(BF16)))
        outs, lses = [], []
        for h, sl in enumerate(heads):
            m, pp, pc = probs[h]
            l = jnp.dot(pp, ones, preferred_element_type=F32) + jnp.dot(pc, ones, preferred_element_type=F32)
            o = jnp.dot(pp, prev_of(v_ref, vp_ref, c, j, sl), preferred_element_type=F32)
            o = o + jnp.dot(pc, v_ref[c, cur, sl], preferred_element_type=F32)
            outs.append(o / l)
            lses.append(m + jnp.log(l))
        for p in range(HEADS_A // 2):
            o_ref[p, rows, :] = _pack_bf16_pair(outs[2 * p], outs[2 * p + 1])
        lse = lses[-1]
        for h in range(HEADS_A - 2, -1, -1):
            lse = jnp.where(lane < (h + 1) * _LSE_LANES, lses[h], lse)
        lse_ref[rows, :] = lse


def _band_attention(pg, bias, name):
    b, r, l, _ = pg.shape
    nq = BAND_SPAN // (BAND * r)
    cur = lambda which: pl.BlockSpec((None, r, nq * BAND, WIDTH_A), lambda bi, n: (bi, 0, n, which))
    prev = lambda which: pl.BlockSpec((None, r, BAND, WIDTH_A), lambda bi, n: (bi, 0, jnp.maximum(n * nq - 1, 0), which))
    return pl.pallas_call(
        functools.partial(_band_attn_kernel, r=r),
        grid=(b, l * r // BAND_SPAN),
        in_specs=[cur(0), cur(1), cur(2), prev(1), prev(2), pl.BlockSpec(bias.shape, lambda bi, n: (0, 0, 0, 0))],
        out_specs=[pl.BlockSpec((None, HEADS_A // 2, BAND_SPAN, HEAD_DIM), lambda bi, n: (bi, 0, n, 0)),
                   pl.BlockSpec((None, BAND_SPAN, LANES), lambda bi, n: (bi, n, 0))],
        out_shape=[jax.ShapeDtypeStruct((b, HEADS_A // 2, l * r, HEAD_DIM), jnp.uint32),
                   jax.ShapeDtypeStruct((b, l * r, LANES), F32)],
        compiler_params=_cparams(("arbitrary", "arbitrary")),
        name=name,
    )(pg, pg, pg, pg, pg, bias)


HGRN_TILE = 512
_N_CHUNK = HGRN_TILE // CHUNK_B


def _dynamic_row(ref, r, ls):
    blk = ref[pl.ds(pl.multiple_of((r // 8) * 8, 8), 8), ls]
    sub = lax.broadcasted_iota(jnp.int32, blk.shape, 0)
    return jnp.sum(jnp.where(sub == r % 8, blk, 0.0), axis=0, keepdims=True)


def _sigmoid(z):
    return 0.5 * jnp.tanh(0.5 * z) + 0.5


_PROJ_PIECE = 256


def _hgrn_kernel(x_ref, xn_ref, w_ref, lbl_ref, ng_ref, o_ref, pq_ref, pf_ref, pi_ref, pg_ref, xb_ref,
                 st_ref, qs_ref, ks_ref, kh_ref, vs_ref, bs_ref, ebl_ref, a_ref, os_ref, gs_ref, kf_ref, qsil_ref,
                 *, tiles_per_seq):
    i = pl.program_id(0)
    parts = (pq_ref, pf_ref, pi_ref, pg_ref)
    per_part = WIDTH_B // _PROJ_PIECE

    def piece(k):
        cols = slice(k * _PROJ_PIECE, (k + 1) * _PROJ_PIECE)
        local = slice((k % per_part) * _PROJ_PIECE, (k % per_part + 1) * _PROJ_PIECE)
        parts[k // per_part][:, local] = jnp.dot(xb_ref[...], w_ref[:, cols], preferred_element_type=F32)

    @pl.when(i == 0)
    def _():
        xb_ref[...] = x_ref[...].astype(BF16)
        for k in range(COLS_B // _PROJ_PIECE):
            piece(k)

    @pl.when(lax.rem(i, tiles_per_seq) == 0)
    def _():
        st_ref[...] = jnp.zeros_like(st_ref)

    xb_ref[...] = xn_ref[...].astype(BF16)
    pieces = [k for part in (2, 3, 1, 0) for k in range(part * per_part, (part + 1) * per_part)]
    next_piece = lambda: piece(pieces.pop(0)) if pieces else None
    vs_ref[...] = pi_ref[...].astype(BF16)
    gv = pg_ref[...]
    gs_ref[...] = gv * _sigmoid(gv) * ng_ref[...]

    lg = lbl_ref[...]
    e = jnp.exp(lg - jnp.max(lg, axis=0, keepdims=True))
    lb = e[0:1, :] / jnp.sum(e, axis=0, keepdims=True)

    row = lax.broadcasted_iota(jnp.int32, (CHUNK_B, CHUNK_B), 0)
    colm = lax.broadcasted_iota(jnp.int32, (CHUNK_B, CHUNK_B), 1)
    causal = row >= colm
    tri = causal.astype(BF16)

    for _ in range(2):
        next_piece()
    f = lb + (1.0 - lb) * _sigmoid(pf_ref[...])
    logf = jnp.log(f)
    kf = 1.0 - f
    kf_ref[...] = kf
    l_hi = logf.astype(BF16)
    rem = logf - l_hi.astype(F32)
    l_mid = rem.astype(BF16)
    l_lo = (rem - l_mid.astype(F32)).astype(BF16)
    min_b = None
    for c in range(_N_CHUNK):
        rs = slice(c * CHUNK_B, (c + 1) * CHUNK_B)
        b = (jnp.dot(tri, l_hi[rs], preferred_element_type=F32) + jnp.dot(tri, l_mid[rs], preferred_element_type=F32)
             + jnp.dot(tri, l_lo[rs], preferred_element_type=F32))
        bs_ref[rs, :] = b
        cm = jnp.min(b[CHUNK_B - 1:CHUNK_B, :])
        min_b = cm if min_b is None else jnp.minimum(min_b, cm)
        if c % 2 == 1:
            next_piece()
    safe = min_b > -SAFE_LOG_DECAY

    qv = pq_ref[...]
    qsil_ref[...] = qv * _sigmoid(qv)
    for c in range(_N_CHUNK):
        rs = slice(c * CHUNK_B, (c + 1) * CHUNK_B)
        b = bs_ref[rs, :]
        eb_last = jnp.exp(b[CHUNK_B - 1:CHUNK_B, :])
        ebl_ref[c:c + 1, :] = eb_last
        qs_ref[rs, :] = (qsil_ref[rs, :] * jnp.exp(b)).astype(BF16)
        k_grown = kf_ref[rs, :] * jnp.exp(-b)
        ks_ref[rs, :] = k_grown.astype(BF16)
        kh_ref[rs, :] = (k_grown * eb_last).astype(BF16)
        next_piece()

    @pl.when(safe)
    def _():
        for c in range(_N_CHUNK):
            rs = slice(c * CHUNK_B, (c + 1) * CHUNK_B)
            for h in range(N_HEADS_B):
                ls = slice(h * HEAD_DIM, (h + 1) * HEAD_DIM)
                a = lax.dot_general(qs_ref[rs, ls], ks_ref[rs, ls], _NT, preferred_element_type=F32)
                a_ref[c * N_HEADS_B + h] = jnp.where(causal, a, 0.0)

    @pl.when(jnp.logical_not(safe))
    def _():
        lane = lax.broadcasted_iota(jnp.int32, (1, CHUNK_B), 1)
        trow = lax.broadcasted_iota(jnp.int32, (CHUNK_B, 1), 0)
        for c in range(_N_CHUNK):
            rs = slice(c * CHUNK_B, (c + 1) * CHUNK_B)
            b = bs_ref[rs, :]
            kh_ref[rs, :] = (kf_ref[rs, :] * jnp.exp(b[CHUNK_B - 1:CHUNK_B, :] - b)).astype(BF16)
            for h in range(N_HEADS_B):
                ls = slice(h * HEAD_DIM, (h + 1) * HEAD_DIM)
                bq = bs_ref[rs, ls]
                qh = qsil_ref[rs, ls]

                def body(s, acc, c=c, ls=ls, bq=bq, qh=qh):
                    b_s = _dynamic_row(bs_ref, c * CHUNK_B + s, ls)
                    k_s = _dynamic_row(kf_ref, c * CHUNK_B + s, ls)
                    w = jnp.exp(jnp.minimum(bq - b_s, 0.0))
                    colv = jnp.sum(qh * k_s * w, axis=-1, keepdims=True)
                    colv = jnp.where(trow >= s, colv, 0.0)
                    return acc + colv * (lane == s).astype(F32)

                a_ref[c * N_HEADS_B + h] = lax.fori_loop(0, CHUNK_B, body, jnp.zeros((CHUNK_B, CHUNK_B), F32))

    for c in range(_N_CHUNK):
        rs = slice(c * CHUNK_B, (c + 1) * CHUNK_B)
        for h in range(N_HEADS_B):
            ls = slice(h * HEAD_DIM, (h + 1) * HEAD_DIM)
            st = st_ref[h]
            vh = vs_ref[rs, ls]
            o = lax.dot_general(qs_ref[rs, ls], st.astype(BF16), _NT, preferred_element_type=F32)
            o = o + jnp.dot(a_ref[c * N_HEADS_B + h].astype(BF16), vh, preferred_element_type=F32)
            os_ref[rs, ls] = o
            st_ref[h] = st * ebl_ref[c:c + 1, ls] + lax.dot_general(vh, kh_ref[rs, ls], _TN, preferred_element_type=F32)
            if h == N_HEADS_B - 1:
                next_piece()
    while pieces:
        next_piece()

    for h in range(N_HEADS_B):
        ls = slice(h * HEAD_DIM, (h + 1) * HEAD_DIM)
        o = os_ref[:, ls]
        o = o * lax.rsqrt(jnp.mean(o * o, axis=-1, keepdims=True) + RMS_EPS)
        o_ref[:, ls] = (o * gs_ref[:, ls]).astype(o_ref.dtype)


def _hgrn2(x, w_b, lb_logits, norm_g):
    b, s, d = x.shape
    tb = HGRN_TILE
    n = b * s // tb
    n_slots = lb_logits.shape[0]
    out = pl.pallas_call(
        functools.partial(_hgrn_kernel, tiles_per_seq=s // tb),
        grid=(n,),
        in_specs=[pl.BlockSpec((tb, d), lambda i: (0, 0)),
                  pl.BlockSpec((tb, d), lambda i: (jnp.minimum(i + 1, n - 1), 0)),
                  pl.BlockSpec(w_b.shape, lambda i: (0, 0), pipeline_mode=pl.Buffered(1)),
                  pl.BlockSpec((n_slots, WIDTH_B), lambda i: (0, 0)),
                  pl.BlockSpec((1, WIDTH_B), lambda i: (0, 0))],
        out_specs=pl.BlockSpec((tb, WIDTH_B), lambda i: (i, 0)),
        out_shape=jax.ShapeDtypeStruct((b * s, WIDTH_B), BF16),
        scratch_shapes=[
            pltpu.VMEM((tb, WIDTH_B), F32),
            pltpu.VMEM((tb, WIDTH_B), F32),
            pltpu.VMEM((tb, WIDTH_B), F32),
            pltpu.VMEM((tb, WIDTH_B), F32),
            pltpu.VMEM((tb, d), BF16),
            pltpu.VMEM((N_HEADS_B, HEAD_DIM, HEAD_DIM), F32),
            pltpu.VMEM((tb, WIDTH_B), BF16),
            pltpu.VMEM((tb, WIDTH_B), BF16),
            pltpu.VMEM((tb, WIDTH_B), BF16),
            pltpu.VMEM((tb, WIDTH_B), BF16),
            pltpu.VMEM((tb, WIDTH_B), F32),
            pltpu.VMEM((8, WIDTH_B), F32),
            pltpu.VMEM((_N_CHUNK * N_HEADS_B, CHUNK_B, CHUNK_B), F32),
            pltpu.VMEM((tb, WIDTH_B), F32),
            pltpu.VMEM((tb, WIDTH_B), F32),
            pltpu.VMEM((tb, WIDTH_B), F32),
            pltpu.VMEM((tb, WIDTH_B), F32),
        ],
        compiler_params=_cparams(("arbitrary",)),
        name="hgrn2",
    )(x.reshape(b * s, d), x.reshape(b * s, d), w_b, lb_logits, norm_g)
    return out.reshape(b, s, WIDTH_B)


MERGE_TILE = 512


def _layer_norm(y, g, b):
    mu = jnp.mean(y, axis=-1, keepdims=True)
    d = y - mu
    var = jnp.mean(d * d, axis=-1, keepdims=True)
    return d * lax.rsqrt(var + LN_EPS) * g + b


def _merge_kernel(x_ref, o1_ref, l1_ref, o2_ref, l2_ref, o3_ref, l3_ref, ob_ref, mk_ref, mv_ref,
                  wc_ref, wg_ref, wa_ref, wb_ref, wcc_ref, wo_ref, g_ref, b_ref, wr_ref, br_ref,
                  h_ref, code_ref, wrow_ref, cnt_ref, oc_ref, cnt_sc, *, n_tokens):
    x = x_ref[...]
    xb = x.astype(BF16)

    qc = jnp.dot(xb, wc_ref[...], preferred_element_type=F32).astype(BF16)
    heads_c = [slice(h * HEAD_DIM, (h + 1) * HEAD_DIM) for h in range(N_HEADS_C)]
    s_all = [lax.dot_general(qc[:, ls], mk_ref[:, ls], _NT, preferred_element_type=F32) for ls in heads_c]
    gate_pre = [None] * 3
    gate_pre[0] = jnp.dot(xb, wg_ref[:, :D_MODEL], preferred_element_type=F32)
    p_all = [jnp.exp(s - jnp.max(s, axis=-1, keepdims=True)).astype(BF16) for s in s_all]
    ones = jnp.ones((mk_ref.shape[0], HEAD_DIM), BF16)
    for ls, p in zip(heads_c, p_all):
        l = jnp.dot(p, ones, preferred_element_type=F32)
        oc_ref[:, ls] = (jnp.dot(p, mv_ref[:, ls], preferred_element_type=F32) / l).astype(BF16)
    gate_pre[1] = jnp.dot(xb, wg_ref[:, D_MODEL:2 * D_MODEL], preferred_element_type=F32)
    branch_b = jnp.dot(ob_ref[...], wb_ref[...], preferred_element_type=F32)
    gate_pre[2] = jnp.dot(xb, wg_ref[:, 2 * D_MODEL:], preferred_element_type=F32)

    lses = [l1_ref[...], l2_ref[...], l3_ref[...]]
    m = functools.reduce(jnp.maximum, lses)
    es = [jnp.exp(l - m) for l in lses]
    inv = 1.0 / functools.reduce(lambda a, c: a + c, es)
    mix = [e * inv for e in es]
    o_pairs = [[_unpack_bf16_pair(o_ref[p]) for p in range(HEADS_A // 2)] for o_ref in (o1_ref, o2_ref, o3_ref)]
    oa_heads = []
    for h in range(HEADS_A):
        acc = None
        for gi in range(N_GROUPS_A):
            w = jnp.broadcast_to(mix[gi][:, h * _LSE_LANES:h * _LSE_LANES + 1], (x.shape[0], HEAD_DIM))
            term = w * o_pairs[gi][h // 2][h % 2]
            acc = term if acc is None else acc + term
        oa_heads.append(acc.astype(BF16))
    oa = jnp.concatenate(oa_heads, axis=-1)

    branch_a = jnp.dot(oa, wa_ref[...], preferred_element_type=F32)
    branch_c = jnp.dot(oc_ref[...], wcc_ref[...], preferred_element_type=F32)
    merged = _sigmoid(gate_pre[1]) * branch_b
    merged = merged + _sigmoid(gate_pre[0]) * branch_a
    merged = merged + _sigmoid(gate_pre[2]) * branch_c
    y = DN_ALPHA * x + jnp.dot(merged.astype(BF16), wo_ref[...], preferred_element_type=F32)
    hv = _layer_norm(y, g_ref[...], b_ref[...])
    h_ref[...] = hv

    first = jnp.logical_and(pl.program_id(0) == 0, pl.program_id(1) == 0)
    _route(hv, wr_ref, br_ref, code_ref, wrow_ref, cnt_ref, cnt_sc, first, n_tokens)


def _merge(x, oa_parts, ob, mkv, wc, wg, wa, wb, wcc, wo, ln_g, ln_b, wr, br):
    b, s, d = x.shape
    tm = MERGE_TILE
    nt = s // tm
    tok = lambda w: pl.BlockSpec((None, tm, w), lambda bi, t: (bi, t, 0))
    full = lambda a: pl.BlockSpec(a.shape, lambda bi, t: (0,) * a.ndim, pipeline_mode=pl.Buffered(1))
    mem = lambda half: pl.BlockSpec((None, mkv.shape[1], WIDTH_C), lambda bi, t: (bi, 0, half))
    o_spec = pl.BlockSpec((None, HEADS_A // 2, tm, HEAD_DIM), lambda bi, t: (bi, 0, t, 0))
    flat = [a for pair in oa_parts for a in pair]
    return pl.pallas_call(
        functools.partial(_merge_kernel, n_tokens=b * s),
        grid=(b, nt),
        in_specs=[tok(d)] + [o_spec, tok(LANES)] * N_GROUPS_A + [tok(WIDTH_B), mem(0), mem(1),
                  full(wc), full(wg), full(wa), full(wb), full(wcc), full(wo), full(ln_g), full(ln_b), full(wr), full(br)],
        out_specs=[tok(d), pl.BlockSpec((1, tm), lambda bi, t: (0, bi * nt + t)),
                   pl.BlockSpec((tm, LANES), lambda bi, t: (bi * nt + t, 0)),
                   pl.BlockSpec((N_ROUTE_CLASSES, LANES), lambda bi, t: (0, 0))],
        out_shape=[jax.ShapeDtypeStruct((b, s, d), F32), jax.ShapeDtypeStruct((1, b * s), jnp.int32),
                   jax.ShapeDtypeStruct((b * s, LANES), F32), jax.ShapeDtypeStruct((N_ROUTE_CLASSES, LANES), jnp.int32)],
        scratch_shapes=[pltpu.VMEM((tm, WIDTH_C), BF16), pltpu.VMEM((N_ROUTE_CLASSES, LANES), F32)],
        compiler_params=_cparams(("arbitrary", "arbitrary")),
        name="merge",
    )(x, *flat, ob, mkv, mkv, wc, wg, wa, wb, wcc, wo, ln_g, ln_b, wr, br)


N_ROUTE_CLASSES = N_EXPERT_GROUPS * EXPERTS_PER_GROUP * EXPERTS_PER_GROUP
_ROUTER_ROWS = 8 + N_EXPERTS


def _route(hv, wr_ref, br_ref, dest_ref, wrow_ref, cnt_ref, cnt_sc, first_step, group_capacity):
    tm = hv.shape[0]

    @pl.when(first_step)
    def _():
        cnt_sc[...] = jnp.zeros_like(cnt_sc)

    h_hi = hv.astype(BF16)
    h_lo = (hv - h_hi.astype(F32)).astype(BF16)
    wv = wr_ref[...]
    w_hi = wv.astype(BF16)
    w_lo = (wv - w_hi.astype(F32)).astype(BF16)
    both = jnp.dot(h_hi, jnp.concatenate([w_hi, w_lo], axis=1), preferred_element_type=F32)
    logits = both[:, :LANES] + both[:, LANES:] + jnp.dot(h_lo, w_hi, preferred_element_type=F32)
    logits = logits.T[:_ROUTER_ROWS, :] + br_ref[:, 0:1]
    g = [logits[i:i + 1, :] for i in range(N_EXPERT_GROUPS)]
    gmax = functools.reduce(jnp.maximum, g)
    gsel = jnp.full_like(gmax, N_EXPERT_GROUPS - 1).astype(jnp.int32)
    for i in range(N_EXPERT_GROUPS - 2, -1, -1):
        gsel = jnp.where(g[i] == gmax, i, gsel)
    gprob = 1.0 / functools.reduce(lambda a, c: a + c, [jnp.exp(gi - gmax) for gi in g])

    esel = logits[8 + (N_EXPERT_GROUPS - 1) * EXPERTS_PER_GROUP:8 + N_EXPERTS, :]
    for i in range(N_EXPERT_GROUPS - 2, -1, -1):
        esel = jnp.where(gsel == i, logits[8 + i * EXPERTS_PER_GROUP:8 + (i + 1) * EXPERTS_PER_GROUP, :], esel)
    ridx = lax.broadcasted_iota(jnp.int32, (EXPERTS_PER_GROUP, tm), 0)
    v1 = jnp.max(esel, axis=0, keepdims=True)
    i1 = jnp.min(jnp.where(esel == v1, ridx, EXPERTS_PER_GROUP), axis=0, keepdims=True)
    rest = jnp.where(ridx == i1, -jnp.inf, esel)
    v2 = jnp.max(rest, axis=0, keepdims=True)
    i2 = jnp.min(jnp.where(rest == v2, ridx, EXPERTS_PER_GROUP), axis=0, keepdims=True)
    t = jnp.exp(v2 - v1)
    w1 = gprob / (1.0 + t)
    w2 = gprob * t / (1.0 + t)
    wt = jnp.where(ridx == i1, w1, jnp.where(ridx == i2, w2, 0.0))
    eye = (lax.broadcasted_iota(jnp.int32, (EXPERTS_PER_GROUP, LANES), 0)
           == lax.broadcasted_iota(jnp.int32, (EXPERTS_PER_GROUP, LANES), 1)).astype(F32)
    wrow_ref[...] = lax.dot_general(wt, eye, _TN, preferred_element_type=F32, precision=lax.Precision.HIGHEST)

    cls = (gsel * EXPERTS_PER_GROUP + jnp.minimum(i1, i2)) * EXPERTS_PER_GROUP + jnp.maximum(i1, i2)
    crow = lax.broadcasted_iota(jnp.int32, (N_ROUTE_CLASSES, tm), 0)
    onehot = (crow == cls).astype(F32)
    before = (lax.broadcasted_iota(jnp.int32, (tm, tm), 0) < lax.broadcasted_iota(jnp.int32, (tm, tm), 1))
    excl = jnp.dot(onehot.astype(BF16), before.astype(BF16), preferred_element_type=F32)
    pos = cnt_sc[:, 0:1] + excl
    rank = jnp.sum(onehot * pos, axis=0, keepdims=True).astype(jnp.int32)
    dest_ref[...] = cls * group_capacity + rank
    cnt_sc[...] = cnt_sc[...] + jnp.sum(onehot, axis=1, keepdims=True)
    cnt_ref[...] = cnt_sc[...].astype(jnp.int32)


DISPATCH_TILE = 1024


SUBLANES = 8
_HALF = D_MODEL // 2


def _tile_copy(src_ref, src_tok, dst_ref, dst_tok, sem):
    src = src_ref.at[pl.ds(pl.multiple_of(src_tok * SUBLANES, SUBLANES), SUBLANES)]
    dst = dst_ref.at[pl.ds(pl.multiple_of(dst_tok * SUBLANES, SUBLANES), SUBLANES)]
    return pltpu.make_async_copy(src, dst, sem)


def _tile_rows(ref, j, n):
    return ref.at[pl.ds(j, n, stride=SUBLANES), :]


def _all_tiles_wait(src_ref, dst_ref, sem):
    pltpu.make_async_copy(src_ref, dst_ref.at[pl.ds(0, src_ref.shape[0])], sem).wait()


def _dispatch_kernel(zblk_ref, dest_ref, h_ref, w_ref, xbuf_ref, pay_ref, zero_ref, sems, zsem):
    i = pl.program_id(0)
    n_steps = pl.num_programs(0)
    slot = lax.rem(i, 2)
    tm = h_ref.shape[0]
    pay = pay_ref.at[slot]

    @pl.when(i == 0)
    def _():
        zero_ref[...] = jnp.zeros_like(zero_ref)
        for k in range(zblk_ref.shape[0]):
            @pl.when(zblk_ref[k] >= 0)
            def _():
                start = pl.multiple_of(zblk_ref[k] * zero_ref.shape[0], zero_ref.shape[0])
                cp = pltpu.make_async_copy(zero_ref, xbuf_ref.at[pl.ds(start, zero_ref.shape[0])], zsem)
                cp.start()
                cp.wait()

    @pl.when(i >= 2)
    def _():
        _all_tiles_wait(pay, xbuf_ref, sems.at[slot])

    hv = h_ref[...]
    for j in range(_HALF // LANES):
        lo, hi = hv[:, j * LANES:(j + 1) * LANES], hv[:, _HALF + j * LANES:_HALF + (j + 1) * LANES]
        _tile_rows(pay, j, tm)[...] = _pack_bf16_pair(lo, hi)
    _tile_rows(pay, _HALF // LANES, tm)[...] = pltpu.bitcast(w_ref[...], jnp.uint32)
    for j in range(_HALF // LANES + 1, SUBLANES):
        _tile_rows(pay, j, tm)[...] = jnp.zeros((tm, LANES), jnp.uint32)

    def issue(t, carry):
        _tile_copy(pay, t, xbuf_ref, dest_ref[0, t], sems.at[slot]).start()
        return carry

    lax.fori_loop(0, tm, issue, 0, unroll=8)

    @pl.when(i == n_steps - 1)
    def _():
        @pl.when(i >= 1)
        def _():
            _all_tiles_wait(pay_ref.at[1 - slot], xbuf_ref, sems.at[1 - slot])

        _all_tiles_wait(pay, xbuf_ref, sems.at[slot])


def _dispatch(h2d, wrow, dest, zero_blocks, n_blocks):
    t, d = h2d.shape
    tm = DISPATCH_TILE
    grid_spec = pltpu.PrefetchScalarGridSpec(
        num_scalar_prefetch=1,
        grid=(t // tm,),
        in_specs=[pl.BlockSpec((1, tm), lambda i, zb: (0, i), memory_space=pltpu.SMEM),
                  pl.BlockSpec((tm, d), lambda i, zb: (i, 0)),
                  pl.BlockSpec((tm, LANES), lambda i, zb: (i, 0))],
        out_specs=pl.BlockSpec(memory_space=pl.ANY),
        scratch_shapes=[pltpu.VMEM((2, tm * SUBLANES, LANES), jnp.uint32),
                        pltpu.VMEM((EXPERT_BLOCK * SUBLANES, LANES), jnp.uint32),
                        pltpu.SemaphoreType.DMA((2,)), pltpu.SemaphoreType.DMA(())],
    )
    return pl.pallas_call(
        _dispatch_kernel,
        grid_spec=grid_spec,
        out_shape=jax.ShapeDtypeStruct((n_blocks * EXPERT_BLOCK * SUBLANES, LANES), jnp.uint32),
        compiler_params=_cparams(("arbitrary",)),
        name="dispatch",
    )(zero_blocks, dest, h2d, wrow)


def _expert_kernel(bg_ref, pres_ref, x_ref, wg_ref, wu_ref, wd_ref, y_ref, xs_ref, ws_ref, ys_ref):
    i = pl.program_id(0)
    eb = xs_ref.shape[0]
    lo, hi = [], []
    for j in range(_HALF // LANES):
        lo_j, hi_j = _unpack_bf16_pair(_tile_rows(x_ref, j, eb)[...])
        lo.append(lo_j)
        hi.append(hi_j)
    xs_ref[...] = jnp.concatenate(lo + hi, axis=-1).astype(BF16)
    ws_ref[...] = pltpu.bitcast(_tile_rows(x_ref, _HALF // LANES, eb)[...], F32)
    ys_ref[...] = jnp.zeros_like(ys_ref)
    halves = [slice(0, eb // 2), slice(eb // 2, eb)]

    for e in range(EXPERTS_PER_GROUP):
        @pl.when(pres_ref[i * EXPERTS_PER_GROUP + e] > 0)
        def _(e=e):
            gu = [(jnp.dot(xs_ref[p, :], wg_ref[e], preferred_element_type=F32),
                   jnp.dot(xs_ref[p, :], wu_ref[e], preferred_element_type=F32)) for p in halves]
            for p, (a, u) in zip(halves, gu):
                we = ws_ref[p, e:e + 1]
                hid = jnp.where(we != 0.0, a * _sigmoid(a) * u * we, 0.0).astype(BF16)
                ys_ref[p, :] += jnp.dot(hid, wd_ref[e], preferred_element_type=F32)

    for j in range(SUBLANES):
        _tile_rows(y_ref, j, eb)[...] = ys_ref[:, j * LANES:(j + 1) * LANES]


def _experts(xbuf, block_g, present, wg, wu, wd):
    eb = EXPERT_BLOCK
    blk = pl.BlockSpec((eb * SUBLANES, LANES), lambda i, bg, pres: (i, 0))
    gw = lambda a: pl.BlockSpec((EXPERTS_PER_GROUP,) + a.shape[1:], lambda i, bg, pres: (bg[i], 0, 0))
    grid_spec = pltpu.PrefetchScalarGridSpec(
        num_scalar_prefetch=2,
        grid=(block_g.shape[0],),
        in_specs=[blk, gw(wg), gw(wu), gw(wd)],
        out_specs=blk,
        scratch_shapes=[pltpu.VMEM((eb, D_MODEL), BF16), pltpu.VMEM((eb, LANES), F32), pltpu.VMEM((eb, D_MODEL), F32)],
    )
    return pl.pallas_call(
        _expert_kernel,
        grid_spec=grid_spec,
        out_shape=jax.ShapeDtypeStruct(xbuf.shape, F32),
        compiler_params=_cparams(("arbitrary",)),
        name="experts",
    )(block_g, present, xbuf, wg, wu, wd)


FINAL_TILE = 512


def _final_kernel(dest_ref, dest_next_ref, h_ref, g_ref, b_ref, ybuf_ref, o_ref, ys_ref, sems):
    i = pl.program_id(0)
    slot = lax.rem(i, 2)
    tm = h_ref.shape[0]

    def gather(d_ref, s):
        def issue(t, carry):
            _tile_copy(ybuf_ref, d_ref[0, t], ys_ref.at[s], t, sems.at[s]).start()
            return carry

        lax.fori_loop(0, tm, issue, 0, unroll=8)

    @pl.when(i == 0)
    def _():
        gather(dest_ref, slot)

    @pl.when(i + 1 < pl.num_programs(0))
    def _():
        gather(dest_next_ref, 1 - slot)

    ys = ys_ref.at[slot]
    pltpu.make_async_copy(ybuf_ref.at[pl.ds(0, ys.shape[0])], ys, sems.at[slot]).wait()
    y = jnp.concatenate([_tile_rows(ys, j, tm)[...] for j in range(SUBLANES)], axis=-1)
    o_ref[...] = _layer_norm(DN_ALPHA * h_ref[...] + y, g_ref[...], b_ref[...])


def _final(h2d, ybuf, dest, ln_g, ln_b):
    t, d = h2d.shape
    tm = FINAL_TILE
    n = t // tm
    row = pl.BlockSpec((tm, d), lambda i: (i, 0))
    vec = pl.BlockSpec((1, d), lambda i: (0, 0))
    return pl.pallas_call(
        _final_kernel,
        grid=(n,),
        in_specs=[pl.BlockSpec((1, tm), lambda i: (0, i), memory_space=pltpu.SMEM),
                  pl.BlockSpec((1, tm), lambda i: (0, jnp.minimum(i + 1, n - 1)), memory_space=pltpu.SMEM),
                  row, vec, vec, pl.BlockSpec(memory_space=pl.ANY)],
        out_specs=row,
        out_shape=jax.ShapeDtypeStruct((t, d), F32),
        scratch_shapes=[pltpu.VMEM((2, tm * SUBLANES, LANES), F32), pltpu.SemaphoreType.DMA((2,))],
        compiler_params=_cparams(("arbitrary",)),
        name="final",
    )(dest, dest, h2d, ln_g, ln_b, ybuf)


def kernel(x, mem, rel_bias, hgrn_lb_logits, w_in, w_mem_kv, hgrn_norm_g, w_branch_a, w_branch_b, w_branch_c, w_out, ln1_g, ln1_b, w_router_group, b_router_group, w_router_expert, b_router_expert, w_exp_gate, w_exp_up, w_exp_down, ln2_g, ln2_b):
    b, s, d = x.shape
    t = b * s
    assert d == D_MODEL and w_in.shape[0] == DEPTH == 1
    assert s % (BAND * DILATED_GROUPS[-1][1]) == 0 and s % HGRN_TILE == 0 and s % MERGE_TILE == 0 and t % EXPERT_BLOCK == 0
    li = 0
    scale = HEAD_DIM ** -0.5
    x2d = x.reshape(t, d)

    w = w_in[li]
    col_scale = np.ones((COLS_A,), np.float32).reshape(N_GROUPS_A, 3, WIDTH_A)
    col_scale[:, 0] = scale
    w_a = (w[:, :COLS_A] * jnp.asarray(col_scale.reshape(-1))).astype(BF16)
    w_b = w[:, COLS_A:COLS_A + COLS_B].astype(BF16)
    w_c = (w[:, COLS_A + COLS_B:COLS_A + COLS_B + COLS_C] * scale).astype(BF16)
    w_g = w[:, COLS_A + COLS_B + COLS_C:].astype(BF16)

    mkv = _matmul(mem.reshape(-1, d), w_mem_kv[li].astype(BF16), BF16, 256, 2 * WIDTH_C, "mem_kv")
    mkv = mkv.reshape(b, -1, 2 * WIDTH_C)

    oa_parts = []
    for gi, (_, dilation) in enumerate(DILATED_GROUPS):
        bias = _band_bias(rel_bias[:, gi * HEADS_A:(gi + 1) * HEADS_A], dilation)
        pg = _proj_classes(x, w_a, gi, 3 * WIDTH_A, dilation, f"proj_a_{gi}")
        oa_parts.append(_band_attention(pg, bias, f"band_attn_{gi}"))

    ob = _hgrn2(x, w_b, hgrn_lb_logits.astype(F32), hgrn_norm_g[li].reshape(1, WIDTH_B))

    wr = jnp.zeros((d, LANES), F32)
    wr = wr.at[:, :N_EXPERT_GROUPS].set(w_router_group[li]).at[:, 8:_ROUTER_ROWS].set(w_router_expert[li])
    br = jnp.zeros((_ROUTER_ROWS,), F32).at[:N_EXPERT_GROUPS].set(b_router_group[li]).at[8:].set(b_router_expert[li])
    br = jnp.broadcast_to(br[:, None], (_ROUTER_ROWS, 128))

    h, code, wrow, counts = _merge(
        x, oa_parts, ob, mkv, w_c, w_g, w_branch_a[li].astype(BF16), w_branch_b[li].astype(BF16),
        w_branch_c[li].astype(BF16), w_out[li].astype(BF16), ln1_g[li].reshape(1, d), ln1_b[li].reshape(1, d), wr, br)
    h2d = h.reshape(t, d)
    eb = EXPERT_BLOCK
    counts = counts[:, 0]
    per_group = EXPERTS_PER_GROUP * EXPERTS_PER_GROUP
    group_counts = jnp.sum(counts.reshape(N_EXPERT_GROUPS, per_group), axis=1)

    blocks_g = (group_counts + eb - 1) // eb
    ends = jnp.cumsum(blocks_g)
    starts = ends - blocks_g
    n_active = ends[-1:].astype(jnp.int32)
    n_blocks = t // eb + N_EXPERT_GROUPS
    blk = jnp.arange(n_blocks, dtype=jnp.int32)
    block_g = jnp.minimum(jnp.sum(ends[None, :] <= blk[:, None], axis=1), N_EXPERT_GROUPS - 1).astype(jnp.int32)
    in_group = counts.reshape(N_EXPERT_GROUPS, per_group)
    class_start = ((starts * eb)[:, None] + jnp.cumsum(in_group, axis=1) - in_group).reshape(-1)
    cls = code // t
    is_c = cls == jnp.arange(N_ROUTE_CLASSES, dtype=jnp.int32)[:, None]
    dest = (code - cls * t + jnp.sum(jnp.where(is_c, class_start[:, None], 0), axis=0, keepdims=True)).astype(jnp.int32)
    trailing = n_active + jnp.arange(N_EXPERT_GROUPS, dtype=jnp.int32)
    zero_blocks = jnp.concatenate([jnp.where(blocks_g > 0, ends - 1, -1),
                                   jnp.where(trailing < n_blocks, trailing, -1)]).astype(jnp.int32)
    cidx = np.arange(N_ROUTE_CLASSES)
    pair_lo, pair_hi = (cidx // EXPERTS_PER_GROUP) % EXPERTS_PER_GROUP, cidx % EXPERTS_PER_GROUP
    member = np.asarray((np.arange(EXPERTS_PER_GROUP)[None, :] == pair_lo[:, None])
                        | (np.arange(EXPERTS_PER_GROUP)[None, :] == pair_hi[:, None]), np.int32)
    overlap = ((class_start[None, :] < (blk[:, None] + 1) * eb) & (class_start[None, :] + counts[None, :] > blk[:, None] * eb)
               & (counts[None, :] > 0))
    present = (jnp.sum(overlap[:, :, None].astype(jnp.int32) * jnp.asarray(member)[None], axis=1) > 0).astype(jnp.int32)

    xbuf = _dispatch(h2d, wrow, dest, zero_blocks, n_blocks)
    ybuf = _experts(xbuf, block_g, present.reshape(-1),
                    w_exp_gate[li].astype(BF16), w_exp_up[li].astype(BF16), w_exp_down[li].astype(BF16))
    out = _final(h2d, ybuf, dest, ln2_g[li].reshape(1, d), ln2_b[li].reshape(1, d))
    return out.reshape(b, s, d)
```

```python
import functools
import math

import numpy as np
import jax
import jax.numpy as jnp
from jax import lax
from jax.experimental import pallas as pl
from jax.experimental.pallas import tpu as pltpu

F32 = jnp.float32
BF16 = jnp.bfloat16

D_MODEL = 1024
DEPTH = 1
DILATED_GROUPS = ((128, 1), (512, 4), (2048, 16))
N_GROUPS_A = len(DILATED_GROUPS)
HEADS_A = 4
HEAD_DIM = 128
WIDTH_A = HEADS_A * HEAD_DIM
BAND = 128
NUM_BUCKETS = 32
MAX_DISTANCE = 2048
N_HEADS_B = 8
WIDTH_B = N_HEADS_B * HEAD_DIM
CHUNK_B = 64
N_HEADS_C = 4
WIDTH_C = N_HEADS_C * HEAD_DIM
COLS_A = N_GROUPS_A * 3 * WIDTH_A
COLS_B = 4 * WIDTH_B
COLS_C = WIDTH_C
COLS_GATE = 3 * D_MODEL
N_EXPERT_GROUPS = 4
EXPERTS_PER_GROUP = 8
N_EXPERTS = N_EXPERT_GROUPS * EXPERTS_PER_GROUP
DN_ALPHA = (2 * DEPTH) ** 0.25
LN_EPS = 1e-5
RMS_EPS = 1e-6

NEG_BIG = -1e30
SAFE_LOG_DECAY = 80.0
VMEM_LIMIT = 56 * 1024 * 1024
EXPERT_BLOCK = 512

_NT = (((1,), (1,)), ((), ()))
_TN = (((0,), (0,)), ((), ()))


def _cparams(sem):
    return pltpu.CompilerParams(dimension_semantics=sem, vmem_limit_bytes=VMEM_LIMIT)


def _matmul_kernel(x_ref, w_ref, o_ref):
    o_ref[...] = jnp.dot(x_ref[...].astype(BF16), w_ref[...], preferred_element_type=F32).astype(o_ref.dtype)


def _matmul(x, w, out_dtype, tm, tn, name):
    t, k = x.shape
    n = w.shape[1]
    return pl.pallas_call(
        _matmul_kernel,
        grid=(n // tn, t // tm),
        in_specs=[pl.BlockSpec((tm, k), lambda j, i: (i, 0)), pl.BlockSpec((k, tn), lambda j, i: (0, j))],
        out_specs=pl.BlockSpec((tm, tn), lambda j, i: (i, j)),
        out_shape=jax.ShapeDtypeStruct((t, n), out_dtype),
        compiler_params=_cparams(("arbitrary", "arbitrary")),
        name=name,
    )(x, w)


def _t5_bucket_np(dist):
    dist = np.asarray(dist, np.int32)
    max_exact = NUM_BUCKETS // 2
    d = np.maximum(dist, 1).astype(np.float32)
    large = max_exact + (np.log(d / max_exact) / math.log(MAX_DISTANCE / max_exact) * (NUM_BUCKETS - max_exact)).astype(np.int32)
    large = np.minimum(large, NUM_BUCKETS - 1)
    return np.where(dist < max_exact, dist, large).astype(np.int32)


def _band_bias(bias_tab, dilation):
    i = np.arange(BAND)[:, None]
    j = np.arange(2 * BAND)[None, :]
    u = i + BAND - j
    in_band = (u >= 0) & (u <= BAND)
    bucket = _t5_bucket_np(np.clip(u, 0, BAND) * dilation)
    onehot = np.asarray(bucket[:, :, None] == np.arange(NUM_BUCKETS)[None, None, :], np.float32)
    bias = jnp.einsum('pqb,bh->hpq', jnp.asarray(onehot), bias_tab.astype(F32), precision=lax.Precision.HIGHEST)
    general = jnp.where(in_band[None], bias, NEG_BIG)
    first = jnp.where((in_band & (j >= BAND))[None], bias, NEG_BIG)
    return jnp.stack([first, general])


LANES = 128


def _proj_perm_kernel(*refs, r):
    n_slab = D_MODEL // LANES
    x_refs, (w_ref, o_ref, xp_ref) = refs[:n_slab], refs[n_slab:]
    tm = xp_ref.shape[0]
    n = tm // r
    for j, x_ref in enumerate(x_refs):
        for c in range(r):
            rows = x_ref[pl.ds(c, n, stride=r), :] if r > 1 else x_ref[...]
            xp_ref[c * n:(c + 1) * n, j * LANES:(j + 1) * LANES] = rows.astype(BF16)
    res = jnp.dot(xp_ref[...], w_ref[...], preferred_element_type=F32).astype(o_ref.dtype)
    for c in range(r):
        o_ref[c] = res[c * n:(c + 1) * n]


PROJ_A_TILE = 1024


def _proj_classes(x, w, col_block, n, r, name):
    b, s, d = x.shape
    tm = PROJ_A_TILE
    n_slab = d // LANES
    slabs = [pl.BlockSpec((None, tm, LANES), lambda bi, t, j=j: (bi, t, j)) for j in range(n_slab)]
    return pl.pallas_call(
        functools.partial(_proj_perm_kernel, r=r),
        grid=(b, s // tm),
        in_specs=slabs + [pl.BlockSpec((d, n), lambda bi, t: (0, col_block))],
        out_specs=pl.BlockSpec((None, r, tm // r, n), lambda bi, t: (bi, 0, t, 0)),
        out_shape=jax.ShapeDtypeStruct((b, r, s // r, n), BF16),
        scratch_shapes=[pltpu.VMEM((tm, d), BF16)],
        compiler_params=_cparams(("arbitrary", "arbitrary")),
        name=name,
    )(*([x] * n_slab), w)


BAND_SPAN = 2048
_HI_MASK = 0xFFFF0000
_LSE_LANES = LANES // HEADS_A


def _pack_bf16_pair(a, b):
    abits = pltpu.bitcast(a.astype(BF16).astype(F32), jnp.uint32)
    bbits = pltpu.bitcast(b.astype(BF16).astype(F32), jnp.uint32)
    return (abits >> 16) | (bbits & jnp.uint32(_HI_MASK))


def _unpack_bf16_pair(word):
    return pltpu.bitcast(word << 16, F32), pltpu.bitcast(word & jnp.uint32(_HI_MASK), F32)


def _band_attn_kernel(q_ref, k_ref, v_ref, kp_ref, vp_ref, bias_ref, o_ref, lse_ref, *, r):
    nq = q_ref.shape[1] // BAND
    variant = jnp.minimum(pl.program_id(1), 1)
    ones = jnp.ones((BAND, HEAD_DIM), BF16)
    lane = lax.broadcasted_iota(jnp.int32, (BAND, LANES), 1)
    heads = [slice(h * HEAD_DIM, (h + 1) * HEAD_DIM) for h in range(HEADS_A)]
    units = [(c, j) for c in range(r) for j in range(nq)]

    def prev_of(k_or_v_ref, prev_ref, c, j, sl):
        return prev_ref[c, :, sl] if j == 0 else k_or_v_ref[c, (j - 1) * BAND:j * BAND, sl]

    def scores(c, j):
        cur = slice(j * BAND, (j + 1) * BAND)
        out = []
        for h, sl in enumerate(heads):
            q = q_ref[c, cur, sl]
            bias = bias_ref[variant, h] if j == 0 else bias_ref[1, h]
            sp = lax.dot_general(q, prev_of(k_ref, kp_ref, c, j, sl), _NT, preferred_element_type=F32)
            sc = lax.dot_general(q, k_ref[c, cur, sl], _NT, preferred_element_type=F32)
            out.append((sp + bias[:, :BAND], sc + bias[:, BAND:]))
        return out

    nxt = scores(*units[0])
    for u, (c, j) in enumerate(units):
        s_all = nxt
        if u + 1 < len(units):
            nxt = scores(*units[u + 1])
        cur = slice(j * BAND, (j + 1) * BAND)
        rows = pl.ds(j * BAND * r + c, BAND, stride=r) if r > 1 else cur
        probs = []
        for sp, sc in s_all:
            m = jnp.max(jnp.maximum(sp, sc), axis=-1, keepdims=True)
            probs.append((m, jnp.exp(sp - m).astype(BF16), jnp.exp(sc - m).astype(BF16)))
        outs, lses = [], []
        for h, sl in enumerate(heads):
            m, pp, pc = probs[h]
            l = jnp.dot(pp, ones, preferred_element_type=F32) + jnp.dot(pc, ones, preferred_element_type=F32)
            o = jnp.dot(pp, prev_of(v_ref, vp_ref, c, j, sl), preferred_element_type=F32)
            o = o + jnp.dot(pc, v_ref[c, cur, sl], preferred_element_type=F32)
            outs.append(o / l)
            lses.append(m + jnp.log(l))
        for p in range(HEADS_A // 2):
            o_ref[p, rows, :] = _pack_bf16_pair(outs[2 * p], outs[2 * p + 1])
        lse = lses[-1]
        for h in range(HEADS_A - 2, -1, -1):
            lse = jnp.where(lane < (h + 1) * _LSE_LANES, lses[h], lse)
        lse_ref[rows, :] = lse


def _band_attention(pg, bias, name):
    b, r, l, _ = pg.shape
    nq = BAND_SPAN // (BAND * r)
    cur = lambda which: pl.BlockSpec((None, r, nq * BAND, WIDTH_A), lambda bi, n: (bi, 0, n, which))
    prev = lambda which: pl.BlockSpec((None, r, BAND, WIDTH_A), lambda bi, n: (bi, 0, jnp.maximum(n * nq - 1, 0), which))
    return pl.pallas_call(
        functools.partial(_band_attn_kernel, r=r),
        grid=(b, l * r // BAND_SPAN),
        in_specs=[cur(0), cur(1), cur(2), prev(1), prev(2), pl.BlockSpec(bias.shape, lambda bi, n: (0, 0, 0, 0))],
        out_specs=[pl.BlockSpec((None, HEADS_A // 2, BAND_SPAN, HEAD_DIM), lambda bi, n: (bi, 0, n, 0)),
                   pl.BlockSpec((None, BAND_SPAN, LANES), lambda bi, n: (bi, n, 0))],
        out_shape=[jax.ShapeDtypeStruct((b, HEADS_A // 2, l * r, HEAD_DIM), jnp.uint32),
                   jax.ShapeDtypeStruct((b, l * r, LANES), F32)],
        compiler_params=_cparams(("arbitrary", "arbitrary")),
        name=name,
    )(pg, pg, pg, pg, pg, bias)


HGRN_TILE = 512
_N_CHUNK = HGRN_TILE // CHUNK_B


def _dynamic_row(ref, r, ls):
    blk = ref[pl.ds(pl.multiple_of((r // 8) * 8, 8), 8), ls]
    sub = lax.broadcasted_iota(jnp.int32, blk.shape, 0)
    return jnp.sum(jnp.where(sub == r % 8, blk, 0.0), axis=0, keepdims=True)


def _sigmoid(z):
    return 0.5 * jnp.tanh(0.5 * z) + 0.5


_PROJ_PIECE = 256


def _hgrn_kernel(x_ref, xn_ref, w_ref, lbl_ref, ng_ref, o_ref, pq_ref, pf_ref, pi_ref, pg_ref, xb_ref,
                 st_ref, qs_ref, ks_ref, kh_ref, vs_ref, bs_ref, ebl_ref, a_ref, os_ref, gs_ref, kf_ref, qsil_ref,
                 *, tiles_per_seq):
    i = pl.program_id(0)
    parts = (pq_ref, pf_ref, pi_ref, pg_ref)
    per_part = WIDTH_B // _PROJ_PIECE

    def piece(k):
        cols = slice(k * _PROJ_PIECE, (k + 1) * _PROJ_PIECE)
        local = slice((k % per_part) * _PROJ_PIECE, (k % per_part + 1) * _PROJ_PIECE)
        parts[k // per_part][:, local] = jnp.dot(xb_ref[...], w_ref[:, cols], preferred_element_type=F32)

    @pl.when(i == 0)
    def _():
        xb_ref[...] = x_ref[...].astype(BF16)
        for k in range(COLS_B // _PROJ_PIECE):
            piece(k)

    @pl.when(lax.rem(i, tiles_per_seq) == 0)
    def _():
        st_ref[...] = jnp.zeros_like(st_ref)

    xb_ref[...] = xn_ref[...].astype(BF16)
    pieces = [k for part in (2, 3, 1, 0) for k in range(part * per_part, (part + 1) * per_part)]
    next_piece = lambda: piece(pieces.pop(0)) if pieces else None
    vs_ref[...] = pi_ref[...].astype(BF16)
    gv = pg_ref[...]
    gs_ref[...] = gv * _sigmoid(gv) * ng_ref[...]

    lg = lbl_ref[...]
    e = jnp.exp(lg - jnp.max(lg, axis=0, keepdims=True))
    lb = e[0:1, :] / jnp.sum(e, axis=0, keepdims=True)

    row = lax.broadcasted_iota(jnp.int32, (CHUNK_B, CHUNK_B), 0)
    colm = lax.broadcasted_iota(jnp.int32, (CHUNK_B, CHUNK_B), 1)
    causal = row >= colm
    tri = causal.astype(BF16)

    for _ in range(2):
        next_piece()
    f = lb + (1.0 - lb) * _sigmoid(pf_ref[...])
    logf = jnp.log(f)
    kf = 1.0 - f
    kf_ref[...] = kf
    l_hi = logf.astype(BF16)
    rem = logf - l_hi.astype(F32)
    l_mid = rem.astype(BF16)
    l_lo = (rem - l_mid.astype(F32)).astype(BF16)
    min_b = None
    for c in range(_N_CHUNK):
        rs = slice(c * CHUNK_B, (c + 1) * CHUNK_B)
        b = (jnp.dot(tri, l_hi[rs], preferred_element_type=F32) + jnp.dot(tri, l_mid[rs], preferred_element_type=F32)
             + jnp.dot(tri, l_lo[rs], preferred_element_type=F32))
        bs_ref[rs, :] = b
        cm = jnp.min(b[CHUNK_B - 1:CHUNK_B, :])
        min_b = cm if min_b is None else jnp.minimum(min_b, cm)
        if c % 2 == 1:
            next_piece()
    safe = min_b > -SAFE_LOG_DECAY

    qv = pq_ref[...]
    qsil_ref[...] = qv * _sigmoid(qv)
    for c in range(_N_CHUNK):
        rs = slice(c * CHUNK_B, (c + 1) * CHUNK_B)
        b = bs_ref[rs, :]
        eb_last = jnp.exp(b[CHUNK_B - 1:CHUNK_B, :])
        ebl_ref[c:c + 1, :] = eb_last
        qs_ref[rs, :] = (qsil_ref[rs, :] * jnp.exp(b)).astype(BF16)
        k_grown = kf_ref[rs, :] * jnp.exp(-b)
        ks_ref[rs, :] = k_grown.astype(BF16)
        kh_ref[rs, :] = (k_grown * eb_last).astype(BF16)
        next_piece()

    @pl.when(safe)
    def _():
        for c in range(_N_CHUNK):
            rs = slice(c * CHUNK_B, (c + 1) * CHUNK_B)
            for h in range(N_HEADS_B):
                ls = slice(h * HEAD_DIM, (h + 1) * HEAD_DIM)
                a = lax.dot_general(qs_ref[rs, ls], ks_ref[rs, ls], _NT, preferred_element_type=F32)
                a_ref[c * N_HEADS_B + h] = jnp.where(causal, a, 0.0)

    @pl.when(jnp.logical_not(safe))
    def _():
        lane = lax.broadcasted_iota(jnp.int32, (1, CHUNK_B), 1)
        trow = lax.broadcasted_iota(jnp.int32, (CHUNK_B, 1), 0)
        for c in range(_N_CHUNK):
            rs = slice(c * CHUNK_B, (c + 1) * CHUNK_B)
            b = bs_ref[rs, :]
            kh_ref[rs, :] = (kf_ref[rs, :] * jnp.exp(b[CHUNK_B - 1:CHUNK_B, :] - b)).astype(BF16)
            for h in range(N_HEADS_B):
                ls = slice(h * HEAD_DIM, (h + 1) * HEAD_DIM)
                bq = bs_ref[rs, ls]
                qh = qsil_ref[rs, ls]

                def body(s, acc, c=c, ls=ls, bq=bq, qh=qh):
                    b_s = _dynamic_row(bs_ref, c * CHUNK_B + s, ls)
                    k_s = _dynamic_row(kf_ref, c * CHUNK_B + s, ls)
                    w = jnp.exp(jnp.minimum(bq - b_s, 0.0))
                    colv = jnp.sum(qh * k_s * w, axis=-1, keepdims=True)
                    colv = jnp.where(trow >= s, colv, 0.0)
                    return acc + colv * (lane == s).astype(F32)

                a_ref[c * N_HEADS_B + h] = lax.fori_loop(0, CHUNK_B, body, jnp.zeros((CHUNK_B, CHUNK_B), F32))

    for c in range(_N_CHUNK):
        rs = slice(c * CHUNK_B, (c + 1) * CHUNK_B)
        for h in range(N_HEADS_B):
            ls = slice(h * HEAD_DIM, (h + 1) * HEAD_DIM)
            st = st_ref[h]
            vh = vs_ref[rs, ls]
            o = lax.dot_general(qs_ref[rs, ls], st.astype(BF16), _NT, preferred_element_type=F32)
            o = o + jnp.dot(a_ref[c * N_HEADS_B + h].astype(BF16), vh, preferred_element_type=F32)
            os_ref[rs, ls] = o
            st_ref[h] = st * ebl_ref[c:c + 1, ls] + lax.dot_general(vh, kh_ref[rs, ls], _TN, preferred_element_type=F32)
            if h == N_HEADS_B - 1:
                next_piece()
    while pieces:
        next_piece()

    for h in range(N_HEADS_B):
        ls = slice(h * HEAD_DIM, (h + 1) * HEAD_DIM)
        o = os_ref[:, ls]
        o = o * lax.rsqrt(jnp.mean(o * o, axis=-1, keepdims=True) + RMS_EPS)
        o_ref[:, ls] = (o * gs_ref[:, ls]).astype(o_ref.dtype)


def _hgrn2(x, w_b, lb_logits, norm_g):
    b, s, d = x.shape
    tb = HGRN_TILE
    n = b * s // tb
    n_slots = lb_logits.shape[0]
    out = pl.pallas_call(
        functools.partial(_hgrn_kernel, tiles_per_seq=s // tb),
        grid=(n,),
        in_specs=[pl.BlockSpec((tb, d), lambda i: (0, 0)),
                  pl.BlockSpec((tb, d), lambda i: (jnp.minimum(i + 1, n - 1), 0)),
                  pl.BlockSpec(w_b.shape, lambda i: (0, 0), pipeline_mode=pl.Buffered(1)),
                  pl.BlockSpec((n_slots, WIDTH_B), lambda i: (0, 0)),
                  pl.BlockSpec((1, WIDTH_B), lambda i: (0, 0))],
        out_specs=pl.BlockSpec((tb, WIDTH_B), lambda i: (i, 0)),
        out_shape=jax.ShapeDtypeStruct((b * s, WIDTH_B), BF16),
        scratch_shapes=[
            pltpu.VMEM((tb, WIDTH_B), F32),
            pltpu.VMEM((tb, WIDTH_B), F32),
            pltpu.VMEM((tb, WIDTH_B), F32),
            pltpu.VMEM((tb, WIDTH_B), F32),
            pltpu.VMEM((tb, d), BF16),
            pltpu.VMEM((N_HEADS_B, HEAD_DIM, HEAD_DIM), F32),
            pltpu.VMEM((tb, WIDTH_B), BF16),
            pltpu.VMEM((tb, WIDTH_B), BF16),
            pltpu.VMEM((tb, WIDTH_B), BF16),
            pltpu.VMEM((tb, WIDTH_B), BF16),
            pltpu.VMEM((tb, WIDTH_B), F32),
            pltpu.VMEM((8, WIDTH_B), F32),
            pltpu.VMEM((_N_CHUNK * N_HEADS_B, CHUNK_B, CHUNK_B), F32),
            pltpu.VMEM((tb, WIDTH_B), F32),
            pltpu.VMEM((tb, WIDTH_B), F32),
            pltpu.VMEM((tb, WIDTH_B), F32),
            pltpu.VMEM((tb, WIDTH_B), F32),
        ],
        compiler_params=_cparams(("arbitrary",)),
        name="hgrn2",
    )(x.reshape(b * s, d), x.reshape(b * s, d), w_b, lb_logits, norm_g)
    return out.reshape(b, s, WIDTH_B)


MERGE_TILE = 512


def _layer_norm(y, g, b):
    mu = jnp.mean(y, axis=-1, keepdims=True)
    d = y - mu
    var = jnp.mean(d * d, axis=-1, keepdims=True)
    return d * lax.rsqrt(var + LN_EPS) * g + b


def _merge_kernel(x_ref, o1_ref, l1_ref, o2_ref, l2_ref, o3_ref, l3_ref, ob_ref, mk_ref, mv_ref,
                  wc_ref, wg_ref, wa_ref, wb_ref, wcc_ref, wo_ref, g_ref, b_ref, wr_ref, br_ref,
                  h_ref, code_ref, wrow_ref, cnt_ref, oc_ref, cnt_sc, *, n_tokens):
    x = x_ref[...]
    xb = x.astype(BF16)

    qc = jnp.dot(xb, wc_ref[...], preferred_element_type=F32).astype(BF16)
    heads_c = [slice(h * HEAD_DIM, (h + 1) * HEAD_DIM) for h in range(N_HEADS_C)]
    s_all = [lax.dot_general(qc[:, ls], mk_ref[:, ls], _NT, preferred_element_type=F32) for ls in heads_c]
    gate_pre = [None] * 3
    gate_pre[0] = jnp.dot(xb, wg_ref[:, :D_MODEL], preferred_element_type=F32)
    p_all = [jnp.exp(s - jnp.max(s, axis=-1, keepdims=True)).astype(BF16) for s in s_all]
    ones = jnp.ones((mk_ref.shape[0], HEAD_DIM), BF16)
    for ls, p in zip(heads_c, p_all):
        l = jnp.dot(p, ones, preferred_element_type=F32)
        oc_ref[:, ls] = (jnp.dot(p, mv_ref[:, ls], preferred_element_type=F32) / l).astype(BF16)
    gate_pre[1] = jnp.dot(xb, wg_ref[:, D_MODEL:2 * D_MODEL], preferred_element_type=F32)
    branch_b = jnp.dot(ob_ref[...], wb_ref[...], preferred_element_type=F32)
    gate_pre[2] = jnp.dot(xb, wg_ref[:, 2 * D_MODEL:], preferred_element_type=F32)

    lses = [l1_ref[...], l2_ref[...], l3_ref[...]]
    m = functools.reduce(jnp.maximum, lses)
    es = [jnp.exp(l - m) for l in lses]
    inv = 1.0 / functools.reduce(lambda a, c: a + c, es)
    mix = [e * inv for e in es]
    o_pairs = [[_unpack_bf16_pair(o_ref[p]) for p in range(HEADS_A // 2)] for o_ref in (o1_ref, o2_ref, o3_ref)]
    oa_heads = []
    for h in range(HEADS_A):
        acc = None
        for gi in range(N_GROUPS_A):
            w = jnp.broadcast_to(mix[gi][:, h * _LSE_LANES:h * _LSE_LANES + 1], (x.shape[0], HEAD_DIM))
            term = w * o_pairs[gi][h // 2][h % 2]
            acc = term if acc is None else acc + term
        oa_heads.append(acc.astype(BF16))
    oa = jnp.concatenate(oa_heads, axis=-1)

    branch_a = jnp.dot(oa, wa_ref[...], preferred_element_type=F32)
    branch_c = jnp.dot(oc_ref[...], wcc_ref[...], preferred_element_type=F32)
    merged = _sigmoid(gate_pre[1]) * branch_b
    merged = merged + _sigmoid(gate_pre[0]) * branch_a
    merged = merged + _sigmoid(gate_pre[2]) * branch_c
    y = DN_ALPHA * x + jnp.dot(merged.astype(BF16), wo_ref[...], preferred_element_type=F32)
    hv = _layer_norm(y, g_ref[...], b_ref[...])
    h_ref[...] = hv

    first = jnp.logical_and(pl.program_id(0) == 0, pl.program_id(1) == 0)
    _route(hv, wr_ref, br_ref, code_ref, wrow_ref, cnt_ref, cnt_sc, first, n_tokens)


def _merge(x, oa_parts, ob, mkv, wc, wg, wa, wb, wcc, wo, ln_g, ln_b, wr, br):
    b, s, d = x.shape
    tm = MERGE_TILE
    nt = s // tm
    tok = lambda w: pl.BlockSpec((None, tm, w), lambda bi, t: (bi, t, 0))
    full = lambda a: pl.BlockSpec(a.shape, lambda bi, t: (0,) * a.ndim, pipeline_mode=pl.Buffered(1))
    mem = lambda half: pl.BlockSpec((None, mkv.shape[1], WIDTH_C), lambda bi, t: (bi, 0, half))
    o_spec = pl.BlockSpec((None, HEADS_A // 2, tm, HEAD_DIM), lambda bi, t: (bi, 0, t, 0))
    flat = [a for pair in oa_parts for a in pair]
    return pl.pallas_call(
        functools.partial(_merge_kernel, n_tokens=b * s),
        grid=(b, nt),
        in_specs=[tok(d)] + [o_spec, tok(LANES)] * N_GROUPS_A + [tok(WIDTH_B), mem(0), mem(1),
                  full(wc), full(wg), full(wa), full(wb), full(wcc), full(wo), full(ln_g), full(ln_b), full(wr), full(br)],
        out_specs=[tok(d), pl.BlockSpec((1, tm), lambda bi, t: (0, bi * nt + t)),
                   pl.BlockSpec((tm, LANES), lambda bi, t: (bi * nt + t, 0)),
                   pl.BlockSpec((N_ROUTE_CLASSES, LANES), lambda bi, t: (0, 0))],
        out_shape=[jax.ShapeDtypeStruct((b, s, d), F32), jax.ShapeDtypeStruct((1, b * s), jnp.int32),
                   jax.ShapeDtypeStruct((b * s, LANES), F32), jax.ShapeDtypeStruct((N_ROUTE_CLASSES, LANES), jnp.int32)],
        scratch_shapes=[pltpu.VMEM((tm, WIDTH_C), BF16), pltpu.VMEM((N_ROUTE_CLASSES, LANES), F32)],
        compiler_params=_cparams(("arbitrary", "arbitrary")),
        name="merge",
    )(x, *flat, ob, mkv, mkv, wc, wg, wa, wb, wcc, wo, ln_g, ln_b, wr, br)


N_ROUTE_CLASSES = N_EXPERT_GROUPS * EXPERTS_PER_GROUP * EXPERTS_PER_GROUP
_ROUTER_ROWS = 8 + N_EXPERTS


def _route(hv, wr_ref, br_ref, dest_ref, wrow_ref, cnt_ref, cnt_sc, first_step, group_capacity):
    tm = hv.shape[0]

    @pl.when(first_step)
    def _():
        cnt_sc[...] = jnp.zeros_like(cnt_sc)

    h_hi = hv.astype(BF16)
    h_lo = (hv - h_hi.astype(F32)).astype(BF16)
    wv = wr_ref[...]
    w_hi = wv.astype(BF16)
    w_lo = (wv - w_hi.astype(F32)).astype(BF16)
    both = jnp.dot(h_hi, jnp.concatenate([w_hi, w_lo], axis=1), preferred_element_type=F32)
    logits = both[:, :LANES] + both[:, LANES:] + jnp.dot(h_lo, w_hi, preferred_element_type=F32)
    logits = logits.T[:_ROUTER_ROWS, :] + br_ref[:, 0:1]
    g = [logits[i:i + 1, :] for i in range(N_EXPERT_GROUPS)]
    gmax = functools.reduce(jnp.maximum, g)
    gsel = jnp.full_like(gmax, N_EXPERT_GROUPS - 1).astype(jnp.int32)
    for i in range(N_EXPERT_GROUPS - 2, -1, -1):
        gsel = jnp.where(g[i] == gmax, i, gsel)
    gprob = 1.0 / functools.reduce(lambda a, c: a + c, [jnp.exp(gi - gmax) for gi in g])

    esel = logits[8 + (N_EXPERT_GROUPS - 1) * EXPERTS_PER_GROUP:8 + N_EXPERTS, :]
    for i in range(N_EXPERT_GROUPS - 2, -1, -1):
        esel = jnp.where(gsel == i, logits[8 + i * EXPERTS_PER_GROUP:8 + (i + 1) * EXPERTS_PER_GROUP, :], esel)
    ridx = lax.broadcasted_iota(jnp.int32, (EXPERTS_PER_GROUP, tm), 0)
    v1 = jnp.max(esel, axis=0, keepdims=True)
    i1 = jnp.min(jnp.where(esel == v1, ridx, EXPERTS_PER_GROUP), axis=0, keepdims=True)
    rest = jnp.where(ridx == i1, -jnp.inf, esel)
    v2 = jnp.max(rest, axis=0, keepdims=True)
    i2 = jnp.min(jnp.where(rest == v2, ridx, EXPERTS_PER_GROUP), axis=0, keepdims=True)
    t = jnp.exp(v2 - v1)
    w1 = gprob / (1.0 + t)
    w2 = gprob * t / (1.0 + t)
    wt = jnp.where(ridx == i1, w1, jnp.where(ridx == i2, w2, 0.0))
    eye = (lax.broadcasted_iota(jnp.int32, (EXPERTS_PER_GROUP, LANES), 0)
           == lax.broadcasted_iota(jnp.int32, (EXPERTS_PER_GROUP, LANES), 1)).astype(F32)
    wrow_ref[...] = lax.dot_general(wt, eye, _TN, preferred_element_type=F32, precision=lax.Precision.HIGHEST)

    cls = (gsel * EXPERTS_PER_GROUP + jnp.minimum(i1, i2)) * EXPERTS_PER_GROUP + jnp.maximum(i1, i2)
    crow = lax.broadcasted_iota(jnp.int32, (N_ROUTE_CLASSES, tm), 0)
    onehot = (crow == cls).astype(F32)
    before = (lax.broadcasted_iota(jnp.int32, (tm, tm), 0) < lax.broadcasted_iota(jnp.int32, (tm, tm), 1))
    excl = jnp.dot(onehot.astype(BF16), before.astype(BF16), preferred_element_type=F32)
    pos = cnt_sc[:, 0:1] + excl
    rank = jnp.sum(onehot * pos, axis=0, keepdims=True).astype(jnp.int32)
    dest_ref[...] = cls * group_capacity + rank
    cnt_sc[...] = cnt_sc[...] + jnp.sum(onehot, axis=1, keepdims=True)
    cnt_ref[...] = cnt_sc[...].astype(jnp.int32)


DISPATCH_TILE = 2048


SUBLANES = 8
_HALF = D_MODEL // 2


def _tile_copy(src_ref, src_tok, dst_ref, dst_tok, sem):
    src = src_ref.at[pl.ds(pl.multiple_of(src_tok * SUBLANES, SUBLANES), SUBLANES)]
    dst = dst_ref.at[pl.ds(pl.multiple_of(dst_tok * SUBLANES, SUBLANES), SUBLANES)]
    return pltpu.make_async_copy(src, dst, sem)


def _tile_rows(ref, j, n):
    return ref.at[pl.ds(j, n, stride=SUBLANES), :]


def _all_tiles_wait(src_ref, dst_ref, sem):
    pltpu.make_async_copy(src_ref, dst_ref.at[pl.ds(0, src_ref.shape[0])], sem).wait()


def _dispatch_kernel(zblk_ref, dest_ref, h_ref, w_ref, xbuf_ref, pay_ref, zero_ref, sems, zsem):
    i = pl.program_id(0)
    n_steps = pl.num_programs(0)
    slot = lax.rem(i, 2)
    tm = h_ref.shape[0]
    pay = pay_ref.at[slot]

    @pl.when(i == 0)
    def _():
        zero_ref[...] = jnp.zeros_like(zero_ref)
        for k in range(zblk_ref.shape[0]):
            @pl.when(zblk_ref[k] >= 0)
            def _():
                start = pl.multiple_of(zblk_ref[k] * zero_ref.shape[0], zero_ref.shape[0])
                cp = pltpu.make_async_copy(zero_ref, xbuf_ref.at[pl.ds(start, zero_ref.shape[0])], zsem)
                cp.start()
                cp.wait()

    @pl.when(i >= 2)
    def _():
        _all_tiles_wait(pay, xbuf_ref, sems.at[slot])

    hv = h_ref[...]
    for j in range(_HALF // LANES):
        lo, hi = hv[:, j * LANES:(j + 1) * LANES], hv[:, _HALF + j * LANES:_HALF + (j + 1) * LANES]
        _tile_rows(pay, j, tm)[...] = _pack_bf16_pair(lo, hi)
    _tile_rows(pay, _HALF // LANES, tm)[...] = pltpu.bitcast(w_ref[...], jnp.uint32)
    for j in range(_HALF // LANES + 1, SUBLANES):
        _tile_rows(pay, j, tm)[...] = jnp.zeros((tm, LANES), jnp.uint32)

    def issue(t, carry):
        _tile_copy(pay, t, xbuf_ref, dest_ref[0, t], sems.at[slot]).start()
        return carry

    lax.fori_loop(0, tm, issue, 0, unroll=8)

    @pl.when(i == n_steps - 1)
    def _():
        @pl.when(i >= 1)
        def _():
            _all_tiles_wait(pay_ref.at[1 - slot], xbuf_ref, sems.at[1 - slot])

        _all_tiles_wait(pay, xbuf_ref, sems.at[slot])


def _dispatch(h2d, wrow, dest, zero_blocks, n_blocks):
    t, d = h2d.shape
    tm = DISPATCH_TILE
    grid_spec = pltpu.PrefetchScalarGridSpec(
        num_scalar_prefetch=1,
        grid=(t // tm,),
        in_specs=[pl.BlockSpec((1, tm), lambda i, zb: (0, i), memory_space=pltpu.SMEM),
                  pl.BlockSpec((tm, d), lambda i, zb: (i, 0)),
                  pl.BlockSpec((tm, LANES), lambda i, zb: (i, 0))],
        out_specs=pl.BlockSpec(memory_space=pl.ANY),
        scratch_shapes=[pltpu.VMEM((2, tm * SUBLANES, LANES), jnp.uint32),
                        pltpu.VMEM((EXPERT_BLOCK * SUBLANES, LANES), jnp.uint32),
                        pltpu.SemaphoreType.DMA((2,)), pltpu.SemaphoreType.DMA(())],
    )
    return pl.pallas_call(
        _dispatch_kernel,
        grid_spec=grid_spec,
        out_shape=jax.ShapeDtypeStruct((n_blocks * EXPERT_BLOCK * SUBLANES, LANES), jnp.uint32),
        compiler_params=_cparams(("arbitrary",)),
        name="dispatch",
    )(zero_blocks, dest, h2d, wrow)


def _expert_kernel(bg_ref, pres_ref, x_ref, wg_ref, wu_ref, wd_ref, y_ref, xs_ref, ws_ref, ys_ref):
    i = pl.program_id(0)
    eb = xs_ref.shape[0]
    lo, hi = [], []
    for j in range(_HALF // LANES):
        lo_j, hi_j = _unpack_bf16_pair(_tile_rows(x_ref, j, eb)[...])
        lo.append(lo_j)
        hi.append(hi_j)
    xs_ref[...] = jnp.concatenate(lo + hi, axis=-1).astype(BF16)
    ws_ref[...] = pltpu.bitcast(_tile_rows(x_ref, _HALF // LANES, eb)[...], F32)
    ys_ref[...] = jnp.zeros_like(ys_ref)
    halves = [slice(0, eb // 2), slice(eb // 2, eb)]

    for e in range(EXPERTS_PER_GROUP):
        @pl.when(pres_ref[i * EXPERTS_PER_GROUP + e] > 0)
        def _(e=e):
            gu = [(jnp.dot(xs_ref[p, :], wg_ref[e], preferred_element_type=F32),
                   jnp.dot(xs_ref[p, :], wu_ref[e], preferred_element_type=F32)) for p in halves]
            for p, (a, u) in zip(halves, gu):
                we = ws_ref[p, e:e + 1]
                hid = jnp.where(we != 0.0, a * _sigmoid(a) * u * we, 0.0).astype(BF16)
                ys_ref[p, :] += jnp.dot(hid, wd_ref[e], preferred_element_type=F32)

    for j in range(SUBLANES):
        _tile_rows(y_ref, j, eb)[...] = ys_ref[:, j * LANES:(j + 1) * LANES]


def _experts(xbuf, block_g, present, wg, wu, wd):
    eb = EXPERT_BLOCK
    blk = pl.BlockSpec((eb * SUBLANES, LANES), lambda i, bg, pres: (i, 0))
    gw = lambda a: pl.BlockSpec((EXPERTS_PER_GROUP,) + a.shape[1:], lambda i, bg, pres: (bg[i], 0, 0))
    grid_spec = pltpu.PrefetchScalarGridSpec(
        num_scalar_prefetch=2,
        grid=(block_g.shape[0],),
        in_specs=[blk, gw(wg), gw(wu), gw(wd)],
        out_specs=blk,
        scratch_shapes=[pltpu.VMEM((eb, D_MODEL), BF16), pltpu.VMEM((eb, LANES), F32), pltpu.VMEM((eb, D_MODEL), F32)],
    )
    return pl.pallas_call(
        _expert_kernel,
        grid_spec=grid_spec,
        out_shape=jax.ShapeDtypeStruct(xbuf.shape, F32),
        compiler_params=_cparams(("arbitrary",)),
        name="experts",
    )(block_g, present, xbuf, wg, wu, wd)


FINAL_TILE = 256


def _final_kernel(dest_ref, dest_next_ref, h_ref, g_ref, b_ref, ybuf_ref, o_ref, ys_ref, sems):
    i = pl.program_id(0)
    slot = lax.rem(i, 2)
    tm = h_ref.shape[0]

    def gather(d_ref, s):
        def issue(t, carry):
            _tile_copy(ybuf_ref, d_ref[0, t], ys_ref.at[s], t, sems.at[s]).start()
            return carry

        lax.fori_loop(0, tm, issue, 0, unroll=8)

    @pl.when(i == 0)
    def _():
        gather(dest_ref, slot)

    @pl.when(i + 1 < pl.num_programs(0))
    def _():
        gather(dest_next_ref, 1 - slot)

    ys = ys_ref.at[slot]
    pltpu.make_async_copy(ybuf_ref.at[pl.ds(0, ys.shape[0])], ys, sems.at[slot]).wait()
    y = jnp.concatenate([_tile_rows(ys, j, tm)[...] for j in range(SUBLANES)], axis=-1)
    o_ref[...] = _layer_norm(DN_ALPHA * h_ref[...] + y, g_ref[...], b_ref[...])


def _final(h2d, ybuf, dest, ln_g, ln_b):
    t, d = h2d.shape
    tm = FINAL_TILE
    n = t // tm
    row = pl.BlockSpec((tm, d), lambda i: (i, 0))
    vec = pl.BlockSpec((1, d), lambda i: (0, 0))
    return pl.pallas_call(
        _final_kernel,
        grid=(n,),
        in_specs=[pl.BlockSpec((1, tm), lambda i: (0, i), memory_space=pltpu.SMEM),
                  pl.BlockSpec((1, tm), lambda i: (0, jnp.minimum(i + 1, n - 1)), memory_space=pltpu.SMEM),
                  row, vec, vec, pl.BlockSpec(memory_space=pl.ANY)],
        out_specs=row,
        out_shape=jax.ShapeDtypeStruct((t, d), F32),
        scratch_shapes=[pltpu.VMEM((2, tm * SUBLANES, LANES), F32), pltpu.SemaphoreType.DMA((2,))],
        compiler_params=_cparams(("arbitrary",)),
        name="final",
    )(dest, dest, h2d, ln_g, ln_b, ybuf)


def kernel(x, mem, rel_bias, hgrn_lb_logits, w_in, w_mem_kv, hgrn_norm_g, w_branch_a, w_branch_b, w_branch_c, w_out, ln1_g, ln1_b, w_router_group, b_router_group, w_router_expert, b_router_expert, w_exp_gate, w_exp_up, w_exp_down, ln2_g, ln2_b):
    b, s, d = x.shape
    t = b * s
    assert d == D_MODEL and w_in.shape[0] == DEPTH == 1
    assert s % (BAND * DILATED_GROUPS[-1][1]) == 0 and s % HGRN_TILE == 0 and s % MERGE_TILE == 0 and t % EXPERT_BLOCK == 0
    li = 0
    scale = HEAD_DIM ** -0.5
    x2d = x.reshape(t, d)

    w = w_in[li]
    col_scale = np.ones((COLS_A,), np.float32).reshape(N_GROUPS_A, 3, WIDTH_A)
    col_scale[:, 0] = scale
    w_a = (w[:, :COLS_A] * jnp.asarray(col_scale.reshape(-1))).astype(BF16)
    w_b = w[:, COLS_A:COLS_A + COLS_B].astype(BF16)
    w_c = (w[:, COLS_A + COLS_B:COLS_A + COLS_B + COLS_C] * scale).astype(BF16)
    w_g = w[:, COLS_A + COLS_B + COLS_C:].astype(BF16)

    mkv = _matmul(mem.reshape(-1, d), w_mem_kv[li].astype(BF16), BF16, 256, 2 * WIDTH_C, "mem_kv")
    mkv = mkv.reshape(b, -1, 2 * WIDTH_C)

    oa_parts = []
    for gi, (_, dilation) in enumerate(DILATED_GROUPS):
        bias = _band_bias(rel_bias[:, gi * HEADS_A:(gi + 1) * HEADS_A], dilation)
        pg = _proj_classes(x, w_a, gi, 3 * WIDTH_A, dilation, f"proj_a_{gi}")
        oa_parts.append(_band_attention(pg, bias, f"band_attn_{gi}"))

    ob = _hgrn2(x, w_b, hgrn_lb_logits.astype(F32), hgrn_norm_g[li].reshape(1, WIDTH_B))

    wr = jnp.zeros((d, LANES), F32)
    wr = wr.at[:, :N_EXPERT_GROUPS].set(w_router_group[li]).at[:, 8:_ROUTER_ROWS].set(w_router_expert[li])
    br = jnp.zeros((_ROUTER_ROWS,), F32).at[:N_EXPERT_GROUPS].set(b_router_group[li]).at[8:].set(b_router_expert[li])
    br = jnp.broadcast_to(br[:, None], (_ROUTER_ROWS, 128))

    h, code, wrow, counts = _merge(
        x, oa_parts, ob, mkv, w_c, w_g, w_branch_a[li].astype(BF16), w_branch_b[li].astype(BF16),
        w_branch_c[li].astype(BF16), w_out[li].astype(BF16), ln1_g[li].reshape(1, d), ln1_b[li].reshape(1, d), wr, br)
    h2d = h.reshape(t, d)
    eb = EXPERT_BLOCK
    counts = counts[:, 0]
    per_group = EXPERTS_PER_GROUP * EXPERTS_PER_GROUP
    group_counts = jnp.sum(counts.reshape(N_EXPERT_GROUPS, per_group), axis=1)

    blocks_g = (group_counts + eb - 1) // eb
    ends = jnp.cumsum(blocks_g)
    starts = ends - blocks_g
    n_active = ends[-1:].astype(jnp.int32)
    n_blocks = t // eb + N_EXPERT_GROUPS
    blk = jnp.arange(n_blocks, dtype=jnp.int32)
    block_g = jnp.minimum(jnp.sum(ends[None, :] <= blk[:, None], axis=1), N_EXPERT_GROUPS - 1).astype(jnp.int32)
    in_group = counts.reshape(N_EXPERT_GROUPS, per_group)
    class_start = ((starts * eb)[:, None] + jnp.cumsum(in_group, axis=1) - in_group).reshape(-1)
    cls = code // t
    is_c = cls == jnp.arange(N_ROUTE_CLASSES, dtype=jnp.int32)[:, None]
    dest = (code - cls * t + jnp.sum(jnp.where(is_c, class_start[:, None], 0), axis=0, keepdims=True)).astype(jnp.int32)
    trailing = n_active + jnp.arange(N_EXPERT_GROUPS, dtype=jnp.int32)
    zero_blocks = jnp.concatenate([jnp.where(blocks_g > 0, ends - 1, -1),
                                   jnp.where(trailing < n_blocks, trailing, -1)]).astype(jnp.int32)
    cidx = np.arange(N_ROUTE_CLASSES)
    pair_lo, pair_hi = (cidx // EXPERTS_PER_GROUP) % EXPERTS_PER_GROUP, cidx % EXPERTS_PER_GROUP
    member = np.asarray((np.arange(EXPERTS_PER_GROUP)[None, :] == pair_lo[:, None])
                        | (np.arange(EXPERTS_PER_GROUP)[None, :] == pair_hi[:, None]), np.int32)
    overlap = ((class_start[None, :] < (blk[:, None] + 1) * eb) & (class_start[None, :] + counts[None, :] > blk[:, None] * eb)
               & (counts[None, :] > 0))
    present = (jnp.sum(overlap[:, :, None].astype(jnp.int32) * jnp.asarray(member)[None], axis=1) > 0).astype(jnp.int32)

    xbuf = _dispatch(h2d, wrow, dest, zero_blocks, n_blocks)
    ybuf = _experts(xbuf, block_g, present.reshape(-1),
                    w_exp_gate[li].astype(BF16), w_exp_up[li].astype(BF16), w_exp_down[li].astype(BF16))
    out = _final(h2d, ybuf, dest, ln2_g[li].reshape(1, d), ln2_b[li].reshape(1, d))
    return out.reshape(b, s, d)
```

```python
import functools
import math

import numpy as np
import jax
import jax.numpy as jnp
from jax import lax
from jax.experimental import pallas as pl
from jax.experimental.pallas import tpu as pltpu

F32 = jnp.float32
BF16 = jnp.bfloat16

D_MODEL = 1024
DEPTH = 1
DILATED_GROUPS = ((128, 1), (512, 4), (2048, 16))
N_GROUPS_A = len(DILATED_GROUPS)
HEADS_A = 4
HEAD_DIM = 128
WIDTH_A = HEADS_A * HEAD_DIM
BAND = 128
NUM_BUCKETS = 32
MAX_DISTANCE = 2048
N_HEADS_B = 8
WIDTH_B = N_HEADS_B * HEAD_DIM
CHUNK_B = 64
N_HEADS_C = 4
WIDTH_C = N_HEADS_C * HEAD_DIM
COLS_A = N_GROUPS_A * 3 * WIDTH_A
COLS_B = 4 * WIDTH_B
COLS_C = WIDTH_C
COLS_GATE = 3 * D_MODEL
N_EXPERT_GROUPS = 4
EXPERTS_PER_GROUP = 8
N_EXPERTS = N_EXPERT_GROUPS * EXPERTS_PER_GROUP
DN_ALPHA = (2 * DEPTH) ** 0.25
LN_EPS = 1e-5
RMS_EPS = 1e-6

NEG_BIG = -1e30
SAFE_LOG_DECAY = 80.0
VMEM_LIMIT = 56 * 1024 * 1024
EXPERT_BLOCK = 512

_NT = (((1,), (1,)), ((), ()))
_TN = (((0,), (0,)), ((), ()))


def _cparams(sem):
    return pltpu.CompilerParams(dimension_semantics=sem, vmem_limit_bytes=VMEM_LIMIT)


def _matmul_kernel(x_ref, w_ref, o_ref):
    o_ref[...] = jnp.dot(x_ref[...].astype(BF16), w_ref[...], preferred_element_type=F32).astype(o_ref.dtype)


def _matmul(x, w, out_dtype, tm, tn, name):
    t, k = x.shape
    n = w.shape[1]
    return pl.pallas_call(
        _matmul_kernel,
        grid=(n // tn, t // tm),
        in_specs=[pl.BlockSpec((tm, k), lambda j, i: (i, 0)), pl.BlockSpec((k, tn), lambda j, i: (0, j))],
        out_specs=pl.BlockSpec((tm, tn), lambda j, i: (i, j)),
        out_shape=jax.ShapeDtypeStruct((t, n), out_dtype),
        compiler_params=_cparams(("arbitrary", "arbitrary")),
        name=name,
    )(x, w)


def _t5_bucket_np(dist):
    dist = np.asarray(dist, np.int32)
    max_exact = NUM_BUCKETS // 2
    d = np.maximum(dist, 1).astype(np.float32)
    large = max_exact + (np.log(d / max_exact) / math.log(MAX_DISTANCE / max_exact) * (NUM_BUCKETS - max_exact)).astype(np.int32)
    large = np.minimum(large, NUM_BUCKETS - 1)
    return np.where(dist < max_exact, dist, large).astype(np.int32)


def _band_bias(bias_tab, dilation):
    i = np.arange(BAND)[:, None]
    j = np.arange(2 * BAND)[None, :]
    u = i + BAND - j
    in_band = (u >= 0) & (u <= BAND)
    bucket = _t5_bucket_np(np.clip(u, 0, BAND) * dilation)
    onehot = np.asarray(bucket[:, :, None] == np.arange(NUM_BUCKETS)[None, None, :], np.float32)
    bias = jnp.einsum('pqb,bh->hpq', jnp.asarray(onehot), bias_tab.astype(F32), precision=lax.Precision.HIGHEST)
    general = jnp.where(in_band[None], bias, NEG_BIG)
    first = jnp.where((in_band & (j >= BAND))[None], bias, NEG_BIG)
    return jnp.stack([first, general])


LANES = 128


def _proj_perm_kernel(*refs, r):
    n_slab = D_MODEL // LANES
    x_refs, (w_ref, o_ref, xp_ref) = refs[:n_slab], refs[n_slab:]
    tm = xp_ref.shape[0]
    n = tm // r
    for j, x_ref in enumerate(x_refs):
        for c in range(r):
            rows = x_ref[pl.ds(c, n, stride=r), :] if r > 1 else x_ref[...]
            xp_ref[c * n:(c + 1) * n, j * LANES:(j + 1) * LANES] = rows.astype(BF16)
    res = jnp.dot(xp_ref[...], w_ref[...], preferred_element_type=F32).astype(o_ref.dtype)
    for c in range(r):
        o_ref[c] = res[c * n:(c + 1) * n]


PROJ_A_TILE = 1024


def _proj_classes(x, w, col_block, n, r, name):
    b, s, d = x.shape
    tm = PROJ_A_TILE
    n_slab = d // LANES
    slabs = [pl.BlockSpec((None, tm, LANES), lambda bi, t, j=j: (bi, t, j)) for j in range(n_slab)]
    return pl.pallas_call(
        functools.partial(_proj_perm_kernel, r=r),
        grid=(b, s // tm),
        in_specs=slabs + [pl.BlockSpec((d, n), lambda bi, t: (0, col_block))],
        out_specs=pl.BlockSpec((None, r, tm // r, n), lambda bi, t: (bi, 0, t, 0)),
        out_shape=jax.ShapeDtypeStruct((b, r, s // r, n), BF16),
        scratch_shapes=[pltpu.VMEM((tm, d), BF16)],
        compiler_params=_cparams(("arbitrary", "arbitrary")),
        name=name,
    )(*([x] * n_slab), w)


BAND_SPAN = 2048
_HI_MASK = 0xFFFF0000
_LSE_LANES = LANES // HEADS_A


def _pack_bf16_pair(a, b):
    abits = pltpu.bitcast(a.astype(BF16).astype(F32), jnp.uint32)
    bbits = pltpu.bitcast(b.astype(BF16).astype(F32), jnp.uint32)
    return (abits >> 16) | (bbits & jnp.uint32(_HI_MASK))


def _unpack_bf16_pair(word):
    return pltpu.bitcast(word << 16, F32), pltpu.bitcast(word & jnp.uint32(_HI_MASK), F32)


def _band_attn_kernel(q_ref, k_ref, v_ref, kp_ref, vp_ref, bias_ref, o_ref, lse_ref, *, r):
    nq = q_ref.shape[1] // BAND
    variant = jnp.minimum(pl.program_id(1), 1)
    ones = jnp.ones((BAND, HEAD_DIM), BF16)
    lane = lax.broadcasted_iota(jnp.int32, (BAND, LANES), 1)
    heads = [slice(h * HEAD_DIM, (h + 1) * HEAD_DIM) for h in range(HEADS_A)]
    units = [(c, j) for c in range(r) for j in range(nq)]

    def prev_of(k_or_v_ref, prev_ref, c, j, sl):
        return prev_ref[c, :, sl] if j == 0 else k_or_v_ref[c, (j - 1) * BAND:j * BAND, sl]

    def scores(c, j):
        cur = slice(j * BAND, (j + 1) * BAND)
        out = []
        for h, sl in enumerate(heads):
            q = q_ref[c, cur, sl]
            bias = bias_ref[variant, h] if j == 0 else bias_ref[1, h]
            sp = lax.dot_general(q, prev_of(k_ref, kp_ref, c, j, sl), _NT, preferred_element_type=F32)
            sc = lax.dot_general(q, k_ref[c, cur, sl], _NT, preferred_element_type=F32)
            out.append((sp + bias[:, :BAND], sc + bias[:, BAND:]))
        return out

    nxt = scores(*units[0])
    for u, (c, j) in enumerate(units):
        s_all = nxt
        if u + 1 < len(units):
            nxt = scores(*units[u + 1])
        cur = slice(j * BAND, (j + 1) * BAND)
        rows = pl.ds(j * BAND * r + c, BAND, stride=r) if r > 1 else cur
        probs = []
        for sp, sc in s_all:
            m = jnp.max(jnp.maximum(sp, sc), axis=-1, keepdims=True)
            probs.append((m, jnp.exp(sp - m).astype(BF16), jnp.exp(sc - m).astype(BF16)))
        outs, lses = [], []
        for h, sl in enumerate(heads):
            m, pp, pc = probs[h]
            l = jnp.dot(pp, ones, preferred_element_type=F32) + jnp.dot(pc, ones, preferred_element_type=F32)
            o = jnp.dot(pp, prev_of(v_ref, vp_ref, c, j, sl), preferred_element_type=F32)
            o = o + jnp.dot(pc, v_ref[c, cur, sl], preferred_element_type=F32)
            outs.append(o / l)
            lses.append(m + jnp.log(l))
        for p in range(HEADS_A // 2):
            o_ref[p, rows, :] = _pack_bf16_pair(outs[2 * p], outs[2 * p + 1])
        lse = lses[-1]
        for h in range(HEADS_A - 2, -1, -1):
            lse = jnp.where(lane < (h + 1) * _LSE_LANES, lses[h], lse)
        lse_ref[rows, :] = lse


def _band_attention(pg, bias, name):
    b, r, l, _ = pg.shape
    nq = BAND_SPAN // (BAND * r)
    cur = lambda which: pl.BlockSpec((None, r, nq * BAND, WIDTH_A), lambda bi, n: (bi, 0, n, which))
    prev = lambda which: pl.BlockSpec((None, r, BAND, WIDTH_A), lambda bi, n: (bi, 0, jnp.maximum(n * nq - 1, 0), which))
    return pl.pallas_call(
        functools.partial(_band_attn_kernel, r=r),
        grid=(b, l * r // BAND_SPAN),
        in_specs=[cur(0), cur(1), cur(2), prev(1), prev(2), pl.BlockSpec(bias.shape, lambda bi, n: (0, 0, 0, 0))],
        out_specs=[pl.BlockSpec((None, HEADS_A // 2, BAND_SPAN, HEAD_DIM), lambda bi, n: (bi, 0, n, 0)),
                   pl.BlockSpec((None, BAND_SPAN, LANES), lambda bi, n: (bi, n, 0))],
        out_shape=[jax.ShapeDtypeStruct((b, HEADS_A // 2, l * r, HEAD_DIM), jnp.uint32),
                   jax.ShapeDtypeStruct((b, l * r, LANES), F32)],
        compiler_params=_cparams(("arbitrary", "arbitrary")),
        name=name,
    )(pg, pg, pg, pg, pg, bias)


HGRN_TILE = 512
_N_CHUNK = HGRN_TILE // CHUNK_B


def _dynamic_row(ref, r, ls):
    blk = ref[pl.ds(pl.multiple_of((r // 8) * 8, 8), 8), ls]
    sub = lax.broadcasted_iota(jnp.int32, blk.shape, 0)
    return jnp.sum(jnp.where(sub == r % 8, blk, 0.0), axis=0, keepdims=True)


def _sigmoid(z):
    return 0.5 * jnp.tanh(0.5 * z) + 0.5


_PROJ_PIECE = 256


def _hgrn_kernel(x_ref, xn_ref, w_ref, lbl_ref, ng_ref, o_ref, pq_ref, pf_ref, pi_ref, pg_ref, xb_ref,
                 st_ref, qs_ref, ks_ref, kh_ref, vs_ref, bs_ref, ebl_ref, a_ref, os_ref, gs_ref, kf_ref, qsil_ref,
                 *, tiles_per_seq):
    i = pl.program_id(0)
    parts = (pq_ref, pf_ref, pi_ref, pg_ref)
    per_part = WIDTH_B // _PROJ_PIECE

    def piece(k):
        cols = slice(k * _PROJ_PIECE, (k + 1) * _PROJ_PIECE)
        local = slice((k % per_part) * _PROJ_PIECE, (k % per_part + 1) * _PROJ_PIECE)
        parts[k // per_part][:, local] = jnp.dot(xb_ref[...], w_ref[:, cols], preferred_element_type=F32)

    @pl.when(i == 0)
    def _():
        xb_ref[...] = x_ref[...].astype(BF16)
        for k in range(COLS_B // _PROJ_PIECE):
            piece(k)

    @pl.when(lax.rem(i, tiles_per_seq) == 0)
    def _():
        st_ref[...] = jnp.zeros_like(st_ref)

    xb_ref[...] = xn_ref[...].astype(BF16)
    pieces = [k for part in (2, 3, 1, 0) for k in range(part * per_part, (part + 1) * per_part)]
    next_piece = lambda: piece(pieces.pop(0)) if pieces else None
    vs_ref[...] = pi_ref[...].astype(BF16)
    gv = pg_ref[...]
    gs_ref[...] = gv * _sigmoid(gv) * ng_ref[...]

    lg = lbl_ref[...]
    e = jnp.exp(lg - jnp.max(lg, axis=0, keepdims=True))
    lb = e[0:1, :] / jnp.sum(e, axis=0, keepdims=True)

    row = lax.broadcasted_iota(jnp.int32, (CHUNK_B, CHUNK_B), 0)
    colm = lax.broadcasted_iota(jnp.int32, (CHUNK_B, CHUNK_B), 1)
    causal = row >= colm
    tri = causal.astype(BF16)

    for _ in range(2):
        next_piece()
    f = lb + (1.0 - lb) * _sigmoid(pf_ref[...])
    logf = jnp.log(f)
    kf = 1.0 - f
    kf_ref[...] = kf
    l_hi = logf.astype(BF16)
    rem = logf - l_hi.astype(F32)
    l_mid = rem.astype(BF16)
    l_lo = (rem - l_mid.astype(F32)).astype(BF16)
    min_b = None
    for c in range(_N_CHUNK):
        rs = slice(c * CHUNK_B, (c + 1) * CHUNK_B)
        b = (jnp.dot(tri, l_hi[rs], preferred_element_type=F32) + jnp.dot(tri, l_mid[rs], preferred_element_type=F32)
             + jnp.dot(tri, l_lo[rs], preferred_element_type=F32))
        bs_ref[rs, :] = b
        cm = jnp.min(b[CHUNK_B - 1:CHUNK_B, :])
        min_b = cm if min_b is None else jnp.minimum(min_b, cm)
        if c % 2 == 1:
            next_piece()
    safe = min_b > -SAFE_LOG_DECAY

    qv = pq_ref[...]
    qsil_ref[...] = qv * _sigmoid(qv)
    for c in range(_N_CHUNK):
        rs = slice(c * CHUNK_B, (c + 1) * CHUNK_B)
        b = bs_ref[rs, :]
        eb_last = jnp.exp(b[CHUNK_B - 1:CHUNK_B, :])
        ebl_ref[c:c + 1, :] = eb_last
        qs_ref[rs, :] = (qsil_ref[rs, :] * jnp.exp(b)).astype(BF16)
        k_grown = kf_ref[rs, :] * jnp.exp(-b)
        ks_ref[rs, :] = k_grown.astype(BF16)
        kh_ref[rs, :] = (k_grown * eb_last).astype(BF16)
        next_piece()

    @pl.when(safe)
    def _():
        for c in range(_N_CHUNK):
            rs = slice(c * CHUNK_B, (c + 1) * CHUNK_B)
            for h in range(N_HEADS_B):
                ls = slice(h * HEAD_DIM, (h + 1) * HEAD_DIM)
                a = lax.dot_general(qs_ref[rs, ls], ks_ref[rs, ls], _NT, preferred_element_type=F32)
                a_ref[c * N_HEADS_B + h] = jnp.where(causal, a, 0.0)

    @pl.when(jnp.logical_not(safe))
    def _():
        lane = lax.broadcasted_iota(jnp.int32, (1, CHUNK_B), 1)
        trow = lax.broadcasted_iota(jnp.int32, (CHUNK_B, 1), 0)
        for c in range(_N_CHUNK):
            rs = slice(c * CHUNK_B, (c + 1) * CHUNK_B)
            b = bs_ref[rs, :]
            kh_ref[rs, :] = (kf_ref[rs, :] * jnp.exp(b[CHUNK_B - 1:CHUNK_B, :] - b)).astype(BF16)
            for h in range(N_HEADS_B):
                ls = slice(h * HEAD_DIM, (h + 1) * HEAD_DIM)
                bq = bs_ref[rs, ls]
                qh = qsil_ref[rs, ls]

                def body(s, acc, c=c, ls=ls, bq=bq, qh=qh):
                    b_s = _dynamic_row(bs_ref, c * CHUNK_B + s, ls)
                    k_s = _dynamic_row(kf_ref, c * CHUNK_B + s, ls)
                    w = jnp.exp(jnp.minimum(bq - b_s, 0.0))
                    colv = jnp.sum(qh * k_s * w, axis=-1, keepdims=True)
                    colv = jnp.where(trow >= s, colv, 0.0)
                    return acc + colv * (lane == s).astype(F32)

                a_ref[c * N_HEADS_B + h] = lax.fori_loop(0, CHUNK_B, body, jnp.zeros((CHUNK_B, CHUNK_B), F32))

    for c in range(_N_CHUNK):
        rs = slice(c * CHUNK_B, (c + 1) * CHUNK_B)
        for h in range(N_HEADS_B):
            ls = slice(h * HEAD_DIM, (h + 1) * HEAD_DIM)
            st = st_ref[h]
            vh = vs_ref[rs, ls]
            o = lax.dot_general(qs_ref[rs, ls], st.astype(BF16), _NT, preferred_element_type=F32)
            o = o + jnp.dot(a_ref[c * N_HEADS_B + h].astype(BF16), vh, preferred_element_type=F32)
            os_ref[rs, ls] = o
            st_ref[h] = st * ebl_ref[c:c + 1, ls] + lax.dot_general(vh, kh_ref[rs, ls], _TN, preferred_element_type=F32)
            if h == N_HEADS_B - 1:
                next_piece()
    while pieces:
        next_piece()

    for h in range(N_HEADS_B):
        ls = slice(h * HEAD_DIM, (h + 1) * HEAD_DIM)
        o = os_ref[:, ls]
        o = o * lax.rsqrt(jnp.mean(o * o, axis=-1, keepdims=True) + RMS_EPS)
        o_ref[:, ls] = (o * gs_ref[:, ls]).astype(o_ref.dtype)


def _hgrn2(x, w_b, lb_logits, norm_g):
    b, s, d = x.shape
    tb = HGRN_TILE
    n = b * s // tb
    n_slots = lb_logits.shape[0]
    out = pl.pallas_call(
        functools.partial(_hgrn_kernel, tiles_per_seq=s // tb),
        grid=(n,),
        in_specs=[pl.BlockSpec((tb, d), lambda i: (0, 0)),
                  pl.BlockSpec((tb, d), lambda i: (jnp.minimum(i + 1, n - 1), 0)),
                  pl.BlockSpec(w_b.shape, lambda i: (0, 0), pipeline_mode=pl.Buffered(1)),
                  pl.BlockSpec((n_slots, WIDTH_B), lambda i: (0, 0)),
                  pl.BlockSpec((1, WIDTH_B), lambda i: (0, 0))],
        out_specs=pl.BlockSpec((tb, WIDTH_B), lambda i: (i, 0)),
        out_shape=jax.ShapeDtypeStruct((b * s, WIDTH_B), BF16),
        scratch_shapes=[
            pltpu.VMEM((tb, WIDTH_B), F32),
            pltpu.VMEM((tb, WIDTH_B), F32),
            pltpu.VMEM((tb, WIDTH_B), F32),
            pltpu.VMEM((tb, WIDTH_B), F32),
            pltpu.VMEM((tb, d), BF16),
            pltpu.VMEM((N_HEADS_B, HEAD_DIM, HEAD_DIM), F32),
            pltpu.VMEM((tb, WIDTH_B), BF16),
            pltpu.VMEM((tb, WIDTH_B), BF16),
            pltpu.VMEM((tb, WIDTH_B), BF16),
            pltpu.VMEM((tb, WIDTH_B), BF16),
            pltpu.VMEM((tb, WIDTH_B), F32),
            pltpu.VMEM((8, WIDTH_B), F32),
            pltpu.VMEM((_N_CHUNK * N_HEADS_B, CHUNK_B, CHUNK_B), F32),
            pltpu.VMEM((tb, WIDTH_B), F32),
            pltpu.VMEM((tb, WIDTH_B), F32),
            pltpu.VMEM((tb, WIDTH_B), F32),
            pltpu.VMEM((tb, WIDTH_B), F32),
        ],
        compiler_params=_cparams(("arbitrary",)),
        name="hgrn2",
    )(x.reshape(b * s, d), x.reshape(b * s, d), w_b, lb_logits, norm_g)
    return out.reshape(b, s, WIDTH_B)


MERGE_TILE = 512


def _layer_norm(y, g, b):
    mu = jnp.mean(y, axis=-1, keepdims=True)
    d = y - mu
    var = jnp.mean(d * d, axis=-1, keepdims=True)
    return d * lax.rsqrt(var + LN_EPS) * g + b


def _merge_kernel(x_ref, o1_ref, l1_ref, o2_ref, l2_ref, o3_ref, l3_ref, ob_ref, mk_ref, mv_ref,
                  wc_ref, wg_ref, wa_ref, wb_ref, wcc_ref, wo_ref, g_ref, b_ref, wr_ref, br_ref,
                  h_ref, code_ref, wrow_ref, cnt_ref, oc_ref, cnt_sc, *, n_tokens):
    x = x_ref[...]
    xb = x.astype(BF16)

    qc = jnp.dot(xb, wc_ref[...], preferred_element_type=F32).astype(BF16)
    heads_c = [slice(h * HEAD_DIM, (h + 1) * HEAD_DIM) for h in range(N_HEADS_C)]
    s_all = [lax.dot_general(qc[:, ls], mk_ref[:, ls], _NT, preferred_element_type=F32) for ls in heads_c]
    gate_pre = [None] * 3
    gate_pre[0] = jnp.dot(xb, wg_ref[:, :D_MODEL], preferred_element_type=F32)
    p_all = [jnp.exp(s - jnp.max(s, axis=-1, keepdims=True)).astype(BF16) for s in s_all]
    ones = jnp.ones((mk_ref.shape[0], HEAD_DIM), BF16)
    for ls, p in zip(heads_c, p_all):
        l = jnp.dot(p, ones, preferred_element_type=F32)
        oc_ref[:, ls] = (jnp.dot(p, mv_ref[:, ls], preferred_element_type=F32) / l).astype(BF16)
    gate_pre[1] = jnp.dot(xb, wg_ref[:, D_MODEL:2 * D_MODEL], preferred_element_type=F32)
    branch_b = jnp.dot(ob_ref[...], wb_ref[...], preferred_element_type=F32)
    gate_pre[2] = jnp.dot(xb, wg_ref[:, 2 * D_MODEL:], preferred_element_type=F32)

    lses = [l1_ref[...], l2_ref[...], l3_ref[...]]
    m = functools.reduce(jnp.maximum, lses)
    es = [jnp.exp(l - m) for l in lses]
    inv = 1.0 / functools.reduce(lambda a, c: a + c, es)
    mix = [e * inv for e in es]
    o_pairs = [[_unpack_bf16_pair(o_ref[p]) for p in range(HEADS_A // 2)] for o_ref in (o1_ref, o2_ref, o3_ref)]
    oa_heads = []
    for h in range(HEADS_A):
        acc = None
        for gi in range(N_GROUPS_A):
            w = jnp.broadcast_to(mix[gi][:, h * _LSE_LANES:h * _LSE_LANES + 1], (x.shape[0], HEAD_DIM))
            term = w * o_pairs[gi][h // 2][h % 2]
            acc = term if acc is None else acc + term
        oa_heads.append(acc.astype(BF16))
    oa = jnp.concatenate(oa_heads, axis=-1)

    branch_a = jnp.dot(oa, wa_ref[...], preferred_element_type=F32)
    branch_c = jnp.dot(oc_ref[...], wcc_ref[...], preferred_element_type=F32)
    merged = _sigmoid(gate_pre[1]) * branch_b
    merged = merged + _sigmoid(gate_pre[0]) * branch_a
    merged = merged + _sigmoid(gate_pre[2]) * branch_c
    y = DN_ALPHA * x + jnp.dot(merged.astype(BF16), wo_ref[...], preferred_element_type=F32)
    hv = _layer_norm(y, g_ref[...], b_ref[...])
    h_ref[...] = hv

    first = jnp.logical_and(pl.program_id(0) == 0, pl.program_id(1) == 0)
    _route(hv, wr_ref, br_ref, code_ref, wrow_ref, cnt_ref, cnt_sc, first, n_tokens)


def _merge(x, oa_parts, ob, mkv, wc, wg, wa, wb, wcc, wo, ln_g, ln_b, wr, br):
    b, s, d = x.shape
    tm = MERGE_TILE
    nt = s // tm
    tok = lambda w: pl.BlockSpec((None, tm, w), lambda bi, t: (bi, t, 0))
    full = lambda a: pl.BlockSpec(a.shape, lambda bi, t: (0,) * a.ndim, pipeline_mode=pl.Buffered(1))
    mem = lambda half: pl.BlockSpec((None, mkv.shape[1], WIDTH_C), lambda bi, t: (bi, 0, half))
    o_spec = pl.BlockSpec((None, HEADS_A // 2, tm, HEAD_DIM), lambda bi, t: (bi, 0, t, 0))
    flat = [a for pair in oa_parts for a in pair]
    return pl.pallas_call(
        functools.partial(_merge_kernel, n_tokens=b * s),
        grid=(b, nt),
        in_specs=[tok(d)] + [o_spec, tok(LANES)] * N_GROUPS_A + [tok(WIDTH_B), mem(0), mem(1),
                  full(wc), full(wg), full(wa), full(wb), full(wcc), full(wo), full(ln_g), full(ln_b), full(wr), full(br)],
        out_specs=[tok(d), pl.BlockSpec((1, tm), lambda bi, t: (0, bi * nt + t)),
                   pl.BlockSpec((tm, LANES), lambda bi, t: (bi * nt + t, 0)),
                   pl.BlockSpec((N_ROUTE_CLASSES, LANES), lambda bi, t: (0, 0))],
        out_shape=[jax.ShapeDtypeStruct((b, s, d), F32), jax.ShapeDtypeStruct((1, b * s), jnp.int32),
                   jax.ShapeDtypeStruct((b * s, LANES), F32), jax.ShapeDtypeStruct((N_ROUTE_CLASSES, LANES), jnp.int32)],
        scratch_shapes=[pltpu.VMEM((tm, WIDTH_C), BF16), pltpu.VMEM((N_ROUTE_CLASSES, LANES), F32)],
        compiler_params=_cparams(("arbitrary", "arbitrary")),
        name="merge",
    )(x, *flat, ob, mkv, mkv, wc, wg, wa, wb, wcc, wo, ln_g, ln_b, wr, br)


N_ROUTE_CLASSES = N_EXPERT_GROUPS * EXPERTS_PER_GROUP * EXPERTS_PER_GROUP
_ROUTER_ROWS = 8 + N_EXPERTS


def _route(hv, wr_ref, br_ref, dest_ref, wrow_ref, cnt_ref, cnt_sc, first_step, group_capacity):
    tm = hv.shape[0]

    @pl.when(first_step)
    def _():
        cnt_sc[...] = jnp.zeros_like(cnt_sc)

    h_hi = hv.astype(BF16)
    h_lo = (hv - h_hi.astype(F32)).astype(BF16)
    wv = wr_ref[...]
    w_hi = wv.astype(BF16)
    w_lo = (wv - w_hi.astype(F32)).astype(BF16)
    both = jnp.dot(h_hi, jnp.concatenate([w_hi, w_lo], axis=1), preferred_element_type=F32)
    logits = both[:, :LANES] + both[:, LANES:] + jnp.dot(h_lo, w_hi, preferred_element_type=F32)
    logits = logits.T[:_ROUTER_ROWS, :] + br_ref[:, 0:1]
    g = [logits[i:i + 1, :] for i in range(N_EXPERT_GROUPS)]
    gmax = functools.reduce(jnp.maximum, g)
    gsel = jnp.full_like(gmax, N_EXPERT_GROUPS - 1).astype(jnp.int32)
    for i in range(N_EXPERT_GROUPS - 2, -1, -1):
        gsel = jnp.where(g[i] == gmax, i, gsel)
    gprob = 1.0 / functools.reduce(lambda a, c: a + c, [jnp.exp(gi - gmax) for gi in g])

    esel = logits[8 + (N_EXPERT_GROUPS - 1) * EXPERTS_PER_GROUP:8 + N_EXPERTS, :]
    for i in range(N_EXPERT_GROUPS - 2, -1, -1):
        esel = jnp.where(gsel == i, logits[8 + i * EXPERTS_PER_GROUP:8 + (i + 1) * EXPERTS_PER_GROUP, :], esel)
    ridx = lax.broadcasted_iota(jnp.int32, (EXPERTS_PER_GROUP, tm), 0)
    v1 = jnp.max(esel, axis=0, keepdims=True)
    i1 = jnp.min(jnp.where(esel == v1, ridx, EXPERTS_PER_GROUP), axis=0, keepdims=True)
    rest = jnp.where(ridx == i1, -jnp.inf, esel)
    v2 = jnp.max(rest, axis=0, keepdims=True)
    i2 = jnp.min(jnp.where(rest == v2, ridx, EXPERTS_PER_GROUP), axis=0, keepdims=True)
    t = jnp.exp(v2 - v1)
    w1 = gprob / (1.0 + t)
    w2 = gprob * t / (1.0 + t)
    wt = jnp.where(ridx == i1, w1, jnp.where(ridx == i2, w2, 0.0))
    eye = (lax.broadcasted_iota(jnp.int32, (EXPERTS_PER_GROUP, LANES), 0)
           == lax.broadcasted_iota(jnp.int32, (EXPERTS_PER_GROUP, LANES), 1)).astype(F32)
    wrow_ref[...] = lax.dot_general(wt, eye, _TN, preferred_element_type=F32, precision=lax.Precision.HIGHEST)

    cls = (gsel * EXPERTS_PER_GROUP + jnp.minimum(i1, i2)) * EXPERTS_PER_GROUP + jnp.maximum(i1, i2)
    crow = lax.broadcasted_iota(jnp.int32, (N_ROUTE_CLASSES, tm), 0)
    onehot = (crow == cls).astype(F32)
    before = (lax.broadcasted_iota(jnp.int32, (tm, tm), 0) < lax.broadcasted_iota(jnp.int32, (tm, tm), 1))
    excl = jnp.dot(onehot.astype(BF16), before.astype(BF16), preferred_element_type=F32)
    pos = cnt_sc[:, 0:1] + excl
    rank = jnp.sum(onehot * pos, axis=0, keepdims=True).astype(jnp.int32)
    dest_ref[...] = cls * group_capacity + rank
    cnt_sc[...] = cnt_sc[...] + jnp.sum(onehot, axis=1, keepdims=True)
    cnt_ref[...] = cnt_sc[...].astype(jnp.int32)


DISPATCH_TILE = 1024


SUBLANES = 8
_HALF = D_MODEL // 2


def _tile_copy(src_ref, src_tok, dst_ref, dst_tok, sem):
    src = src_ref.at[pl.ds(pl.multiple_of(src_tok * SUBLANES, SUBLANES), SUBLANES)]
    dst = dst_ref.at[pl.ds(pl.multiple_of(dst_tok * SUBLANES, SUBLANES), SUBLANES)]
    return pltpu.make_async_copy(src, dst, sem)


def _tile_rows(ref, j, n):
    return ref.at[pl.ds(j, n, stride=SUBLANES), :]


def _all_tiles_wait(src_ref, dst_ref, sem):
    pltpu.make_async_copy(src_ref, dst_ref.at[pl.ds(0, src_ref.shape[0])], sem).wait()


def _dispatch_kernel(zblk_ref, dest_ref, h_ref, w_ref, xbuf_ref, pay_ref, zero_ref, sems, zsem):
    i = pl.program_id(0)
    n_steps = pl.num_programs(0)
    slot = lax.rem(i, 2)
    tm = h_ref.shape[0]
    pay = pay_ref.at[slot]

    @pl.when(i == 0)
    def _():
        zero_ref[...] = jnp.zeros_like(zero_ref)
        for k in range(zblk_ref.shape[0]):
            @pl.when(zblk_ref[k] >= 0)
            def _():
                start = pl.multiple_of(zblk_ref[k] * zero_ref.shape[0], zero_ref.shape[0])
                cp = pltpu.make_async_copy(zero_ref, xbuf_ref.at[pl.ds(start, zero_ref.shape[0])], zsem)
                cp.start()
                cp.wait()

    @pl.when(i >= 2)
    def _():
        _all_tiles_wait(pay, xbuf_ref, sems.at[slot])

    hv = h_ref[...]
    for j in range(_HALF // LANES):
        lo, hi = hv[:, j * LANES:(j + 1) * LANES], hv[:, _HALF + j * LANES:_HALF + (j + 1) * LANES]
        _tile_rows(pay, j, tm)[...] = _pack_bf16_pair(lo, hi)
    _tile_rows(pay, _HALF // LANES, tm)[...] = pltpu.bitcast(w_ref[...], jnp.uint32)
    for j in range(_HALF // LANES + 1, SUBLANES):
        _tile_rows(pay, j, tm)[...] = jnp.zeros((tm, LANES), jnp.uint32)

    def issue(t, carry):
        _tile_copy(pay, t, xbuf_ref, dest_ref[0, t], sems.at[slot]).start()
        return carry

    lax.fori_loop(0, tm, issue, 0, unroll=8)

    @pl.when(i == n_steps - 1)
    def _():
        @pl.when(i >= 1)
        def _():
            _all_tiles_wait(pay_ref.at[1 - slot], xbuf_ref, sems.at[1 - slot])

        _all_tiles_wait(pay, xbuf_ref, sems.at[slot])


def _dispatch(h2d, wrow, dest, zero_blocks, n_blocks):
    t, d = h2d.shape
    tm = DISPATCH_TILE
    grid_spec = pltpu.PrefetchScalarGridSpec(
        num_scalar_prefetch=1,
        grid=(t // tm,),
        in_specs=[pl.BlockSpec((1, tm), lambda i, zb: (0, i), memory_space=pltpu.SMEM),
                  pl.BlockSpec((tm, d), lambda i, zb: (i, 0)),
                  pl.BlockSpec((tm, LANES), lambda i, zb: (i, 0))],
        out_specs=pl.BlockSpec(memory_space=pl.ANY),
        scratch_shapes=[pltpu.VMEM((2, tm * SUBLANES, LANES), jnp.uint32),
                        pltpu.VMEM((EXPERT_BLOCK * SUBLANES, LANES), jnp.uint32),
                        pltpu.SemaphoreType.DMA((2,)), pltpu.SemaphoreType.DMA(())],
    )
    return pl.pallas_call(
        _dispatch_kernel,
        grid_spec=grid_spec,
        out_shape=jax.ShapeDtypeStruct((n_blocks * EXPERT_BLOCK * SUBLANES, LANES), jnp.uint32),
        compiler_params=_cparams(("arbitrary",)),
        name="dispatch",
    )(zero_blocks, dest, h2d, wrow)


def _expert_kernel(bg_ref, pres_ref, x_ref, wg_ref, wu_ref, wd_ref, y_ref, xs_ref, ws_ref, ys_ref):
    i = pl.program_id(0)
    eb = xs_ref.shape[0]
    lo, hi = [], []
    for j in range(_HALF // LANES):
        lo_j, hi_j = _unpack_bf16_pair(_tile_rows(x_ref, j, eb)[...])
        lo.append(lo_j)
        hi.append(hi_j)
    xs_ref[...] = jnp.concatenate(lo + hi, axis=-1).astype(BF16)
    ws_ref[...] = pltpu.bitcast(_tile_rows(x_ref, _HALF // LANES, eb)[...], F32)
    ys_ref[...] = jnp.zeros_like(ys_ref)
    halves = [slice(0, eb // 2), slice(eb // 2, eb)]

    for e in range(EXPERTS_PER_GROUP):
        @pl.when(pres_ref[i * EXPERTS_PER_GROUP + e] > 0)
        def _(e=e):
            gu = [(jnp.dot(xs_ref[p, :], wg_ref[e], preferred_element_type=F32),
                   jnp.dot(xs_ref[p, :], wu_ref[e], preferred_element_type=F32)) for p in halves]
            for p, (a, u) in zip(halves, gu):
                we = ws_ref[p, e:e + 1]
                hid = jnp.where(we != 0.0, a * _sigmoid(a) * u * we, 0.0).astype(BF16)
                ys_ref[p, :] += jnp.dot(hid, wd_ref[e], preferred_element_type=F32)

    for j in range(SUBLANES):
        _tile_rows(y_ref, j, eb)[...] = ys_ref[:, j * LANES:(j + 1) * LANES]


def _experts(xbuf, block_g, present, wg, wu, wd):
    eb = EXPERT_BLOCK
    blk = pl.BlockSpec((eb * SUBLANES, LANES), lambda i, bg, pres: (i, 0))
    gw = lambda a: pl.BlockSpec((EXPERTS_PER_GROUP,) + a.shape[1:], lambda i, bg, pres: (bg[i], 0, 0))
    grid_spec = pltpu.PrefetchScalarGridSpec(
        num_scalar_prefetch=2,
        grid=(block_g.shape[0],),
        in_specs=[blk, gw(wg), gw(wu), gw(wd)],
        out_specs=blk,
        scratch_shapes=[pltpu.VMEM((eb, D_MODEL), BF16), pltpu.VMEM((eb, LANES), F32), pltpu.VMEM((eb, D_MODEL), F32)],
    )
    return pl.pallas_call(
        _expert_kernel,
        grid_spec=grid_spec,
        out_shape=jax.ShapeDtypeStruct(xbuf.shape, F32),
        compiler_params=_cparams(("arbitrary",)),
        name="experts",
    )(block_g, present, xbuf, wg, wu, wd)


FINAL_TILE = 512


_FINAL_CHUNK = 128


def _final_kernel(dest_ref, dest_next_ref, h_ref, g_ref, b_ref, ybuf_ref, o_ref, ys0_ref, ys1_ref, sems):
    i = pl.program_id(0)
    n_steps = pl.num_programs(0)
    tm = h_ref.shape[0]

    def wait_all(ys, sem):
        pltpu.make_async_copy(ybuf_ref.at[pl.ds(0, ys.shape[0])], ys, sem).wait()

    @pl.when(i == 0)
    def _():
        def issue(t, carry):
            _tile_copy(ybuf_ref, dest_ref[0, t], ys0_ref, t, sems.at[0]).start()
            return carry

        lax.fori_loop(0, tm, issue, 0, unroll=8)

    def step(ys, sem, ys_next, sem_next):
        wait_all(ys, sem)

        def chunk(c, carry):
            r0 = pl.multiple_of(c * _FINAL_CHUNK, _FINAL_CHUNK)
            for k in range(_FINAL_CHUNK):
                _tile_copy(ybuf_ref, dest_next_ref[0, r0 + k], ys_next, r0 + k, sem_next).start()
            y = jnp.concatenate([ys[pl.ds(r0 * SUBLANES + j, _FINAL_CHUNK, stride=SUBLANES), :]
                                 for j in range(SUBLANES)], axis=-1)
            rows = pl.ds(r0, _FINAL_CHUNK)
            o_ref[rows, :] = _layer_norm(DN_ALPHA * h_ref[rows, :] + y, g_ref[...], b_ref[...])
            return carry

        lax.fori_loop(0, tm // _FINAL_CHUNK, chunk, 0)
        pl.when(i == n_steps - 1)(functools.partial(wait_all, ys_next, sem_next))

    even = lax.rem(i, 2) == 0
    pl.when(even)(functools.partial(step, ys0_ref, sems.at[0], ys1_ref, sems.at[1]))
    pl.when(jnp.logical_not(even))(functools.partial(step, ys1_ref, sems.at[1], ys0_ref, sems.at[0]))


def _final(h2d, ybuf, dest, ln_g, ln_b):
    t, d = h2d.shape
    tm = FINAL_TILE
    n = t // tm
    row = pl.BlockSpec((tm, d), lambda i: (i, 0))
    vec = pl.BlockSpec((1, d), lambda i: (0, 0))
    return pl.pallas_call(
        _final_kernel,
        grid=(n,),
        in_specs=[pl.BlockSpec((1, tm), lambda i: (0, i), memory_space=pltpu.SMEM),
                  pl.BlockSpec((1, tm), lambda i: (0, jnp.minimum(i + 1, n - 1)), memory_space=pltpu.SMEM),
                  row, vec, vec, pl.BlockSpec(memory_space=pl.ANY)],
        out_specs=row,
        out_shape=jax.ShapeDtypeStruct((t, d), F32),
        scratch_shapes=[pltpu.VMEM((tm * SUBLANES, LANES), F32), pltpu.VMEM((tm * SUBLANES, LANES), F32),
                        pltpu.SemaphoreType.DMA((2,))],
        compiler_params=_cparams(("arbitrary",)),
        name="final",
    )(dest, dest, h2d, ln_g, ln_b, ybuf)


def kernel(x, mem, rel_bias, hgrn_lb_logits, w_in, w_mem_kv, hgrn_norm_g, w_branch_a, w_branch_b, w_branch_c, w_out, ln1_g, ln1_b, w_router_group, b_router_group, w_router_expert, b_router_expert, w_exp_gate, w_exp_up, w_exp_down, ln2_g, ln2_b):
    b, s, d = x.shape
    t = b * s
    assert d == D_MODEL and w_in.shape[0] == DEPTH == 1
    assert s % (BAND * DILATED_GROUPS[-1][1]) == 0 and s % HGRN_TILE == 0 and s % MERGE_TILE == 0 and t % EXPERT_BLOCK == 0
    li = 0
    scale = HEAD_DIM ** -0.5
    x2d = x.reshape(t, d)

    w = w_in[li]
    col_scale = np.ones((COLS_A,), np.float32).reshape(N_GROUPS_A, 3, WIDTH_A)
    col_scale[:, 0] = scale
    w_a = (w[:, :COLS_A] * jnp.asarray(col_scale.reshape(-1))).astype(BF16)
    w_b = w[:, COLS_A:COLS_A + COLS_B].astype(BF16)
    w_c = (w[:, COLS_A + COLS_B:COLS_A + COLS_B + COLS_C] * scale).astype(BF16)
    w_g = w[:, COLS_A + COLS_B + COLS_C:].astype(BF16)

    mkv = _matmul(mem.reshape(-1, d), w_mem_kv[li].astype(BF16), BF16, 256, 2 * WIDTH_C, "mem_kv")
    mkv = mkv.reshape(b, -1, 2 * WIDTH_C)

    oa_parts = []
    for gi, (_, dilation) in enumerate(DILATED_GROUPS):
        bias = _band_bias(rel_bias[:, gi * HEADS_A:(gi + 1) * HEADS_A], dilation)
        pg = _proj_classes(x, w_a, gi, 3 * WIDTH_A, dilation, f"proj_a_{gi}")
        oa_parts.append(_band_attention(pg, bias, f"band_attn_{gi}"))

    ob = _hgrn2(x, w_b, hgrn_lb_logits.astype(F32), hgrn_norm_g[li].reshape(1, WIDTH_B))

    wr = jnp.zeros((d, LANES), F32)
    wr = wr.at[:, :N_EXPERT_GROUPS].set(w_router_group[li]).at[:, 8:_ROUTER_ROWS].set(w_router_expert[li])
    br = jnp.zeros((_ROUTER_ROWS,), F32).at[:N_EXPERT_GROUPS].set(b_router_group[li]).at[8:].set(b_router_expert[li])
    br = jnp.broadcast_to(br[:, None], (_ROUTER_ROWS, 128))

    h, code, wrow, counts = _merge(
        x, oa_parts, ob, mkv, w_c, w_g, w_branch_a[li].astype(BF16), w_branch_b[li].astype(BF16),
        w_branch_c[li].astype(BF16), w_out[li].astype(BF16), ln1_g[li].reshape(1, d), ln1_b[li].reshape(1, d), wr, br)
    h2d = h.reshape(t, d)
    eb = EXPERT_BLOCK
    counts = counts[:, 0]
    per_group = EXPERTS_PER_GROUP * EXPERTS_PER_GROUP
    group_counts = jnp.sum(counts.reshape(N_EXPERT_GROUPS, per_group), axis=1)

    blocks_g = (group_counts + eb - 1) // eb
    ends = jnp.cumsum(blocks_g)
    starts = ends - blocks_g
    n_active = ends[-1:].astype(jnp.int32)
    n_blocks = t // eb + N_EXPERT_GROUPS
    blk = jnp.arange(n_blocks, dtype=jnp.int32)
    block_g = jnp.minimum(jnp.sum(ends[None, :] <= blk[:, None], axis=1), N_EXPERT_GROUPS - 1).astype(jnp.int32)
    in_group = counts.reshape(N_EXPERT_GROUPS, per_group)
    class_start = ((starts * eb)[:, None] + jnp.cumsum(in_group, axis=1) - in_group).reshape(-1)
    cls = code // t
    is_c = cls == jnp.arange(N_ROUTE_CLASSES, dtype=jnp.int32)[:, None]
    dest = (code - cls * t + jnp.sum(jnp.where(is_c, class_start[:, None], 0), axis=0, keepdims=True)).astype(jnp.int32)
    trailing = n_active + jnp.arange(N_EXPERT_GROUPS, dtype=jnp.int32)
    zero_blocks = jnp.concatenate([jnp.where(blocks_g > 0, ends - 1, -1),
                                   jnp.where(trailing < n_blocks, trailing, -1)]).astype(jnp.int32)
    cidx = np.arange(N_ROUTE_CLASSES)
    pair_lo, pair_hi = (cidx // EXPERTS_PER_GROUP) % EXPERTS_PER_GROUP, cidx % EXPERTS_PER_GROUP
    member = np.asarray((np.arange(EXPERTS_PER_GROUP)[None, :] == pair_lo[:, None])
                        | (np.arange(EXPERTS_PER_GROUP)[None, :] == pair_hi[:, None]), np.int32)
    overlap = ((class_start[None, :] < (blk[:, None] + 1) * eb) & (class_start[None, :] + counts[None, :] > blk[:, None] * eb)
               & (counts[None, :] > 0))
    present = (jnp.sum(overlap[:, :, None].astype(jnp.int32) * jnp.asarray(member)[None], axis=1) > 0).astype(jnp.int32)

    xbuf = _dispatch(h2d, wrow, dest, zero_blocks, n_blocks)
    ybuf = _experts(xbuf, block_g, present.reshape(-1),
                    w_exp_gate[li].astype(BF16), w_exp_up[li].astype(BF16), w_exp_down[li].astype(BF16))
    out = _final(h2d, ybuf, dest, ln2_g[li].reshape(1, d), ln2_b[li].reshape(1, d))
    return out.reshape(b, s, d)
```

```python
import functools
import math

import numpy as np
import jax
import jax.numpy as jnp
from jax import lax
from jax.experimental import pallas as pl
from jax.experimental.pallas import tpu as pltpu

F32 = jnp.float32
BF16 = jnp.bfloat16

D_MODEL = 1024
DEPTH = 1
DILATED_GROUPS = ((128, 1), (512, 4), (2048, 16))
N_GROUPS_A = len(DILATED_GROUPS)
HEADS_A = 4
HEAD_DIM = 128
WIDTH_A = HEADS_A * HEAD_DIM
BAND = 128
NUM_BUCKETS = 32
MAX_DISTANCE = 2048
N_HEADS_B = 8
WIDTH_B = N_HEADS_B * HEAD_DIM
CHUNK_B = 64
N_HEADS_C = 4
WIDTH_C = N_HEADS_C * HEAD_DIM
COLS_A = N_GROUPS_A * 3 * WIDTH_A
COLS_B = 4 * WIDTH_B
COLS_C = WIDTH_C
COLS_GATE = 3 * D_MODEL
N_EXPERT_GROUPS = 4
EXPERTS_PER_GROUP = 8
N_EXPERTS = N_EXPERT_GROUPS * EXPERTS_PER_GROUP
DN_ALPHA = (2 * DEPTH) ** 0.25
LN_EPS = 1e-5
RMS_EPS = 1e-6

NEG_BIG = -1e30
SAFE_LOG_DECAY = 80.0
VMEM_LIMIT = 56 * 1024 * 1024
EXPERT_BLOCK = 512

_NT = (((1,), (1,)), ((), ()))
_TN = (((0,), (0,)), ((), ()))


def _cparams(sem):
    return pltpu.CompilerParams(dimension_semantics=sem, vmem_limit_bytes=VMEM_LIMIT)


def _matmul_kernel(x_ref, w_ref, o_ref):
    o_ref[...] = jnp.dot(x_ref[...].astype(BF16), w_ref[...], preferred_element_type=F32).astype(o_ref.dtype)


def _matmul(x, w, out_dtype, tm, tn, name):
    t, k = x.shape
    n = w.shape[1]
    return pl.pallas_call(
        _matmul_kernel,
        grid=(n // tn, t // tm),
        in_specs=[pl.BlockSpec((tm, k), lambda j, i: (i, 0)), pl.BlockSpec((k, tn), lambda j, i: (0, j))],
        out_specs=pl.BlockSpec((tm, tn), lambda j, i: (i, j)),
        out_shape=jax.ShapeDtypeStruct((t, n), out_dtype),
        compiler_params=_cparams(("arbitrary", "arbitrary")),
        name=name,
    )(x, w)


def _t5_bucket_np(dist):
    dist = np.asarray(dist, np.int32)
    max_exact = NUM_BUCKETS // 2
    d = np.maximum(dist, 1).astype(np.float32)
    large = max_exact + (np.log(d / max_exact) / math.log(MAX_DISTANCE / max_exact) * (NUM_BUCKETS - max_exact)).astype(np.int32)
    large = np.minimum(large, NUM_BUCKETS - 1)
    return np.where(dist < max_exact, dist, large).astype(np.int32)


def _band_bias(bias_tab, dilation):
    i = np.arange(BAND)[:, None]
    j = np.arange(2 * BAND)[None, :]
    u = i + BAND - j
    in_band = (u >= 0) & (u <= BAND)
    bucket = _t5_bucket_np(np.clip(u, 0, BAND) * dilation)
    onehot = np.asarray(bucket[:, :, None] == np.arange(NUM_BUCKETS)[None, None, :], np.float32)
    bias = jnp.einsum('pqb,bh->hpq', jnp.asarray(onehot), bias_tab.astype(F32), precision=lax.Precision.HIGHEST)
    general = jnp.where(in_band[None], bias, NEG_BIG)
    first = jnp.where((in_band & (j >= BAND))[None], bias, NEG_BIG)
    return jnp.stack([first, general])


LANES = 128


def _proj_perm_kernel(*refs, r):
    n_slab = D_MODEL // LANES
    x_refs, (w_ref, o_ref, xp_ref) = refs[:n_slab], refs[n_slab:]
    tm = xp_ref.shape[0]
    n = tm // r
    for j, x_ref in enumerate(x_refs):
        for c in range(r):
            rows = x_ref[pl.ds(c, n, stride=r), :] if r > 1 else x_ref[...]
            xp_ref[c * n:(c + 1) * n, j * LANES:(j + 1) * LANES] = rows.astype(BF16)
    res = jnp.dot(xp_ref[...], w_ref[...], preferred_element_type=F32).astype(o_ref.dtype)
    for c in range(r):
        o_ref[c] = res[c * n:(c + 1) * n]


PROJ_A_TILE = 1024


def _proj_classes(x, w, col_block, n, r, name):
    b, s, d = x.shape
    tm = PROJ_A_TILE
    n_slab = d // LANES
    slabs = [pl.BlockSpec((None, tm, LANES), lambda bi, t, j=j: (bi, t, j)) for j in range(n_slab)]
    return pl.pallas_call(
        functools.partial(_proj_perm_kernel, r=r),
        grid=(b, s // tm),
        in_specs=slabs + [pl.BlockSpec((d, n), lambda bi, t: (0, col_block))],
        out_specs=pl.BlockSpec((None, r, tm // r, n), lambda bi, t: (bi, 0, t, 0)),
        out_shape=jax.ShapeDtypeStruct((b, r, s // r, n), BF16),
        scratch_shapes=[pltpu.VMEM((tm, d), BF16)],
        compiler_params=_cparams(("arbitrary", "arbitrary")),
        name=name,
    )(*([x] * n_slab), w)


BAND_SPAN = 2048
_HI_MASK = 0xFFFF0000
_LSE_LANES = LANES // HEADS_A


def _pack_bf16_pair(a, b):
    abits = pltpu.bitcast(a.astype(BF16).astype(F32), jnp.uint32)
    bbits = pltpu.bitcast(b.astype(BF16).astype(F32), jnp.uint32)
    return (abits >> 16) | (bbits & jnp.uint32(_HI_MASK))


def _unpack_bf16_pair(word):
    return pltpu.bitcast(word << 16, F32), pltpu.bitcast(word & jnp.uint32(_HI_MASK), F32)


def _band_attn_kernel(q_ref, k_ref, v_ref, kp_ref, vp_ref, bias_ref, o_ref, lse_ref, *, r):
    nq = q_ref.shape[1] // BAND
    variant = jnp.minimum(pl.program_id(1), 1)
    ones = jnp.ones((BAND, HEAD_DIM), BF16)
    lane = lax.broadcasted_iota(jnp.int32, (BAND, LANES), 1)
    heads = [slice(h * HEAD_DIM, (h + 1) * HEAD_DIM) for h in range(HEADS_A)]
    units = [(c, j) for c in range(r) for j in range(nq)]

    def prev_of(k_or_v_ref, prev_ref, c, j, sl):
        return prev_ref[c, :, sl] if j == 0 else k_or_v_ref[c, (j - 1) * BAND:j * BAND, sl]

    def scores(c, j):
        cur = slice(j * BAND, (j + 1) * BAND)
        out = []
        for h, sl in enumerate(heads):
            q = q_ref[c, cur, sl]
            bias = bias_ref[variant, h] if j == 0 else bias_ref[1, h]
            sp = lax.dot_general(q, prev_of(k_ref, kp_ref, c, j, sl), _NT, preferred_element_type=F32)
            sc = lax.dot_general(q, k_ref[c, cur, sl], _NT, preferred_element_type=F32)
            out.append((sp + bias[:, :BAND], sc + bias[:, BAND:]))
        return out

    nxt = scores(*units[0])
    for u, (c, j) in enumerate(units):
        s_all = nxt
        if u + 1 < len(units):
            nxt = scores(*units[u + 1])
        cur = slice(j * BAND, (j + 1) * BAND)
        rows = pl.ds(j * BAND * r + c, BAND, stride=r) if r > 1 else cur
        probs = []
        for sp, sc in s_all:
            m = jnp.max(jnp.maximum(sp, sc), axis=-1, keepdims=True)
            probs.append((m, jnp.exp(sp - m).astype(BF16), jnp.exp(sc - m).astype(BF16)))
        outs, lses = [], []
        for h, sl in enumerate(heads):
            m, pp, pc = probs[h]
            l = jnp.dot(pp, ones, preferred_element_type=F32) + jnp.dot(pc, ones, preferred_element_type=F32)
            o = jnp.dot(pp, prev_of(v_ref, vp_ref, c, j, sl), preferred_element_type=F32)
            o = o + jnp.dot(pc, v_ref[c, cur, sl], preferred_element_type=F32)
            outs.append(o / l)
            lses.append(m + jnp.log(l))
        for p in range(HEADS_A // 2):
            o_ref[p, rows, :] = _pack_bf16_pair(outs[2 * p], outs[2 * p + 1])
        lse = lses[-1]
        for h in range(HEADS_A - 2, -1, -1):
            lse = jnp.where(lane < (h + 1) * _LSE_LANES, lses[h], lse)
        lse_ref[rows, :] = lse


def _band_attention(pg, bias, name):
    b, r, l, _ = pg.shape
    span = BAND_SPAN if r * BAND * 2 > BAND_SPAN else 2 * BAND_SPAN
    nq = span // (BAND * r)
    cur = lambda which: pl.BlockSpec((None, r, nq * BAND, WIDTH_A), lambda bi, n: (bi, 0, n, which))
    prev = lambda which: pl.BlockSpec((None, r, BAND, WIDTH_A), lambda bi, n: (bi, 0, jnp.maximum(n * nq - 1, 0), which))
    return pl.pallas_call(
        functools.partial(_band_attn_kernel, r=r),
        grid=(b, l * r // span),
        in_specs=[cur(0), cur(1), cur(2), prev(1), prev(2), pl.BlockSpec(bias.shape, lambda bi, n: (0, 0, 0, 0))],
        out_specs=[pl.BlockSpec((None, HEADS_A // 2, span, HEAD_DIM), lambda bi, n: (bi, 0, n, 0)),
                   pl.BlockSpec((None, span, LANES), lambda bi, n: (bi, n, 0))],
        out_shape=[jax.ShapeDtypeStruct((b, HEADS_A // 2, l * r, HEAD_DIM), jnp.uint32),
                   jax.ShapeDtypeStruct((b, l * r, LANES), F32)],
        compiler_params=_cparams(("arbitrary", "arbitrary")),
        name=name,
    )(pg, pg, pg, pg, pg, bias)


HGRN_TILE = 512
_N_CHUNK = HGRN_TILE // CHUNK_B


def _dynamic_row(ref, r, ls):
    blk = ref[pl.ds(pl.multiple_of((r // 8) * 8, 8), 8), ls]
    sub = lax.broadcasted_iota(jnp.int32, blk.shape, 0)
    return jnp.sum(jnp.where(sub == r % 8, blk, 0.0), axis=0, keepdims=True)


def _sigmoid(z):
    return 0.5 * jnp.tanh(0.5 * z) + 0.5


_PROJ_PIECE = 256


def _hgrn_kernel(x_ref, xn_ref, w_ref, lbl_ref, ng_ref, o_ref, pq_ref, pf_ref, pi_ref, pg_ref, xb_ref,
                 st_ref, qs_ref, ks_ref, kh_ref, vs_ref, bs_ref, ebl_ref, a_ref, os_ref, gs_ref, kf_ref, qsil_ref,
                 *, tiles_per_seq):
    i = pl.program_id(0)
    parts = (pq_ref, pf_ref, pi_ref, pg_ref)
    per_part = WIDTH_B // _PROJ_PIECE

    def piece(k):
        cols = slice(k * _PROJ_PIECE, (k + 1) * _PROJ_PIECE)
        local = slice((k % per_part) * _PROJ_PIECE, (k % per_part + 1) * _PROJ_PIECE)
        parts[k // per_part][:, local] = jnp.dot(xb_ref[...], w_ref[:, cols], preferred_element_type=F32)

    @pl.when(i == 0)
    def _():
        xb_ref[...] = x_ref[...].astype(BF16)
        for k in range(COLS_B // _PROJ_PIECE):
            piece(k)

    @pl.when(lax.rem(i, tiles_per_seq) == 0)
    def _():
        st_ref[...] = jnp.zeros_like(st_ref)

    xb_ref[...] = xn_ref[...].astype(BF16)
    pieces = [k for part in (2, 3, 1, 0) for k in range(part * per_part, (part + 1) * per_part)]
    next_piece = lambda: piece(pieces.pop(0)) if pieces else None
    vs_ref[...] = pi_ref[...].astype(BF16)
    gv = pg_ref[...]
    gs_ref[...] = gv * _sigmoid(gv) * ng_ref[...]

    lg = lbl_ref[...]
    e = jnp.exp(lg - jnp.max(lg, axis=0, keepdims=True))
    lb = e[0:1, :] / jnp.sum(e, axis=0, keepdims=True)

    row = lax.broadcasted_iota(jnp.int32, (CHUNK_B, CHUNK_B), 0)
    colm = lax.broadcasted_iota(jnp.int32, (CHUNK_B, CHUNK_B), 1)
    causal = row >= colm
    tri = causal.astype(BF16)

    for _ in range(2):
        next_piece()
    f = lb + (1.0 - lb) * _sigmoid(pf_ref[...])
    logf = jnp.log(f)
    kf = 1.0 - f
    kf_ref[...] = kf
    l_hi = logf.astype(BF16)
    rem = logf - l_hi.astype(F32)
    l_mid = rem.astype(BF16)
    l_lo = (rem - l_mid.astype(F32)).astype(BF16)
    min_b = None
    for c in range(_N_CHUNK):
        rs = slice(c * CHUNK_B, (c + 1) * CHUNK_B)
        b = (jnp.dot(tri, l_hi[rs], preferred_element_type=F32) + jnp.dot(tri, l_mid[rs], preferred_element_type=F32)
             + jnp.dot(tri, l_lo[rs], preferred_element_type=F32))
        bs_ref[rs, :] = b
        cm = jnp.min(b[CHUNK_B - 1:CHUNK_B, :])
        min_b = cm if min_b is None else jnp.minimum(min_b, cm)
        if c % 2 == 1:
            next_piece()
    safe = min_b > -SAFE_LOG_DECAY

    qv = pq_ref[...]
    qsil_ref[...] = qv * _sigmoid(qv)
    for c in range(_N_CHUNK):
        rs = slice(c * CHUNK_B, (c + 1) * CHUNK_B)
        b = bs_ref[rs, :]
        eb_last = jnp.exp(b[CHUNK_B - 1:CHUNK_B, :])
        ebl_ref[c:c + 1, :] = eb_last
        qs_ref[rs, :] = (qsil_ref[rs, :] * jnp.exp(b)).astype(BF16)
        k_grown = kf_ref[rs, :] * jnp.exp(-b)
        ks_ref[rs, :] = k_grown.astype(BF16)
        kh_ref[rs, :] = (k_grown * eb_last).astype(BF16)
        next_piece()

    @pl.when(safe)
    def _():
        for c in range(_N_CHUNK):
            rs = slice(c * CHUNK_B, (c + 1) * CHUNK_B)
            for h in range(N_HEADS_B):
                ls = slice(h * HEAD_DIM, (h + 1) * HEAD_DIM)
                a = lax.dot_general(qs_ref[rs, ls], ks_ref[rs, ls], _NT, preferred_element_type=F32)
                a_ref[c * N_HEADS_B + h] = jnp.where(causal, a, 0.0)

    @pl.when(jnp.logical_not(safe))
    def _():
        lane = lax.broadcasted_iota(jnp.int32, (1, CHUNK_B), 1)
        trow = lax.broadcasted_iota(jnp.int32, (CHUNK_B, 1), 0)
        for c in range(_N_CHUNK):
            rs = slice(c * CHUNK_B, (c + 1) * CHUNK_B)
            b = bs_ref[rs, :]
            kh_ref[rs, :] = (kf_ref[rs, :] * jnp.exp(b[CHUNK_B - 1:CHUNK_B, :] - b)).astype(BF16)
            for h in range(N_HEADS_B):
                ls = slice(h * HEAD_DIM, (h + 1) * HEAD_DIM)
                bq = bs_ref[rs, ls]
                qh = qsil_ref[rs, ls]

                def body(s, acc, c=c, ls=ls, bq=bq, qh=qh):
                    b_s = _dynamic_row(bs_ref, c * CHUNK_B + s, ls)
                    k_s = _dynamic_row(kf_ref, c * CHUNK_B + s, ls)
                    w = jnp.exp(jnp.minimum(bq - b_s, 0.0))
                    colv = jnp.sum(qh * k_s * w, axis=-1, keepdims=True)
                    colv = jnp.where(trow >= s, colv, 0.0)
                    return acc + colv * (lane == s).astype(F32)

                a_ref[c * N_HEADS_B + h] = lax.fori_loop(0, CHUNK_B, body, jnp.zeros((CHUNK_B, CHUNK_B), F32))

    for c in range(_N_CHUNK):
        rs = slice(c * CHUNK_B, (c + 1) * CHUNK_B)
        for h in range(N_HEADS_B):
            ls = slice(h * HEAD_DIM, (h + 1) * HEAD_DIM)
            st = st_ref[h]
            vh = vs_ref[rs, ls]
            o = lax.dot_general(qs_ref[rs, ls], st.astype(BF16), _NT, preferred_element_type=F32)
            o = o + jnp.dot(a_ref[c * N_HEADS_B + h].astype(BF16), vh, preferred_element_type=F32)
            os_ref[rs, ls] = o
            st_ref[h] = st * ebl_ref[c:c + 1, ls] + lax.dot_general(vh, kh_ref[rs, ls], _TN, preferred_element_type=F32)
            if h == N_HEADS_B - 1:
                next_piece()
    while pieces:
        next_piece()

    for h in range(N_HEADS_B):
        ls = slice(h * HEAD_DIM, (h + 1) * HEAD_DIM)
        o = os_ref[:, ls]
        o = o * lax.rsqrt(jnp.mean(o * o, axis=-1, keepdims=True) + RMS_EPS)
        o_ref[:, ls] = (o * gs_ref[:, ls]).astype(o_ref.dtype)


def _hgrn2(x, w_b, lb_logits, norm_g):
    b, s, d = x.shape
    tb = HGRN_TILE
    n = b * s // tb
    n_slots = lb_logits.shape[0]
    out = pl.pallas_call(
        functools.partial(_hgrn_kernel, tiles_per_seq=s // tb),
        grid=(n,),
        in_specs=[pl.BlockSpec((tb, d), lambda i: (0, 0)),
                  pl.BlockSpec((tb, d), lambda i: (jnp.minimum(i + 1, n - 1), 0)),
                  pl.BlockSpec(w_b.shape, lambda i: (0, 0), pipeline_mode=pl.Buffered(1)),
                  pl.BlockSpec((n_slots, WIDTH_B), lambda i: (0, 0)),
                  pl.BlockSpec((1, WIDTH_B), lambda i: (0, 0))],
        out_specs=pl.BlockSpec((tb, WIDTH_B), lambda i: (i, 0)),
        out_shape=jax.ShapeDtypeStruct((b * s, WIDTH_B), BF16),
        scratch_shapes=[
            pltpu.VMEM((tb, WIDTH_B), F32),
            pltpu.VMEM((tb, WIDTH_B), F32),
            pltpu.VMEM((tb, WIDTH_B), F32),
            pltpu.VMEM((tb, WIDTH_B), F32),
            pltpu.VMEM((tb, d), BF16),
            pltpu.VMEM((N_HEADS_B, HEAD_DIM, HEAD_DIM), F32),
            pltpu.VMEM((tb, WIDTH_B), BF16),
            pltpu.VMEM((tb, WIDTH_B), BF16),
            pltpu.VMEM((tb, WIDTH_B), BF16),
            pltpu.VMEM((tb, WIDTH_B), BF16),
            pltpu.VMEM((tb, WIDTH_B), F32),
            pltpu.VMEM((8, WIDTH_B), F32),
            pltpu.VMEM((_N_CHUNK * N_HEADS_B, CHUNK_B, CHUNK_B), F32),
            pltpu.VMEM((tb, WIDTH_B), F32),
            pltpu.VMEM((tb, WIDTH_B), F32),
            pltpu.VMEM((tb, WIDTH_B), F32),
            pltpu.VMEM((tb, WIDTH_B), F32),
        ],
        compiler_params=_cparams(("arbitrary",)),
        name="hgrn2",
    )(x.reshape(b * s, d), x.reshape(b * s, d), w_b, lb_logits, norm_g)
    return out.reshape(b, s, WIDTH_B)


MERGE_TILE = 512


def _layer_norm(y, g, b):
    mu = jnp.mean(y, axis=-1, keepdims=True)
    d = y - mu
    var = jnp.mean(d * d, axis=-1, keepdims=True)
    return d * lax.rsqrt(var + LN_EPS) * g + b


def _merge_kernel(x_ref, o1_ref, l1_ref, o2_ref, l2_ref, o3_ref, l3_ref, ob_ref, mk_ref, mv_ref,
                  wc_ref, wg_ref, wa_ref, wb_ref, wcc_ref, wo_ref, g_ref, b_ref, wr_ref, br_ref,
                  h_ref, code_ref, wrow_ref, cnt_ref, oc_ref, cnt_sc, *, n_tokens):
    x = x_ref[...]
    xb = x.astype(BF16)

    qc = jnp.dot(xb, wc_ref[...], preferred_element_type=F32).astype(BF16)
    heads_c = [slice(h * HEAD_DIM, (h + 1) * HEAD_DIM) for h in range(N_HEADS_C)]
    s_all = [lax.dot_general(qc[:, ls], mk_ref[:, ls], _NT, preferred_element_type=F32) for ls in heads_c]
    gate_pre = [None] * 3
    gate_pre[0] = jnp.dot(xb, wg_ref[:, :D_MODEL], preferred_element_type=F32)
    p_all = [jnp.exp(s - jnp.max(s, axis=-1, keepdims=True)).astype(BF16) for s in s_all]
    ones = jnp.ones((mk_ref.shape[0], HEAD_DIM), BF16)
    for ls, p in zip(heads_c, p_all):
        l = jnp.dot(p, ones, preferred_element_type=F32)
        oc_ref[:, ls] = (jnp.dot(p, mv_ref[:, ls], preferred_element_type=F32) / l).astype(BF16)
    gate_pre[1] = jnp.dot(xb, wg_ref[:, D_MODEL:2 * D_MODEL], preferred_element_type=F32)
    branch_b = jnp.dot(ob_ref[...], wb_ref[...], preferred_element_type=F32)
    gate_pre[2] = jnp.dot(xb, wg_ref[:, 2 * D_MODEL:], preferred_element_type=F32)

    lses = [l1_ref[...], l2_ref[...], l3_ref[...]]
    m = functools.reduce(jnp.maximum, lses)
    es = [jnp.exp(l - m) for l in lses]
    inv = 1.0 / functools.reduce(lambda a, c: a + c, es)
    mix = [e * inv for e in es]
    o_pairs = [[_unpack_bf16_pair(o_ref[p]) for p in range(HEADS_A // 2)] for o_ref in (o1_ref, o2_ref, o3_ref)]
    oa_heads = []
    for h in range(HEADS_A):
        acc = None
        for gi in range(N_GROUPS_A):
            w = jnp.broadcast_to(mix[gi][:, h * _LSE_LANES:h * _LSE_LANES + 1], (x.shape[0], HEAD_DIM))
            term = w * o_pairs[gi][h // 2][h % 2]
            acc = term if acc is None else acc + term
        oa_heads.append(acc.astype(BF16))
    oa = jnp.concatenate(oa_heads, axis=-1)

    branch_a = jnp.dot(oa, wa_ref[...], preferred_element_type=F32)
    branch_c = jnp.dot(oc_ref[...], wcc_ref[...], preferred_element_type=F32)
    merged = _sigmoid(gate_pre[1]) * branch_b
    merged = merged + _sigmoid(gate_pre[0]) * branch_a
    merged = merged + _sigmoid(gate_pre[2]) * branch_c
    y = DN_ALPHA * x + jnp.dot(merged.astype(BF16), wo_ref[...], preferred_element_type=F32)
    hv = _layer_norm(y, g_ref[...], b_ref[...])
    h_ref[...] = hv

    first = jnp.logical_and(pl.program_id(0) == 0, pl.program_id(1) == 0)
    _route(hv, wr_ref, br_ref, code_ref, wrow_ref, cnt_ref, cnt_sc, first, n_tokens)


def _merge(x, oa_parts, ob, mkv, wc, wg, wa, wb, wcc, wo, ln_g, ln_b, wr, br):
    b, s, d = x.shape
    tm = MERGE_TILE
    nt = s // tm
    tok = lambda w: pl.BlockSpec((None, tm, w), lambda bi, t: (bi, t, 0))
    full = lambda a: pl.BlockSpec(a.shape, lambda bi, t: (0,) * a.ndim, pipeline_mode=pl.Buffered(1))
    mem = lambda half: pl.BlockSpec((None, mkv.shape[1], WIDTH_C), lambda bi, t: (bi, 0, half))
    o_spec = pl.BlockSpec((None, HEADS_A // 2, tm, HEAD_DIM), lambda bi, t: (bi, 0, t, 0))
    flat = [a for pair in oa_parts for a in pair]
    return pl.pallas_call(
        functools.partial(_merge_kernel, n_tokens=b * s),
        grid=(b, nt),
        in_specs=[tok(d)] + [o_spec, tok(LANES)] * N_GROUPS_A + [tok(WIDTH_B), mem(0), mem(1),
                  full(wc), full(wg), full(wa), full(wb), full(wcc), full(wo), full(ln_g), full(ln_b), full(wr), full(br)],
        out_specs=[tok(d), pl.BlockSpec((1, tm), lambda bi, t: (0, bi * nt + t)),
                   pl.BlockSpec((tm, LANES), lambda bi, t: (bi * nt + t, 0)),
                   pl.BlockSpec((N_ROUTE_CLASSES, LANES), lambda bi, t: (0, 0))],
        out_shape=[jax.ShapeDtypeStruct((b, s, d), F32), jax.ShapeDtypeStruct((1, b * s), jnp.int32),
                   jax.ShapeDtypeStruct((b * s, LANES), F32), jax.ShapeDtypeStruct((N_ROUTE_CLASSES, LANES), jnp.int32)],
        scratch_shapes=[pltpu.VMEM((tm, WIDTH_C), BF16), pltpu.VMEM((N_ROUTE_CLASSES, LANES), F32)],
        compiler_params=_cparams(("arbitrary", "arbitrary")),
        name="merge",
    )(x, *flat, ob, mkv, mkv, wc, wg, wa, wb, wcc, wo, ln_g, ln_b, wr, br)


N_ROUTE_CLASSES = N_EXPERT_GROUPS * EXPERTS_PER_GROUP * EXPERTS_PER_GROUP
_ROUTER_ROWS = 8 + N_EXPERTS


def _route(hv, wr_ref, br_ref, dest_ref, wrow_ref, cnt_ref, cnt_sc, first_step, group_capacity):
    tm = hv.shape[0]

    @pl.when(first_step)
    def _():
        cnt_sc[...] = jnp.zeros_like(cnt_sc)

    h_hi = hv.astype(BF16)
    h_lo = (hv - h_hi.astype(F32)).astype(BF16)
    wv = wr_ref[...]
    w_hi = wv.astype(BF16)
    w_lo = (wv - w_hi.astype(F32)).astype(BF16)
    both = jnp.dot(h_hi, jnp.concatenate([w_hi, w_lo], axis=1), preferred_element_type=F32)
    logits = both[:, :LANES] + both[:, LANES:] + jnp.dot(h_lo, w_hi, preferred_element_type=F32)
    logits = logits.T[:_ROUTER_ROWS, :] + br_ref[:, 0:1]
    g = [logits[i:i + 1, :] for i in range(N_EXPERT_GROUPS)]
    gmax = functools.reduce(jnp.maximum, g)
    gsel = jnp.full_like(gmax, N_EXPERT_GROUPS - 1).astype(jnp.int32)
    for i in range(N_EXPERT_GROUPS - 2, -1, -1):
        gsel = jnp.where(g[i] == gmax, i, gsel)
    gprob = 1.0 / functools.reduce(lambda a, c: a + c, [jnp.exp(gi - gmax) for gi in g])

    esel = logits[8 + (N_EXPERT_GROUPS - 1) * EXPERTS_PER_GROUP:8 + N_EXPERTS, :]
    for i in range(N_EXPERT_GROUPS - 2, -1, -1):
        esel = jnp.where(gsel == i, logits[8 + i * EXPERTS_PER_GROUP:8 + (i + 1) * EXPERTS_PER_GROUP, :], esel)
    ridx = lax.broadcasted_iota(jnp.int32, (EXPERTS_PER_GROUP, tm), 0)
    v1 = jnp.max(esel, axis=0, keepdims=True)
    i1 = jnp.min(jnp.where(esel == v1, ridx, EXPERTS_PER_GROUP), axis=0, keepdims=True)
    rest = jnp.where(ridx == i1, -jnp.inf, esel)
    v2 = jnp.max(rest, axis=0, keepdims=True)
    i2 = jnp.min(jnp.where(rest == v2, ridx, EXPERTS_PER_GROUP), axis=0, keepdims=True)
    t = jnp.exp(v2 - v1)
    w1 = gprob / (1.0 + t)
    w2 = gprob * t / (1.0 + t)
    wt = jnp.where(ridx == i1, w1, jnp.where(ridx == i2, w2, 0.0))
    eye = (lax.broadcasted_iota(jnp.int32, (EXPERTS_PER_GROUP, LANES), 0)
           == lax.broadcasted_iota(jnp.int32, (EXPERTS_PER_GROUP, LANES), 1)).astype(F32)
    wrow_ref[...] = lax.dot_general(wt, eye, _TN, preferred_element_type=F32, precision=lax.Precision.HIGHEST)

    cls = (gsel * EXPERTS_PER_GROUP + jnp.minimum(i1, i2)) * EXPERTS_PER_GROUP + jnp.maximum(i1, i2)
    crow = lax.broadcasted_iota(jnp.int32, (N_ROUTE_CLASSES, tm), 0)
    onehot = (crow == cls).astype(F32)
    before = (lax.broadcasted_iota(jnp.int32, (tm, tm), 0) < lax.broadcasted_iota(jnp.int32, (tm, tm), 1))
    excl = jnp.dot(onehot.astype(BF16), before.astype(BF16), preferred_element_type=F32)
    pos = cnt_sc[:, 0:1] + excl
    rank = jnp.sum(onehot * pos, axis=0, keepdims=True).astype(jnp.int32)
    dest_ref[...] = cls * group_capacity + rank
    cnt_sc[...] = cnt_sc[...] + jnp.sum(onehot, axis=1, keepdims=True)
    cnt_ref[...] = cnt_sc[...].astype(jnp.int32)


DISPATCH_TILE = 1024


SUBLANES = 8
_HALF = D_MODEL // 2


def _tile_copy(src_ref, src_tok, dst_ref, dst_tok, sem):
    src = src_ref.at[pl.ds(pl.multiple_of(src_tok * SUBLANES, SUBLANES), SUBLANES)]
    dst = dst_ref.at[pl.ds(pl.multiple_of(dst_tok * SUBLANES, SUBLANES), SUBLANES)]
    return pltpu.make_async_copy(src, dst, sem)


def _tile_rows(ref, j, n):
    return ref.at[pl.ds(j, n, stride=SUBLANES), :]


def _all_tiles_wait(src_ref, dst_ref, sem):
    pltpu.make_async_copy(src_ref, dst_ref.at[pl.ds(0, src_ref.shape[0])], sem).wait()


def _dispatch_kernel(zblk_ref, dest_ref, h_ref, w_ref, xbuf_ref, pay_ref, zero_ref, sems, zsem):
    i = pl.program_id(0)
    n_steps = pl.num_programs(0)
    slot = lax.rem(i, 2)
    tm = h_ref.shape[0]
    pay = pay_ref.at[slot]

    @pl.when(i == 0)
    def _():
        zero_ref[...] = jnp.zeros_like(zero_ref)
        for k in range(zblk_ref.shape[0]):
            @pl.when(zblk_ref[k] >= 0)
            def _():
                start = pl.multiple_of(zblk_ref[k] * zero_ref.shape[0], zero_ref.shape[0])
                cp = pltpu.make_async_copy(zero_ref, xbuf_ref.at[pl.ds(start, zero_ref.shape[0])], zsem)
                cp.start()
                cp.wait()

    @pl.when(i >= 2)
    def _():
        _all_tiles_wait(pay, xbuf_ref, sems.at[slot])

    hv = h_ref[...]
    for j in range(_HALF // LANES):
        lo, hi = hv[:, j * LANES:(j + 1) * LANES], hv[:, _HALF + j * LANES:_HALF + (j + 1) * LANES]
        _tile_rows(pay, j, tm)[...] = _pack_bf16_pair(lo, hi)
    _tile_rows(pay, _HALF // LANES, tm)[...] = pltpu.bitcast(w_ref[...], jnp.uint32)
    for j in range(_HALF // LANES + 1, SUBLANES):
        _tile_rows(pay, j, tm)[...] = jnp.zeros((tm, LANES), jnp.uint32)

    def issue(t, carry):
        _tile_copy(pay, t, xbuf_ref, dest_ref[0, t], sems.at[slot]).start()
        return carry

    lax.fori_loop(0, tm, issue, 0, unroll=8)

    @pl.when(i == n_steps - 1)
    def _():
        @pl.when(i >= 1)
        def _():
            _all_tiles_wait(pay_ref.at[1 - slot], xbuf_ref, sems.at[1 - slot])

        _all_tiles_wait(pay, xbuf_ref, sems.at[slot])


def _dispatch(h2d, wrow, dest, zero_blocks, n_blocks):
    t, d = h2d.shape
    tm = DISPATCH_TILE
    grid_spec = pltpu.PrefetchScalarGridSpec(
        num_scalar_prefetch=1,
        grid=(t // tm,),
        in_specs=[pl.BlockSpec((1, tm), lambda i, zb: (0, i), memory_space=pltpu.SMEM),
                  pl.BlockSpec((tm, d), lambda i, zb: (i, 0)),
                  pl.BlockSpec((tm, LANES), lambda i, zb: (i, 0))],
        out_specs=pl.BlockSpec(memory_space=pl.ANY),
        scratch_shapes=[pltpu.VMEM((2, tm * SUBLANES, LANES), jnp.uint32),
                        pltpu.VMEM((EXPERT_BLOCK * SUBLANES, LANES), jnp.uint32),
                        pltpu.SemaphoreType.DMA((2,)), pltpu.SemaphoreType.DMA(())],
    )
    return pl.pallas_call(
        _dispatch_kernel,
        grid_spec=grid_spec,
        out_shape=jax.ShapeDtypeStruct((n_blocks * EXPERT_BLOCK * SUBLANES, LANES), jnp.uint32),
        compiler_params=_cparams(("arbitrary",)),
        name="dispatch",
    )(zero_blocks, dest, h2d, wrow)


def _expert_kernel(bg_ref, pres_ref, x_ref, wg_ref, wu_ref, wd_ref, y_ref, xs_ref, ws_ref, ys_ref):
    i = pl.program_id(0)
    eb = xs_ref.shape[0]
    lo, hi = [], []
    for j in range(_HALF // LANES):
        lo_j, hi_j = _unpack_bf16_pair(_tile_rows(x_ref, j, eb)[...])
        lo.append(lo_j)
        hi.append(hi_j)
    xs_ref[...] = jnp.concatenate(lo + hi, axis=-1).astype(BF16)
    ws_ref[...] = pltpu.bitcast(_tile_rows(x_ref, _HALF // LANES, eb)[...], F32)
    ys_ref[...] = jnp.zeros_like(ys_ref)
    halves = [slice(0, eb // 2), slice(eb // 2, eb)]

    for e in range(EXPERTS_PER_GROUP):
        @pl.when(pres_ref[i * EXPERTS_PER_GROUP + e] > 0)
        def _(e=e):
            gu = [(jnp.dot(xs_ref[p, :], wg_ref[e], preferred_element_type=F32),
                   jnp.dot(xs_ref[p, :], wu_ref[e], preferred_element_type=F32)) for p in halves]
            for p, (a, u) in zip(halves, gu):
                we = ws_ref[p, e:e + 1]
                hid = jnp.where(we != 0.0, a * _sigmoid(a) * u * we, 0.0).astype(BF16)
                ys_ref[p, :] += jnp.dot(hid, wd_ref[e], preferred_element_type=F32)

    for j in range(SUBLANES):
        _tile_rows(y_ref, j, eb)[...] = ys_ref[:, j * LANES:(j + 1) * LANES]


def _experts(xbuf, block_g, present, wg, wu, wd):
    eb = EXPERT_BLOCK
    blk = pl.BlockSpec((eb * SUBLANES, LANES), lambda i, bg, pres: (i, 0))
    gw = lambda a: pl.BlockSpec((EXPERTS_PER_GROUP,) + a.shape[1:], lambda i, bg, pres: (bg[i], 0, 0))
    grid_spec = pltpu.PrefetchScalarGridSpec(
        num_scalar_prefetch=2,
        grid=(block_g.shape[0],),
        in_specs=[blk, gw(wg), gw(wu), gw(wd)],
        out_specs=blk,
        scratch_shapes=[pltpu.VMEM((eb, D_MODEL), BF16), pltpu.VMEM((eb, LANES), F32), pltpu.VMEM((eb, D_MODEL), F32)],
    )
    return pl.pallas_call(
        _expert_kernel,
        grid_spec=grid_spec,
        out_shape=jax.ShapeDtypeStruct(xbuf.shape, F32),
        compiler_params=_cparams(("arbitrary",)),
        name="experts",
    )(block_g, present, xbuf, wg, wu, wd)


FINAL_TILE = 512


def _final_kernel(dest_ref, dest_next_ref, h_ref, g_ref, b_ref, ybuf_ref, o_ref, ys_ref, sems):
    i = pl.program_id(0)
    slot = lax.rem(i, 2)
    tm = h_ref.shape[0]

    def gather(d_ref, s):
        def issue(t, carry):
            _tile_copy(ybuf_ref, d_ref[0, t], ys_ref.at[s], t, sems.at[s]).start()
            return carry

        lax.fori_loop(0, tm, issue, 0, unroll=8)

    @pl.when(i == 0)
    def _():
        gather(dest_ref, slot)

    @pl.when(i + 1 < pl.num_programs(0))
    def _():
        gather(dest_next_ref, 1 - slot)

    ys = ys_ref.at[slot]
    pltpu.make_async_copy(ybuf_ref.at[pl.ds(0, ys.shape[0])], ys, sems.at[slot]).wait()
    y = jnp.concatenate([_tile_rows(ys, j, tm)[...] for j in range(SUBLANES)], axis=-1)
    o_ref[...] = _layer_norm(DN_ALPHA * h_ref[...] + y, g_ref[...], b_ref[...])


def _final(h2d, ybuf, dest, ln_g, ln_b):
    t, d = h2d.shape
    tm = FINAL_TILE
    n = t // tm
    row = pl.BlockSpec((tm, d), lambda i: (i, 0))
    vec = pl.BlockSpec((1, d), lambda i: (0, 0))
    return pl.pallas_call(
        _final_kernel,
        grid=(n,),
        in_specs=[pl.BlockSpec((1, tm), lambda i: (0, i), memory_space=pltpu.SMEM),
                  pl.BlockSpec((1, tm), lambda i: (0, jnp.minimum(i + 1, n - 1)), memory_space=pltpu.SMEM),
                  row, vec, vec, pl.BlockSpec(memory_space=pl.ANY)],
        out_specs=row,
        out_shape=jax.ShapeDtypeStruct((t, d), F32),
        scratch_shapes=[pltpu.VMEM((2, tm * SUBLANES, LANES), F32), pltpu.SemaphoreType.DMA((2,))],
        compiler_params=_cparams(("arbitrary",)),
        name="final",
    )(dest, dest, h2d, ln_g, ln_b, ybuf)


def kernel(x, mem, rel_bias, hgrn_lb_logits, w_in, w_mem_kv, hgrn_norm_g, w_branch_a, w_branch_b, w_branch_c, w_out, ln1_g, ln1_b, w_router_group, b_router_group, w_router_expert, b_router_expert, w_exp_gate, w_exp_up, w_exp_down, ln2_g, ln2_b):
    b, s, d = x.shape
    t = b * s
    assert d == D_MODEL and w_in.shape[0] == DEPTH == 1
    assert s % (BAND * DILATED_GROUPS[-1][1]) == 0 and s % HGRN_TILE == 0 and s % MERGE_TILE == 0 and t % EXPERT_BLOCK == 0
    li = 0
    scale = HEAD_DIM ** -0.5
    x2d = x.reshape(t, d)

    w = w_in[li]
    col_scale = np.ones((COLS_A,), np.float32).reshape(N_GROUPS_A, 3, WIDTH_A)
    col_scale[:, 0] = scale
    w_a = (w[:, :COLS_A] * jnp.asarray(col_scale.reshape(-1))).astype(BF16)
    w_b = w[:, COLS_A:COLS_A + COLS_B].astype(BF16)
    w_c = (w[:, COLS_A + COLS_B:COLS_A + COLS_B + COLS_C] * scale).astype(BF16)
    w_g = w[:, COLS_A + COLS_B + COLS_C:].astype(BF16)

    mkv = _matmul(mem.reshape(-1, d), w_mem_kv[li].astype(BF16), BF16, 256, 2 * WIDTH_C, "mem_kv")
    mkv = mkv.reshape(b, -1, 2 * WIDTH_C)

    oa_parts = []
    for gi, (_, dilation) in enumerate(DILATED_GROUPS):
        bias = _band_bias(rel_bias[:, gi * HEADS_A:(gi + 1) * HEADS_A], dilation)
        pg = _proj_classes(x, w_a, gi, 3 * WIDTH_A, dilation, f"proj_a_{gi}")
        oa_parts.append(_band_attention(pg, bias, f"band_attn_{gi}"))

    ob = _hgrn2(x, w_b, hgrn_lb_logits.astype(F32), hgrn_norm_g[li].reshape(1, WIDTH_B))

    wr = jnp.zeros((d, LANES), F32)
    wr = wr.at[:, :N_EXPERT_GROUPS].set(w_router_group[li]).at[:, 8:_ROUTER_ROWS].set(w_router_expert[li])
    br = jnp.zeros((_ROUTER_ROWS,), F32).at[:N_EXPERT_GROUPS].set(b_router_group[li]).at[8:].set(b_router_expert[li])
    br = jnp.broadcast_to(br[:, None], (_ROUTER_ROWS, 128))

    h, code, wrow, counts = _merge(
        x, oa_parts, ob, mkv, w_c, w_g, w_branch_a[li].astype(BF16), w_branch_b[li].astype(BF16),
        w_branch_c[li].astype(BF16), w_out[li].astype(BF16), ln1_g[li].reshape(1, d), ln1_b[li].reshape(1, d), wr, br)
    h2d = h.reshape(t, d)
    eb = EXPERT_BLOCK
    counts = counts[:, 0]
    per_group = EXPERTS_PER_GROUP * EXPERTS_PER_GROUP
    group_counts = jnp.sum(counts.reshape(N_EXPERT_GROUPS, per_group), axis=1)

    blocks_g = (group_counts + eb - 1) // eb
    ends = jnp.cumsum(blocks_g)
    starts = ends - blocks_g
    n_active = ends[-1:].astype(jnp.int32)
    n_blocks = t // eb + N_EXPERT_GROUPS
    blk = jnp.arange(n_blocks, dtype=jnp.int32)
    block_g = jnp.minimum(jnp.sum(ends[None, :] <= blk[:, None], axis=1), N_EXPERT_GROUPS - 1).astype(jnp.int32)
    in_group = counts.reshape(N_EXPERT_GROUPS, per_group)
    class_start = ((starts * eb)[:, None] + jnp.cumsum(in_group, axis=1) - in_group).reshape(-1)
    cls = code // t
    is_c = cls == jnp.arange(N_ROUTE_CLASSES, dtype=jnp.int32)[:, None]
    dest = (code - cls * t + jnp.sum(jnp.where(is_c, class_start[:, None], 0), axis=0, keepdims=True)).astype(jnp.int32)
    trailing = n_active + jnp.arange(N_EXPERT_GROUPS, dtype=jnp.int32)
    zero_blocks = jnp.concatenate([jnp.where(blocks_g > 0, ends - 1, -1),
                                   jnp.where(trailing < n_blocks, trailing, -1)]).astype(jnp.int32)
    cidx = np.arange(N_ROUTE_CLASSES)
    pair_lo, pair_hi = (cidx // EXPERTS_PER_GROUP) % EXPERTS_PER_GROUP, cidx % EXPERTS_PER_GROUP
    member = np.asarray((np.arange(EXPERTS_PER_GROUP)[None, :] == pair_lo[:, None])
                        | (np.arange(EXPERTS_PER_GROUP)[None, :] == pair_hi[:, None]), np.int32)
    overlap = ((class_start[None, :] < (blk[:, None] + 1) * eb) & (class_start[None, :] + counts[None, :] > blk[:, None] * eb)
               & (counts[None, :] > 0))
    present = (jnp.sum(overlap[:, :, None].astype(jnp.int32) * jnp.asarray(member)[None], axis=1) > 0).astype(jnp.int32)

    xbuf = _dispatch(h2d, wrow, dest, zero_blocks, n_blocks)
    ybuf = _experts(xbuf, block_g, present.reshape(-1),
                    w_exp_gate[li].astype(BF16), w_exp_up[li].astype(BF16), w_exp_down[li].astype(BF16))
    out = _final(h2d, ybuf, dest, ln2_g[li].reshape(1, d), ln2_b[li].reshape(1, d))
    return out.reshape(b, s, d)
```

```python
import functools
import math

import numpy as np
import jax
import jax.numpy as jnp
from jax import lax
from jax.experimental import pallas as pl
from jax.experimental.pallas import tpu as pltpu

F32 = jnp.float32
BF16 = jnp.bfloat16

D_MODEL = 1024
DEPTH = 1
DILATED_GROUPS = ((128, 1), (512, 4), (2048, 16))
N_GROUPS_A = len(DILATED_GROUPS)
HEADS_A = 4
HEAD_DIM = 128
WIDTH_A = HEADS_A * HEAD_DIM
BAND = 128
NUM_BUCKETS = 32
MAX_DISTANCE = 2048
N_HEADS_B = 8
WIDTH_B = N_HEADS_B * HEAD_DIM
CHUNK_B = 64
N_HEADS_C = 4
WIDTH_C = N_HEADS_C * HEAD_DIM
COLS_A = N_GROUPS_A * 3 * WIDTH_A
COLS_B = 4 * WIDTH_B
COLS_C = WIDTH_C
COLS_GATE = 3 * D_MODEL
N_EXPERT_GROUPS = 4
EXPERTS_PER_GROUP = 8
N_EXPERTS = N_EXPERT_GROUPS * EXPERTS_PER_GROUP
DN_ALPHA = (2 * DEPTH) ** 0.25
LN_EPS = 1e-5
RMS_EPS = 1e-6

NEG_BIG = -1e30
SAFE_LOG_DECAY = 80.0
VMEM_LIMIT = 56 * 1024 * 1024
EXPERT_BLOCK = 512

_NT = (((1,), (1,)), ((), ()))
_TN = (((0,), (0,)), ((), ()))


def _cparams(sem):
    return pltpu.CompilerParams(dimension_semantics=sem, vmem_limit_bytes=VMEM_LIMIT)


def _matmul_kernel(x_ref, w_ref, o_ref):
    o_ref[...] = jnp.dot(x_ref[...].astype(BF16), w_ref[...], preferred_element_type=F32).astype(o_ref.dtype)


def _matmul(x, w, out_dtype, tm, tn, name):
    t, k = x.shape
    n = w.shape[1]
    return pl.pallas_call(
        _matmul_kernel,
        grid=(n // tn, t // tm),
        in_specs=[pl.BlockSpec((tm, k), lambda j, i: (i, 0)), pl.BlockSpec((k, tn), lambda j, i: (0, j))],
        out_specs=pl.BlockSpec((tm, tn), lambda j, i: (i, j)),
        out_shape=jax.ShapeDtypeStruct((t, n), out_dtype),
        compiler_params=_cparams(("arbitrary", "arbitrary")),
        name=name,
    )(x, w)


def _t5_bucket_np(dist):
    dist = np.asarray(dist, np.int32)
    max_exact = NUM_BUCKETS // 2
    d = np.maximum(dist, 1).astype(np.float32)
    large = max_exact + (np.log(d / max_exact) / math.log(MAX_DISTANCE / max_exact) * (NUM_BUCKETS - max_exact)).astype(np.int32)
    large = np.minimum(large, NUM_BUCKETS - 1)
    return np.where(dist < max_exact, dist, large).astype(np.int32)


def _band_bias(bias_tab, dilation):
    i = np.arange(BAND)[:, None]
    j = np.arange(2 * BAND)[None, :]
    u = i + BAND - j
    in_band = (u >= 0) & (u <= BAND)
    bucket = _t5_bucket_np(np.clip(u, 0, BAND) * dilation)
    onehot = np.asarray(bucket[:, :, None] == np.arange(NUM_BUCKETS)[None, None, :], np.float32)
    bias = jnp.einsum('pqb,bh->hpq', jnp.asarray(onehot), bias_tab.astype(F32), precision=lax.Precision.HIGHEST)
    general = jnp.where(in_band[None], bias, NEG_BIG)
    first = jnp.where((in_band & (j >= BAND))[None], bias, NEG_BIG)
    return jnp.stack([first, general])


LANES = 128


def _proj_perm_kernel(*refs, r):
    n_slab = D_MODEL // LANES
    x_refs, (w_ref, o_ref, xp_ref) = refs[:n_slab], refs[n_slab:]
    tm = xp_ref.shape[0]
    n = tm // r
    for j, x_ref in enumerate(x_refs):
        for c in range(r):
            rows = x_ref[pl.ds(c, n, stride=r), :] if r > 1 else x_ref[...]
            xp_ref[c * n:(c + 1) * n, j * LANES:(j + 1) * LANES] = rows.astype(BF16)
    res = jnp.dot(xp_ref[...], w_ref[...], preferred_element_type=F32).astype(o_ref.dtype)
    for c in range(r):
        o_ref[c] = res[c * n:(c + 1) * n]


PROJ_A_TILE = 1024


def _proj_classes(x, w, col_block, n, r, name):
    b, s, d = x.shape
    tm = PROJ_A_TILE
    n_slab = d // LANES
    slabs = [pl.BlockSpec((None, tm, LANES), lambda bi, t, j=j: (bi, t, j)) for j in range(n_slab)]
    return pl.pallas_call(
        functools.partial(_proj_perm_kernel, r=r),
        grid=(b, s // tm),
        in_specs=slabs + [pl.BlockSpec((d, n), lambda bi, t: (0, col_block))],
        out_specs=pl.BlockSpec((None, r, tm // r, n), lambda bi, t: (bi, 0, t, 0)),
        out_shape=jax.ShapeDtypeStruct((b, r, s // r, n), BF16),
        scratch_shapes=[pltpu.VMEM((tm, d), BF16)],
        compiler_params=_cparams(("arbitrary", "arbitrary")),
        name=name,
    )(*([x] * n_slab), w)


BAND_SPAN = 2048
_HI_MASK = 0xFFFF0000
_LSE_LANES = LANES // HEADS_A


def _pack_bf16_pair(a, b):
    abits = pltpu.bitcast(a.astype(BF16).astype(F32), jnp.uint32)
    bbits = pltpu.bitcast(b.astype(BF16).astype(F32), jnp.uint32)
    return (abits >> 16) | (bbits & jnp.uint32(_HI_MASK))


def _unpack_bf16_pair(word):
    return pltpu.bitcast(word << 16, F32), pltpu.bitcast(word & jnp.uint32(_HI_MASK), F32)


def _band_attn_kernel(q_ref, k_ref, v_ref, kp_ref, vp_ref, bias_ref, o_ref, lse_ref, *, r):
    nq = q_ref.shape[1] // BAND
    variant = jnp.minimum(pl.program_id(1), 1)
    ones = jnp.ones((BAND, HEAD_DIM), BF16)
    lane = lax.broadcasted_iota(jnp.int32, (BAND, LANES), 1)
    heads = [slice(h * HEAD_DIM, (h + 1) * HEAD_DIM) for h in range(HEADS_A)]
    units = [(c, j) for c in range(r) for j in range(nq)]

    def prev_of(k_or_v_ref, prev_ref, c, j, sl):
        return prev_ref[c, :, sl] if j == 0 else k_or_v_ref[c, (j - 1) * BAND:j * BAND, sl]

    def scores(c, j):
        cur = slice(j * BAND, (j + 1) * BAND)
        out = []
        for h, sl in enumerate(heads):
            q = q_ref[c, cur, sl]
            bias = bias_ref[variant, h] if j == 0 else bias_ref[1, h]
            sp = lax.dot_general(q, prev_of(k_ref, kp_ref, c, j, sl), _NT, preferred_element_type=F32)
            sc = lax.dot_general(q, k_ref[c, cur, sl], _NT, preferred_element_type=F32)
            out.append((sp + bias[:, :BAND], sc + bias[:, BAND:]))
        return out

    nxt = scores(*units[0])
    for u, (c, j) in enumerate(units):
        s_all = nxt
        if u + 1 < len(units):
            nxt = scores(*units[u + 1])
        cur = slice(j * BAND, (j + 1) * BAND)
        rows = pl.ds(j * BAND * r + c, BAND, stride=r) if r > 1 else cur
        probs = []
        for sp, sc in s_all:
            m = jnp.max(jnp.maximum(sp, sc), axis=-1, keepdims=True)
            probs.append((m, jnp.exp(sp - m).astype(BF16), jnp.exp(sc - m).astype(BF16)))
        outs, lses = [], []
        for h, sl in enumerate(heads):
            m, pp, pc = probs[h]
            l = jnp.dot(pp, ones, preferred_element_type=F32) + jnp.dot(pc, ones, preferred_element_type=F32)
            o = jnp.dot(pp, prev_of(v_ref, vp_ref, c, j, sl), preferred_element_type=F32)
            o = o + jnp.dot(pc, v_ref[c, cur, sl], preferred_element_type=F32)
            outs.append(o / l)
            lses.append(m + jnp.log(l))
        for p in range(HEADS_A // 2):
            o_ref[p, rows, :] = _pack_bf16_pair(outs[2 * p], outs[2 * p + 1])
        lse = lses[-1]
        for h in range(HEADS_A - 2, -1, -1):
            lse = jnp.where(lane < (h + 1) * _LSE_LANES, lses[h], lse)
        lse_ref[rows, :] = lse


def _band_attention(pg, bias, name):
    b, r, l, _ = pg.shape
    nq = BAND_SPAN // (BAND * r)
    cur = lambda which: pl.BlockSpec((None, r, nq * BAND, WIDTH_A), lambda bi, n: (bi, 0, n, which))
    prev = lambda which: pl.BlockSpec((None, r, BAND, WIDTH_A), lambda bi, n: (bi, 0, jnp.maximum(n * nq - 1, 0), which))
    return pl.pallas_call(
        functools.partial(_band_attn_kernel, r=r),
        grid=(b, l * r // BAND_SPAN),
        in_specs=[cur(0), cur(1), cur(2), prev(1), prev(2), pl.BlockSpec(bias.shape, lambda bi, n: (0, 0, 0, 0))],
        out_specs=[pl.BlockSpec((None, HEADS_A // 2, BAND_SPAN, HEAD_DIM), lambda bi, n: (bi, 0, n, 0)),
                   pl.BlockSpec((None, BAND_SPAN, LANES), lambda bi, n: (bi, n, 0))],
        out_shape=[jax.ShapeDtypeStruct((b, HEADS_A // 2, l * r, HEAD_DIM), jnp.uint32),
                   jax.ShapeDtypeStruct((b, l * r, LANES), F32)],
        compiler_params=_cparams(("arbitrary", "arbitrary")),
        name=name,
    )(pg, pg, pg, pg, pg, bias)


HGRN_TILE = 512
_N_CHUNK = HGRN_TILE // CHUNK_B


def _dynamic_row(ref, r, ls):
    blk = ref[pl.ds(pl.multiple_of((r // 8) * 8, 8), 8), ls]
    sub = lax.broadcasted_iota(jnp.int32, blk.shape, 0)
    return jnp.sum(jnp.where(sub == r % 8, blk, 0.0), axis=0, keepdims=True)


def _sigmoid(z):
    return 0.5 * jnp.tanh(0.5 * z) + 0.5


_PROJ_PIECE = 256


def _hgrn_kernel(x_ref, xn_ref, w_ref, lbl_ref, ng_ref, o_ref, pq_ref, pf_ref, pi_ref, pg_ref, xb_ref,
                 st_ref, qs_ref, ks_ref, kh_ref, vs_ref, bs_ref, ebl_ref, a_ref, os_ref, gs_ref, kf_ref, qsil_ref,
                 *, tiles_per_seq):
    i = pl.program_id(0)
    parts = (pq_ref, pf_ref, pi_ref, pg_ref)
    per_part = WIDTH_B // _PROJ_PIECE

    def piece(k):
        cols = slice(k * _PROJ_PIECE, (k + 1) * _PROJ_PIECE)
        local = slice((k % per_part) * _PROJ_PIECE, (k % per_part + 1) * _PROJ_PIECE)
        parts[k // per_part][:, local] = jnp.dot(xb_ref[...], w_ref[:, cols], preferred_element_type=F32)

    @pl.when(i == 0)
    def _():
        xb_ref[...] = x_ref[...].astype(BF16)
        for k in range(COLS_B // _PROJ_PIECE):
            piece(k)

    @pl.when(lax.rem(i, tiles_per_seq) == 0)
    def _():
        st_ref[...] = jnp.zeros_like(st_ref)

    xb_ref[...] = xn_ref[...].astype(BF16)
    pieces = [k for part in (2, 3, 1, 0) for k in range(part * per_part, (part + 1) * per_part)]
    next_piece = lambda: piece(pieces.pop(0)) if pieces else None
    vs_ref[...] = pi_ref[...].astype(BF16)
    gv = pg_ref[...]
    gs_ref[...] = gv * _sigmoid(gv) * ng_ref[...]

    lg = lbl_ref[...]
    e = jnp.exp(lg - jnp.max(lg, axis=0, keepdims=True))
    lb = e[0:1, :] / jnp.sum(e, axis=0, keepdims=True)

    row = lax.broadcasted_iota(jnp.int32, (CHUNK_B, CHUNK_B), 0)
    colm = lax.broadcasted_iota(jnp.int32, (CHUNK_B, CHUNK_B), 1)
    causal = row >= colm
    tri = causal.astype(BF16)

    for _ in range(2):
        next_piece()
    f = lb + (1.0 - lb) * _sigmoid(pf_ref[...])
    logf = jnp.log(f)
    kf = 1.0 - f
    kf_ref[...] = kf
    l_hi = logf.astype(BF16)
    rem = logf - l_hi.astype(F32)
    l_mid = rem.astype(BF16)
    l_lo = (rem - l_mid.astype(F32)).astype(BF16)
    min_b = None
    for c in range(_N_CHUNK):
        rs = slice(c * CHUNK_B, (c + 1) * CHUNK_B)
        b = (jnp.dot(tri, l_hi[rs], preferred_element_type=F32) + jnp.dot(tri, l_mid[rs], preferred_element_type=F32)
             + jnp.dot(tri, l_lo[rs], preferred_element_type=F32))
        bs_ref[rs, :] = b
        cm = jnp.min(b[CHUNK_B - 1:CHUNK_B, :])
        min_b = cm if min_b is None else jnp.minimum(min_b, cm)
        if c % 2 == 1:
            next_piece()
    safe = min_b > -SAFE_LOG_DECAY

    qv = pq_ref[...]
    qsil_ref[...] = qv * _sigmoid(qv)
    for c in range(_N_CHUNK):
        rs = slice(c * CHUNK_B, (c + 1) * CHUNK_B)
        b = bs_ref[rs, :]
        eb_last = jnp.exp(b[CHUNK_B - 1:CHUNK_B, :])
        ebl_ref[c:c + 1, :] = eb_last
        qs_ref[rs, :] = (qsil_ref[rs, :] * jnp.exp(b)).astype(BF16)
        k_grown = kf_ref[rs, :] * jnp.exp(-b)
        ks_ref[rs, :] = k_grown.astype(BF16)
        kh_ref[rs, :] = (k_grown * eb_last).astype(BF16)
        next_piece()

    @pl.when(safe)
    def _():
        for c in range(_N_CHUNK):
            rs = slice(c * CHUNK_B, (c + 1) * CHUNK_B)
            for h in range(N_HEADS_B):
                ls = slice(h * HEAD_DIM, (h + 1) * HEAD_DIM)
                a = lax.dot_general(qs_ref[rs, ls], ks_ref[rs, ls], _NT, preferred_element_type=F32)
                a_ref[c * N_HEADS_B + h] = jnp.where(causal, a, 0.0)

    @pl.when(jnp.logical_not(safe))
    def _():
        lane = lax.broadcasted_iota(jnp.int32, (1, CHUNK_B), 1)
        trow = lax.broadcasted_iota(jnp.int32, (CHUNK_B, 1), 0)
        for c in range(_N_CHUNK):
            rs = slice(c * CHUNK_B, (c + 1) * CHUNK_B)
            b = bs_ref[rs, :]
            kh_ref[rs, :] = (kf_ref[rs, :] * jnp.exp(b[CHUNK_B - 1:CHUNK_B, :] - b)).astype(BF16)
            for h in range(N_HEADS_B):
                ls = slice(h * HEAD_DIM, (h + 1) * HEAD_DIM)
                bq = bs_ref[rs, ls]
                qh = qsil_ref[rs, ls]

                def body(s, acc, c=c, ls=ls, bq=bq, qh=qh):
                    b_s = _dynamic_row(bs_ref, c * CHUNK_B + s, ls)
                    k_s = _dynamic_row(kf_ref, c * CHUNK_B + s, ls)
                    w = jnp.exp(jnp.minimum(bq - b_s, 0.0))
                    colv = jnp.sum(qh * k_s * w, axis=-1, keepdims=True)
                    colv = jnp.where(trow >= s, colv, 0.0)
                    return acc + colv * (lane == s).astype(F32)

                a_ref[c * N_HEADS_B + h] = lax.fori_loop(0, CHUNK_B, body, jnp.zeros((CHUNK_B, CHUNK_B), F32))

    for c in range(_N_CHUNK):
        rs = slice(c * CHUNK_B, (c + 1) * CHUNK_B)
        for h in range(N_HEADS_B):
            ls = slice(h * HEAD_DIM, (h + 1) * HEAD_DIM)
            st = st_ref[h]
            vh = vs_ref[rs, ls]
            o = lax.dot_general(qs_ref[rs, ls], st.astype(BF16), _NT, preferred_element_type=F32)
            o = o + jnp.dot(a_ref[c * N_HEADS_B + h].astype(BF16), vh, preferred_element_type=F32)
            os_ref[rs, ls] = o
            st_ref[h] = st * ebl_ref[c:c + 1, ls] + lax.dot_general(vh, kh_ref[rs, ls], _TN, preferred_element_type=F32)
            if h == N_HEADS_B - 1:
                next_piece()
    while pieces:
        next_piece()

    for h in range(N_HEADS_B):
        ls = slice(h * HEAD_DIM, (h + 1) * HEAD_DIM)
        o = os_ref[:, ls]
        o = o * lax.rsqrt(jnp.mean(o * o, axis=-1, keepdims=True) + RMS_EPS)
        o_ref[:, ls] = (o * gs_ref[:, ls]).astype(o_ref.dtype)


def _hgrn2(x, w_b, lb_logits, norm_g):
    b, s, d = x.shape
    tb = HGRN_TILE
    n = b * s // tb
    n_slots = lb_logits.shape[0]
    out = pl.pallas_call(
        functools.partial(_hgrn_kernel, tiles_per_seq=s // tb),
        grid=(n,),
        in_specs=[pl.BlockSpec((tb, d), lambda i: (0, 0)),
                  pl.BlockSpec((tb, d), lambda i: (jnp.minimum(i + 1, n - 1), 0)),
                  pl.BlockSpec(w_b.shape, lambda i: (0, 0), pipeline_mode=pl.Buffered(1)),
                  pl.BlockSpec((n_slots, WIDTH_B), lambda i: (0, 0)),
                  pl.BlockSpec((1, WIDTH_B), lambda i: (0, 0))],
        out_specs=pl.BlockSpec((tb, WIDTH_B), lambda i: (i, 0)),
        out_shape=jax.ShapeDtypeStruct((b * s, WIDTH_B), BF16),
        scratch_shapes=[
            pltpu.VMEM((tb, WIDTH_B), F32),
            pltpu.VMEM((tb, WIDTH_B), F32),
            pltpu.VMEM((tb, WIDTH_B), F32),
            pltpu.VMEM((tb, WIDTH_B), F32),
            pltpu.VMEM((tb, d), BF16),
            pltpu.VMEM((N_HEADS_B, HEAD_DIM, HEAD_DIM), F32),
            pltpu.VMEM((tb, WIDTH_B), BF16),
            pltpu.VMEM((tb, WIDTH_B), BF16),
            pltpu.VMEM((tb, WIDTH_B), BF16),
            pltpu.VMEM((tb, WIDTH_B), BF16),
            pltpu.VMEM((tb, WIDTH_B), F32),
            pltpu.VMEM((8, WIDTH_B), F32),
            pltpu.VMEM((_N_CHUNK * N_HEADS_B, CHUNK_B, CHUNK_B), F32),
            pltpu.VMEM((tb, WIDTH_B), F32),
            pltpu.VMEM((tb, WIDTH_B), F32),
            pltpu.VMEM((tb, WIDTH_B), F32),
            pltpu.VMEM((tb, WIDTH_B), F32),
        ],
        compiler_params=_cparams(("arbitrary",)),
        name="hgrn2",
    )(x.reshape(b * s, d), x.reshape(b * s, d), w_b, lb_logits, norm_g)
    return out.reshape(b, s, WIDTH_B)


MERGE_TILE = 512


def _layer_norm(y, g, b):
    mu = jnp.mean(y, axis=-1, keepdims=True)
    d = y - mu
    var = jnp.mean(d * d, axis=-1, keepdims=True)
    return d * lax.rsqrt(var + LN_EPS) * g + b


def _merge_kernel(x_ref, o1_ref, l1_ref, o2_ref, l2_ref, o3_ref, l3_ref, ob_ref, mk_ref, mv_ref,
                  wc_ref, wg_ref, wa_ref, wb_ref, wcc_ref, wo_ref, g_ref, b_ref, wr_ref, br_ref,
                  h_ref, code_ref, wrow_ref, cnt_ref, oc_ref, cnt_sc, *, n_tokens):
    x = x_ref[...]
    xb = x.astype(BF16)

    qc = jnp.dot(xb, wc_ref[...], preferred_element_type=F32).astype(BF16)
    heads_c = [slice(h * HEAD_DIM, (h + 1) * HEAD_DIM) for h in range(N_HEADS_C)]
    s_all = [lax.dot_general(qc[:, ls], mk_ref[:, ls], _NT, preferred_element_type=F32) for ls in heads_c]
    gate_pre = [None] * 3
    gate_pre[0] = jnp.dot(xb, wg_ref[:, :D_MODEL], preferred_element_type=F32)
    p_all = [jnp.exp(s - jnp.max(s, axis=-1, keepdims=True)).astype(BF16) for s in s_all]
    ones = jnp.ones((mk_ref.shape[0], HEAD_DIM), BF16)
    for ls, p in zip(heads_c, p_all):
        l = jnp.dot(p, ones, preferred_element_type=F32)
        oc_ref[:, ls] = (jnp.dot(p, mv_ref[:, ls], preferred_element_type=F32) / l).astype(BF16)
    gate_pre[1] = jnp.dot(xb, wg_ref[:, D_MODEL:2 * D_MODEL], preferred_element_type=F32)
    branch_b = jnp.dot(ob_ref[...], wb_ref[...], preferred_element_type=F32)
    gate_pre[2] = jnp.dot(xb, wg_ref[:, 2 * D_MODEL:], preferred_element_type=F32)

    lses = [l1_ref[...], l2_ref[...], l3_ref[...]]
    m = functools.reduce(jnp.maximum, lses)
    es = [jnp.exp(l - m) for l in lses]
    inv = 1.0 / functools.reduce(lambda a, c: a + c, es)
    mix = [e * inv for e in es]
    o_pairs = [[_unpack_bf16_pair(o_ref[p]) for p in range(HEADS_A // 2)] for o_ref in (o1_ref, o2_ref, o3_ref)]
    oa_heads = []
    for h in range(HEADS_A):
        acc = None
        for gi in range(N_GROUPS_A):
            w = jnp.broadcast_to(mix[gi][:, h * _LSE_LANES:h * _LSE_LANES + 1], (x.shape[0], HEAD_DIM))
            term = w * o_pairs[gi][h // 2][h % 2]
            acc = term if acc is None else acc + term
        oa_heads.append(acc.astype(BF16))
    oa = jnp.concatenate(oa_heads, axis=-1)

    branch_a = jnp.dot(oa, wa_ref[...], preferred_element_type=F32)
    branch_c = jnp.dot(oc_ref[...], wcc_ref[...], preferred_element_type=F32)
    merged = _sigmoid(gate_pre[1]) * branch_b
    merged = merged + _sigmoid(gate_pre[0]) * branch_a
    merged = merged + _sigmoid(gate_pre[2]) * branch_c
    y = DN_ALPHA * x + jnp.dot(merged.astype(BF16), wo_ref[...], preferred_element_type=F32)
    hv = _layer_norm(y, g_ref[...], b_ref[...])
    h_ref[...] = hv

    first = jnp.logical_and(pl.program_id(0) == 0, pl.program_id(1) == 0)
    _route(hv, wr_ref, br_ref, code_ref, wrow_ref, cnt_ref, cnt_sc, first, n_tokens)


def _merge(x, oa_parts, ob, mkv, wc, wg, wa, wb, wcc, wo, ln_g, ln_b, wr, br):
    b, s, d = x.shape
    tm = MERGE_TILE
    nt = s // tm
    tok = lambda w: pl.BlockSpec((None, tm, w), lambda bi, t: (bi, t, 0))
    full = lambda a: pl.BlockSpec(a.shape, lambda bi, t: (0,) * a.ndim, pipeline_mode=pl.Buffered(1))
    mem = lambda half: pl.BlockSpec((None, mkv.shape[1], WIDTH_C), lambda bi, t: (bi, 0, half))
    o_spec = pl.BlockSpec((None, HEADS_A // 2, tm, HEAD_DIM), lambda bi, t: (bi, 0, t, 0))
    flat = [a for pair in oa_parts for a in pair]
    return pl.pallas_call(
        functools.partial(_merge_kernel, n_tokens=b * s),
        grid=(b, nt),
        in_specs=[tok(d)] + [o_spec, tok(LANES)] * N_GROUPS_A + [tok(WIDTH_B), mem(0), mem(1),
                  full(wc), full(wg), full(wa), full(wb), full(wcc), full(wo), full(ln_g), full(ln_b), full(wr), full(br)],
        out_specs=[tok(d), pl.BlockSpec((1, tm), lambda bi, t: (0, bi * nt + t)),
                   pl.BlockSpec((tm, LANES), lambda bi, t: (bi * nt + t, 0)),
                   pl.BlockSpec((N_ROUTE_CLASSES, LANES), lambda bi, t: (0, 0))],
        out_shape=[jax.ShapeDtypeStruct((b, s, d), F32), jax.ShapeDtypeStruct((1, b * s), jnp.int32),
                   jax.ShapeDtypeStruct((b * s, LANES), F32), jax.ShapeDtypeStruct((N_ROUTE_CLASSES, LANES), jnp.int32)],
        scratch_shapes=[pltpu.VMEM((tm, WIDTH_C), BF16), pltpu.VMEM((N_ROUTE_CLASSES, LANES), F32)],
        compiler_params=_cparams(("arbitrary", "arbitrary")),
        name="merge",
    )(x, *flat, ob, mkv, mkv, wc, wg, wa, wb, wcc, wo, ln_g, ln_b, wr, br)


N_ROUTE_CLASSES = N_EXPERT_GROUPS * EXPERTS_PER_GROUP * EXPERTS_PER_GROUP
_ROUTER_ROWS = 8 + N_EXPERTS


def _snake(lo, x):
    return jnp.where(lo % 2 == 1, EXPERTS_PER_GROUP - 1 - x, x)


def _route(hv, wr_ref, br_ref, dest_ref, wrow_ref, cnt_ref, cnt_sc, first_step, group_capacity):
    tm = hv.shape[0]

    @pl.when(first_step)
    def _():
        cnt_sc[...] = jnp.zeros_like(cnt_sc)

    h_hi = hv.astype(BF16)
    h_lo = (hv - h_hi.astype(F32)).astype(BF16)
    wv = wr_ref[...]
    w_hi = wv.astype(BF16)
    w_lo = (wv - w_hi.astype(F32)).astype(BF16)
    both = jnp.dot(h_hi, jnp.concatenate([w_hi, w_lo], axis=1), preferred_element_type=F32)
    logits = both[:, :LANES] + both[:, LANES:] + jnp.dot(h_lo, w_hi, preferred_element_type=F32)
    logits = logits.T[:_ROUTER_ROWS, :] + br_ref[:, 0:1]
    g = [logits[i:i + 1, :] for i in range(N_EXPERT_GROUPS)]
    gmax = functools.reduce(jnp.maximum, g)
    gsel = jnp.full_like(gmax, N_EXPERT_GROUPS - 1).astype(jnp.int32)
    for i in range(N_EXPERT_GROUPS - 2, -1, -1):
        gsel = jnp.where(g[i] == gmax, i, gsel)
    gprob = 1.0 / functools.reduce(lambda a, c: a + c, [jnp.exp(gi - gmax) for gi in g])

    esel = logits[8 + (N_EXPERT_GROUPS - 1) * EXPERTS_PER_GROUP:8 + N_EXPERTS, :]
    for i in range(N_EXPERT_GROUPS - 2, -1, -1):
        esel = jnp.where(gsel == i, logits[8 + i * EXPERTS_PER_GROUP:8 + (i + 1) * EXPERTS_PER_GROUP, :], esel)
    ridx = lax.broadcasted_iota(jnp.int32, (EXPERTS_PER_GROUP, tm), 0)
    v1 = jnp.max(esel, axis=0, keepdims=True)
    i1 = jnp.min(jnp.where(esel == v1, ridx, EXPERTS_PER_GROUP), axis=0, keepdims=True)
    rest = jnp.where(ridx == i1, -jnp.inf, esel)
    v2 = jnp.max(rest, axis=0, keepdims=True)
    i2 = jnp.min(jnp.where(rest == v2, ridx, EXPERTS_PER_GROUP), axis=0, keepdims=True)
    t = jnp.exp(v2 - v1)
    w1 = gprob / (1.0 + t)
    w2 = gprob * t / (1.0 + t)
    wt = jnp.where(ridx == i1, w1, jnp.where(ridx == i2, w2, 0.0))
    eye = (lax.broadcasted_iota(jnp.int32, (EXPERTS_PER_GROUP, LANES), 0)
           == lax.broadcasted_iota(jnp.int32, (EXPERTS_PER_GROUP, LANES), 1)).astype(F32)
    wrow_ref[...] = lax.dot_general(wt, eye, _TN, preferred_element_type=F32, precision=lax.Precision.HIGHEST)

    lo_e, hi_e = jnp.minimum(i1, i2), jnp.maximum(i1, i2)
    cls = (gsel * EXPERTS_PER_GROUP + lo_e) * EXPERTS_PER_GROUP + _snake(lo_e, hi_e)
    crow = lax.broadcasted_iota(jnp.int32, (N_ROUTE_CLASSES, tm), 0)
    onehot = (crow == cls).astype(F32)
    before = (lax.broadcasted_iota(jnp.int32, (tm, tm), 0) < lax.broadcasted_iota(jnp.int32, (tm, tm), 1))
    excl = jnp.dot(onehot.astype(BF16), before.astype(BF16), preferred_element_type=F32)
    pos = cnt_sc[:, 0:1] + excl
    rank = jnp.sum(onehot * pos, axis=0, keepdims=True).astype(jnp.int32)
    dest_ref[...] = cls * group_capacity + rank
    cnt_sc[...] = cnt_sc[...] + jnp.sum(onehot, axis=1, keepdims=True)
    cnt_ref[...] = cnt_sc[...].astype(jnp.int32)


DISPATCH_TILE = 1024


SUBLANES = 8
_HALF = D_MODEL // 2


def _tile_copy(src_ref, src_tok, dst_ref, dst_tok, sem):
    src = src_ref.at[pl.ds(pl.multiple_of(src_tok * SUBLANES, SUBLANES), SUBLANES)]
    dst = dst_ref.at[pl.ds(pl.multiple_of(dst_tok * SUBLANES, SUBLANES), SUBLANES)]
    return pltpu.make_async_copy(src, dst, sem)


def _tile_rows(ref, j, n):
    return ref.at[pl.ds(j, n, stride=SUBLANES), :]


def _all_tiles_wait(src_ref, dst_ref, sem):
    pltpu.make_async_copy(src_ref, dst_ref.at[pl.ds(0, src_ref.shape[0])], sem).wait()


def _dispatch_kernel(zblk_ref, dest_ref, h_ref, w_ref, xbuf_ref, pay_ref, zero_ref, sems, zsem):
    i = pl.program_id(0)
    n_steps = pl.num_programs(0)
    slot = lax.rem(i, 2)
    tm = h_ref.shape[0]
    pay = pay_ref.at[slot]

    @pl.when(i == 0)
    def _():
        zero_ref[...] = jnp.zeros_like(zero_ref)
        for k in range(zblk_ref.shape[0]):
            @pl.when(zblk_ref[k] >= 0)
            def _():
                start = pl.multiple_of(zblk_ref[k] * zero_ref.shape[0], zero_ref.shape[0])
                cp = pltpu.make_async_copy(zero_ref, xbuf_ref.at[pl.ds(start, zero_ref.shape[0])], zsem)
                cp.start()
                cp.wait()

    @pl.when(i >= 2)
    def _():
        _all_tiles_wait(pay, xbuf_ref, sems.at[slot])

    hv = h_ref[...]
    for j in range(_HALF // LANES):
        lo, hi = hv[:, j * LANES:(j + 1) * LANES], hv[:, _HALF + j * LANES:_HALF + (j + 1) * LANES]
        _tile_rows(pay, j, tm)[...] = _pack_bf16_pair(lo, hi)
    _tile_rows(pay, _HALF // LANES, tm)[...] = pltpu.bitcast(w_ref[...], jnp.uint32)
    for j in range(_HALF // LANES + 1, SUBLANES):
        _tile_rows(pay, j, tm)[...] = jnp.zeros((tm, LANES), jnp.uint32)

    def issue(t, carry):
        _tile_copy(pay, t, xbuf_ref, dest_ref[0, t], sems.at[slot]).start()
        return carry

    lax.fori_loop(0, tm, issue, 0, unroll=8)

    @pl.when(i == n_steps - 1)
    def _():
        @pl.when(i >= 1)
        def _():
            _all_tiles_wait(pay_ref.at[1 - slot], xbuf_ref, sems.at[1 - slot])

        _all_tiles_wait(pay, xbuf_ref, sems.at[slot])


def _dispatch(h2d, wrow, dest, zero_blocks, n_blocks):
    t, d = h2d.shape
    tm = DISPATCH_TILE
    grid_spec = pltpu.PrefetchScalarGridSpec(
        num_scalar_prefetch=1,
        grid=(t // tm,),
        in_specs=[pl.BlockSpec((1, tm), lambda i, zb: (0, i), memory_space=pltpu.SMEM),
                  pl.BlockSpec((tm, d), lambda i, zb: (i, 0)),
                  pl.BlockSpec((tm, LANES), lambda i, zb: (i, 0))],
        out_specs=pl.BlockSpec(memory_space=pl.ANY),
        scratch_shapes=[pltpu.VMEM((2, tm * SUBLANES, LANES), jnp.uint32),
                        pltpu.VMEM((EXPERT_BLOCK * SUBLANES, LANES), jnp.uint32),
                        pltpu.SemaphoreType.DMA((2,)), pltpu.SemaphoreType.DMA(())],
    )
    return pl.pallas_call(
        _dispatch_kernel,
        grid_spec=grid_spec,
        out_shape=jax.ShapeDtypeStruct((n_blocks * EXPERT_BLOCK * SUBLANES, LANES), jnp.uint32),
        compiler_params=_cparams(("arbitrary",)),
        name="dispatch",
    )(zero_blocks, dest, h2d, wrow)


def _expert_kernel(bg_ref, pres_ref, x_ref, wg_ref, wu_ref, wd_ref, y_ref, xs_ref, ws_ref, ys_ref):
    i = pl.program_id(0)
    eb = xs_ref.shape[0]
    lo, hi = [], []
    for j in range(_HALF // LANES):
        lo_j, hi_j = _unpack_bf16_pair(_tile_rows(x_ref, j, eb)[...])
        lo.append(lo_j)
        hi.append(hi_j)
    xs_ref[...] = jnp.concatenate(lo + hi, axis=-1).astype(BF16)
    ws_ref[...] = pltpu.bitcast(_tile_rows(x_ref, _HALF // LANES, eb)[...], F32)
    ys_ref[...] = jnp.zeros_like(ys_ref)
    halves = [slice(0, eb // 2), slice(eb // 2, eb)]

    for e in range(EXPERTS_PER_GROUP):
        @pl.when(pres_ref[i * EXPERTS_PER_GROUP + e] > 0)
        def _(e=e):
            gu = [(jnp.dot(xs_ref[p, :], wg_ref[e], preferred_element_type=F32),
                   jnp.dot(xs_ref[p, :], wu_ref[e], preferred_element_type=F32)) for p in halves]
            for p, (a, u) in zip(halves, gu):
                we = ws_ref[p, e:e + 1]
                hid = jnp.where(we != 0.0, a * _sigmoid(a) * u * we, 0.0).astype(BF16)
                ys_ref[p, :] += jnp.dot(hid, wd_ref[e], preferred_element_type=F32)

    for j in range(SUBLANES):
        _tile_rows(y_ref, j, eb)[...] = ys_ref[:, j * LANES:(j + 1) * LANES]


def _experts(xbuf, block_g, present, wg, wu, wd):
    eb = EXPERT_BLOCK
    blk = pl.BlockSpec((eb * SUBLANES, LANES), lambda i, bg, pres: (i, 0))
    gw = lambda a: pl.BlockSpec((EXPERTS_PER_GROUP,) + a.shape[1:], lambda i, bg, pres: (bg[i], 0, 0))
    grid_spec = pltpu.PrefetchScalarGridSpec(
        num_scalar_prefetch=2,
        grid=(block_g.shape[0],),
        in_specs=[blk, gw(wg), gw(wu), gw(wd)],
        out_specs=blk,
        scratch_shapes=[pltpu.VMEM((eb, D_MODEL), BF16), pltpu.VMEM((eb, LANES), F32), pltpu.VMEM((eb, D_MODEL), F32)],
    )
    return pl.pallas_call(
        _expert_kernel,
        grid_spec=grid_spec,
        out_shape=jax.ShapeDtypeStruct(xbuf.shape, F32),
        compiler_params=_cparams(("arbitrary",)),
        name="experts",
    )(block_g, present, xbuf, wg, wu, wd)


FINAL_TILE = 512


def _final_kernel(dest_ref, dest_next_ref, h_ref, g_ref, b_ref, ybuf_ref, o_ref, ys_ref, sems):
    i = pl.program_id(0)
    slot = lax.rem(i, 2)
    tm = h_ref.shape[0]

    def gather(d_ref, s):
        def issue(t, carry):
            _tile_copy(ybuf_ref, d_ref[0, t], ys_ref.at[s], t, sems.at[s]).start()
            return carry

        lax.fori_loop(0, tm, issue, 0, unroll=8)

    @pl.when(i == 0)
    def _():
        gather(dest_ref, slot)

    @pl.when(i + 1 < pl.num_programs(0))
    def _():
        gather(dest_next_ref, 1 - slot)

    ys = ys_ref.at[slot]
    pltpu.make_async_copy(ybuf_ref.at[pl.ds(0, ys.shape[0])], ys, sems.at[slot]).wait()
    y = jnp.concatenate([_tile_rows(ys, j, tm)[...] for j in range(SUBLANES)], axis=-1)
    o_ref[...] = _layer_norm(DN_ALPHA * h_ref[...] + y, g_ref[...], b_ref[...])


def _final(h2d, ybuf, dest, ln_g, ln_b):
    t, d = h2d.shape
    tm = FINAL_TILE
    n = t // tm
    row = pl.BlockSpec((tm, d), lambda i: (i, 0))
    vec = pl.BlockSpec((1, d), lambda i: (0, 0))
    return pl.pallas_call(
        _final_kernel,
        grid=(n,),
        in_specs=[pl.BlockSpec((1, tm), lambda i: (0, i), memory_space=pltpu.SMEM),
                  pl.BlockSpec((1, tm), lambda i: (0, jnp.minimum(i + 1, n - 1)), memory_space=pltpu.SMEM),
                  row, vec, vec, pl.BlockSpec(memory_space=pl.ANY)],
        out_specs=row,
        out_shape=jax.ShapeDtypeStruct((t, d), F32),
        scratch_shapes=[pltpu.VMEM((2, tm * SUBLANES, LANES), F32), pltpu.SemaphoreType.DMA((2,))],
        compiler_params=_cparams(("arbitrary",)),
        name="final",
    )(dest, dest, h2d, ln_g, ln_b, ybuf)


def kernel(x, mem, rel_bias, hgrn_lb_logits, w_in, w_mem_kv, hgrn_norm_g, w_branch_a, w_branch_b, w_branch_c, w_out, ln1_g, ln1_b, w_router_group, b_router_group, w_router_expert, b_router_expert, w_exp_gate, w_exp_up, w_exp_down, ln2_g, ln2_b):
    b, s, d = x.shape
    t = b * s
    assert d == D_MODEL and w_in.shape[0] == DEPTH == 1
    assert s % (BAND * DILATED_GROUPS[-1][1]) == 0 and s % HGRN_TILE == 0 and s % MERGE_TILE == 0 and t % EXPERT_BLOCK == 0
    li = 0
    scale = HEAD_DIM ** -0.5
    x2d = x.reshape(t, d)

    w = w_in[li]
    col_scale = np.ones((COLS_A,), np.float32).reshape(N_GROUPS_A, 3, WIDTH_A)
    col_scale[:, 0] = scale
    w_a = (w[:, :COLS_A] * jnp.asarray(col_scale.reshape(-1))).astype(BF16)
    w_b = w[:, COLS_A:COLS_A + COLS_B].astype(BF16)
    w_c = (w[:, COLS_A + COLS_B:COLS_A + COLS_B + COLS_C] * scale).astype(BF16)
    w_g = w[:, COLS_A + COLS_B + COLS_C:].astype(BF16)

    mkv = _matmul(mem.reshape(-1, d), w_mem_kv[li].astype(BF16), BF16, 256, 2 * WIDTH_C, "mem_kv")
    mkv = mkv.reshape(b, -1, 2 * WIDTH_C)

    oa_parts = []
    for gi, (_, dilation) in enumerate(DILATED_GROUPS):
        bias = _band_bias(rel_bias[:, gi * HEADS_A:(gi + 1) * HEADS_A], dilation)
        pg = _proj_classes(x, w_a, gi, 3 * WIDTH_A, dilation, f"proj_a_{gi}")
        oa_parts.append(_band_attention(pg, bias, f"band_attn_{gi}"))

    ob = _hgrn2(x, w_b, hgrn_lb_logits.astype(F32), hgrn_norm_g[li].reshape(1, WIDTH_B))

    wr = jnp.zeros((d, LANES), F32)
    wr = wr.at[:, :N_EXPERT_GROUPS].set(w_router_group[li]).at[:, 8:_ROUTER_ROWS].set(w_router_expert[li])
    br = jnp.zeros((_ROUTER_ROWS,), F32).at[:N_EXPERT_GROUPS].set(b_router_group[li]).at[8:].set(b_router_expert[li])
    br = jnp.broadcast_to(br[:, None], (_ROUTER_ROWS, 128))

    h, code, wrow, counts = _merge(
        x, oa_parts, ob, mkv, w_c, w_g, w_branch_a[li].astype(BF16), w_branch_b[li].astype(BF16),
        w_branch_c[li].astype(BF16), w_out[li].astype(BF16), ln1_g[li].reshape(1, d), ln1_b[li].reshape(1, d), wr, br)
    h2d = h.reshape(t, d)
    eb = EXPERT_BLOCK
    counts = counts[:, 0]
    per_group = EXPERTS_PER_GROUP * EXPERTS_PER_GROUP
    group_counts = jnp.sum(counts.reshape(N_EXPERT_GROUPS, per_group), axis=1)

    blocks_g = (group_counts + eb - 1) // eb
    ends = jnp.cumsum(blocks_g)
    starts = ends - blocks_g
    n_active = ends[-1:].astype(jnp.int32)
    n_blocks = t // eb + N_EXPERT_GROUPS
    blk = jnp.arange(n_blocks, dtype=jnp.int32)
    block_g = jnp.minimum(jnp.sum(ends[None, :] <= blk[:, None], axis=1), N_EXPERT_GROUPS - 1).astype(jnp.int32)
    in_group = counts.reshape(N_EXPERT_GROUPS, per_group)
    class_start = ((starts * eb)[:, None] + jnp.cumsum(in_group, axis=1) - in_group).reshape(-1)
    cls = code // t
    is_c = cls == jnp.arange(N_ROUTE_CLASSES, dtype=jnp.int32)[:, None]
    dest = (code - cls * t + jnp.sum(jnp.where(is_c, class_start[:, None], 0), axis=0, keepdims=True)).astype(jnp.int32)
    trailing = n_active + jnp.arange(N_EXPERT_GROUPS, dtype=jnp.int32)
    zero_blocks = jnp.concatenate([jnp.where(blocks_g > 0, ends - 1, -1),
                                   jnp.where(trailing < n_blocks, trailing, -1)]).astype(jnp.int32)
    cidx = np.arange(N_ROUTE_CLASSES)
    pair_lo = (cidx // EXPERTS_PER_GROUP) % EXPERTS_PER_GROUP
    pair_hi = np.where(pair_lo % 2 == 1, EXPERTS_PER_GROUP - 1 - cidx % EXPERTS_PER_GROUP, cidx % EXPERTS_PER_GROUP)
    member = np.asarray((np.arange(EXPERTS_PER_GROUP)[None, :] == pair_lo[:, None])
                        | (np.arange(EXPERTS_PER_GROUP)[None, :] == pair_hi[:, None]), np.int32)
    overlap = ((class_start[None, :] < (blk[:, None] + 1) * eb) & (class_start[None, :] + counts[None, :] > blk[:, None] * eb)
               & (counts[None, :] > 0))
    present = (jnp.sum(overlap[:, :, None].astype(jnp.int32) * jnp.asarray(member)[None], axis=1) > 0).astype(jnp.int32)

    xbuf = _dispatch(h2d, wrow, dest, zero_blocks, n_blocks)
    ybuf = _experts(xbuf, block_g, present.reshape(-1),
                    w_exp_gate[li].astype(BF16), w_exp_up[li].astype(BF16), w_exp_down[li].astype(BF16))
    out = _final(h2d, ybuf, dest, ln2_g[li].reshape(1, d), ln2_b[li].reshape(1, d))
    return out.reshape(b, s, d)
```
